```python
import math
import jax, jax.numpy as jnp
from jax import lax
import numpy as np

D_MODEL = 2048
BATCH = 4
SEQ = 2048
DEPTH = 4
DEC_BATCH = 16
DEC_SEQ = 64
PAST_LEN = 2048

CHUNK = 64
D_SSM = D_MODEL // 2
SSM_GROUP = 16
N_SSM_GROUPS = D_SSM // SSM_GROUP
SSM_STATE = 64
HEAD_DIM = 64
N_HEADS = (D_MODEL // 2) // HEAD_DIM
N_KV_HEADS = 4
KV_REP = N_HEADS // N_KV_HEADS
D_ATTN = N_HEADS * HEAD_DIM
D_KV = N_KV_HEADS * HEAD_DIM
IN_WIDTH = D_SSM + D_ATTN + 2 * D_KV
WINDOW = 128
WINDOW_CHUNKS = WINDOW // CHUNK
ROPE_DIM = HEAD_DIM // 4
ROPE_THETA = 500000.0
D_FF = ((8 * D_MODEL // 3 + 255) // 256) * 256
EPS = 1e-6

kernel_name = 'hybrid_s5_swa_streaming_step'


def rms_norm(x, g):
    xf = x.astype(jnp.float32)
    y = xf * lax.rsqrt(jnp.mean(xf * xf, axis=-1, keepdims=True) + EPS)
    return (y * g.astype(jnp.float32)).astype(x.dtype)


def partial_rope(x, pos):
    half = ROPE_DIM // 2
    inv_freq = ROPE_THETA ** (-jnp.arange(half, dtype=jnp.float32) / half)
    ang = pos.astype(jnp.float32)[:, None] * inv_freq[None, :]
    cos = jnp.cos(ang)[None, :, None, :]
    sin = jnp.sin(ang)[None, :, None, :]
    xr = x[..., :ROPE_DIM].astype(jnp.float32)
    x1, x2 = xr[..., :half], xr[..., half:]
    rot = jnp.concatenate([x1 * cos - x2 * sin, x2 * cos + x1 * sin], axis=-1)
    return jnp.concatenate([rot.astype(x.dtype), x[..., ROPE_DIM:]], axis=-1)


def s5_discretize(a_re, a_im, log_dt, b_re, b_im):
    a_re = a_re.astype(jnp.float32)
    a_im = a_im.astype(jnp.float32)
    dt = jnp.exp(log_dt.astype(jnp.float32))[:, None]
    z_re, z_im = a_re * dt, a_im * dt
    mag = jnp.exp(z_re)
    l_re, l_im = mag * jnp.cos(z_im), mag * jnp.sin(z_im)
    den = a_re * a_re + a_im * a_im
    n_re = l_re - 1.0
    f_re = (n_re * a_re + l_im * a_im) / den
    f_im = (l_im * a_re - n_re * a_im) / den
    b_re = b_re.astype(jnp.float32)
    b_im = b_im.astype(jnp.float32)
    bb_re = f_re[..., None] * b_re - f_im[..., None] * b_im
    bb_im = f_re[..., None] * b_im + f_im[..., None] * b_re
    return z_re, z_im, l_re, l_im, bb_re, bb_im


def _complex_affine_combine(e1, e2):
    a1r, a1i, b1r, b1i = e1
    a2r, a2i, b2r, b2i = e2
    return (a1r * a2r - a1i * a2i,
            a1r * a2i + a1i * a2r,
            a2r * b1r - a2i * b1i + b2r,
            a2r * b1i + a2i * b1r + b2i)


def s5_mixer(u, a_re, a_im, log_dt, b_re, b_im, c_re, c_im, d_skip, w_glu, b_glu, h0):
    bsz, t_len, _ = u.shape
    z_re, z_im, l_re, l_im, bb_re, bb_im = s5_discretize(a_re, a_im, log_dt, b_re, b_im)
    uf = u.astype(jnp.float32)
    ug = uf.reshape(bsz, t_len, N_SSM_GROUPS, SSM_GROUP)
    bu_re = jnp.einsum('btgc,gpc->btgp', ug, bb_re)
    bu_im = jnp.einsum('btgc,gpc->btgp', ug, bb_im)
    a_re_b = jnp.broadcast_to(l_re, bu_re.shape)
    a_im_b = jnp.broadcast_to(l_im, bu_im.shape)
    _, _, h_re, h_im = lax.associative_scan(_complex_affine_combine,
                                            (a_re_b, a_im_b, bu_re, bu_im), axis=1)
    if h0 is not None:
        steps = jnp.arange(1, t_len + 1, dtype=jnp.float32)[:, None, None]
        mag = jnp.exp(z_re[None] * steps)
        ang = z_im[None] * steps
        p_re, p_im = mag * jnp.cos(ang), mag * jnp.sin(ang)
        h0_re = h0[0].astype(jnp.float32)[:, None]
        h0_im = h0[1].astype(jnp.float32)[:, None]
        h_re = h_re + p_re * h0_re - p_im * h0_im
        h_im = h_im + p_re * h0_im + p_im * h0_re
    y = (jnp.einsum('btgp,gcp->btgc', h_re, c_re.astype(jnp.float32))
         - jnp.einsum('btgp,gcp->btgc', h_im, c_im.astype(jnp.float32)))
    y = y.reshape(bsz, t_len, D_SSM) + d_skip.astype(jnp.float32) * uf
    z = jax.nn.gelu(y)
    out = z * jax.nn.sigmoid(z @ w_glu.astype(jnp.float32) + b_glu.astype(jnp.float32))
    return out.astype(u.dtype), h_re[:, -1], h_im[:, -1]


def banded_attention(q, k, v, valid, sink):
    s = jnp.einsum('bnqhrd,bnkhd->bnhrqk', q.astype(jnp.float32), k.astype(jnp.float32))
    s = s * (HEAD_DIM ** -0.5)
    s = jnp.where(valid[None, :, None, None, None, :], s, -1e30)
    sk = sink.astype(jnp.float32).reshape(1, 1, N_KV_HEADS, KV_REP, 1, 1)
    m = jnp.maximum(jnp.max(s, axis=-1, keepdims=True), sk)
    p = jnp.exp(s - m)
    den = jnp.sum(p, axis=-1, keepdims=True) + jnp.exp(sk - m)
    return jnp.einsum('bnhrqk,bnkhd->bnqhrd', p / den, v.astype(jnp.float32))


def chunk_windows(t, n_chunks):
    bsz = t.shape[0]
    tc = t.reshape(bsz, n_chunks, CHUNK, N_KV_HEADS, HEAD_DIM)
    tp = jnp.pad(tc, ((0, 0), (WINDOW_CHUNKS, 0), (0, 0), (0, 0), (0, 0)))
    return jnp.concatenate([tp[:, i:i + n_chunks] for i in range(WINDOW_CHUNKS + 1)], axis=2)


def layer(x, c, pos, p, kv_cache, ssm_state):
    bsz, t_len, _ = x.shape
    mod = (c @ p['w_mod'] + p['b_mod'])[:, None, :]
    sh1, sc1, g1, sh2, sc2, g2 = jnp.split(mod, 6, axis=-1)
    h = rms_norm(x, p['norm1_g']) * (1 + sc1) + sh1
    proj = h @ p['w_in']
    u = proj[..., :D_SSM]
    q = proj[..., D_SSM:D_SSM + D_ATTN].reshape(bsz, t_len, N_HEADS, HEAD_DIM)
    k = proj[..., D_SSM + D_ATTN:D_SSM + D_ATTN + D_KV].reshape(bsz, t_len, N_KV_HEADS, HEAD_DIM)
    v = proj[..., D_SSM + D_ATTN + D_KV:].reshape(bsz, t_len, N_KV_HEADS, HEAD_DIM)
    q = partial_rope(rms_norm(q, p['q_norm_g']), pos)
    k = partial_rope(rms_norm(k, p['k_norm_g']), pos)

    ssm_out, s_re, s_im = s5_mixer(u, p['ssm_a_re'], p['ssm_a_im'], p['ssm_log_dt'],
                                   p['ssm_b_re'], p['ssm_b_im'], p['ssm_c_re'], p['ssm_c_im'],
                                   p['ssm_d'], p['w_glu'], p['b_glu'], ssm_state)

    if kv_cache is None:
        n_chunks = t_len // CHUNK
        qc = q.reshape(bsz, n_chunks, CHUNK, N_KV_HEADS, KV_REP, HEAD_DIM)
        key_chunk = (jnp.arange(n_chunks)[:, None] - WINDOW_CHUNKS
                     + jnp.arange((WINDOW_CHUNKS + 1) * CHUNK)[None, :] // CHUNK)
        valid = key_chunk >= 0
        o = banded_attention(qc, chunk_windows(k, n_chunks), chunk_windows(v, n_chunks),
                             valid, p['attn_sink'])
        new_k, new_v = k[:, -WINDOW:], v[:, -WINDOW:]
    else:
        kk = jnp.concatenate([kv_cache[0].astype(k.dtype), k], axis=1)
        vv = jnp.concatenate([kv_cache[1].astype(v.dtype), v], axis=1)
        qc = q.reshape(bsz, 1, t_len, N_KV_HEADS, KV_REP, HEAD_DIM)
        valid = jnp.ones((1, kk.shape[1]), dtype=bool)
        o = banded_attention(qc, kk[:, None], vv[:, None], valid, p['attn_sink'])
        new_k, new_v = kk[:, -WINDOW:], vv[:, -WINDOW:]
    attn_out = o.reshape(bsz, t_len, D_ATTN).astype(x.dtype)

    gates = jax.nn.sigmoid(h @ p['w_gate'] + p['b_gate'])
    gate_a, gate_b = jnp.split(gates, 2, axis=-1)
    mixed = gate_a * (ssm_out @ p['w_proj_ssm']) + gate_b * (attn_out @ p['w_proj_attn'])
    x = x + g1 * (mixed @ p['w_out'])

    h2 = rms_norm(x, p['norm2_g']) * (1 + sc2) + sh2
    ffn = (jax.nn.silu(h2 @ p['w_ffn_gate']) * (h2 @ p['w_ffn_up'])) @ p['w_ffn_down']
    x = x + g2 * ffn
    return x, new_k, new_v, s_re, s_im


def setup_inputs(seed: int = 0) -> dict:
    key = jax.random.key(seed)
    keys = iter(jax.random.split(key, 48))
    f32 = jnp.float32

    def nrm(shape, scale):
        return scale * jax.random.normal(next(keys), shape, f32)

    def gain(shape):
        return 1.0 + nrm(shape, 0.02)

    L, G, P = DEPTH, N_SSM_GROUPS, SSM_STATE
    a_im_init = math.pi * jnp.arange(P, dtype=f32)
    return {
        'x_prompt': nrm((BATCH, SEQ, D_MODEL), 1.0),
        'x_sample': nrm((DEC_BATCH, DEC_SEQ, D_MODEL), 1.0),
        'cache_k': nrm((DEPTH, DEC_BATCH, WINDOW, N_KV_HEADS, HEAD_DIM), 1.0),
        'cache_v': nrm((DEPTH, DEC_BATCH, WINDOW, N_KV_HEADS, HEAD_DIM), 1.0),
        'state_ssm_re': nrm((DEPTH, DEC_BATCH, G, P), 0.1),
        'state_ssm_im': nrm((DEPTH, DEC_BATCH, G, P), 0.1),
        'c_prompt': nrm((BATCH, D_MODEL), 1.0),
        'c_sample': nrm((DEC_BATCH, D_MODEL), 1.0),
        'w_mod': nrm((L, D_MODEL, 6 * D_MODEL), 0.2 * D_MODEL ** -0.5),
        'b_mod': nrm((L, 6 * D_MODEL), 0.01),
        'norm1_g': gain((L, D_MODEL)),
        'norm2_g': gain((L, D_MODEL)),
        'w_in': nrm((L, D_MODEL, IN_WIDTH), D_MODEL ** -0.5),
        'ssm_a_re': -0.5 + nrm((L, G, P), 0.01),
        'ssm_a_im': a_im_init[None, None, :] + nrm((L, G, P), 0.01),
        'ssm_log_dt': jax.random.uniform(next(keys), (L, G), f32, math.log(1e-3), math.log(1e-1)),
        'ssm_b_re': nrm((L, G, P, SSM_GROUP), (0.5 / SSM_GROUP) ** 0.5),
        'ssm_b_im': nrm((L, G, P, SSM_GROUP), (0.5 / SSM_GROUP) ** 0.5),
        'ssm_c_re': nrm((L, G, SSM_GROUP, P), (0.5 / P) ** 0.5),
        'ssm_c_im': nrm((L, G, SSM_GROUP, P), (0.5 / P) ** 0.5),
        'ssm_d': nrm((L, D_SSM), 0.5),
        'w_glu': nrm((L, D_SSM, D_SSM), D_SSM ** -0.5),
        'b_glu': nrm((L, D_SSM), 0.01),
        'q_norm_g': gain((L, HEAD_DIM)),
        'k_norm_g': gain((L, HEAD_DIM)),
        'attn_sink': nrm((L, N_HEADS), 0.5),
        'w_gate': nrm((L, D_MODEL, 2 * D_MODEL), D_MODEL ** -0.5),
        'b_gate': nrm((L, 2 * D_MODEL), 0.01),
        'w_proj_ssm': nrm((L, D_SSM, D_MODEL), D_SSM ** -0.5),
        'w_proj_attn': nrm((L, D_ATTN, D_MODEL), D_ATTN ** -0.5),
        'w_out': nrm((L, D_MODEL, D_MODEL), D_MODEL ** -0.5),
        'w_ffn_gate': nrm((L, D_MODEL, D_FF), D_MODEL ** -0.5),
        'w_ffn_up': nrm((L, D_MODEL, D_FF), D_MODEL ** -0.5),
        'w_ffn_down': nrm((L, D_FF, D_MODEL), D_FF ** -0.5),
    }


def reference(x_prompt, x_sample, cache_k, cache_v, state_ssm_re, state_ssm_im, c_prompt, c_sample,
              w_mod, b_mod, norm1_g, norm2_g, w_in, ssm_a_re, ssm_a_im, ssm_log_dt, ssm_b_re, ssm_b_im,
              ssm_c_re, ssm_c_im, ssm_d, w_glu, b_glu, q_norm_g, k_norm_g, attn_sink, w_gate, b_gate,
              w_proj_ssm, w_proj_attn, w_out, w_ffn_gate, w_ffn_up, w_ffn_down):
    pos_prompt = jnp.arange(x_prompt.shape[1])
    pos_sample = PAST_LEN + jnp.arange(x_sample.shape[1])
    xp, xs = x_prompt, x_sample
    pk, pv, pre, pim = [], [], [], []
    sk, sv, sre, sim = [], [], [], []
    for l in range(DEPTH):
        p = {
            'w_mod': w_mod[l], 'b_mod': b_mod[l], 'norm1_g': norm1_g[l], 'norm2_g': norm2_g[l],
            'w_in': w_in[l], 'ssm_a_re': ssm_a_re[l], 'ssm_a_im': ssm_a_im[l],
            'ssm_log_dt': ssm_log_dt[l], 'ssm_b_re': ssm_b_re[l], 'ssm_b_im': ssm_b_im[l],
            'ssm_c_re': ssm_c_re[l], 'ssm_c_im': ssm_c_im[l], 'ssm_d': ssm_d[l],
            'w_glu': w_glu[l], 'b_glu': b_glu[l], 'q_norm_g': q_norm_g[l], 'k_norm_g': k_norm_g[l],
            'attn_sink': attn_sink[l], 'w_gate': w_gate[l], 'b_gate': b_gate[l],
            'w_proj_ssm': w_proj_ssm[l], 'w_proj_attn': w_proj_attn[l], 'w_out': w_out[l],
            'w_ffn_gate': w_ffn_gate[l], 'w_ffn_up': w_ffn_up[l], 'w_ffn_down': w_ffn_down[l],
        }
        xp, k_p, v_p, re_p, im_p = layer(xp, c_prompt, pos_prompt, p, None, None)
        xs, k_s, v_s, re_s, im_s = layer(xs, c_sample, pos_sample, p,
                                         (cache_k[l], cache_v[l]),
                                         (state_ssm_re[l], state_ssm_im[l]))
        pk.append(k_p); pv.append(v_p); pre.append(re_p); pim.append(im_p)
        sk.append(k_s); sv.append(v_s); sre.append(re_s); sim.append(im_s)
    return (xp, xs,
            jnp.stack(pk), jnp.stack(pv), jnp.stack(pre), jnp.stack(pim),
            jnp.stack(sk), jnp.stack(sv), jnp.stack(sre), jnp.stack(sim))
```

```python
import functools
import math

import jax
import jax.numpy as jnp
from jax import lax
from jax.experimental import pallas as pl
from jax.experimental.pallas import tpu as pltpu

D_MODEL = 2048
DEPTH = 4
CHUNK = 64
D_SSM = 1024
SSM_GROUP = 16
N_SSM_GROUPS = 64
SSM_STATE = 64
HEAD_DIM = 64
N_HEADS = 16
N_KV_HEADS = 4
KV_REP = N_HEADS // N_KV_HEADS
D_ATTN = N_HEADS * HEAD_DIM
D_KV = N_KV_HEADS * HEAD_DIM
IN_WIDTH = D_SSM + D_ATTN + 2 * D_KV
WINDOW = 128
ROPE_DIM = 16
ROPE_THETA = 500000.0
D_FF = 5632
EPS = 1e-6
PAST_LEN = 2048

LANES = 128
SUBLANES = 8
MOD_BLOCK = CHUNK
STATE_W = N_SSM_GROUPS * SSM_STATE
SSM_KC = 4
VMEM_LIMIT = 56 * 1024 * 1024

F32 = jnp.float32
BF16 = jnp.bfloat16


def _cparams(sem):
    return pltpu.CompilerParams(dimension_semantics=sem, vmem_limit_bytes=VMEM_LIMIT)


def _sigmoid(x):
    return 1.0 / (1.0 + jnp.exp(-x))


def _mod_kernel(c_ref, w_ref, b_ref, o_ref):
    c = c_ref[...].astype(BF16)
    w = w_ref[...].astype(BF16)
    o_ref[...] = jnp.dot(c, w, preferred_element_type=F32) + b_ref[...]


def _mod_call(c_all, w_mod, b_mod):
    nb = c_all.shape[0]
    tn = 1024
    n_out = w_mod.shape[2]
    return pl.pallas_call(
        _mod_kernel,
        out_shape=jax.ShapeDtypeStruct((DEPTH, nb, n_out), F32),
        grid=(DEPTH, n_out // tn),
        in_specs=[
            pl.BlockSpec((nb, D_MODEL), lambda l, j: (0, 0)),
            pl.BlockSpec((None, D_MODEL, tn), lambda l, j: (l, 0, j)),
            pl.BlockSpec((None, 1, tn), lambda l, j: (l, 0, j)),
        ],
        out_specs=pl.BlockSpec((None, nb, tn), lambda l, j: (l, 0, j)),
        compiler_params=_cparams(("arbitrary", "arbitrary")),
        name="adaln_mod",
    )(c_all, w_mod, b_mod.reshape(DEPTH, 1, n_out))


def _s5_prep_kernel(are_ref, aim_ref, ldt_ref, bre_ref, bim_ref, pwr_ref, pwi_ref, bbr_ref, bbi_ref):
    a_re = are_ref[...]
    a_im = aim_ref[...]
    dt = jnp.exp(ldt_ref[...])
    z_re = a_re * dt
    z_im = a_im * dt
    for n in range(1, SUBLANES + 1):
        mag = jnp.exp(z_re * float(n))
        pwr_ref[n - 1] = mag * jnp.cos(z_im * float(n))
        pwi_ref[n - 1] = mag * jnp.sin(z_im * float(n))
    l_re = pwr_ref[0]
    l_im = pwi_ref[0]
    den = a_re * a_re + a_im * a_im
    n_re = l_re - 1.0
    f_re = (n_re * a_re + l_im * a_im) / den
    f_im = (l_im * a_re - n_re * a_im) / den
    for c in range(SSM_GROUP):
        b_re = bre_ref[c]
        b_im = bim_ref[c]
        bbr_ref[c] = f_re * b_re - f_im * b_im
        bbi_ref[c] = f_re * b_im + f_im * b_re


def _s5_prep_call(a_re, a_im, log_dt, bt_re, bt_im):
    g, p = N_SSM_GROUPS, SSM_STATE
    mat = pl.BlockSpec((None, g, p), lambda l: (l, 0, 0))
    stack_c = pl.BlockSpec((None, SSM_GROUP, g, p), lambda l: (l, 0, 0, 0))
    stack_n = pl.BlockSpec((None, SUBLANES, g, p), lambda l: (l, 0, 0, 0))
    return pl.pallas_call(
        _s5_prep_kernel,
        out_shape=(jax.ShapeDtypeStruct((DEPTH, SUBLANES, g, p), F32),
                   jax.ShapeDtypeStruct((DEPTH, SUBLANES, g, p), F32),
                   jax.ShapeDtypeStruct((DEPTH, SSM_GROUP, g, p), F32),
                   jax.ShapeDtypeStruct((DEPTH, SSM_GROUP, g, p), F32)),
        grid=(DEPTH,),
        in_specs=[mat, mat, pl.BlockSpec((None, g, 1), lambda l: (l, 0, 0)), stack_c, stack_c],
        out_specs=(stack_n, stack_n, stack_c, stack_c),
        compiler_params=_cparams(("arbitrary",)),
        name="s5_discretize",
    )(a_re, a_im, log_dt.reshape(DEPTH, g, 1), bt_re, bt_im)


def _norm_mm_kernel(*refs, mode, nsub):
    x_ref, g_ref, sc_ref, sh_ref, w_ref = refs[:5]
    rest = refs[5:]
    if mode == "plain":
        o_ref, h_scr = rest
    elif mode == "sigmoid":
        b_ref, o_ref, h_scr = rest
    else:
        w2_ref, o_ref, h_scr = rest

    @pl.when(pl.program_id(1) == 0)
    def _():
        gain = g_ref[...]
        for s in range(nsub):
            rows = slice(s * MOD_BLOCK, (s + 1) * MOD_BLOCK)
            x = x_ref[rows, :]
            ms = jnp.mean(x * x, axis=-1, keepdims=True)
            y = x * lax.rsqrt(ms + EPS) * gain
            h = y * (1.0 + sc_ref[s:s + 1, :]) + sh_ref[s:s + 1, :]
            h_scr[rows, :] = h.astype(BF16)

    h = h_scr[...]
    acc = jnp.dot(h, w_ref[...], preferred_element_type=F32)
    if mode == "plain":
        o_ref[...] = acc
    elif mode == "sigmoid":
        o_ref[...] = _sigmoid(acc + b_ref[...])
    else:
        up = jnp.dot(h, w2_ref[...], preferred_element_type=F32)
        o_ref[...] = (acc * _sigmoid(acc) * up).astype(o_ref.dtype)


def _norm_mm_call(x, gain, modx, sc_idx, sh_idx, w, *, mode, bias=None, w2=None, tm=1024, tn=512,
                  out_dtype=F32, name):
    n_tok = x.shape[0]
    n_out = w.shape[1]
    nsub = tm // MOD_BLOCK
    in_specs = [
        pl.BlockSpec((tm, D_MODEL), lambda i, j: (i, 0)),
        pl.BlockSpec((1, D_MODEL), lambda i, j: (0, 0)),
        pl.BlockSpec((nsub, D_MODEL), lambda i, j: (i, sc_idx)),
        pl.BlockSpec((nsub, D_MODEL), lambda i, j: (i, sh_idx)),
        pl.BlockSpec((D_MODEL, tn), lambda i, j: (0, j)),
    ]
    args = [x, gain.reshape(1, D_MODEL), modx, modx, w]
    if mode == "sigmoid":
        in_specs.append(pl.BlockSpec((1, tn), lambda i, j: (0, j)))
        args.append(bias.reshape(1, n_out))
    elif mode == "swiglu":
        in_specs.append(pl.BlockSpec((D_MODEL, tn), lambda i, j: (0, j)))
        args.append(w2)
    return pl.pallas_call(
        functools.partial(_norm_mm_kernel, mode=mode, nsub=nsub),
        out_shape=jax.ShapeDtypeStruct((n_tok, n_out), out_dtype),
        grid=(n_tok // tm, n_out // tn),
        in_specs=in_specs,
        out_specs=pl.BlockSpec((tm, tn), lambda i, j: (i, j)),
        scratch_shapes=[pltpu.VMEM((tm, D_MODEL), BF16)],
        compiler_params=_cparams(("arbitrary", "arbitrary")),
        name=name,
    )(*args)


def _mix_kernel(s_ref, a_ref, ga_ref, gb_ref, ws_ref, wa_ref, o_ref):
    ps = jnp.dot(s_ref[...], ws_ref[...], preferred_element_type=F32)
    pa = jnp.dot(a_ref[...].astype(BF16), wa_ref[...], preferred_element_type=F32)
    o_ref[...] = (ga_ref[...] * ps + gb_ref[...] * pa).astype(o_ref.dtype)


def _mix_call(ssm_out, attn_out, gates, w_ps, w_pa, *, tm=1024, tn=512):
    n_tok = ssm_out.shape[0]
    nj = D_MODEL // tn
    return pl.pallas_call(
        _mix_kernel,
        out_shape=jax.ShapeDtypeStruct((n_tok, D_MODEL), BF16),
        grid=(n_tok // tm, nj),
        in_specs=[
            pl.BlockSpec((tm, D_SSM), lambda i, j: (i, 0)),
            pl.BlockSpec((tm, D_ATTN), lambda i, j: (i, 0)),
            pl.BlockSpec((tm, tn), lambda i, j: (i, j)),
            pl.BlockSpec((tm, tn), lambda i, j: (i, j + nj)),
            pl.BlockSpec((D_SSM, tn), lambda i, j: (0, j)),
            pl.BlockSpec((D_ATTN, tn), lambda i, j: (0, j)),
        ],
        out_specs=pl.BlockSpec((tm, tn), lambda i, j: (i, j)),
        compiler_params=_cparams(("arbitrary", "arbitrary")),
        name="branch_merge",
    )(ssm_out, attn_out, gates, gates, w_ps, w_pa)


def _resid_mm_kernel(a_ref, w_ref, x_ref, g_ref, o_ref, *, nsub):
    acc = jnp.dot(a_ref[...], w_ref[...], preferred_element_type=F32)
    for s in range(nsub):
        rows = slice(s * MOD_BLOCK, (s + 1) * MOD_BLOCK)
        o_ref[rows, :] = x_ref[rows, :] + g_ref[s:s + 1, :] * acc[rows, :]


def _resid_mm_call(a, w, x, modx, g_idx, *, tm=1024, tn=512, name):
    n_tok, k = a.shape
    nsub = tm // MOD_BLOCK
    nj = D_MODEL // tn
    return pl.pallas_call(
        functools.partial(_resid_mm_kernel, nsub=nsub),
        out_shape=jax.ShapeDtypeStruct((n_tok, D_MODEL), F32),
        grid=(n_tok // tm, nj),
        in_specs=[
            pl.BlockSpec((tm, k), lambda i, j: (i, 0)),
            pl.BlockSpec((k, tn), lambda i, j: (0, j)),
            pl.BlockSpec((tm, tn), lambda i, j: (i, j)),
            pl.BlockSpec((nsub, tn), lambda i, j: (i, g_idx * nj + j)),
        ],
        out_specs=pl.BlockSpec((tm, tn), lambda i, j: (i, j)),
        compiler_params=_cparams(("arbitrary", "arbitrary")),
        name=name,
    )(a, w, x, modx)


SCAN_LW = 256


def _ssm_kernel(u_ref, h0r_ref, h0i_ref, wbr_ref, wbi_ref, tab_ref, wcr_ref, wci_ref, d_ref, wglu_ref,
                bglu_ref, o_ref, htr_ref, hti_ref, bur, bui, cre, cim, z_scr, *, tt):
    @pl.when(pl.program_id(1) == 0)
    def _():
        cre[...] = jnp.broadcast_to(h0r_ref[...], (SUBLANES, STATE_W))
        cim[...] = jnp.broadcast_to(h0i_ref[...], (SUBLANES, STATE_W))

    u = u_ref[...]
    ub = u.astype(BF16)
    kw = D_SSM // SSM_KC
    sw = STATE_W // SSM_KC
    for kc in range(SSM_KC):
        uk = ub[:, kc * kw:(kc + 1) * kw]
        bur[:, kc * sw:(kc + 1) * sw] = jnp.dot(uk, wbr_ref[kc], preferred_element_type=F32)
        bui[:, kc * sw:(kc + 1) * sw] = jnp.dot(uk, wbi_ref[kc], preferred_element_type=F32)

    for lc in range(STATE_W // SCAN_LW):
        sl = slice(lc * SCAN_LW, (lc + 1) * SCAN_LW)
        steps = [(k, tab_ref[2 * i, :, sl], tab_ref[2 * i + 1, :, sl]) for i, k in enumerate((1, 2, 4))]
        p_re = tab_ref[6, :, sl]
        p_im = tab_ref[7, :, sl]

        def body(i, carry, sl=sl, steps=steps, p_re=p_re, p_im=p_im):
            c_re, c_im = carry
            r0 = pl.multiple_of(i * SUBLANES, SUBLANES)
            x_re = bur[pl.ds(r0, SUBLANES), sl]
            x_im = bui[pl.ds(r0, SUBLANES), sl]
            for k, m_re, m_im in steps:
                s_re = pltpu.roll(x_re, k, 0)
                s_im = pltpu.roll(x_im, k, 0)
                x_re, x_im = (x_re + m_re * s_re - m_im * s_im,
                              x_im + m_re * s_im + m_im * s_re)
            h_re = jnp.broadcast_to(c_re[SUBLANES - 1:SUBLANES, :], (SUBLANES, SCAN_LW))
            h_im = jnp.broadcast_to(c_im[SUBLANES - 1:SUBLANES, :], (SUBLANES, SCAN_LW))
            x_re, x_im = (x_re + p_re * h_re - p_im * h_im,
                          x_im + p_re * h_im + p_im * h_re)
            bur[pl.ds(r0, SUBLANES), sl] = x_re
            bui[pl.ds(r0, SUBLANES), sl] = x_im
            return x_re, x_im

        c_re, c_im = lax.fori_loop(0, tt // SUBLANES, body, (cre[:, sl], cim[:, sl]))
        cre[:, sl] = c_re
        cim[:, sl] = c_im

    htr_ref[...] = cre[SUBLANES - 1:SUBLANES, :]
    hti_ref[...] = cim[SUBLANES - 1:SUBLANES, :]

    d = d_ref[...]
    for kc in range(SSM_KC):
        h_re = bur[:, kc * sw:(kc + 1) * sw].astype(BF16)
        h_im = bui[:, kc * sw:(kc + 1) * sw].astype(BF16)
        y = (jnp.dot(h_re, wcr_ref[kc], preferred_element_type=F32)
             - jnp.dot(h_im, wci_ref[kc], preferred_element_type=F32))
        cols = slice(kc * kw, (kc + 1) * kw)
        y = y + d[:, cols] * u[:, cols]
        z_scr[:, cols] = 0.5 * y * (1.0 + jnp.tanh(math.sqrt(2.0 / math.pi) * (y + 0.044715 * (y * y * y))))
    z = z_scr[...]
    gate = _sigmoid(jnp.dot(z.astype(BF16), wglu_ref[...], preferred_element_type=F32) + bglu_ref[...])
    o_ref[...] = (z * gate).astype(o_ref.dtype)


def _ssm_call(proj, row0, n_seq, t_len, tt, h0_re, h0_im, wb_re, wb_im, tab, wc_re, wc_im, d_skip, w_glu,
              b_glu, *, name):
    nt = t_len // tt
    rb0 = row0 // tt
    full3 = lambda b, t: (0, 0, 0)
    full2 = lambda b, t: (0, 0)
    state_spec = pl.BlockSpec((None, 1, STATE_W), lambda b, t: (b, 0, 0))
    return pl.pallas_call(
        functools.partial(_ssm_kernel, tt=tt),
        out_shape=(jax.ShapeDtypeStruct((n_seq * t_len, D_SSM), BF16),
                   jax.ShapeDtypeStruct((n_seq, 1, STATE_W), F32),
                   jax.ShapeDtypeStruct((n_seq, 1, STATE_W), F32)),
        grid=(n_seq, nt),
        in_specs=[
            pl.BlockSpec((tt, D_SSM), lambda b, t: (rb0 + b * nt + t, 0)),
            state_spec, state_spec,
            pl.BlockSpec(wb_re.shape, full3), pl.BlockSpec(wb_im.shape, full3),
            pl.BlockSpec(tab.shape, full3),
            pl.BlockSpec(wc_re.shape, full3), pl.BlockSpec(wc_im.shape, full3),
            pl.BlockSpec((1, D_SSM), full2),
            pl.BlockSpec((D_SSM, D_SSM), full2),
            pl.BlockSpec((1, D_SSM), full2),
        ],
        out_specs=(pl.BlockSpec((tt, D_SSM), lambda b, t: (b * nt + t, 0)), state_spec, state_spec),
        scratch_shapes=[pltpu.VMEM((tt, STATE_W), F32), pltpu.VMEM((tt, STATE_W), F32),
                        pltpu.VMEM((SUBLANES, STATE_W), F32), pltpu.VMEM((SUBLANES, STATE_W), F32),
                        pltpu.VMEM((tt, D_SSM), F32)],
        compiler_params=_cparams(("arbitrary", "arbitrary")),
        name=name,
    )(proj, h0_re, h0_im, wb_re, wb_im, tab, wc_re, wc_im, d_skip.reshape(1, D_SSM), w_glu,
      b_glu.reshape(1, D_SSM))


def _norm_rope_heads(x, gain, tab, n_heads):
    width = x.shape[1]
    reps = width // LANES
    cos = jnp.concatenate([tab[0]] * reps, axis=1)
    s_lo = jnp.concatenate([tab[1]] * reps, axis=1)
    s_hi = jnp.concatenate([tab[2]] * reps, axis=1)
    half = ROPE_DIM // 2
    xg = x * gain
    xr = xg * cos + pltpu.roll(xg, width - half, 1) * s_lo + pltpu.roll(xg, half, 1) * s_hi
    sq = x * x
    heads = []
    for h in range(n_heads):
        cols = slice(h * HEAD_DIM, (h + 1) * HEAD_DIM)
        ms = jnp.sum(sq[:, cols], axis=-1, keepdims=True) * (1.0 / HEAD_DIM)
        heads.append(xr[:, cols] * lax.rsqrt(ms + EPS))
    return heads


def _attn_kernel(sink_ref, q_ref, k0_ref, k1_ref, k2_ref, v0_ref, v1_ref, v2_ref, t0_ref, t1_ref, t2_ref,
                 gq_ref, gk_ref, o_ref, kn_ref, *, prev_normed, masked):
    c = pl.program_id(1)
    gk = gk_ref[...]
    q_heads = _norm_rope_heads(q_ref[...], gq_ref[...], t2_ref[...], N_HEADS)
    k_own = _norm_rope_heads(k2_ref[...], gk, t2_ref[...], N_KV_HEADS)
    if prev_normed:
        k_prev = [[ref[:, h * HEAD_DIM:(h + 1) * HEAD_DIM] for h in range(N_KV_HEADS)] for ref in (k0_ref, k1_ref)]
    else:
        k_prev = [_norm_rope_heads(k0_ref[...], gk, t0_ref[...], N_KV_HEADS),
                  _norm_rope_heads(k1_ref[...], gk, t1_ref[...], N_KV_HEADS)]
    n_keys = 3 * CHUNK
    if masked:
        key_chunk = c - 2 + lax.broadcasted_iota(jnp.int32, (KV_REP * CHUNK, n_keys), 1) // CHUNK
        valid = key_chunk >= 0
    for kh in range(N_KV_HEADS):
        cols = slice(kh * HEAD_DIM, (kh + 1) * HEAD_DIM)
        kn_ref[:, cols] = k_own[kh]
        keys = jnp.concatenate([k_prev[0][kh], k_prev[1][kh], k_own[kh]], axis=0).astype(BF16)
        vals = jnp.concatenate([v0_ref[:, cols], v1_ref[:, cols], v2_ref[:, cols]], axis=0).astype(BF16)
        q4 = jnp.concatenate([q_heads[kh * KV_REP + r] for r in range(KV_REP)], axis=0)
        q4 = (q4 * (HEAD_DIM ** -0.5)).astype(BF16)
        s = lax.dot_general(q4, keys, (((1,), (1,)), ((), ())), preferred_element_type=F32)
        if masked:
            s = jnp.where(valid, s, -1e30)
        sink = jnp.concatenate([jnp.full((CHUNK, 1), sink_ref[kh * KV_REP + r], F32) for r in range(KV_REP)],
                               axis=0)
        m = jnp.maximum(jnp.max(s, axis=-1, keepdims=True), sink)
        p = jnp.exp(s - m)
        den = jnp.sum(p, axis=-1, keepdims=True) + jnp.exp(sink - m)
        o = jnp.dot(p.astype(BF16), vals, preferred_element_type=F32) / den
        for r in range(KV_REP):
            h = kh * KV_REP + r
            o_ref[:, h * HEAD_DIM:(h + 1) * HEAD_DIM] = o[r * CHUNK:(r + 1) * CHUNK, :]


def _attn_call(proj, row0, n_seq, n_chunks, sink, rope_tab, gq, gk, cache_k=None, cache_v=None, *, name):
    rb0 = row0 // CHUNK
    qcol = D_SSM // D_ATTN
    kcol = (D_SSM + D_ATTN) // D_KV
    vcol = kcol + 1
    own = lambda b, c: rb0 + b * n_chunks + c
    prev_normed = cache_k is not None
    if prev_normed:
        pos_chunk = PAST_LEN // CHUNK
        kv_prev_specs = [pl.BlockSpec((CHUNK, D_KV), lambda b, c, i=i: (2 * b + i, 0)) for i in range(2)]
        k_prev_args = [cache_k, cache_k]
        v_prev_args = [cache_v, cache_v]
        v_prev_specs = kv_prev_specs
        tab_idx = [lambda b, c: (0, pos_chunk, 0)] * 3
    else:
        prev = [lambda b, c, d=d: rb0 + b * n_chunks + jnp.maximum(c - d, 0) for d in (2, 1)]
        kv_prev_specs = [pl.BlockSpec((CHUNK, D_KV), lambda b, c, f=f: (f(b, c), kcol)) for f in prev]
        v_prev_specs = [pl.BlockSpec((CHUNK, D_KV), lambda b, c, f=f: (f(b, c), vcol)) for f in prev]
        k_prev_args = [proj, proj]
        v_prev_args = [proj, proj]
        tab_idx = [lambda b, c: (0, jnp.maximum(c - 2, 0), 0), lambda b, c: (0, jnp.maximum(c - 1, 0), 0),
                   lambda b, c: (0, c, 0)]
    tab_specs = [pl.BlockSpec((3, CHUNK, LANES), f) for f in tab_idx]
    n_rows = n_seq * n_chunks * CHUNK
    return pl.pallas_call(
        functools.partial(_attn_kernel, prev_normed=prev_normed, masked=not prev_normed),
        out_shape=(jax.ShapeDtypeStruct((n_rows, D_ATTN), F32),
                   jax.ShapeDtypeStruct((n_rows, D_KV), F32)),
        grid=(n_seq, n_chunks),
        in_specs=[
            pl.BlockSpec(memory_space=pltpu.SMEM),
            pl.BlockSpec((CHUNK, D_ATTN), lambda b, c: (own(b, c), qcol)),
            *kv_prev_specs,
            pl.BlockSpec((CHUNK, D_KV), lambda b, c: (own(b, c), kcol)),
            *v_prev_specs,
            pl.BlockSpec((CHUNK, D_KV), lambda b, c: (own(b, c), vcol)),
            *tab_specs,
            pl.BlockSpec((1, D_ATTN), lambda b, c: (0, 0)),
            pl.BlockSpec((1, D_KV), lambda b, c: (0, 0)),
        ],
        out_specs=(pl.BlockSpec((CHUNK, D_ATTN), lambda b, c: (b * n_chunks + c, 0)),
                   pl.BlockSpec((CHUNK, D_KV), lambda b, c: (b * n_chunks + c, 0))),
        compiler_params=_cparams(("arbitrary", "arbitrary")),
        name=name,
    )(sink, proj, *k_prev_args, proj, *v_prev_args, proj, rope_tab, rope_tab, rope_tab, gq, gk)


def _block_diag(blocks):
    kc, n, r, c = blocks.shape
    eye = jnp.eye(n, dtype=blocks.dtype)
    return (blocks[:, :, :, None, :] * eye[None, :, None, :, None]).reshape(kc, n * r, n * c)


def _rope_table(n_pos):
    half = ROPE_DIM // 2
    inv_freq = ROPE_THETA ** (-jnp.arange(half, dtype=F32) / half)
    ang = jnp.arange(n_pos, dtype=F32)[:, None] * inv_freq[None, :]
    cos, sin = jnp.cos(ang), jnp.sin(ang)
    ones = jnp.ones((n_pos, HEAD_DIM - ROPE_DIM), F32)
    zeros = jnp.zeros((n_pos, HEAD_DIM - half), F32)
    c_tab = jnp.concatenate([cos, cos, ones], axis=1)
    lo_tab = jnp.concatenate([-sin, zeros], axis=1)
    hi_tab = jnp.concatenate([jnp.zeros((n_pos, half), F32), sin, jnp.zeros((n_pos, HEAD_DIM - ROPE_DIM), F32)],
                             axis=1)
    tab = jnp.stack([c_tab, lo_tab, hi_tab])
    return jnp.concatenate([tab, tab], axis=2)


def kernel(x_prompt, x_sample, cache_k, cache_v, state_ssm_re, state_ssm_im, c_prompt, c_sample, w_mod, b_mod, norm1_g, norm2_g, w_in, ssm_a_re, ssm_a_im, ssm_log_dt, ssm_b_re, ssm_b_im, ssm_c_re, ssm_c_im, ssm_d, w_glu, b_glu, q_norm_g, k_norm_g, attn_sink, w_gate, b_gate, w_proj_ssm, w_proj_attn, w_out, w_ffn_gate, w_ffn_up, w_ffn_down):
    bp, tp, _ = x_prompt.shape
    bs, ts, _ = x_sample.shape
    n_p, n_s = bp * tp, bs * ts
    x = jnp.concatenate([x_prompt.reshape(n_p, D_MODEL), x_sample.reshape(n_s, D_MODEL)], axis=0)

    n_cond = bp + bs
    pad = (-n_cond) % SUBLANES
    c_all = jnp.concatenate([c_prompt, c_sample, jnp.zeros((pad, D_MODEL), F32)], axis=0)
    mod = _mod_call(c_all, w_mod, b_mod)
    n_mod = mod.shape[2]

    def per_block(m, n_seq, t_len):
        reps = t_len // MOD_BLOCK
        return jnp.broadcast_to(m[:, :, None, :], (DEPTH, n_seq, reps, n_mod)).reshape(DEPTH, n_seq * reps, n_mod)

    modx = jnp.concatenate([per_block(mod[:, :bp], bp, tp), per_block(mod[:, bp:n_cond], bs, ts)],
                           axis=1)

    g, p = N_SSM_GROUPS, SSM_STATE
    pw_re, pw_im, bb_re, bb_im = _s5_prep_call(ssm_a_re, ssm_a_im, ssm_log_dt,
                                               ssm_b_re.transpose(0, 3, 1, 2), ssm_b_im.transpose(0, 3, 1, 2))
    gpc = g // SSM_KC

    def scan_table(l):
        pr = pw_re[l].reshape(SUBLANES, STATE_W)
        pi = pw_im[l].reshape(SUBLANES, STATE_W)
        rows = []
        for k in (1, 2, 4):
            keep = jnp.arange(SUBLANES)[:, None] >= k
            rows.append(jnp.where(keep, pr[k - 1][None, :], 0.0))
            rows.append(jnp.where(keep, pi[k - 1][None, :], 0.0))
        rows += [pr, pi]
        return jnp.stack(rows)

    def wb_blocks(bb):
        return _block_diag(bb.transpose(1, 0, 2).reshape(SSM_KC, gpc, SSM_GROUP, p)).astype(BF16)

    def wc_blocks(cc):
        return _block_diag(cc.transpose(0, 2, 1).reshape(SSM_KC, gpc, p, SSM_GROUP)).astype(BF16)

    rope_tab = _rope_table(PAST_LEN + ts)
    zeros_state = jnp.zeros((bp, 1, STATE_W), F32)

    outs = {k: [] for k in ("pk", "pv", "pre", "pim", "sk", "sv", "sre", "sim")}
    for l in range(DEPTH):
        mx = modx[l]
        proj = _norm_mm_call(x, norm1_g[l], mx, 1, 0, w_in[l].astype(BF16), mode="plain", name="in_proj")
        gates = _norm_mm_call(x, norm1_g[l], mx, 1, 0, w_gate[l].astype(BF16), mode="sigmoid", bias=b_gate[l],
                              name="merge_gates")

        tab = scan_table(l)
        ssm_w = (wb_blocks(bb_re[l]), wb_blocks(bb_im[l]), tab, wc_blocks(ssm_c_re[l]), wc_blocks(ssm_c_im[l]),
                 ssm_d[l], w_glu[l].astype(BF16), b_glu[l])
        ssm_p, pre, pim = _ssm_call(proj, 0, bp, tp, 256, zeros_state, zeros_state, *ssm_w, name="s5_prompt")
        ssm_s, sre, sim = _ssm_call(proj, n_p, bs, ts, ts, state_ssm_re[l].reshape(bs, 1, STATE_W),
                                    state_ssm_im[l].reshape(bs, 1, STATE_W), *ssm_w, name="s5_sample")

        gq = jnp.tile(q_norm_g[l], N_HEADS).reshape(1, D_ATTN)
        gk = jnp.tile(k_norm_g[l], N_KV_HEADS).reshape(1, D_KV)
        att_p, kn_p = _attn_call(proj, 0, bp, tp // CHUNK, attn_sink[l], rope_tab, gq, gk, name="attn_prompt")
        att_s, kn_s = _attn_call(proj, n_p, bs, ts // CHUNK, attn_sink[l], rope_tab, gq, gk,
                                 cache_k[l].reshape(bs * WINDOW, D_KV), cache_v[l].reshape(bs * WINDOW, D_KV),
                                 name="attn_sample")

        ssm_out = jnp.concatenate([ssm_p, ssm_s], axis=0)
        attn_out = jnp.concatenate([att_p, att_s], axis=0)
        mixed = _mix_call(ssm_out, attn_out, gates, w_proj_ssm[l].astype(BF16), w_proj_attn[l].astype(BF16))
        x = _resid_mm_call(mixed, w_out[l].astype(BF16), x, mx, 2, name="out_proj")

        act = _norm_mm_call(x, norm2_g[l], mx, 4, 3, w_ffn_gate[l].astype(BF16), mode="swiglu",
                            w2=w_ffn_up[l].astype(BF16), out_dtype=BF16, name="ffn_up")
        x = _resid_mm_call(act, w_ffn_down[l].astype(BF16), x, mx, 5, name="ffn_down")

        v_p = proj[:n_p, D_SSM + D_ATTN + D_KV:].reshape(bp, tp, N_KV_HEADS, HEAD_DIM)
        v_s = proj[n_p:, D_SSM + D_ATTN + D_KV:].reshape(bs, ts, N_KV_HEADS, HEAD_DIM)
        outs["pk"].append(kn_p.reshape(bp, tp, N_KV_HEADS, HEAD_DIM)[:, -WINDOW:])
        outs["pv"].append(v_p[:, -WINDOW:])
        outs["pre"].append(pre.reshape(bp, g, p))
        outs["pim"].append(pim.reshape(bp, g, p))
        outs["sk"].append(jnp.concatenate([cache_k[l], kn_s.reshape(bs, ts, N_KV_HEADS, HEAD_DIM)],
                                          axis=1)[:, -WINDOW:])
        outs["sv"].append(jnp.concatenate([cache_v[l], v_s], axis=1)[:, -WINDOW:])
        outs["sre"].append(sre.reshape(bs, g, p))
        outs["sim"].append(sim.reshape(bs, g, p))

    return (x[:n_p].reshape(bp, tp, D_MODEL), x[n_p:].reshape(bs, ts, D_MODEL),
            jnp.stack(outs["pk"]), jnp.stack(outs["pv"]), jnp.stack(outs["pre"]), jnp.stack(outs["pim"]),
            jnp.stack(outs["sk"]), jnp.stack(outs["sv"]), jnp.stack(outs["sre"]), jnp.stack(outs["sim"]))
```

```python
import functools
import math

import jax
import jax.numpy as jnp
from jax import lax
from jax.experimental import pallas as pl
from jax.experimental.pallas import tpu as pltpu

D_MODEL = 2048
DEPTH = 4
CHUNK = 64
D_SSM = 1024
SSM_GROUP = 16
N_SSM_GROUPS = 64
SSM_STATE = 64
HEAD_DIM = 64
N_HEADS = 16
N_KV_HEADS = 4
KV_REP = N_HEADS // N_KV_HEADS
D_ATTN = N_HEADS * HEAD_DIM
D_KV = N_KV_HEADS * HEAD_DIM
IN_WIDTH = D_SSM + D_ATTN + 2 * D_KV
WINDOW = 128
ROPE_DIM = 16
ROPE_THETA = 500000.0
D_FF = 5632
EPS = 1e-6
PAST_LEN = 2048

LANES = 128
SUBLANES = 8
MOD_BLOCK = CHUNK
STATE_W = N_SSM_GROUPS * SSM_STATE
SSM_KC = 4
VMEM_LIMIT = 56 * 1024 * 1024

F32 = jnp.float32
BF16 = jnp.bfloat16


def _cparams(sem):
    return pltpu.CompilerParams(dimension_semantics=sem, vmem_limit_bytes=VMEM_LIMIT)


def _sigmoid(x):
    return 1.0 / (1.0 + jnp.exp(-x))


def _mod_kernel(c_ref, w_ref, b_ref, o_ref):
    c = c_ref[...].astype(BF16)
    w = w_ref[...].astype(BF16)
    o_ref[...] = jnp.dot(c, w, preferred_element_type=F32) + b_ref[...]


def _mod_call(c_all, w_mod, b_mod):
    nb = c_all.shape[0]
    tn = 1024
    n_out = w_mod.shape[2]
    return pl.pallas_call(
        _mod_kernel,
        out_shape=jax.ShapeDtypeStruct((DEPTH, nb, n_out), F32),
        grid=(DEPTH, n_out // tn),
        in_specs=[
            pl.BlockSpec((nb, D_MODEL), lambda l, j: (0, 0)),
            pl.BlockSpec((None, D_MODEL, tn), lambda l, j: (l, 0, j)),
            pl.BlockSpec((None, 1, tn), lambda l, j: (l, 0, j)),
        ],
        out_specs=pl.BlockSpec((None, nb, tn), lambda l, j: (l, 0, j)),
        compiler_params=_cparams(("arbitrary", "arbitrary")),
        name="adaln_mod",
    )(c_all, w_mod, b_mod.reshape(DEPTH, 1, n_out))


def _s5_prep_kernel(are_ref, aim_ref, ldt_ref, bre_ref, bim_ref, pwr_ref, pwi_ref, bbr_ref, bbi_ref):
    a_re = are_ref[...]
    a_im = aim_ref[...]
    dt = jnp.exp(ldt_ref[...])
    z_re = a_re * dt
    z_im = a_im * dt
    for n in range(1, SUBLANES + 1):
        mag = jnp.exp(z_re * float(n))
        pwr_ref[n - 1] = mag * jnp.cos(z_im * float(n))
        pwi_ref[n - 1] = mag * jnp.sin(z_im * float(n))
    l_re = pwr_ref[0]
    l_im = pwi_ref[0]
    den = a_re * a_re + a_im * a_im
    n_re = l_re - 1.0
    f_re = (n_re * a_re + l_im * a_im) / den
    f_im = (l_im * a_re - n_re * a_im) / den
    for c in range(SSM_GROUP):
        b_re = bre_ref[c]
        b_im = bim_ref[c]
        bbr_ref[c] = f_re * b_re - f_im * b_im
        bbi_ref[c] = f_re * b_im + f_im * b_re


def _s5_prep_call(a_re, a_im, log_dt, bt_re, bt_im):
    g, p = N_SSM_GROUPS, SSM_STATE
    mat = pl.BlockSpec((None, g, p), lambda l: (l, 0, 0))
    stack_c = pl.BlockSpec((None, SSM_GROUP, g, p), lambda l: (l, 0, 0, 0))
    stack_n = pl.BlockSpec((None, SUBLANES, g, p), lambda l: (l, 0, 0, 0))
    return pl.pallas_call(
        _s5_prep_kernel,
        out_shape=(jax.ShapeDtypeStruct((DEPTH, SUBLANES, g, p), F32),
                   jax.ShapeDtypeStruct((DEPTH, SUBLANES, g, p), F32),
                   jax.ShapeDtypeStruct((DEPTH, SSM_GROUP, g, p), F32),
                   jax.ShapeDtypeStruct((DEPTH, SSM_GROUP, g, p), F32)),
        grid=(DEPTH,),
        in_specs=[mat, mat, pl.BlockSpec((None, g, 1), lambda l: (l, 0, 0)), stack_c, stack_c],
        out_specs=(stack_n, stack_n, stack_c, stack_c),
        compiler_params=_cparams(("arbitrary",)),
        name="s5_discretize",
    )(a_re, a_im, log_dt.reshape(DEPTH, g, 1), bt_re, bt_im)


def _norm_mm_kernel(*refs, mode, nsub):
    x_ref, g_ref, sc_ref, sh_ref, w_ref = refs[:5]
    rest = refs[5:]
    if mode == "plain":
        o_ref, h_scr = rest
    elif mode == "sigmoid":
        b_ref, o_ref, h_scr = rest
    else:
        w2_ref, o_ref, h_scr = rest

    @pl.when(pl.program_id(1) == 0)
    def _():
        gain = g_ref[...]
        for s in range(nsub):
            rows = slice(s * MOD_BLOCK, (s + 1) * MOD_BLOCK)
            x = x_ref[rows, :]
            ms = jnp.mean(x * x, axis=-1, keepdims=True)
            y = x * lax.rsqrt(ms + EPS) * gain
            h = y * (1.0 + sc_ref[s:s + 1, :]) + sh_ref[s:s + 1, :]
            h_scr[rows, :] = h.astype(BF16)

    h = h_scr[...]
    acc = jnp.dot(h, w_ref[...], preferred_element_type=F32)
    if mode == "plain":
        o_ref[...] = acc
    elif mode == "sigmoid":
        o_ref[...] = _sigmoid(acc + b_ref[...])
    else:
        up = jnp.dot(h, w2_ref[...], preferred_element_type=F32)
        o_ref[...] = (acc * _sigmoid(acc) * up).astype(o_ref.dtype)


def _norm_mm_call(x, gain, modx, sc_idx, sh_idx, w, *, mode, bias=None, w2=None, tm=1024, tn=512,
                  out_dtype=F32, name):
    n_tok = x.shape[0]
    n_out = w.shape[1]
    nsub = tm // MOD_BLOCK
    in_specs = [
        pl.BlockSpec((tm, D_MODEL), lambda i, j: (i, 0)),
        pl.BlockSpec((1, D_MODEL), lambda i, j: (0, 0)),
        pl.BlockSpec((nsub, D_MODEL), lambda i, j: (i, sc_idx)),
        pl.BlockSpec((nsub, D_MODEL), lambda i, j: (i, sh_idx)),
        pl.BlockSpec((D_MODEL, tn), lambda i, j: (0, j)),
    ]
    args = [x, gain.reshape(1, D_MODEL), modx, modx, w]
    if mode == "sigmoid":
        in_specs.append(pl.BlockSpec((1, tn), lambda i, j: (0, j)))
        args.append(bias.reshape(1, n_out))
    elif mode == "swiglu":
        in_specs.append(pl.BlockSpec((D_MODEL, tn), lambda i, j: (0, j)))
        args.append(w2)
    return pl.pallas_call(
        functools.partial(_norm_mm_kernel, mode=mode, nsub=nsub),
        out_shape=jax.ShapeDtypeStruct((n_tok, n_out), out_dtype),
        grid=(n_tok // tm, n_out // tn),
        in_specs=in_specs,
        out_specs=pl.BlockSpec((tm, tn), lambda i, j: (i, j)),
        scratch_shapes=[pltpu.VMEM((tm, D_MODEL), BF16)],
        compiler_params=_cparams(("arbitrary", "arbitrary")),
        name=name,
    )(*args)


def _mix_kernel(s_ref, a_ref, ga_ref, gb_ref, ws_ref, wa_ref, o_ref):
    ps = jnp.dot(s_ref[...], ws_ref[...], preferred_element_type=F32)
    pa = jnp.dot(a_ref[...], wa_ref[...], preferred_element_type=F32)
    o_ref[...] = (ga_ref[...] * ps + gb_ref[...] * pa).astype(o_ref.dtype)


def _mix_call(ssm_out, attn_out, gates, w_ps, w_pa, *, tm=1024, tn=512):
    n_tok = ssm_out.shape[0]
    nj = D_MODEL // tn
    return pl.pallas_call(
        _mix_kernel,
        out_shape=jax.ShapeDtypeStruct((n_tok, D_MODEL), BF16),
        grid=(n_tok // tm, nj),
        in_specs=[
            pl.BlockSpec((tm, D_SSM), lambda i, j: (i, 0)),
            pl.BlockSpec((tm, D_ATTN), lambda i, j: (i, 0)),
            pl.BlockSpec((tm, tn), lambda i, j: (i, j)),
            pl.BlockSpec((tm, tn), lambda i, j: (i, j + nj)),
            pl.BlockSpec((D_SSM, tn), lambda i, j: (0, j)),
            pl.BlockSpec((D_ATTN, tn), lambda i, j: (0, j)),
        ],
        out_specs=pl.BlockSpec((tm, tn), lambda i, j: (i, j)),
        compiler_params=_cparams(("arbitrary", "arbitrary")),
        name="branch_merge",
    )(ssm_out, attn_out, gates, gates, w_ps, w_pa)


def _resid_mm_kernel(a_ref, w_ref, x_ref, g_ref, *o_refs, nsub, n_split):
    acc = jnp.dot(a_ref[...], w_ref[...], preferred_element_type=F32)

    def write(o_ref):
        for s in range(nsub):
            rows = slice(s * MOD_BLOCK, (s + 1) * MOD_BLOCK)
            o_ref[rows, :] = x_ref[rows, :] + g_ref[s:s + 1, :] * acc[rows, :]

    if n_split is None:
        write(o_refs[0])
    else:
        i = pl.program_id(0)
        pl.when(i < n_split)(lambda: write(o_refs[0]))
        pl.when(i >= n_split)(lambda: write(o_refs[1]))


def _resid_mm_call(a, w, x, modx, g_idx, *, n_split=None, tm=1024, tn=512, name):
    n_tok, k = a.shape
    nsub = tm // MOD_BLOCK
    nj = D_MODEL // tn
    n_tiles = n_tok // tm
    if n_split is None:
        out_shape = jax.ShapeDtypeStruct((n_tok, D_MODEL), F32)
        out_specs = pl.BlockSpec((tm, tn), lambda i, j: (i, j))
    else:
        out_shape = (jax.ShapeDtypeStruct((n_split * tm, D_MODEL), F32),
                     jax.ShapeDtypeStruct(((n_tiles - n_split) * tm, D_MODEL), F32))
        out_specs = (
            pl.BlockSpec((tm, tn), lambda i, j: (jnp.minimum(i, n_split - 1), jnp.where(i < n_split, j, nj - 1))),
            pl.BlockSpec((tm, tn), lambda i, j: (jnp.maximum(i - n_split, 0), jnp.where(i < n_split, 0, j))),
        )
    return pl.pallas_call(
        functools.partial(_resid_mm_kernel, nsub=nsub, n_split=n_split),
        out_shape=out_shape,
        grid=(n_tiles, nj),
        in_specs=[
            pl.BlockSpec((tm, k), lambda i, j: (i, 0)),
            pl.BlockSpec((k, tn), lambda i, j: (0, j)),
            pl.BlockSpec((tm, tn), lambda i, j: (i, j)),
            pl.BlockSpec((nsub, tn), lambda i, j: (i, g_idx * nj + j)),
        ],
        out_specs=out_specs,
        compiler_params=_cparams(("arbitrary", "arbitrary")),
        name=name,
    )(a, w, x, modx)


SCAN_LW = 512


def _ssm_kernel(u_ref, h0r_ref, h0i_ref, wbr_ref, wbi_ref, tab_ref, wcr_ref, wci_ref, d_ref, wglu_ref,
                bglu_ref, o_ref, htr_ref, hti_ref, bur, bui, cre, cim, z_scr, *, tt):
    @pl.when(pl.program_id(1) == 0)
    def _():
        cre[...] = jnp.broadcast_to(h0r_ref[...], (SUBLANES, STATE_W))
        cim[...] = jnp.broadcast_to(h0i_ref[...], (SUBLANES, STATE_W))

    u = u_ref[...]
    ub = u.astype(BF16)
    kw = D_SSM // SSM_KC
    sw = STATE_W // SSM_KC
    for kc in range(SSM_KC):
        uk = ub[:, kc * kw:(kc + 1) * kw]
        bur[:, kc * sw:(kc + 1) * sw] = jnp.dot(uk, wbr_ref[kc], preferred_element_type=F32)
        bui[:, kc * sw:(kc + 1) * sw] = jnp.dot(uk, wbi_ref[kc], preferred_element_type=F32)

    for lc in range(STATE_W // SCAN_LW):
        sl = slice(lc * SCAN_LW, (lc + 1) * SCAN_LW)

        def body(i, carry, sl=sl):
            c_re, c_im = carry
            r0 = pl.multiple_of(i * SUBLANES, SUBLANES)
            x_re = bur[pl.ds(r0, SUBLANES), sl]
            x_im = bui[pl.ds(r0, SUBLANES), sl]
            for idx, k in enumerate((1, 2, 4)):
                m_re = tab_ref[2 * idx, :, sl]
                m_im = tab_ref[2 * idx + 1, :, sl]
                s_re = pltpu.roll(x_re, k, 0)
                s_im = pltpu.roll(x_im, k, 0)
                x_re, x_im = (x_re + m_re * s_re - m_im * s_im,
                              x_im + m_re * s_im + m_im * s_re)
            p_re = tab_ref[6, :, sl]
            p_im = tab_ref[7, :, sl]
            h_re = jnp.broadcast_to(c_re[SUBLANES - 1:SUBLANES, :], (SUBLANES, SCAN_LW))
            h_im = jnp.broadcast_to(c_im[SUBLANES - 1:SUBLANES, :], (SUBLANES, SCAN_LW))
            x_re, x_im = (x_re + p_re * h_re - p_im * h_im,
                          x_im + p_re * h_im + p_im * h_re)
            bur[pl.ds(r0, SUBLANES), sl] = x_re
            bui[pl.ds(r0, SUBLANES), sl] = x_im
            return x_re, x_im

        c_re, c_im = lax.fori_loop(0, tt // SUBLANES, body, (cre[:, sl], cim[:, sl]), unroll=2)
        cre[:, sl] = c_re
        cim[:, sl] = c_im

    htr_ref[...] = cre[SUBLANES - 1:SUBLANES, :]
    hti_ref[...] = cim[SUBLANES - 1:SUBLANES, :]

    d = d_ref[...]
    for kc in range(SSM_KC):
        h_re = bur[:, kc * sw:(kc + 1) * sw].astype(BF16)
        h_im = bui[:, kc * sw:(kc + 1) * sw].astype(BF16)
        y = (jnp.dot(h_re, wcr_ref[kc], preferred_element_type=F32)
             - jnp.dot(h_im, wci_ref[kc], preferred_element_type=F32))
        cols = slice(kc * kw, (kc + 1) * kw)
        y = y + d[:, cols] * u[:, cols]
        z_scr[:, cols] = 0.5 * y * (1.0 + jnp.tanh(math.sqrt(2.0 / math.pi) * (y + 0.044715 * (y * y * y))))
    z = z_scr[...]
    gate = _sigmoid(jnp.dot(z.astype(BF16), wglu_ref[...], preferred_element_type=F32) + bglu_ref[...])
    o_ref[...] = (z * gate).astype(o_ref.dtype)


def _ssm_call(proj, row0, n_seq, t_len, tt, h0_re, h0_im, wb_re, wb_im, tab, wc_re, wc_im, d_skip, w_glu,
              b_glu, *, name):
    nt = t_len // tt
    rb0 = row0 // tt
    full3 = lambda b, t: (0, 0, 0)
    full2 = lambda b, t: (0, 0)
    state_spec = pl.BlockSpec((None, 1, STATE_W), lambda b, t: (b, 0, 0))
    return pl.pallas_call(
        functools.partial(_ssm_kernel, tt=tt),
        out_shape=(jax.ShapeDtypeStruct((n_seq * t_len, D_SSM), BF16),
                   jax.ShapeDtypeStruct((n_seq, 1, STATE_W), F32),
                   jax.ShapeDtypeStruct((n_seq, 1, STATE_W), F32)),
        grid=(n_seq, nt),
        in_specs=[
            pl.BlockSpec((tt, D_SSM), lambda b, t: (rb0 + b * nt + t, 0)),
            state_spec, state_spec,
            pl.BlockSpec(wb_re.shape, full3), pl.BlockSpec(wb_im.shape, full3),
            pl.BlockSpec(tab.shape, full3),
            pl.BlockSpec(wc_re.shape, full3), pl.BlockSpec(wc_im.shape, full3),
            pl.BlockSpec((1, D_SSM), full2),
            pl.BlockSpec((D_SSM, D_SSM), full2),
            pl.BlockSpec((1, D_SSM), full2),
        ],
        out_specs=(pl.BlockSpec((tt, D_SSM), lambda b, t: (b * nt + t, 0)), state_spec, state_spec),
        scratch_shapes=[pltpu.VMEM((tt, STATE_W), F32), pltpu.VMEM((tt, STATE_W), F32),
                        pltpu.VMEM((SUBLANES, STATE_W), F32), pltpu.VMEM((SUBLANES, STATE_W), F32),
                        pltpu.VMEM((tt, D_SSM), F32)],
        compiler_params=_cparams(("arbitrary", "arbitrary")),
        name=name,
    )(proj, h0_re, h0_im, wb_re, wb_im, tab, wc_re, wc_im, d_skip.reshape(1, D_SSM), w_glu,
      b_glu.reshape(1, D_SSM))


RING = 3
N_KEYS = RING * CHUNK


def _stack_blocks(x):
    return jnp.concatenate([x[:, j * LANES:(j + 1) * LANES] for j in range(x.shape[1] // LANES)], axis=0)


def _pair_norm_rope(xs, gain, tab, ones_bd):
    reps = xs.shape[0] // CHUNK
    cos, s_lo, s_hi = (jnp.concatenate([tab[i]] * reps, axis=0) for i in range(3))
    half = ROPE_DIM // 2
    sq = xs * xs
    sq_hi = sq.astype(BF16)
    sq_lo = (sq - sq_hi.astype(F32)).astype(BF16)
    ss = (jnp.dot(sq_hi, ones_bd, preferred_element_type=F32)
          + jnp.dot(sq_lo, ones_bd, preferred_element_type=F32))
    xg = xs * gain
    xr = xg * cos + pltpu.roll(xg, LANES - half, 1) * s_lo + pltpu.roll(xg, half, 1) * s_hi
    return xr * lax.rsqrt(ss * (1.0 / HEAD_DIM) + EPS)


def _attn_kernel(*refs, from_cache):
    if from_cache:
        sink_ref, q_ref, k_ref, v_ref, ck_ref, cv_ref, tab_ref, gq_ref, gk_ref, o_ref, kn_ref, kd, vd = refs
    else:
        sink_ref, q_ref, k_ref, v_ref, tab_ref, gq_ref, gk_ref, o_ref, kn_ref, kd, vd = refs
    c = pl.program_id(1)
    first = lax.broadcasted_iota(jnp.int32, (1, LANES), 1) < HEAD_DIM
    ones_bd = jnp.where(lax.broadcasted_iota(jnp.int32, (LANES, LANES), 0) // HEAD_DIM
                        == lax.broadcasted_iota(jnp.int32, (LANES, LANES), 1) // HEAD_DIM, 1.0, 0.0).astype(BF16)

    def store_dup(dst, x, rows):
        for pair in range(D_KV // LANES):
            blk = x[:, pair * LANES:(pair + 1) * LANES]
            swapped = pltpu.roll(blk, HEAD_DIM, 1)
            dst[2 * pair, rows, 0:LANES] = jnp.where(first, blk, swapped).astype(BF16)
            dst[2 * pair + 1, rows, 0:LANES] = jnp.where(first, swapped, blk).astype(BF16)

    @pl.when(c == 0)
    def _():
        vd[:, :, LANES:] = jnp.ones((N_KV_HEADS, N_KEYS, LANES), BF16)
        if from_cache:
            store_dup(kd, ck_ref[...], slice(CHUNK, N_KEYS))
            store_dup(vd, cv_ref[...], slice(CHUNK, N_KEYS))
        else:
            kd[:, CHUNK:N_KEYS, :] = jnp.zeros((N_KV_HEADS, N_KEYS - CHUNK, LANES), BF16)
            vd[:, CHUNK:N_KEYS, 0:LANES] = jnp.zeros((N_KV_HEADS, N_KEYS - CHUNK, LANES), BF16)

    tab = tab_ref[...]
    own = pl.ds(pl.multiple_of(lax.rem(c, RING) * CHUNK, CHUNK), CHUNK)
    kn = _pair_norm_rope(_stack_blocks(k_ref[...]), gk_ref[...], tab, ones_bd)
    kn = jnp.concatenate([kn[0:CHUNK], kn[CHUNK:2 * CHUNK]], axis=1)
    kn_ref[...] = kn
    store_dup(kd, kn, own)
    store_dup(vd, v_ref[...], own)

    qn = _pair_norm_rope(_stack_blocks(q_ref[...]), gq_ref[...], tab, ones_bd) * (HEAD_DIM ** -0.5)
    q_lo = jnp.where(first, qn, 0.0).astype(BF16)
    q_hi = jnp.where(first, 0.0, qn).astype(BF16)

    if not from_cache:
        valid = lax.broadcasted_iota(jnp.int32, (1, N_KEYS), 1) // CHUNK <= c
    for kh in range(N_KV_HEADS):
        r0 = kh * 2 * CHUNK
        lhs = jnp.concatenate([q_lo[r0:r0 + CHUNK], q_hi[r0:r0 + CHUNK],
                               q_lo[r0 + CHUNK:r0 + 2 * CHUNK], q_hi[r0 + CHUNK:r0 + 2 * CHUNK]], axis=0)
        s = lax.dot_general(lhs, kd[kh], (((1,), (1,)), ((), ())), preferred_element_type=F32)
        if not from_cache:
            s = jnp.where(valid, s, -1e30)
        sink = jnp.concatenate([jnp.full((CHUNK, 1), sink_ref[kh * KV_REP + r], F32) for r in range(KV_REP)],
                               axis=0)
        m = jnp.maximum(jnp.max(s, axis=-1, keepdims=True), sink)
        p = jnp.exp(s - m).astype(BF16)
        od = jnp.dot(p, vd[kh], preferred_element_type=F32)
        o = od[:, 0:LANES] / (od[:, LANES:] + jnp.exp(sink - m))
        for pair in range(2):
            rows = pair * 2 * CHUNK
            blk = jnp.where(first, o[rows:rows + CHUNK], o[rows + CHUNK:rows + 2 * CHUNK])
            col = (2 * kh + pair) * LANES
            o_ref[:, col:col + LANES] = blk.astype(o_ref.dtype)


def _attn_call(proj, row0, n_seq, n_chunks, pos0, sink, rope_tab, gq, gk, cache_k=None, cache_v=None, *, name):
    rb0 = row0 // CHUNK
    pc0 = pos0 // CHUNK
    qcol = D_SSM // D_ATTN
    kcol = (D_SSM + D_ATTN) // D_KV
    vcol = kcol + 1
    own = lambda b, c: rb0 + b * n_chunks + c
    from_cache = cache_k is not None
    cache_specs, cache_args = [], []
    if from_cache:
        cache_specs = [pl.BlockSpec((WINDOW, D_KV), lambda b, c: (b, 0))] * 2
        cache_args = [cache_k, cache_v]
    n_rows = n_seq * n_chunks * CHUNK
    return pl.pallas_call(
        functools.partial(_attn_kernel, from_cache=from_cache),
        out_shape=(jax.ShapeDtypeStruct((n_rows, D_ATTN), BF16),
                   jax.ShapeDtypeStruct((n_rows, D_KV), F32)),
        grid=(n_seq, n_chunks),
        in_specs=[
            pl.BlockSpec(memory_space=pltpu.SMEM),
            pl.BlockSpec((CHUNK, D_ATTN), lambda b, c: (own(b, c), qcol)),
            pl.BlockSpec((CHUNK, D_KV), lambda b, c: (own(b, c), kcol)),
            pl.BlockSpec((CHUNK, D_KV), lambda b, c: (own(b, c), vcol)),
            *cache_specs,
            pl.BlockSpec((3, CHUNK, LANES), lambda b, c: (0, pc0 + c, 0)),
            pl.BlockSpec((1, LANES), lambda b, c: (0, 0)),
            pl.BlockSpec((1, LANES), lambda b, c: (0, 0)),
        ],
        out_specs=(pl.BlockSpec((CHUNK, D_ATTN), lambda b, c: (b * n_chunks + c, 0)),
                   pl.BlockSpec((CHUNK, D_KV), lambda b, c: (b * n_chunks + c, 0))),
        scratch_shapes=[pltpu.VMEM((N_KV_HEADS, N_KEYS, LANES), BF16),
                        pltpu.VMEM((N_KV_HEADS, N_KEYS, 2 * LANES), BF16)],
        compiler_params=_cparams(("arbitrary", "arbitrary")),
        name=name,
    )(sink, proj, proj, proj, *cache_args, rope_tab, gq, gk)


def _block_diag(blocks):
    *lead, n, r, c = blocks.shape
    eye = jnp.eye(n, dtype=blocks.dtype)
    return (blocks[..., :, :, None, :] * eye[:, None, :, None]).reshape(*lead, n * r, n * c)


def _rope_table(n_pos):
    half = ROPE_DIM // 2
    inv_freq = ROPE_THETA ** (-jnp.arange(half, dtype=F32) / half)
    ang = jnp.arange(n_pos, dtype=F32)[:, None] * inv_freq[None, :]
    cos, sin = jnp.cos(ang), jnp.sin(ang)
    ones = jnp.ones((n_pos, HEAD_DIM - ROPE_DIM), F32)
    zeros = jnp.zeros((n_pos, HEAD_DIM - half), F32)
    c_tab = jnp.concatenate([cos, cos, ones], axis=1)
    lo_tab = jnp.concatenate([-sin, zeros], axis=1)
    hi_tab = jnp.concatenate([jnp.zeros((n_pos, half), F32), sin, jnp.zeros((n_pos, HEAD_DIM - ROPE_DIM), F32)],
                             axis=1)
    tab = jnp.stack([c_tab, lo_tab, hi_tab])
    return jnp.concatenate([tab, tab], axis=2)


def kernel(x_prompt, x_sample, cache_k, cache_v, state_ssm_re, state_ssm_im, c_prompt, c_sample, w_mod, b_mod, norm1_g, norm2_g, w_in, ssm_a_re, ssm_a_im, ssm_log_dt, ssm_b_re, ssm_b_im, ssm_c_re, ssm_c_im, ssm_d, w_glu, b_glu, q_norm_g, k_norm_g, attn_sink, w_gate, b_gate, w_proj_ssm, w_proj_attn, w_out, w_ffn_gate, w_ffn_up, w_ffn_down):
    bp, tp, _ = x_prompt.shape
    bs, ts, _ = x_sample.shape
    n_p, n_s = bp * tp, bs * ts
    tm = 1024
    x = jnp.concatenate([x_prompt.reshape(n_p, D_MODEL), x_sample.reshape(n_s, D_MODEL)], axis=0)

    n_cond = bp + bs
    pad = (-n_cond) % SUBLANES
    c_all = jnp.concatenate([c_prompt, c_sample, jnp.zeros((pad, D_MODEL), F32)], axis=0)
    mod = _mod_call(c_all, w_mod, b_mod)
    n_mod = mod.shape[2]

    def per_block(m, n_seq, t_len):
        reps = t_len // MOD_BLOCK
        return jnp.broadcast_to(m[:, :, None, :], (DEPTH, n_seq, reps, n_mod)).reshape(DEPTH, n_seq * reps, n_mod)

    modx = jnp.concatenate([per_block(mod[:, :bp], bp, tp), per_block(mod[:, bp:n_cond], bs, ts)],
                           axis=1)

    g, p = N_SSM_GROUPS, SSM_STATE
    gpc = g // SSM_KC
    pw_re, pw_im, bb_re, bb_im = _s5_prep_call(ssm_a_re, ssm_a_im, ssm_log_dt,
                                               ssm_b_re.transpose(0, 3, 1, 2), ssm_b_im.transpose(0, 3, 1, 2))
    pr = pw_re.reshape(DEPTH, SUBLANES, STATE_W)
    pi = pw_im.reshape(DEPTH, SUBLANES, STATE_W)
    tab_rows = []
    for k in (1, 2, 4):
        keep = (jnp.arange(SUBLANES) >= k)[None, :, None]
        tab_rows.append(jnp.where(keep, pr[:, k - 1:k, :], 0.0))
        tab_rows.append(jnp.where(keep, pi[:, k - 1:k, :], 0.0))
    scan_tab = jnp.stack(tab_rows + [pr, pi], axis=1)

    def wb_blocks(bb):
        return _block_diag(bb.transpose(0, 2, 1, 3).reshape(DEPTH, SSM_KC, gpc, SSM_GROUP, p)).astype(BF16)

    def wc_blocks(cc):
        return _block_diag(cc.transpose(0, 1, 3, 2).reshape(DEPTH, SSM_KC, gpc, p, SSM_GROUP)).astype(BF16)

    wb_re, wb_im = wb_blocks(bb_re), wb_blocks(bb_im)
    wc_re, wc_im = wc_blocks(ssm_c_re), wc_blocks(ssm_c_im)
    w_glu_b = w_glu.astype(BF16)

    rope_tab = _rope_table(PAST_LEN + ts)
    zeros_state = jnp.zeros((bp, 1, STATE_W), F32)
    gq = jnp.tile(q_norm_g, (1, LANES // HEAD_DIM)).reshape(DEPTH, 1, LANES)
    gk = jnp.tile(k_norm_g, (1, LANES // HEAD_DIM)).reshape(DEPTH, 1, LANES)
    w_in_b, w_gate_b = w_in.astype(BF16), w_gate.astype(BF16)
    w_ps_b, w_pa_b, w_out_b = w_proj_ssm.astype(BF16), w_proj_attn.astype(BF16), w_out.astype(BF16)
    w_fg_b, w_fu_b, w_fd_b = w_ffn_gate.astype(BF16), w_ffn_up.astype(BF16), w_ffn_down.astype(BF16)
    v0 = D_SSM + D_ATTN + D_KV

    outs = {k: [] for k in ("pk", "pv", "pre", "pim", "sk", "sv", "sre", "sim")}
    for l in range(DEPTH):
        mx = modx[l]
        proj = _norm_mm_call(x, norm1_g[l], mx, 1, 0, w_in_b[l], mode="plain", tm=tm, name="in_proj")
        gates = _norm_mm_call(x, norm1_g[l], mx, 1, 0, w_gate_b[l], mode="sigmoid", bias=b_gate[l], tm=tm,
                              name="merge_gates")

        ssm_w = (wb_re[l], wb_im[l], scan_tab[l], wc_re[l], wc_im[l], ssm_d[l], w_glu_b[l], b_glu[l])
        ssm_p, pre, pim = _ssm_call(proj, 0, bp, tp, 256, zeros_state, zeros_state, *ssm_w, name="s5_prompt")
        ssm_s, sre, sim = _ssm_call(proj, n_p, bs, ts, ts, state_ssm_re[l].reshape(bs, 1, STATE_W),
                                    state_ssm_im[l].reshape(bs, 1, STATE_W), *ssm_w, name="s5_sample")

        att_p, kn_p = _attn_call(proj, 0, bp, tp // CHUNK, 0, attn_sink[l], rope_tab, gq[l], gk[l],
                                 name="attn_prompt")
        att_s, kn_s = _attn_call(proj, n_p, bs, ts // CHUNK, PAST_LEN, attn_sink[l], rope_tab, gq[l], gk[l],
                                 cache_k[l].reshape(bs * WINDOW, D_KV), cache_v[l].reshape(bs * WINDOW, D_KV),
                                 name="attn_sample")

        ssm_out = jnp.concatenate([ssm_p, ssm_s], axis=0)
        attn_out = jnp.concatenate([att_p, att_s], axis=0)
        mixed = _mix_call(ssm_out, attn_out, gates, w_ps_b[l], w_pa_b[l], tm=tm)
        x = _resid_mm_call(mixed, w_out_b[l], x, mx, 2, tm=tm, name="out_proj")

        act = _norm_mm_call(x, norm2_g[l], mx, 4, 3, w_fg_b[l], mode="swiglu", w2=w_fu_b[l], out_dtype=BF16,
                            tm=tm, name="ffn_up")
        last = l == DEPTH - 1
        x = _resid_mm_call(act, w_fd_b[l], x, mx, 5, n_split=n_p // tm if last else None, tm=tm,
                           name="ffn_down_split" if last else "ffn_down")

        v_p = proj[:n_p].reshape(bp, tp, IN_WIDTH)[:, tp - WINDOW:, v0:].reshape(bp, WINDOW, N_KV_HEADS, HEAD_DIM)
        v_s = proj[n_p:, v0:].reshape(bs, ts, N_KV_HEADS, HEAD_DIM)
        outs["pk"].append(kn_p.reshape(bp, tp, N_KV_HEADS, HEAD_DIM)[:, tp - WINDOW:])
        outs["pv"].append(v_p)
        outs["pre"].append(pre.reshape(bp, g, p))
        outs["pim"].append(pim.reshape(bp, g, p))
        outs["sk"].append(jnp.concatenate([cache_k[l][:, ts:], kn_s.reshape(bs, ts, N_KV_HEADS, HEAD_DIM)], axis=1))
        outs["sv"].append(jnp.concatenate([cache_v[l][:, ts:], v_s], axis=1))
        outs["sre"].append(sre.reshape(bs, g, p))
        outs["sim"].append(sim.reshape(bs, g, p))

    y_p, y_s = x
    return (y_p.reshape(bp, tp, D_MODEL), y_s.reshape(bs, ts, D_MODEL),
            jnp.stack(outs["pk"]), jnp.stack(outs["pv"]), jnp.stack(outs["pre"]), jnp.stack(outs["pim"]),
            jnp.stack(outs["sk"]), jnp.stack(outs["sv"]), jnp.stack(outs["sre"]), jnp.stack(outs["sim"]))
```

```python
import functools
import math

import jax
import jax.numpy as jnp
from jax import lax
from jax.experimental import pallas as pl
from jax.experimental.pallas import tpu as pltpu

D_MODEL = 2048
DEPTH = 4
CHUNK = 64
D_SSM = 1024
SSM_GROUP = 16
N_SSM_GROUPS = 64
SSM_STATE = 64
HEAD_DIM = 64
N_HEADS = 16
N_KV_HEADS = 4
KV_REP = N_HEADS // N_KV_HEADS
D_ATTN = N_HEADS * HEAD_DIM
D_KV = N_KV_HEADS * HEAD_DIM
IN_WIDTH = D_SSM + D_ATTN + 2 * D_KV
WINDOW = 128
ROPE_DIM = 16
ROPE_THETA = 500000.0
D_FF = 5632
EPS = 1e-6
PAST_LEN = 2048

LANES = 128
SUBLANES = 8
MOD_BLOCK = CHUNK
STATE_W = N_SSM_GROUPS * SSM_STATE
SSM_KC = 4
VMEM_LIMIT = 56 * 1024 * 1024

F32 = jnp.float32
BF16 = jnp.bfloat16


def _cparams(sem):
    return pltpu.CompilerParams(dimension_semantics=sem, vmem_limit_bytes=VMEM_LIMIT)


def _sigmoid(x):
    return 1.0 / (1.0 + jnp.exp(-x))


def _layer_spec(l, shape):
    zeros = (0,) * len(shape)
    return pl.BlockSpec((None, *shape), lambda *_: (l, *zeros))


def _mod_kernel(c_ref, w_ref, b_ref, o_ref):
    c = c_ref[...].astype(BF16)
    w = w_ref[...].astype(BF16)
    o_ref[...] = jnp.dot(c, w, preferred_element_type=F32) + b_ref[...]


def _mod_call(c_all, w_mod, b_mod):
    nb = c_all.shape[0]
    tn = 1024
    n_out = w_mod.shape[2]
    return pl.pallas_call(
        _mod_kernel,
        out_shape=jax.ShapeDtypeStruct((DEPTH, nb, n_out), F32),
        grid=(DEPTH, n_out // tn),
        in_specs=[
            pl.BlockSpec((nb, D_MODEL), lambda l, j: (0, 0)),
            pl.BlockSpec((None, D_MODEL, tn), lambda l, j: (l, 0, j)),
            pl.BlockSpec((None, 1, tn), lambda l, j: (l, 0, j)),
        ],
        out_specs=pl.BlockSpec((None, nb, tn), lambda l, j: (l, 0, j)),
        compiler_params=_cparams(("arbitrary", "arbitrary")),
        name="adaln_mod",
    )(c_all, w_mod, b_mod.reshape(DEPTH, 1, n_out))


def _s5_prep_kernel(are_ref, aim_ref, ldt_ref, bre_ref, bim_ref, pwr_ref, pwi_ref, bbr_ref, bbi_ref):
    a_re = are_ref[...]
    a_im = aim_ref[...]
    dt = jnp.exp(ldt_ref[...])
    z_re = a_re * dt
    z_im = a_im * dt
    for n in range(1, SUBLANES + 1):
        mag = jnp.exp(z_re * float(n))
        pwr_ref[n - 1] = mag * jnp.cos(z_im * float(n))
        pwi_ref[n - 1] = mag * jnp.sin(z_im * float(n))
    l_re = pwr_ref[0]
    l_im = pwi_ref[0]
    den = a_re * a_re + a_im * a_im
    n_re = l_re - 1.0
    f_re = (n_re * a_re + l_im * a_im) / den
    f_im = (l_im * a_re - n_re * a_im) / den
    for c in range(SSM_GROUP):
        b_re = bre_ref[c]
        b_im = bim_ref[c]
        bbr_ref[c] = f_re * b_re - f_im * b_im
        bbi_ref[c] = f_re * b_im + f_im * b_re


def _s5_prep_call(a_re, a_im, log_dt, bt_re, bt_im):
    g, p = N_SSM_GROUPS, SSM_STATE
    mat = pl.BlockSpec((None, g, p), lambda l: (l, 0, 0))
    stack_c = pl.BlockSpec((None, SSM_GROUP, g, p), lambda l: (l, 0, 0, 0))
    stack_n = pl.BlockSpec((None, SUBLANES, g, p), lambda l: (l, 0, 0, 0))
    return pl.pallas_call(
        _s5_prep_kernel,
        out_shape=(jax.ShapeDtypeStruct((DEPTH, SUBLANES, g, p), F32),
                   jax.ShapeDtypeStruct((DEPTH, SUBLANES, g, p), F32),
                   jax.ShapeDtypeStruct((DEPTH, SSM_GROUP, g, p), F32),
                   jax.ShapeDtypeStruct((DEPTH, SSM_GROUP, g, p), F32)),
        grid=(DEPTH,),
        in_specs=[mat, mat, pl.BlockSpec((None, g, 1), lambda l: (l, 0, 0)), stack_c, stack_c],
        out_specs=(stack_n, stack_n, stack_c, stack_c),
        compiler_params=_cparams(("arbitrary",)),
        name="s5_discretize",
    )(a_re, a_im, log_dt.reshape(DEPTH, g, 1), bt_re, bt_im)


def _norm_mm_kernel(*refs, mode, nsub):
    x_ref, g_ref, sc_ref, sh_ref, w_ref = refs[:5]
    rest = refs[5:]
    if mode == "plain":
        o_ref, h_scr = rest
    elif mode == "sigmoid":
        b_ref, o_ref, h_scr = rest
    else:
        w2_ref, o_ref, h_scr = rest

    @pl.when(pl.program_id(1) == 0)
    def _():
        gain = g_ref[...]
        for s in range(nsub):
            rows = slice(s * MOD_BLOCK, (s + 1) * MOD_BLOCK)
            x = x_ref[rows, :]
            ms = jnp.mean(x * x, axis=-1, keepdims=True)
            y = x * lax.rsqrt(ms + EPS) * gain
            h = y * (1.0 + sc_ref[s:s + 1, :]) + sh_ref[s:s + 1, :]
            h_scr[rows, :] = h.astype(BF16)

    h = h_scr[...]
    acc = jnp.dot(h, w_ref[...], preferred_element_type=F32)
    if mode == "plain":
        o_ref[...] = acc
    elif mode == "sigmoid":
        o_ref[...] = _sigmoid(acc + b_ref[...])
    else:
        up = jnp.dot(h, w2_ref[...], preferred_element_type=F32)
        o_ref[...] = (acc * _sigmoid(acc) * up).astype(o_ref.dtype)


def _norm_mm_call(x, l, gain, modx, sc_idx, sh_idx, w, *, mode, bias=None, w2=None, tm=1024, tn=512,
                  out_dtype=F32, name):
    n_tok = x.shape[0]
    n_out = w.shape[2]
    nsub = tm // MOD_BLOCK
    w_spec = pl.BlockSpec((None, D_MODEL, tn), lambda i, j: (l, 0, j))
    in_specs = [
        pl.BlockSpec((tm, D_MODEL), lambda i, j: (i, 0)),
        _layer_spec(l, (1, D_MODEL)),
        pl.BlockSpec((None, nsub, D_MODEL), lambda i, j: (l, i, sc_idx)),
        pl.BlockSpec((None, nsub, D_MODEL), lambda i, j: (l, i, sh_idx)),
        w_spec,
    ]
    args = [x, gain.reshape(DEPTH, 1, D_MODEL), modx, modx, w]
    if mode == "sigmoid":
        in_specs.append(pl.BlockSpec((None, 1, tn), lambda i, j: (l, 0, j)))
        args.append(bias.reshape(DEPTH, 1, n_out))
    elif mode == "swiglu":
        in_specs.append(w_spec)
        args.append(w2)
    return pl.pallas_call(
        functools.partial(_norm_mm_kernel, mode=mode, nsub=nsub),
        out_shape=jax.ShapeDtypeStruct((n_tok, n_out), out_dtype),
        grid=(n_tok // tm, n_out // tn),
        in_specs=in_specs,
        out_specs=pl.BlockSpec((tm, tn), lambda i, j: (i, j)),
        scratch_shapes=[pltpu.VMEM((tm, D_MODEL), BF16)],
        compiler_params=_cparams(("arbitrary", "arbitrary")),
        name=name,
    )(*args)


def _mix_kernel(sp_ref, ss_ref, a_ref, ga_ref, gb_ref, ws_ref, wa_ref, o_ref, *, n_split):
    def body(s_ref):
        ps = jnp.dot(s_ref[...], ws_ref[...], preferred_element_type=F32)
        pa = jnp.dot(a_ref[...], wa_ref[...], preferred_element_type=F32)
        o_ref[...] = (ga_ref[...] * ps + gb_ref[...] * pa).astype(o_ref.dtype)

    i = pl.program_id(0)
    pl.when(i < n_split)(lambda: body(sp_ref))
    pl.when(i >= n_split)(lambda: body(ss_ref))


def _mix_call(ssm_p, ssm_s, attn_out, gates, l, w_ps, w_pa, *, tm=1024, tn=512):
    n_tok = attn_out.shape[0]
    nj = D_MODEL // tn
    n_split = ssm_p.shape[0] // tm
    return pl.pallas_call(
        functools.partial(_mix_kernel, n_split=n_split),
        out_shape=jax.ShapeDtypeStruct((n_tok, D_MODEL), BF16),
        grid=(n_tok // tm, nj),
        in_specs=[
            pl.BlockSpec((tm, D_SSM), lambda i, j: (jnp.minimum(i, n_split - 1), 0)),
            pl.BlockSpec((tm, D_SSM), lambda i, j: (jnp.maximum(i - n_split, 0), 0)),
            pl.BlockSpec((tm, D_ATTN), lambda i, j: (i, 0)),
            pl.BlockSpec((tm, tn), lambda i, j: (i, j)),
            pl.BlockSpec((tm, tn), lambda i, j: (i, j + nj)),
            pl.BlockSpec((None, D_SSM, tn), lambda i, j: (l, 0, j)),
            pl.BlockSpec((None, D_ATTN, tn), lambda i, j: (l, 0, j)),
        ],
        out_specs=pl.BlockSpec((tm, tn), lambda i, j: (i, j)),
        compiler_params=_cparams(("arbitrary", "arbitrary")),
        name="branch_merge",
    )(ssm_p, ssm_s, attn_out, gates, gates, w_ps, w_pa)


def _resid_mm_kernel(a_ref, w_ref, x_ref, g_ref, *o_refs, nsub, n_split):
    acc = jnp.dot(a_ref[...], w_ref[...], preferred_element_type=F32)

    def write(o_ref):
        for s in range(nsub):
            rows = slice(s * MOD_BLOCK, (s + 1) * MOD_BLOCK)
            o_ref[rows, :] = x_ref[rows, :] + g_ref[s:s + 1, :] * acc[rows, :]

    if n_split is None:
        write(o_refs[0])
    else:
        i = pl.program_id(0)
        pl.when(i < n_split)(lambda: write(o_refs[0]))
        pl.when(i >= n_split)(lambda: write(o_refs[1]))


def _resid_mm_call(a, l, w, x, modx, g_idx, *, n_split=None, tm=1024, tn=512, name):
    n_tok, k = a.shape
    nsub = tm // MOD_BLOCK
    nj = D_MODEL // tn
    n_tiles = n_tok // tm
    if n_split is None:
        out_shape = jax.ShapeDtypeStruct((n_tok, D_MODEL), F32)
        out_specs = pl.BlockSpec((tm, tn), lambda i, j: (i, j))
    else:
        out_shape = (jax.ShapeDtypeStruct((n_split * tm, D_MODEL), F32),
                     jax.ShapeDtypeStruct(((n_tiles - n_split) * tm, D_MODEL), F32))
        out_specs = (
            pl.BlockSpec((tm, tn), lambda i, j: (jnp.minimum(i, n_split - 1), jnp.where(i < n_split, j, nj - 1))),
            pl.BlockSpec((tm, tn), lambda i, j: (jnp.maximum(i - n_split, 0), jnp.where(i < n_split, 0, j))),
        )
    return pl.pallas_call(
        functools.partial(_resid_mm_kernel, nsub=nsub, n_split=n_split),
        out_shape=out_shape,
        grid=(n_tiles, nj),
        in_specs=[
            pl.BlockSpec((tm, k), lambda i, j: (i, 0)),
            pl.BlockSpec((None, k, tn), lambda i, j: (l, 0, j)),
            pl.BlockSpec((tm, tn), lambda i, j: (i, j)),
            pl.BlockSpec((None, nsub, tn), lambda i, j: (l, i, g_idx * nj + j)),
        ],
        out_specs=out_specs,
        compiler_params=_cparams(("arbitrary", "arbitrary")),
        name=name,
    )(a, w, x, modx)


SCAN_LW = 512


def _ssm_kernel(u_ref, h0r_ref, h0i_ref, wbr_ref, wbi_ref, tab_ref, wcr_ref, wci_ref, d_ref, wglu_ref,
                bglu_ref, o_ref, htr_ref, hti_ref, bur, bui, cre, cim, z_scr, *, tt):
    @pl.when(pl.program_id(1) == 0)
    def _():
        cre[...] = jnp.broadcast_to(h0r_ref[...], (SUBLANES, STATE_W))
        cim[...] = jnp.broadcast_to(h0i_ref[...], (SUBLANES, STATE_W))

    u = u_ref[...]
    ub = u.astype(BF16)
    kw = D_SSM // SSM_KC
    sw = STATE_W // SSM_KC
    for kc in range(SSM_KC):
        uk = ub[:, kc * kw:(kc + 1) * kw]
        bur[:, kc * sw:(kc + 1) * sw] = jnp.dot(uk, wbr_ref[kc], preferred_element_type=F32)
        bui[:, kc * sw:(kc + 1) * sw] = jnp.dot(uk, wbi_ref[kc], preferred_element_type=F32)

    for lc in range(STATE_W // SCAN_LW):
        sl = slice(lc * SCAN_LW, (lc + 1) * SCAN_LW)

        def body(i, carry, sl=sl):
            c_re, c_im = carry
            r0 = pl.multiple_of(i * SUBLANES, SUBLANES)
            x_re = bur[pl.ds(r0, SUBLANES), sl]
            x_im = bui[pl.ds(r0, SUBLANES), sl]
            for idx, k in enumerate((1, 2, 4)):
                m_re = tab_ref[2 * idx, :, sl]
                m_im = tab_ref[2 * idx + 1, :, sl]
                s_re = pltpu.roll(x_re, k, 0)
                s_im = pltpu.roll(x_im, k, 0)
                x_re, x_im = (x_re + m_re * s_re - m_im * s_im,
                              x_im + m_re * s_im + m_im * s_re)
            p_re = tab_ref[6, :, sl]
            p_im = tab_ref[7, :, sl]
            h_re = jnp.broadcast_to(c_re[SUBLANES - 1:SUBLANES, :], (SUBLANES, SCAN_LW))
            h_im = jnp.broadcast_to(c_im[SUBLANES - 1:SUBLANES, :], (SUBLANES, SCAN_LW))
            x_re, x_im = (x_re + p_re * h_re - p_im * h_im,
                          x_im + p_re * h_im + p_im * h_re)
            bur[pl.ds(r0, SUBLANES), sl] = x_re
            bui[pl.ds(r0, SUBLANES), sl] = x_im
            return x_re, x_im

        c_re, c_im = lax.fori_loop(0, tt // SUBLANES, body, (cre[:, sl], cim[:, sl]), unroll=2)
        cre[:, sl] = c_re
        cim[:, sl] = c_im

    htr_ref[...] = cre[SUBLANES - 1:SUBLANES, :]
    hti_ref[...] = cim[SUBLANES - 1:SUBLANES, :]

    d = d_ref[...]
    for kc in range(SSM_KC):
        h_re = bur[:, kc * sw:(kc + 1) * sw].astype(BF16)
        h_im = bui[:, kc * sw:(kc + 1) * sw].astype(BF16)
        y = (jnp.dot(h_re, wcr_ref[kc], preferred_element_type=F32)
             - jnp.dot(h_im, wci_ref[kc], preferred_element_type=F32))
        cols = slice(kc * kw, (kc + 1) * kw)
        y = y + d[:, cols] * u[:, cols]
        z_scr[:, cols] = 0.5 * y * (1.0 + jnp.tanh(math.sqrt(2.0 / math.pi) * (y + 0.044715 * (y * y * y))))
    z = z_scr[...]
    gate = _sigmoid(jnp.dot(z.astype(BF16), wglu_ref[...], preferred_element_type=F32) + bglu_ref[...])
    o_ref[...] = (z * gate).astype(o_ref.dtype)


def _ssm_call(proj, row0, n_seq, t_len, tt, h0_re, h0_im, l, wb_re, wb_im, tab, wc_re, wc_im, d_skip, w_glu,
              b_glu, *, name):
    nt = t_len // tt
    rb0 = row0 // tt
    state_spec = pl.BlockSpec((None, 1, STATE_W), lambda b, t: (b, 0, 0))
    return pl.pallas_call(
        functools.partial(_ssm_kernel, tt=tt),
        out_shape=(jax.ShapeDtypeStruct((n_seq * t_len, D_SSM), BF16),
                   jax.ShapeDtypeStruct((n_seq, 1, STATE_W), F32),
                   jax.ShapeDtypeStruct((n_seq, 1, STATE_W), F32)),
        grid=(n_seq, nt),
        in_specs=[
            pl.BlockSpec((tt, D_SSM), lambda b, t: (rb0 + b * nt + t, 0)),
            state_spec, state_spec,
            _layer_spec(l, wb_re.shape[1:]), _layer_spec(l, wb_im.shape[1:]),
            _layer_spec(l, tab.shape[1:]),
            _layer_spec(l, wc_re.shape[1:]), _layer_spec(l, wc_im.shape[1:]),
            _layer_spec(l, (1, D_SSM)),
            _layer_spec(l, (D_SSM, D_SSM)),
            _layer_spec(l, (1, D_SSM)),
        ],
        out_specs=(pl.BlockSpec((tt, D_SSM), lambda b, t: (b * nt + t, 0)), state_spec, state_spec),
        scratch_shapes=[pltpu.VMEM((tt, STATE_W), F32), pltpu.VMEM((tt, STATE_W), F32),
                        pltpu.VMEM((SUBLANES, STATE_W), F32), pltpu.VMEM((SUBLANES, STATE_W), F32),
                        pltpu.VMEM((tt, D_SSM), F32)],
        compiler_params=_cparams(("arbitrary", "arbitrary")),
        name=name,
    )(proj, h0_re, h0_im, wb_re, wb_im, tab, wc_re, wc_im, d_skip.reshape(DEPTH, 1, D_SSM), w_glu,
      b_glu.reshape(DEPTH, 1, D_SSM))


RING = 3
N_KEYS = RING * CHUNK
ATT_CPS = 4


def _stack_blocks(x):
    return jnp.concatenate([x[:, j * LANES:(j + 1) * LANES] for j in range(x.shape[1] // LANES)], axis=0)


def _pair_norm_rope(xs, gain, tab, ones_bd):
    reps = xs.shape[0] // CHUNK
    cos, s_lo, s_hi = (jnp.concatenate([tab[i]] * reps, axis=0) for i in range(3))
    half = ROPE_DIM // 2
    sq = xs * xs
    sq_hi = sq.astype(BF16)
    sq_lo = (sq - sq_hi.astype(F32)).astype(BF16)
    ss = (jnp.dot(sq_hi, ones_bd, preferred_element_type=F32)
          + jnp.dot(sq_lo, ones_bd, preferred_element_type=F32))
    xg = xs * gain
    xr = xg * cos + pltpu.roll(xg, LANES - half, 1) * s_lo + pltpu.roll(xg, half, 1) * s_hi
    return xr * lax.rsqrt(ss * (1.0 / HEAD_DIM) + EPS)


def _attn_kernel(sink_ref, q_ref, k_ref, v_ref, ck_ref, cv_ref, tab_ref, gq_ref, gk_ref, o_ref, kn_ref, kd, vd,
                 *, l, n_prompt_steps, steps_per_seq):
    step = pl.program_id(0)
    is_sample = step >= n_prompt_steps
    c_base = lax.rem(step, steps_per_seq) * ATT_CPS
    first = lax.broadcasted_iota(jnp.int32, (1, LANES), 1) < HEAD_DIM
    ones_bd = jnp.where(lax.broadcasted_iota(jnp.int32, (LANES, LANES), 0) // HEAD_DIM
                        == lax.broadcasted_iota(jnp.int32, (LANES, LANES), 1) // HEAD_DIM, 1.0, 0.0).astype(BF16)
    key_slot = lax.broadcasted_iota(jnp.int32, (1, N_KEYS), 1) // CHUNK
    gq = gq_ref[...]
    gk = gk_ref[...]

    def store_dup(dst, x, rows):
        for pair in range(D_KV // LANES):
            blk = x[:, pair * LANES:(pair + 1) * LANES]
            swapped = pltpu.roll(blk, HEAD_DIM, 1)
            dst[2 * pair, rows, 0:LANES] = jnp.where(first, blk, swapped).astype(BF16)
            dst[2 * pair + 1, rows, 0:LANES] = jnp.where(first, swapped, blk).astype(BF16)

    @pl.when(step == 0)
    def _():
        vd[:, :, LANES:] = jnp.ones((N_KV_HEADS, N_KEYS, LANES), BF16)

    def chunk(ci, carry):
        c = jnp.where(is_sample, 0, c_base + ci)
        rows = pl.ds(pl.multiple_of(ci * CHUNK, CHUNK), CHUNK)
        prev = slice(CHUNK, N_KEYS)

        @pl.when(jnp.logical_and(c == 0, jnp.logical_not(is_sample)))
        def _():
            kd[:, prev, :] = jnp.zeros((N_KV_HEADS, N_KEYS - CHUNK, LANES), BF16)
            vd[:, prev, 0:LANES] = jnp.zeros((N_KV_HEADS, N_KEYS - CHUNK, LANES), BF16)

        @pl.when(is_sample)
        def _():
            cached = pl.ds(pl.multiple_of(ci * WINDOW, WINDOW), WINDOW)
            store_dup(kd, ck_ref[cached, :], prev)
            store_dup(vd, cv_ref[cached, :], prev)

        tab = tab_ref[:, rows, :]
        own = pl.ds(pl.multiple_of(lax.rem(c, RING) * CHUNK, CHUNK), CHUNK)
        kn = _pair_norm_rope(_stack_blocks(k_ref[rows, :]), gk, tab, ones_bd)
        kn = jnp.concatenate([kn[0:CHUNK], kn[CHUNK:2 * CHUNK]], axis=1)
        kn_ref[rows, :] = kn
        store_dup(kd, kn, own)
        store_dup(vd, v_ref[rows, :], own)

        qn = _pair_norm_rope(_stack_blocks(q_ref[rows, :]), gq, tab, ones_bd) * (HEAD_DIM ** -0.5)
        q_lo = jnp.where(first, qn, 0.0).astype(BF16)
        q_hi = jnp.where(first, 0.0, qn).astype(BF16)

        valid = key_slot <= jnp.where(is_sample, RING, c)
        for kh in range(N_KV_HEADS):
            r0 = kh * 2 * CHUNK
            lhs = jnp.concatenate([q_lo[r0:r0 + CHUNK], q_hi[r0:r0 + CHUNK],
                                   q_lo[r0 + CHUNK:r0 + 2 * CHUNK], q_hi[r0 + CHUNK:r0 + 2 * CHUNK]], axis=0)
            s = lax.dot_general(lhs, kd[kh], (((1,), (1,)), ((), ())), preferred_element_type=F32)
            s = jnp.where(valid, s, -1e30)
            sink = jnp.concatenate([jnp.full((CHUNK, 1), sink_ref[l, kh * KV_REP + r], F32)
                                    for r in range(KV_REP)], axis=0)
            m = jnp.maximum(jnp.max(s, axis=-1, keepdims=True), sink)
            p = jnp.exp(s - m).astype(BF16)
            od = jnp.dot(p, vd[kh], preferred_element_type=F32)
            o = od[:, 0:LANES] / (od[:, LANES:] + jnp.exp(sink - m))
            for pair in range(2):
                half = pair * 2 * CHUNK
                blk = jnp.where(first, o[half:half + CHUNK], o[half + CHUNK:half + 2 * CHUNK])
                col = (2 * kh + pair) * LANES
                o_ref[rows, col:col + LANES] = blk.astype(o_ref.dtype)
        return carry

    lax.fori_loop(0, ATT_CPS, chunk, 0)


def _attn_call(proj, l, n_prompt, t_prompt, sink, rope_tab, gq, gk, cache_k, cache_v):
    n_tok = proj.shape[0]
    tr = ATT_CPS * CHUNK
    qcol = D_SSM // D_ATTN
    kcol = (D_SSM + D_ATTN) // D_KV
    n_prompt_steps = n_prompt // tr
    cache_spec = pl.BlockSpec((None, ATT_CPS * WINDOW, D_KV),
                              lambda s: (l, jnp.maximum(s - n_prompt_steps, 0), 0))
    return pl.pallas_call(
        functools.partial(_attn_kernel, l=l, n_prompt_steps=n_prompt_steps, steps_per_seq=t_prompt // tr),
        out_shape=(jax.ShapeDtypeStruct((n_tok, D_ATTN), BF16),
                   jax.ShapeDtypeStruct((n_tok, D_KV), F32)),
        grid=(n_tok // tr,),
        in_specs=[
            pl.BlockSpec(memory_space=pltpu.SMEM),
            pl.BlockSpec((tr, D_ATTN), lambda s: (s, qcol)),
            pl.BlockSpec((tr, D_KV), lambda s: (s, kcol)),
            pl.BlockSpec((tr, D_KV), lambda s: (s, kcol + 1)),
            cache_spec, cache_spec,
            pl.BlockSpec((3, tr, LANES), lambda s: (0, s, 0)),
            _layer_spec(l, (1, LANES)), _layer_spec(l, (1, LANES)),
        ],
        out_specs=(pl.BlockSpec((tr, D_ATTN), lambda s: (s, 0)),
                   pl.BlockSpec((tr, D_KV), lambda s: (s, 0))),
        scratch_shapes=[pltpu.VMEM((N_KV_HEADS, N_KEYS, LANES), BF16),
                        pltpu.VMEM((N_KV_HEADS, N_KEYS, 2 * LANES), BF16)],
        compiler_params=_cparams(("arbitrary",)),
        name="banded_attn",
    )(sink, proj, proj, proj, cache_k, cache_v, rope_tab, gq, gk)


def _block_diag(blocks):
    *lead, n, r, c = blocks.shape
    eye = jnp.eye(n, dtype=blocks.dtype)
    return (blocks[..., :, :, None, :] * eye[:, None, :, None]).reshape(*lead, n * r, n * c)


def _rope_table(pos):
    half = ROPE_DIM // 2
    n_pos = pos.shape[0]
    inv_freq = ROPE_THETA ** (-jnp.arange(half, dtype=F32) / half)
    ang = pos.astype(F32)[:, None] * inv_freq[None, :]
    cos, sin = jnp.cos(ang), jnp.sin(ang)
    ones = jnp.ones((n_pos, HEAD_DIM - ROPE_DIM), F32)
    zeros = jnp.zeros((n_pos, HEAD_DIM - half), F32)
    c_tab = jnp.concatenate([cos, cos, ones], axis=1)
    lo_tab = jnp.concatenate([-sin, zeros], axis=1)
    hi_tab = jnp.concatenate([jnp.zeros((n_pos, half), F32), sin, jnp.zeros((n_pos, HEAD_DIM - ROPE_DIM), F32)],
                             axis=1)
    tab = jnp.stack([c_tab, lo_tab, hi_tab])
    return jnp.concatenate([tab, tab], axis=2)


def kernel(x_prompt, x_sample, cache_k, cache_v, state_ssm_re, state_ssm_im, c_prompt, c_sample, w_mod, b_mod, norm1_g, norm2_g, w_in, ssm_a_re, ssm_a_im, ssm_log_dt, ssm_b_re, ssm_b_im, ssm_c_re, ssm_c_im, ssm_d, w_glu, b_glu, q_norm_g, k_norm_g, attn_sink, w_gate, b_gate, w_proj_ssm, w_proj_attn, w_out, w_ffn_gate, w_ffn_up, w_ffn_down):
    bp, tp, _ = x_prompt.shape
    bs, ts, _ = x_sample.shape
    assert ts == CHUNK and tp % (ATT_CPS * CHUNK) == 0 and bs % ATT_CPS == 0
    n_p, n_s = bp * tp, bs * ts
    tm = 1024
    x = jnp.concatenate([x_prompt.reshape(n_p, D_MODEL), x_sample.reshape(n_s, D_MODEL)], axis=0)

    n_cond = bp + bs
    pad = (-n_cond) % SUBLANES
    c_all = jnp.concatenate([c_prompt, c_sample, jnp.zeros((pad, D_MODEL), F32)], axis=0)
    mod = _mod_call(c_all, w_mod, b_mod)
    n_mod = mod.shape[2]

    def per_block(m, n_seq, t_len):
        reps = t_len // MOD_BLOCK
        return jnp.broadcast_to(m[:, :, None, :], (DEPTH, n_seq, reps, n_mod)).reshape(DEPTH, n_seq * reps, n_mod)

    modx = jnp.concatenate([per_block(mod[:, :bp], bp, tp), per_block(mod[:, bp:n_cond], bs, ts)],
                           axis=1)

    g, p = N_SSM_GROUPS, SSM_STATE
    gpc = g // SSM_KC
    pw_re, pw_im, bb_re, bb_im = _s5_prep_call(ssm_a_re, ssm_a_im, ssm_log_dt,
                                               ssm_b_re.transpose(0, 3, 1, 2), ssm_b_im.transpose(0, 3, 1, 2))
    pr = pw_re.reshape(DEPTH, SUBLANES, STATE_W)
    pi = pw_im.reshape(DEPTH, SUBLANES, STATE_W)
    tab_rows = []
    for k in (1, 2, 4):
        keep = (jnp.arange(SUBLANES) >= k)[None, :, None]
        tab_rows.append(jnp.where(keep, pr[:, k - 1:k, :], 0.0))
        tab_rows.append(jnp.where(keep, pi[:, k - 1:k, :], 0.0))
    scan_tab = jnp.stack(tab_rows + [pr, pi], axis=1)

    def wb_blocks(bb):
        return _block_diag(bb.transpose(0, 2, 1, 3).reshape(DEPTH, SSM_KC, gpc, SSM_GROUP, p)).astype(BF16)

    def wc_blocks(cc):
        return _block_diag(cc.transpose(0, 1, 3, 2).reshape(DEPTH, SSM_KC, gpc, p, SSM_GROUP)).astype(BF16)

    ssm_w = (wb_blocks(bb_re), wb_blocks(bb_im), scan_tab, wc_blocks(ssm_c_re), wc_blocks(ssm_c_im), ssm_d,
             w_glu.astype(BF16), b_glu)
    zeros_state = jnp.zeros((bp, 1, STATE_W), F32)
    h0_re = state_ssm_re.reshape(DEPTH, bs, 1, STATE_W)
    h0_im = state_ssm_im.reshape(DEPTH, bs, 1, STATE_W)

    pos = jnp.concatenate([jnp.tile(jnp.arange(tp), bp), jnp.tile(PAST_LEN + jnp.arange(ts), bs)])
    rope_tab = _rope_table(pos)
    gq = jnp.tile(q_norm_g, (1, LANES // HEAD_DIM)).reshape(DEPTH, 1, LANES)
    gk = jnp.tile(k_norm_g, (1, LANES // HEAD_DIM)).reshape(DEPTH, 1, LANES)
    cache_k2 = cache_k.reshape(DEPTH, bs * WINDOW, D_KV)
    cache_v2 = cache_v.reshape(DEPTH, bs * WINDOW, D_KV)

    w_in_b, w_gate_b = w_in.astype(BF16), w_gate.astype(BF16)
    w_ps_b, w_pa_b, w_out_b = w_proj_ssm.astype(BF16), w_proj_attn.astype(BF16), w_out.astype(BF16)
    w_fg_b, w_fu_b, w_fd_b = w_ffn_gate.astype(BF16), w_ffn_up.astype(BF16), w_ffn_down.astype(BF16)
    v0 = D_SSM + D_ATTN + D_KV

    def heads(t):
        return t.reshape(*t.shape[:-1], N_KV_HEADS, HEAD_DIM)

    outs = {k: [] for k in ("pk", "pv", "pre", "pim", "sk", "sv", "sre", "sim")}
    for l in range(DEPTH):
        proj = _norm_mm_call(x, l, norm1_g, modx, 1, 0, w_in_b, mode="plain", tm=tm, name="in_proj")
        gates = _norm_mm_call(x, l, norm1_g, modx, 1, 0, w_gate_b, mode="sigmoid", bias=b_gate, tm=tm,
                              name="merge_gates")

        ssm_p, pre, pim = _ssm_call(proj, 0, bp, tp, 256, zeros_state, zeros_state, l, *ssm_w, name="s5_prompt")
        ssm_s, sre, sim = _ssm_call(proj, n_p, bs, ts, ts, h0_re[l], h0_im[l], l, *ssm_w, name="s5_sample")
        attn_out, kn = _attn_call(proj, l, n_p, tp, attn_sink, rope_tab, gq, gk, cache_k2, cache_v2)

        mixed = _mix_call(ssm_p, ssm_s, attn_out, gates, l, w_ps_b, w_pa_b, tm=tm)
        x = _resid_mm_call(mixed, l, w_out_b, x, modx, 2, tm=tm, name="out_proj")

        act = _norm_mm_call(x, l, norm2_g, modx, 4, 3, w_fg_b, mode="swiglu", w2=w_fu_b, out_dtype=BF16,
                            tm=tm, name="ffn_up")
        last = l == DEPTH - 1
        x = _resid_mm_call(act, l, w_fd_b, x, modx, 5, n_split=n_p // tm if last else None, tm=tm,
                           name="ffn_down_split" if last else "ffn_down")

        outs["pk"].append(heads(kn[:n_p].reshape(bp, tp, D_KV)[:, tp - WINDOW:]))
        outs["pv"].append(heads(proj[:n_p].reshape(bp, tp, IN_WIDTH)[:, tp - WINDOW:, v0:]))
        outs["pre"].append(pre.reshape(bp, g, p))
        outs["pim"].append(pim.reshape(bp, g, p))
        outs["sk"].append(jnp.concatenate([cache_k[l][:, ts:], heads(kn[n_p:].reshape(bs, ts, D_KV))], axis=1))
        outs["sv"].append(jnp.concatenate([cache_v[l][:, ts:], heads(proj[n_p:, v0:].reshape(bs, ts, D_KV))],
                                          axis=1))
        outs["sre"].append(sre.reshape(bs, g, p))
        outs["sim"].append(sim.reshape(bs, g, p))

    y_p, y_s = x
    return (y_p.reshape(bp, tp, D_MODEL), y_s.reshape(bs, ts, D_MODEL),
            jnp.stack(outs["pk"]), jnp.stack(outs["pv"]), jnp.stack(outs["pre"]), jnp.stack(outs["pim"]),
            jnp.stack(outs["sk"]), jnp.stack(outs["sv"]), jnp.stack(outs["sre"]), jnp.stack(outs["sim"]))
```

```python
import functools
import math

import jax
import jax.numpy as jnp
from jax import lax
from jax.experimental import pallas as pl
from jax.experimental.pallas import tpu as pltpu

D_MODEL = 2048
DEPTH = 4
CHUNK = 64
D_SSM = 1024
SSM_GROUP = 16
N_SSM_GROUPS = 64
SSM_STATE = 64
HEAD_DIM = 64
N_HEADS = 16
N_KV_HEADS = 4
KV_REP = N_HEADS // N_KV_HEADS
D_ATTN = N_HEADS * HEAD_DIM
D_KV = N_KV_HEADS * HEAD_DIM
IN_WIDTH = D_SSM + D_ATTN + 2 * D_KV
WINDOW = 128
ROPE_DIM = 16
ROPE_THETA = 500000.0
D_FF = 5632
EPS = 1e-6
PAST_LEN = 2048

LANES = 128
SUBLANES = 8
MOD_BLOCK = CHUNK
STATE_W = N_SSM_GROUPS * SSM_STATE
SSM_KC = 4
VMEM_LIMIT = 56 * 1024 * 1024

F32 = jnp.float32
BF16 = jnp.bfloat16


def _cparams(sem):
    return pltpu.CompilerParams(dimension_semantics=sem, vmem_limit_bytes=VMEM_LIMIT)


def _sigmoid(x):
    return 1.0 / (1.0 + jnp.exp(-x))


def _layer_spec(l, shape):
    zeros = (0,) * len(shape)
    return pl.BlockSpec((None, *shape), lambda *_: (l, *zeros))


def _mod_kernel(c_ref, w_ref, b_ref, o_ref):
    c = c_ref[...].astype(BF16)
    w = w_ref[...].astype(BF16)
    o_ref[...] = jnp.dot(c, w, preferred_element_type=F32) + b_ref[...]


def _mod_call(c_all, w_mod, b_mod):
    nb = c_all.shape[0]
    tn = 1024
    n_out = w_mod.shape[2]
    return pl.pallas_call(
        _mod_kernel,
        out_shape=jax.ShapeDtypeStruct((DEPTH, nb, n_out), F32),
        grid=(DEPTH, n_out // tn),
        in_specs=[
            pl.BlockSpec((nb, D_MODEL), lambda l, j: (0, 0)),
            pl.BlockSpec((None, D_MODEL, tn), lambda l, j: (l, 0, j)),
            pl.BlockSpec((None, 1, tn), lambda l, j: (l, 0, j)),
        ],
        out_specs=pl.BlockSpec((None, nb, tn), lambda l, j: (l, 0, j)),
        compiler_params=_cparams(("arbitrary", "arbitrary")),
        name="adaln_mod",
    )(c_all, w_mod, b_mod.reshape(DEPTH, 1, n_out))


def _s5_prep_kernel(are_ref, aim_ref, ldt_ref, bre_ref, bim_ref, pwr_ref, pwi_ref, bbr_ref, bbi_ref):
    a_re = are_ref[...]
    a_im = aim_ref[...]
    dt = jnp.exp(ldt_ref[...])
    z_re = a_re * dt
    z_im = a_im * dt
    for n in range(1, SUBLANES + 1):
        mag = jnp.exp(z_re * float(n))
        pwr_ref[n - 1] = mag * jnp.cos(z_im * float(n))
        pwi_ref[n - 1] = mag * jnp.sin(z_im * float(n))
    l_re = pwr_ref[0]
    l_im = pwi_ref[0]
    den = a_re * a_re + a_im * a_im
    n_re = l_re - 1.0
    f_re = (n_re * a_re + l_im * a_im) / den
    f_im = (l_im * a_re - n_re * a_im) / den
    for c in range(SSM_GROUP):
        b_re = bre_ref[c]
        b_im = bim_ref[c]
        bbr_ref[c] = f_re * b_re - f_im * b_im
        bbi_ref[c] = f_re * b_im + f_im * b_re


def _s5_prep_call(a_re, a_im, log_dt, bt_re, bt_im):
    g, p = N_SSM_GROUPS, SSM_STATE
    mat = pl.BlockSpec((None, g, p), lambda l: (l, 0, 0))
    stack_c = pl.BlockSpec((None, SSM_GROUP, g, p), lambda l: (l, 0, 0, 0))
    stack_n = pl.BlockSpec((None, SUBLANES, g, p), lambda l: (l, 0, 0, 0))
    return pl.pallas_call(
        _s5_prep_kernel,
        out_shape=(jax.ShapeDtypeStruct((DEPTH, SUBLANES, g, p), F32),
                   jax.ShapeDtypeStruct((DEPTH, SUBLANES, g, p), F32),
                   jax.ShapeDtypeStruct((DEPTH, SSM_GROUP, g, p), F32),
                   jax.ShapeDtypeStruct((DEPTH, SSM_GROUP, g, p), F32)),
        grid=(DEPTH,),
        in_specs=[mat, mat, pl.BlockSpec((None, g, 1), lambda l: (l, 0, 0)), stack_c, stack_c],
        out_specs=(stack_n, stack_n, stack_c, stack_c),
        compiler_params=_cparams(("arbitrary",)),
        name="s5_discretize",
    )(a_re, a_im, log_dt.reshape(DEPTH, g, 1), bt_re, bt_im)


def _norm_mod_tile(x_ref, g_ref, sc_ref, sh_ref, h_scr, nsub):
    gain = g_ref[...]
    for s in range(nsub):
        rows = slice(s * MOD_BLOCK, (s + 1) * MOD_BLOCK)
        x = x_ref[rows, :]
        ms = jnp.mean(x * x, axis=-1, keepdims=True)
        y = x * lax.rsqrt(ms + EPS) * gain
        h = y * (1.0 + sc_ref[s:s + 1, :]) + sh_ref[s:s + 1, :]
        h_scr[rows, :] = h.astype(BF16)


def _norm_specs(l, tm, sc_idx, sh_idx):
    nsub = tm // MOD_BLOCK
    return [
        pl.BlockSpec((tm, D_MODEL), lambda i, j: (i, 0)),
        _layer_spec(l, (1, D_MODEL)),
        pl.BlockSpec((None, nsub, D_MODEL), lambda i, j: (l, i, sc_idx)),
        pl.BlockSpec((None, nsub, D_MODEL), lambda i, j: (l, i, sh_idx)),
    ]


def _in_gate_kernel(x_ref, g_ref, sc_ref, sh_ref, wi_ref, wg_ref, b_ref, proj_ref, gate_ref, h_scr, *, nsub, n_in):
    j = pl.program_id(1)
    pl.when(j == 0)(lambda: _norm_mod_tile(x_ref, g_ref, sc_ref, sh_ref, h_scr, nsub))

    @pl.when(j < n_in)
    def _():
        proj_ref[...] = jnp.dot(h_scr[...], wi_ref[...], preferred_element_type=F32)

    @pl.when(j >= n_in)
    def _():
        acc = jnp.dot(h_scr[...], wg_ref[...], preferred_element_type=F32)
        gate_ref[...] = _sigmoid(acc + b_ref[...])


def _in_gate_call(x, l, gain, modx, w_in, w_gate, b_gate, *, tm=1024, tn=512):
    n_tok = x.shape[0]
    n_in, n_gate = w_in.shape[2] // tn, w_gate.shape[2] // tn
    in_col = lambda i, j: jnp.minimum(j, n_in - 1)
    gate_col = lambda i, j: jnp.maximum(j - n_in, 0)
    return pl.pallas_call(
        functools.partial(_in_gate_kernel, nsub=tm // MOD_BLOCK, n_in=n_in),
        out_shape=(jax.ShapeDtypeStruct((n_tok, w_in.shape[2]), F32),
                   jax.ShapeDtypeStruct((n_tok, w_gate.shape[2]), F32)),
        grid=(n_tok // tm, n_in + n_gate),
        in_specs=_norm_specs(l, tm, 1, 0) + [
            pl.BlockSpec((None, D_MODEL, tn), lambda i, j: (l, 0, in_col(i, j))),
            pl.BlockSpec((None, D_MODEL, tn), lambda i, j: (l, 0, gate_col(i, j))),
            pl.BlockSpec((None, 1, tn), lambda i, j: (l, 0, gate_col(i, j))),
        ],
        out_specs=(pl.BlockSpec((tm, tn), lambda i, j: (i, in_col(i, j))),
                   pl.BlockSpec((tm, tn), lambda i, j: (i, gate_col(i, j)))),
        scratch_shapes=[pltpu.VMEM((tm, D_MODEL), BF16)],
        compiler_params=_cparams(("arbitrary", "arbitrary")),
        name="in_proj_gates",
    )(x, gain.reshape(DEPTH, 1, D_MODEL), modx, modx, w_in, w_gate, b_gate.reshape(DEPTH, 1, w_gate.shape[2]))


def _ffn_up_kernel(x_ref, g_ref, sc_ref, sh_ref, wg_ref, wu_ref, o_ref, h_scr, *, nsub):
    pl.when(pl.program_id(1) == 0)(lambda: _norm_mod_tile(x_ref, g_ref, sc_ref, sh_ref, h_scr, nsub))
    h = h_scr[...]
    gate = jnp.dot(h, wg_ref[...], preferred_element_type=F32)
    up = jnp.dot(h, wu_ref[...], preferred_element_type=F32)
    o_ref[...] = (gate * _sigmoid(gate) * up).astype(o_ref.dtype)


def _ffn_up_call(x, l, gain, modx, w_gate, w_up, *, tm=1024, tn=512):
    n_tok = x.shape[0]
    n_out = w_gate.shape[2]
    w_spec = pl.BlockSpec((None, D_MODEL, tn), lambda i, j: (l, 0, j))
    return pl.pallas_call(
        functools.partial(_ffn_up_kernel, nsub=tm // MOD_BLOCK),
        out_shape=jax.ShapeDtypeStruct((n_tok, n_out), BF16),
        grid=(n_tok // tm, n_out // tn),
        in_specs=_norm_specs(l, tm, 4, 3) + [w_spec, w_spec],
        out_specs=pl.BlockSpec((tm, tn), lambda i, j: (i, j)),
        scratch_shapes=[pltpu.VMEM((tm, D_MODEL), BF16)],
        compiler_params=_cparams(("arbitrary", "arbitrary")),
        name="ffn_up",
    )(x, gain.reshape(DEPTH, 1, D_MODEL), modx, modx, w_gate, w_up)


def _mix_kernel(sp_ref, ss_ref, a_ref, ga_ref, gb_ref, ws_ref, wa_ref, o_ref, *, n_split):
    def body(s_ref):
        ps = jnp.dot(s_ref[...], ws_ref[...], preferred_element_type=F32)
        pa = jnp.dot(a_ref[...], wa_ref[...], preferred_element_type=F32)
        o_ref[...] = (ga_ref[...] * ps + gb_ref[...] * pa).astype(o_ref.dtype)

    i = pl.program_id(0)
    pl.when(i < n_split)(lambda: body(sp_ref))
    pl.when(i >= n_split)(lambda: body(ss_ref))


def _mix_call(ssm_p, ssm_s, attn_out, gates, l, w_ps, w_pa, *, tm=1024, tn=512):
    n_tok = attn_out.shape[0]
    nj = D_MODEL // tn
    n_split = ssm_p.shape[0] // tm
    return pl.pallas_call(
        functools.partial(_mix_kernel, n_split=n_split),
        out_shape=jax.ShapeDtypeStruct((n_tok, D_MODEL), BF16),
        grid=(n_tok // tm, nj),
        in_specs=[
            pl.BlockSpec((tm, D_SSM), lambda i, j: (jnp.minimum(i, n_split - 1), 0)),
            pl.BlockSpec((tm, D_SSM), lambda i, j: (jnp.maximum(i - n_split, 0), 0)),
            pl.BlockSpec((tm, D_ATTN), lambda i, j: (i, 0)),
            pl.BlockSpec((tm, tn), lambda i, j: (i, j)),
            pl.BlockSpec((tm, tn), lambda i, j: (i, j + nj)),
            pl.BlockSpec((None, D_SSM, tn), lambda i, j: (l, 0, j)),
            pl.BlockSpec((None, D_ATTN, tn), lambda i, j: (l, 0, j)),
        ],
        out_specs=pl.BlockSpec((tm, tn), lambda i, j: (i, j)),
        compiler_params=_cparams(("arbitrary", "arbitrary")),
        name="branch_merge",
    )(ssm_p, ssm_s, attn_out, gates, gates, w_ps, w_pa)


def _resid_mm_kernel(a_ref, w_ref, x_ref, g_ref, *o_refs, nsub, n_split):
    acc = jnp.dot(a_ref[...], w_ref[...], preferred_element_type=F32)

    def write(o_ref):
        for s in range(nsub):
            rows = slice(s * MOD_BLOCK, (s + 1) * MOD_BLOCK)
            o_ref[rows, :] = x_ref[rows, :] + g_ref[s:s + 1, :] * acc[rows, :]

    if n_split is None:
        write(o_refs[0])
    else:
        i = pl.program_id(0)
        pl.when(i < n_split)(lambda: write(o_refs[0]))
        pl.when(i >= n_split)(lambda: write(o_refs[1]))


def _resid_mm_call(a, l, w, x, modx, g_idx, *, n_split=None, tm=1024, tn=512, name):
    n_tok, k = a.shape
    nsub = tm // MOD_BLOCK
    nj = D_MODEL // tn
    n_tiles = n_tok // tm
    if n_split is None:
        out_shape = jax.ShapeDtypeStruct((n_tok, D_MODEL), F32)
        out_specs = pl.BlockSpec((tm, tn), lambda i, j: (i, j))
    else:
        out_shape = (jax.ShapeDtypeStruct((n_split * tm, D_MODEL), F32),
                     jax.ShapeDtypeStruct(((n_tiles - n_split) * tm, D_MODEL), F32))
        out_specs = (
            pl.BlockSpec((tm, tn), lambda i, j: (jnp.minimum(i, n_split - 1), jnp.where(i < n_split, j, nj - 1))),
            pl.BlockSpec((tm, tn), lambda i, j: (jnp.maximum(i - n_split, 0), jnp.where(i < n_split, 0, j))),
        )
    return pl.pallas_call(
        functools.partial(_resid_mm_kernel, nsub=nsub, n_split=n_split),
        out_shape=out_shape,
        grid=(n_tiles, nj),
        in_specs=[
            pl.BlockSpec((tm, k), lambda i, j: (i, 0)),
            pl.BlockSpec((None, k, tn), lambda i, j: (l, 0, j)),
            pl.BlockSpec((tm, tn), lambda i, j: (i, j)),
            pl.BlockSpec((None, nsub, tn), lambda i, j: (l, i, g_idx * nj + j)),
        ],
        out_specs=out_specs,
        compiler_params=_cparams(("arbitrary", "arbitrary")),
        name=name,
    )(a, w, x, modx)


SCAN_LW = 512


def _ssm_kernel(u_ref, h0r_ref, h0i_ref, wbr_ref, wbi_ref, tab_ref, wcr_ref, wci_ref, d_ref, wglu_ref,
                bglu_ref, o_ref, htr_ref, hti_ref, bur, bui, cre, cim, z_scr, *, tt):
    @pl.when(pl.program_id(1) == 0)
    def _():
        cre[...] = jnp.broadcast_to(h0r_ref[...], (SUBLANES, STATE_W))
        cim[...] = jnp.broadcast_to(h0i_ref[...], (SUBLANES, STATE_W))

    u = u_ref[...]
    ub = u.astype(BF16)
    kw = D_SSM // SSM_KC
    sw = STATE_W // SSM_KC
    for kc in range(SSM_KC):
        uk = ub[:, kc * kw:(kc + 1) * kw]
        bur[:, kc * sw:(kc + 1) * sw] = jnp.dot(uk, wbr_ref[kc], preferred_element_type=F32)
        bui[:, kc * sw:(kc + 1) * sw] = jnp.dot(uk, wbi_ref[kc], preferred_element_type=F32)

    for lc in range(STATE_W // SCAN_LW):
        sl = slice(lc * SCAN_LW, (lc + 1) * SCAN_LW)

        def body(i, carry, sl=sl):
            c_re, c_im = carry
            r0 = pl.multiple_of(i * SUBLANES, SUBLANES)
            x_re = bur[pl.ds(r0, SUBLANES), sl]
            x_im = bui[pl.ds(r0, SUBLANES), sl]
            for idx, k in enumerate((1, 2, 4)):
                m_re = tab_ref[2 * idx, :, sl]
                m_im = tab_ref[2 * idx + 1, :, sl]
                s_re = pltpu.roll(x_re, k, 0)
                s_im = pltpu.roll(x_im, k, 0)
                x_re, x_im = (x_re + m_re * s_re - m_im * s_im,
                              x_im + m_re * s_im + m_im * s_re)
            p_re = tab_ref[6, :, sl]
            p_im = tab_ref[7, :, sl]
            h_re = jnp.broadcast_to(c_re[SUBLANES - 1:SUBLANES, :], (SUBLANES, SCAN_LW))
            h_im = jnp.broadcast_to(c_im[SUBLANES - 1:SUBLANES, :], (SUBLANES, SCAN_LW))
            x_re, x_im = (x_re + p_re * h_re - p_im * h_im,
                          x_im + p_re * h_im + p_im * h_re)
            bur[pl.ds(r0, SUBLANES), sl] = x_re
            bui[pl.ds(r0, SUBLANES), sl] = x_im
            return x_re, x_im

        c_re, c_im = lax.fori_loop(0, tt // SUBLANES, body, (cre[:, sl], cim[:, sl]), unroll=2)
        cre[:, sl] = c_re
        cim[:, sl] = c_im

    htr_ref[...] = cre[SUBLANES - 1:SUBLANES, :]
    hti_ref[...] = cim[SUBLANES - 1:SUBLANES, :]

    d = d_ref[...]
    for kc in range(SSM_KC):
        h_re = bur[:, kc * sw:(kc + 1) * sw].astype(BF16)
        h_im = bui[:, kc * sw:(kc + 1) * sw].astype(BF16)
        y = (jnp.dot(h_re, wcr_ref[kc], preferred_element_type=F32)
             - jnp.dot(h_im, wci_ref[kc], preferred_element_type=F32))
        cols = slice(kc * kw, (kc + 1) * kw)
        y = y + d[:, cols] * u[:, cols]
        z_scr[:, cols] = 0.5 * y * (1.0 + jnp.tanh(math.sqrt(2.0 / math.pi) * (y + 0.044715 * (y * y * y))))
    z = z_scr[...]
    gate = _sigmoid(jnp.dot(z.astype(BF16), wglu_ref[...], preferred_element_type=F32) + bglu_ref[...])
    o_ref[...] = (z * gate).astype(o_ref.dtype)


def _ssm_call(proj, row0, n_seq, t_len, tt, h0_re, h0_im, l, wb_re, wb_im, tab, wc_re, wc_im, d_skip, w_glu,
              b_glu, *, name):
    nt = t_len // tt
    rb0 = row0 // tt
    state_spec = pl.BlockSpec((None, 1, STATE_W), lambda b, t: (b, 0, 0))
    return pl.pallas_call(
        functools.partial(_ssm_kernel, tt=tt),
        out_shape=(jax.ShapeDtypeStruct((n_seq * t_len, D_SSM), BF16),
                   jax.ShapeDtypeStruct((n_seq, 1, STATE_W), F32),
                   jax.ShapeDtypeStruct((n_seq, 1, STATE_W), F32)),
        grid=(n_seq, nt),
        in_specs=[
            pl.BlockSpec((tt, D_SSM), lambda b, t: (rb0 + b * nt + t, 0)),
            state_spec, state_spec,
            _layer_spec(l, wb_re.shape[1:]), _layer_spec(l, wb_im.shape[1:]),
            _layer_spec(l, tab.shape[1:]),
            _layer_spec(l, wc_re.shape[1:]), _layer_spec(l, wc_im.shape[1:]),
            _layer_spec(l, (1, D_SSM)),
            _layer_spec(l, (D_SSM, D_SSM)),
            _layer_spec(l, (1, D_SSM)),
        ],
        out_specs=(pl.BlockSpec((tt, D_SSM), lambda b, t: (b * nt + t, 0)), state_spec, state_spec),
        scratch_shapes=[pltpu.VMEM((tt, STATE_W), F32), pltpu.VMEM((tt, STATE_W), F32),
                        pltpu.VMEM((SUBLANES, STATE_W), F32), pltpu.VMEM((SUBLANES, STATE_W), F32),
                        pltpu.VMEM((tt, D_SSM), F32)],
        compiler_params=_cparams(("arbitrary", "arbitrary")),
        name=name,
    )(proj, h0_re, h0_im, wb_re, wb_im, tab, wc_re, wc_im, d_skip.reshape(DEPTH, 1, D_SSM), w_glu,
      b_glu.reshape(DEPTH, 1, D_SSM))


RING = 3
N_KEYS = RING * CHUNK
ATT_CPS = 4


def _stack_blocks(x):
    return jnp.concatenate([x[:, j * LANES:(j + 1) * LANES] for j in range(x.shape[1] // LANES)], axis=0)


def _pair_norm_rope(xs, gain, tab, ones_bd):
    reps = xs.shape[0] // CHUNK
    cos, s_lo, s_hi = (jnp.concatenate([tab[i]] * reps, axis=0) for i in range(3))
    half = ROPE_DIM // 2
    sq = xs * xs
    sq_hi = sq.astype(BF16)
    sq_lo = (sq - sq_hi.astype(F32)).astype(BF16)
    ss = (jnp.dot(sq_hi, ones_bd, preferred_element_type=F32)
          + jnp.dot(sq_lo, ones_bd, preferred_element_type=F32))
    xg = xs * gain
    xr = xg * cos + pltpu.roll(xg, LANES - half, 1) * s_lo + pltpu.roll(xg, half, 1) * s_hi
    return xr * lax.rsqrt(ss * (1.0 / HEAD_DIM) + EPS)


def _attn_kernel(sink_ref, q_ref, k_ref, v_ref, ck_ref, cv_ref, tab_ref, gq_ref, gk_ref, o_ref, kn_ref, kd, vd,
                 *, l, n_prompt_steps, steps_per_seq):
    step = pl.program_id(0)
    is_sample = step >= n_prompt_steps
    c_base = lax.rem(step, steps_per_seq) * ATT_CPS
    first = lax.broadcasted_iota(jnp.int32, (1, LANES), 1) < HEAD_DIM
    ones_bd = jnp.where(lax.broadcasted_iota(jnp.int32, (LANES, LANES), 0) // HEAD_DIM
                        == lax.broadcasted_iota(jnp.int32, (LANES, LANES), 1) // HEAD_DIM, 1.0, 0.0).astype(BF16)
    key_slot = lax.broadcasted_iota(jnp.int32, (1, N_KEYS), 1) // CHUNK
    gq = gq_ref[...]
    gk = gk_ref[...]

    def store_dup(dst, x, rows):
        for pair in range(D_KV // LANES):
            blk = x[:, pair * LANES:(pair + 1) * LANES]
            swapped = pltpu.roll(blk, HEAD_DIM, 1)
            dst[2 * pair, rows, 0:LANES] = jnp.where(first, blk, swapped).astype(BF16)
            dst[2 * pair + 1, rows, 0:LANES] = jnp.where(first, swapped, blk).astype(BF16)

    @pl.when(step == 0)
    def _():
        vd[:, :, LANES:] = jnp.ones((N_KV_HEADS, N_KEYS, LANES), BF16)

    def chunk(ci, carry):
        c = jnp.where(is_sample, 0, c_base + ci)
        rows = pl.ds(pl.multiple_of(ci * CHUNK, CHUNK), CHUNK)
        prev = slice(CHUNK, N_KEYS)

        @pl.when(jnp.logical_and(c == 0, jnp.logical_not(is_sample)))
        def _():
            kd[:, prev, :] = jnp.zeros((N_KV_HEADS, N_KEYS - CHUNK, LANES), BF16)
            vd[:, prev, 0:LANES] = jnp.zeros((N_KV_HEADS, N_KEYS - CHUNK, LANES), BF16)

        @pl.when(is_sample)
        def _():
            cached = pl.ds(pl.multiple_of(ci * WINDOW, WINDOW), WINDOW)
            store_dup(kd, ck_ref[cached, :], prev)
            store_dup(vd, cv_ref[cached, :], prev)

        tab = tab_ref[:, rows, :]
        own = pl.ds(pl.multiple_of(lax.rem(c, RING) * CHUNK, CHUNK), CHUNK)
        kn = _pair_norm_rope(_stack_blocks(k_ref[rows, :]), gk, tab, ones_bd)
        kn = jnp.concatenate([kn[0:CHUNK], kn[CHUNK:2 * CHUNK]], axis=1)
        kn_ref[rows, :] = kn
        store_dup(kd, kn, own)
        store_dup(vd, v_ref[rows, :], own)

        qn = _pair_norm_rope(_stack_blocks(q_ref[rows, :]), gq, tab, ones_bd) * (HEAD_DIM ** -0.5)
        q_lo = jnp.where(first, qn, 0.0).astype(BF16)
        q_hi = jnp.where(first, 0.0, qn).astype(BF16)

        valid = key_slot <= jnp.where(is_sample, RING, c)
        for kh in range(N_KV_HEADS):
            r0 = kh * 2 * CHUNK
            lhs = jnp.concatenate([q_lo[r0:r0 + CHUNK], q_hi[r0:r0 + CHUNK],
                                   q_lo[r0 + CHUNK:r0 + 2 * CHUNK], q_hi[r0 + CHUNK:r0 + 2 * CHUNK]], axis=0)
            s = lax.dot_general(lhs, kd[kh], (((1,), (1,)), ((), ())), preferred_element_type=F32)
            s = jnp.where(valid, s, -1e30)
            sink = jnp.concatenate([jnp.full((CHUNK, 1), sink_ref[l, kh * KV_REP + r], F32)
                                    for r in range(KV_REP)], axis=0)
            m = jnp.maximum(jnp.max(s, axis=-1, keepdims=True), sink)
            p = jnp.exp(s - m).astype(BF16)
            od = jnp.dot(p, vd[kh], preferred_element_type=F32)
            o = od[:, 0:LANES] / (od[:, LANES:] + jnp.exp(sink - m))
            for pair in range(2):
                half = pair * 2 * CHUNK
                blk = jnp.where(first, o[half:half + CHUNK], o[half + CHUNK:half + 2 * CHUNK])
                col = (2 * kh + pair) * LANES
                o_ref[rows, col:col + LANES] = blk.astype(o_ref.dtype)
        return carry

    lax.fori_loop(0, ATT_CPS, chunk, 0)


def _attn_call(proj, l, n_prompt, t_prompt, sink, rope_tab, gq, gk, cache_k, cache_v):
    n_tok = proj.shape[0]
    tr = ATT_CPS * CHUNK
    qcol = D_SSM // D_ATTN
    kcol = (D_SSM + D_ATTN) // D_KV
    n_prompt_steps = n_prompt // tr
    cache_spec = pl.BlockSpec((None, ATT_CPS * WINDOW, D_KV),
                              lambda s: (l, jnp.maximum(s - n_prompt_steps, 0), 0))
    return pl.pallas_call(
        functools.partial(_attn_kernel, l=l, n_prompt_steps=n_prompt_steps, steps_per_seq=t_prompt // tr),
        out_shape=(jax.ShapeDtypeStruct((n_tok, D_ATTN), BF16),
                   jax.ShapeDtypeStruct((n_tok, D_KV), F32)),
        grid=(n_tok // tr,),
        in_specs=[
            pl.BlockSpec(memory_space=pltpu.SMEM),
            pl.BlockSpec((tr, D_ATTN), lambda s: (s, qcol)),
            pl.BlockSpec((tr, D_KV), lambda s: (s, kcol)),
            pl.BlockSpec((tr, D_KV), lambda s: (s, kcol + 1)),
            cache_spec, cache_spec,
            pl.BlockSpec((3, tr, LANES), lambda s: (0, s, 0)),
            _layer_spec(l, (1, LANES)), _layer_spec(l, (1, LANES)),
        ],
        out_specs=(pl.BlockSpec((tr, D_ATTN), lambda s: (s, 0)),
                   pl.BlockSpec((tr, D_KV), lambda s: (s, 0))),
        scratch_shapes=[pltpu.VMEM((N_KV_HEADS, N_KEYS, LANES), BF16),
                        pltpu.VMEM((N_KV_HEADS, N_KEYS, 2 * LANES), BF16)],
        compiler_params=_cparams(("arbitrary",)),
        name="banded_attn",
    )(sink, proj, proj, proj, cache_k, cache_v, rope_tab, gq, gk)


def _block_diag_rows(x, n):
    *lead, r, width = x.shape
    keep = jnp.arange(n)[:, None, None] == (jnp.arange(width) // (width // n))[None, None, :]
    return jnp.where(keep, x[..., None, :, :], 0.0).reshape(*lead, n * r, width)


def _block_diag_cols(x, n):
    *lead, height, c = x.shape
    tiled = jnp.concatenate([x] * n, axis=-1)
    keep = (jnp.arange(height) // (height // n))[:, None] == (jnp.arange(n * c) // c)[None, :]
    return jnp.where(keep, tiled, 0.0)


def _rope_table(pos):
    half = ROPE_DIM // 2
    n_pos = pos.shape[0]
    inv_freq = ROPE_THETA ** (-jnp.arange(half, dtype=F32) / half)
    ang = pos.astype(F32)[:, None] * inv_freq[None, :]
    cos, sin = jnp.cos(ang), jnp.sin(ang)
    ones = jnp.ones((n_pos, HEAD_DIM - ROPE_DIM), F32)
    zeros = jnp.zeros((n_pos, HEAD_DIM - half), F32)
    c_tab = jnp.concatenate([cos, cos, ones], axis=1)
    lo_tab = jnp.concatenate([-sin, zeros], axis=1)
    hi_tab = jnp.concatenate([jnp.zeros((n_pos, half), F32), sin, jnp.zeros((n_pos, HEAD_DIM - ROPE_DIM), F32)],
                             axis=1)
    tab = jnp.stack([c_tab, lo_tab, hi_tab])
    return jnp.concatenate([tab, tab], axis=2)


def kernel(x_prompt, x_sample, cache_k, cache_v, state_ssm_re, state_ssm_im, c_prompt, c_sample, w_mod, b_mod, norm1_g, norm2_g, w_in, ssm_a_re, ssm_a_im, ssm_log_dt, ssm_b_re, ssm_b_im, ssm_c_re, ssm_c_im, ssm_d, w_glu, b_glu, q_norm_g, k_norm_g, attn_sink, w_gate, b_gate, w_proj_ssm, w_proj_attn, w_out, w_ffn_gate, w_ffn_up, w_ffn_down):
    bp, tp, _ = x_prompt.shape
    bs, ts, _ = x_sample.shape
    assert ts == CHUNK and tp % (ATT_CPS * CHUNK) == 0 and bs % ATT_CPS == 0
    n_p, n_s = bp * tp, bs * ts
    tm = 1024
    x = jnp.concatenate([x_prompt.reshape(n_p, D_MODEL), x_sample.reshape(n_s, D_MODEL)], axis=0)

    n_cond = bp + bs
    pad = (-n_cond) % SUBLANES
    c_all = jnp.concatenate([c_prompt, c_sample, jnp.zeros((pad, D_MODEL), F32)], axis=0)
    mod = _mod_call(c_all, w_mod, b_mod)
    n_mod = mod.shape[2]

    def per_block(m, n_seq, t_len):
        reps = t_len // MOD_BLOCK
        return jnp.broadcast_to(m[:, :, None, :], (DEPTH, n_seq, reps, n_mod)).reshape(DEPTH, n_seq * reps, n_mod)

    modx = jnp.concatenate([per_block(mod[:, :bp], bp, tp), per_block(mod[:, bp:n_cond], bs, ts)],
                           axis=1)

    g, p = N_SSM_GROUPS, SSM_STATE
    gpc = g // SSM_KC
    pw_re, pw_im, bb_re, bb_im = _s5_prep_call(ssm_a_re, ssm_a_im, ssm_log_dt,
                                               ssm_b_re.transpose(0, 3, 1, 2), ssm_b_im.transpose(0, 3, 1, 2))
    pr = pw_re.reshape(DEPTH, SUBLANES, STATE_W)
    pi = pw_im.reshape(DEPTH, SUBLANES, STATE_W)
    tab_rows = []
    for k in (1, 2, 4):
        keep = (jnp.arange(SUBLANES) >= k)[None, :, None]
        tab_rows.append(jnp.where(keep, pr[:, k - 1:k, :], 0.0))
        tab_rows.append(jnp.where(keep, pi[:, k - 1:k, :], 0.0))
    scan_tab = jnp.stack(tab_rows + [pr, pi], axis=1)

    def wb_blocks(bb):
        per_chunk = bb.reshape(DEPTH, SSM_GROUP, SSM_KC, gpc * p).transpose(0, 2, 1, 3)
        return _block_diag_rows(per_chunk, gpc).astype(BF16)

    def wc_blocks(cc):
        per_chunk = cc.transpose(0, 1, 3, 2).reshape(DEPTH, SSM_KC, gpc * p, SSM_GROUP)
        return _block_diag_cols(per_chunk, gpc).astype(BF16)

    ssm_w = (wb_blocks(bb_re), wb_blocks(bb_im), scan_tab, wc_blocks(ssm_c_re), wc_blocks(ssm_c_im), ssm_d,
             w_glu.astype(BF16), b_glu)
    zeros_state = jnp.zeros((bp, 1, STATE_W), F32)
    h0_re = state_ssm_re.reshape(DEPTH, bs, 1, STATE_W)
    h0_im = state_ssm_im.reshape(DEPTH, bs, 1, STATE_W)

    pos = jnp.concatenate([jnp.tile(jnp.arange(tp), bp), jnp.tile(PAST_LEN + jnp.arange(ts), bs)])
    rope_tab = _rope_table(pos)
    gq = jnp.tile(q_norm_g, (1, LANES // HEAD_DIM)).reshape(DEPTH, 1, LANES)
    gk = jnp.tile(k_norm_g, (1, LANES // HEAD_DIM)).reshape(DEPTH, 1, LANES)
    cache_k2 = cache_k.reshape(DEPTH, bs * WINDOW, D_KV)
    cache_v2 = cache_v.reshape(DEPTH, bs * WINDOW, D_KV)

    w_in_b, w_gate_b = w_in.astype(BF16), w_gate.astype(BF16)
    w_ps_b, w_pa_b, w_out_b = w_proj_ssm.astype(BF16), w_proj_attn.astype(BF16), w_out.astype(BF16)
    w_fg_b, w_fu_b, w_fd_b = w_ffn_gate.astype(BF16), w_ffn_up.astype(BF16), w_ffn_down.astype(BF16)
    v0 = D_SSM + D_ATTN + D_KV

    def heads(t):
        return t.reshape(*t.shape[:-1], N_KV_HEADS, HEAD_DIM)

    def last_window(t, col0, col1):
        return jnp.stack([t[(b + 1) * tp - WINDOW:(b + 1) * tp, col0:col1] for b in range(bp)])

    outs = {k: [] for k in ("pk", "pv", "pre", "pim", "sk", "sv", "sre", "sim")}
    for l in range(DEPTH):
        proj, gates = _in_gate_call(x, l, norm1_g, modx, w_in_b, w_gate_b, b_gate, tm=tm)

        ssm_p, pre, pim = _ssm_call(proj, 0, bp, tp, 256, zeros_state, zeros_state, l, *ssm_w, name="s5_prompt")
        ssm_s, sre, sim = _ssm_call(proj, n_p, bs, ts, ts, h0_re[l], h0_im[l], l, *ssm_w, name="s5_sample")
        attn_out, kn = _attn_call(proj, l, n_p, tp, attn_sink, rope_tab, gq, gk, cache_k2, cache_v2)

        mixed = _mix_call(ssm_p, ssm_s, attn_out, gates, l, w_ps_b, w_pa_b, tm=tm, tn=1024)
        x = _resid_mm_call(mixed, l, w_out_b, x, modx, 2, tm=tm, tn=1024, name="out_proj")

        act = _ffn_up_call(x, l, norm2_g, modx, w_fg_b, w_fu_b, tm=tm)
        last = l == DEPTH - 1
        x = _resid_mm_call(act, l, w_fd_b, x, modx, 5, n_split=n_p // tm if last else None, tm=tm,
                           name="ffn_down_split" if last else "ffn_down")

        outs["pk"].append(heads(last_window(kn, 0, D_KV)))
        outs["pv"].append(heads(last_window(proj, v0, IN_WIDTH)))
        outs["pre"].append(pre.reshape(bp, g, p))
        outs["pim"].append(pim.reshape(bp, g, p))
        outs["sk"].append(jnp.concatenate([cache_k[l][:, ts:], heads(kn[n_p:].reshape(bs, ts, D_KV))], axis=1))
        outs["sv"].append(jnp.concatenate([cache_v[l][:, ts:], heads(proj[n_p:, v0:].reshape(bs, ts, D_KV))],
                                          axis=1))
        outs["sre"].append(sre.reshape(bs, g, p))
        outs["sim"].append(sim.reshape(bs, g, p))

    y_p, y_s = x
    return (y_p.reshape(bp, tp, D_MODEL), y_s.reshape(bs, ts, D_MODEL),
            jnp.stack(outs["pk"]), jnp.stack(outs["pv"]), jnp.stack(outs["pre"]), jnp.stack(outs["pim"]),
            jnp.stack(outs["sk"]), jnp.stack(outs["sv"]), jnp.stack(outs["sre"]), jnp.stack(outs["sim"]))
```

```python
import functools
import math

import jax
import jax.numpy as jnp
from jax import lax
from jax.experimental import pallas as pl
from jax.experimental.pallas import tpu as pltpu

D_MODEL = 2048
DEPTH = 4
CHUNK = 64
D_SSM = 1024
SSM_GROUP = 16
N_SSM_GROUPS = 64
SSM_STATE = 64
HEAD_DIM = 64
N_HEADS = 16
N_KV_HEADS = 4
KV_REP = N_HEADS // N_KV_HEADS
D_ATTN = N_HEADS * HEAD_DIM
D_KV = N_KV_HEADS * HEAD_DIM
IN_WIDTH = D_SSM + D_ATTN + 2 * D_KV
WINDOW = 128
ROPE_DIM = 16
ROPE_THETA = 500000.0
D_FF = 5632
EPS = 1e-6
PAST_LEN = 2048

LANES = 128
SUBLANES = 8
MOD_BLOCK = CHUNK
STATE_W = N_SSM_GROUPS * SSM_STATE
SSM_KC = 4
VMEM_LIMIT = 56 * 1024 * 1024

F32 = jnp.float32
BF16 = jnp.bfloat16


def _cparams(sem):
    return pltpu.CompilerParams(dimension_semantics=sem, vmem_limit_bytes=VMEM_LIMIT)


def _sigmoid(x):
    return 1.0 / (1.0 + jnp.exp(-x))


def _layer_spec(l, shape):
    zeros = (0,) * len(shape)
    return pl.BlockSpec((None, *shape), lambda *_: (l, *zeros))


def _mod_kernel(c_ref, w_ref, b_ref, o_ref):
    c = c_ref[...].astype(BF16)
    w = w_ref[...].astype(BF16)
    o_ref[...] = jnp.dot(c, w, preferred_element_type=F32) + b_ref[...]


def _mod_call(c_all, w_mod, b_mod):
    nb = c_all.shape[0]
    tn = 1024
    n_out = w_mod.shape[2]
    return pl.pallas_call(
        _mod_kernel,
        out_shape=jax.ShapeDtypeStruct((DEPTH, nb, n_out), F32),
        grid=(DEPTH, n_out // tn),
        in_specs=[
            pl.BlockSpec((nb, D_MODEL), lambda l, j: (0, 0)),
            pl.BlockSpec((None, D_MODEL, tn), lambda l, j: (l, 0, j)),
            pl.BlockSpec((None, 1, tn), lambda l, j: (l, 0, j)),
        ],
        out_specs=pl.BlockSpec((None, nb, tn), lambda l, j: (l, 0, j)),
        compiler_params=_cparams(("arbitrary", "arbitrary")),
        name="adaln_mod",
    )(c_all, w_mod, b_mod.reshape(DEPTH, 1, n_out))


S5_TILES = (256, 64)
S5_SEG = SUBLANES
S5_POWERS = tuple(sorted({n for t in S5_TILES for n in (*range(1, t // S5_SEG + 1),
                                                         *(k * t // S5_SEG for k in (1, 2, 4)))}))


def _s5_prep_kernel(are_ref, aim_ref, ldt_ref, bre_ref, bim_ref, pwr_ref, pwi_ref, bbr_ref, bbi_ref):
    a_re = are_ref[...]
    a_im = aim_ref[...]
    dt = jnp.exp(ldt_ref[...])
    z_re = a_re * dt
    z_im = a_im * dt
    for i, n in enumerate(S5_POWERS):
        mag = jnp.exp(z_re * float(n))
        pwr_ref[i] = mag * jnp.cos(z_im * float(n))
        pwi_ref[i] = mag * jnp.sin(z_im * float(n))
    l_re = pwr_ref[S5_POWERS.index(1)]
    l_im = pwi_ref[S5_POWERS.index(1)]
    den = a_re * a_re + a_im * a_im
    n_re = l_re - 1.0
    f_re = (n_re * a_re + l_im * a_im) / den
    f_im = (l_im * a_re - n_re * a_im) / den
    for c in range(SSM_GROUP):
        b_re = bre_ref[c]
        b_im = bim_ref[c]
        bbr_ref[c] = f_re * b_re - f_im * b_im
        bbi_ref[c] = f_re * b_im + f_im * b_re


def _s5_prep_call(a_re, a_im, log_dt, bt_re, bt_im):
    g, p = N_SSM_GROUPS, SSM_STATE
    mat = pl.BlockSpec((None, g, p), lambda l: (l, 0, 0))
    stack_c = pl.BlockSpec((None, SSM_GROUP, g, p), lambda l: (l, 0, 0, 0))
    n_pow = len(S5_POWERS)
    stack_n = pl.BlockSpec((None, n_pow, g, p), lambda l: (l, 0, 0, 0))
    return pl.pallas_call(
        _s5_prep_kernel,
        out_shape=(jax.ShapeDtypeStruct((DEPTH, n_pow, g, p), F32),
                   jax.ShapeDtypeStruct((DEPTH, n_pow, g, p), F32),
                   jax.ShapeDtypeStruct((DEPTH, SSM_GROUP, g, p), F32),
                   jax.ShapeDtypeStruct((DEPTH, SSM_GROUP, g, p), F32)),
        grid=(DEPTH,),
        in_specs=[mat, mat, pl.BlockSpec((None, g, 1), lambda l: (l, 0, 0)), stack_c, stack_c],
        out_specs=(stack_n, stack_n, stack_c, stack_c),
        compiler_params=_cparams(("arbitrary",)),
        name="s5_discretize",
    )(a_re, a_im, log_dt.reshape(DEPTH, g, 1), bt_re, bt_im)


def _norm_mod_tile(x_ref, g_ref, sc_ref, sh_ref, h_scr, nsub):
    gain = g_ref[...]
    for s in range(nsub):
        rows = slice(s * MOD_BLOCK, (s + 1) * MOD_BLOCK)
        x = x_ref[rows, :]
        ms = jnp.mean(x * x, axis=-1, keepdims=True)
        y = x * lax.rsqrt(ms + EPS) * gain
        h = y * (1.0 + sc_ref[s:s + 1, :]) + sh_ref[s:s + 1, :]
        h_scr[rows, :] = h.astype(BF16)


def _norm_specs(l, tm, sc_idx, sh_idx):
    nsub = tm // MOD_BLOCK
    return [
        pl.BlockSpec((tm, D_MODEL), lambda i, j: (i, 0)),
        _layer_spec(l, (1, D_MODEL)),
        pl.BlockSpec((None, nsub, D_MODEL), lambda i, j: (l, i, sc_idx)),
        pl.BlockSpec((None, nsub, D_MODEL), lambda i, j: (l, i, sh_idx)),
    ]


def _in_gate_kernel(x_ref, g_ref, sc_ref, sh_ref, wi_ref, wg_ref, b_ref, proj_ref, gate_ref, h_scr, *, nsub, n_in):
    j = pl.program_id(1)
    pl.when(j == 0)(lambda: _norm_mod_tile(x_ref, g_ref, sc_ref, sh_ref, h_scr, nsub))

    @pl.when(j < n_in)
    def _():
        proj_ref[...] = jnp.dot(h_scr[...], wi_ref[...], preferred_element_type=F32)

    @pl.when(j >= n_in)
    def _():
        acc = jnp.dot(h_scr[...], wg_ref[...], preferred_element_type=F32)
        gate_ref[...] = _sigmoid(acc + b_ref[...])


def _in_gate_call(x, l, gain, modx, w_in, w_gate, b_gate, *, tm=1024, tn=512):
    n_tok = x.shape[0]
    n_in, n_gate = w_in.shape[2] // tn, w_gate.shape[2] // tn
    in_col = lambda i, j: jnp.minimum(j, n_in - 1)
    gate_col = lambda i, j: jnp.maximum(j - n_in, 0)
    return pl.pallas_call(
        functools.partial(_in_gate_kernel, nsub=tm // MOD_BLOCK, n_in=n_in),
        out_shape=(jax.ShapeDtypeStruct((n_tok, w_in.shape[2]), F32),
                   jax.ShapeDtypeStruct((n_tok, w_gate.shape[2]), F32)),
        grid=(n_tok // tm, n_in + n_gate),
        in_specs=_norm_specs(l, tm, 1, 0) + [
            pl.BlockSpec((None, D_MODEL, tn), lambda i, j: (l, 0, in_col(i, j))),
            pl.BlockSpec((None, D_MODEL, tn), lambda i, j: (l, 0, gate_col(i, j))),
            pl.BlockSpec((None, 1, tn), lambda i, j: (l, 0, gate_col(i, j))),
        ],
        out_specs=(pl.BlockSpec((tm, tn), lambda i, j: (i, in_col(i, j))),
                   pl.BlockSpec((tm, tn), lambda i, j: (i, gate_col(i, j)))),
        scratch_shapes=[pltpu.VMEM((tm, D_MODEL), BF16)],
        compiler_params=_cparams(("arbitrary", "arbitrary")),
        name="in_proj_gates",
    )(x, gain.reshape(DEPTH, 1, D_MODEL), modx, modx, w_in, w_gate, b_gate.reshape(DEPTH, 1, w_gate.shape[2]))


def _ffn_up_kernel(x_ref, g_ref, sc_ref, sh_ref, wg_ref, wu_ref, o_ref, h_scr, *, nsub):
    pl.when(pl.program_id(1) == 0)(lambda: _norm_mod_tile(x_ref, g_ref, sc_ref, sh_ref, h_scr, nsub))
    h = h_scr[...]
    gate = jnp.dot(h, wg_ref[...], preferred_element_type=F32)
    up = jnp.dot(h, wu_ref[...], preferred_element_type=F32)
    o_ref[...] = (gate * _sigmoid(gate) * up).astype(o_ref.dtype)


def _ffn_up_call(x, l, gain, modx, w_gate, w_up, *, tm=1024, tn=512):
    n_tok = x.shape[0]
    n_out = w_gate.shape[2]
    w_spec = pl.BlockSpec((None, D_MODEL, tn), lambda i, j: (l, 0, j))
    return pl.pallas_call(
        functools.partial(_ffn_up_kernel, nsub=tm // MOD_BLOCK),
        out_shape=jax.ShapeDtypeStruct((n_tok, n_out), BF16),
        grid=(n_tok // tm, n_out // tn),
        in_specs=_norm_specs(l, tm, 4, 3) + [w_spec, w_spec],
        out_specs=pl.BlockSpec((tm, tn), lambda i, j: (i, j)),
        scratch_shapes=[pltpu.VMEM((tm, D_MODEL), BF16)],
        compiler_params=_cparams(("arbitrary", "arbitrary")),
        name="ffn_up",
    )(x, gain.reshape(DEPTH, 1, D_MODEL), modx, modx, w_gate, w_up)


def _mix_kernel(sp_ref, ss_ref, a_ref, ga_ref, gb_ref, ws_ref, wa_ref, o_ref, *, n_split):
    def body(s_ref):
        ps = jnp.dot(s_ref[...], ws_ref[...], preferred_element_type=F32)
        pa = jnp.dot(a_ref[...], wa_ref[...], preferred_element_type=F32)
        o_ref[...] = (ga_ref[...] * ps + gb_ref[...] * pa).astype(o_ref.dtype)

    i = pl.program_id(0)
    pl.when(i < n_split)(lambda: body(sp_ref))
    pl.when(i >= n_split)(lambda: body(ss_ref))


def _mix_call(ssm_p, ssm_s, attn_out, gates, l, w_ps, w_pa, *, tm=1024, tn=512):
    n_tok = attn_out.shape[0]
    nj = D_MODEL // tn
    n_split = ssm_p.shape[0] // tm
    return pl.pallas_call(
        functools.partial(_mix_kernel, n_split=n_split),
        out_shape=jax.ShapeDtypeStruct((n_tok, D_MODEL), BF16),
        grid=(n_tok // tm, nj),
        in_specs=[
            pl.BlockSpec((tm, D_SSM), lambda i, j: (jnp.minimum(i, n_split - 1), 0)),
            pl.BlockSpec((tm, D_SSM), lambda i, j: (jnp.maximum(i - n_split, 0), 0)),
            pl.BlockSpec((tm, D_ATTN), lambda i, j: (i, 0)),
            pl.BlockSpec((tm, tn), lambda i, j: (i, j)),
            pl.BlockSpec((tm, tn), lambda i, j: (i, j + nj)),
            pl.BlockSpec((None, D_SSM, tn), lambda i, j: (l, 0, j)),
            pl.BlockSpec((None, D_ATTN, tn), lambda i, j: (l, 0, j)),
        ],
        out_specs=pl.BlockSpec((tm, tn), lambda i, j: (i, j)),
        compiler_params=_cparams(("arbitrary", "arbitrary")),
        name="branch_merge",
    )(ssm_p, ssm_s, attn_out, gates, gates, w_ps, w_pa)


def _resid_mm_kernel(a_ref, w_ref, x_ref, g_ref, *o_refs, nsub, n_split):
    acc = jnp.dot(a_ref[...], w_ref[...], preferred_element_type=F32)

    def write(o_ref):
        for s in range(nsub):
            rows = slice(s * MOD_BLOCK, (s + 1) * MOD_BLOCK)
            o_ref[rows, :] = x_ref[rows, :] + g_ref[s:s + 1, :] * acc[rows, :]

    if n_split is None:
        write(o_refs[0])
    else:
        i = pl.program_id(0)
        pl.when(i < n_split)(lambda: write(o_refs[0]))
        pl.when(i >= n_split)(lambda: write(o_refs[1]))


def _resid_mm_call(a, l, w, x, modx, g_idx, *, n_split=None, tm=1024, tn=512, name):
    n_tok, k = a.shape
    nsub = tm // MOD_BLOCK
    nj = D_MODEL // tn
    n_tiles = n_tok // tm
    if n_split is None:
        out_shape = jax.ShapeDtypeStruct((n_tok, D_MODEL), F32)
        out_specs = pl.BlockSpec((tm, tn), lambda i, j: (i, j))
    else:
        out_shape = (jax.ShapeDtypeStruct((n_split * tm, D_MODEL), F32),
                     jax.ShapeDtypeStruct(((n_tiles - n_split) * tm, D_MODEL), F32))
        out_specs = (
            pl.BlockSpec((tm, tn), lambda i, j: (jnp.minimum(i, n_split - 1), jnp.where(i < n_split, j, nj - 1))),
            pl.BlockSpec((tm, tn), lambda i, j: (jnp.maximum(i - n_split, 0), jnp.where(i < n_split, 0, j))),
        )
    return pl.pallas_call(
        functools.partial(_resid_mm_kernel, nsub=nsub, n_split=n_split),
        out_shape=out_shape,
        grid=(n_tiles, nj),
        in_specs=[
            pl.BlockSpec((tm, k), lambda i, j: (i, 0)),
            pl.BlockSpec((None, k, tn), lambda i, j: (l, 0, j)),
            pl.BlockSpec((tm, tn), lambda i, j: (i, j)),
            pl.BlockSpec((None, nsub, tn), lambda i, j: (l, i, g_idx * nj + j)),
        ],
        out_specs=out_specs,
        compiler_params=_cparams(("arbitrary", "arbitrary")),
        name=name,
    )(a, w, x, modx)


SCAN_LW = 512


def _ssm_kernel(u_ref, h0r_ref, h0i_ref, wbr_ref, wbi_ref, pwj_ref, seg_ref, wcr_ref, wci_ref, d_ref, wglu_ref,
                bglu_ref, o_ref, htr_ref, hti_ref, bur, bui, cre, cim, z_scr, *, tt):
    steps = tt // S5_SEG

    @pl.when(pl.program_id(1) == 0)
    def _():
        cre[...] = jnp.broadcast_to(h0r_ref[...], (SUBLANES, STATE_W))
        cim[...] = jnp.broadcast_to(h0i_ref[...], (SUBLANES, STATE_W))

    r_idx = lax.broadcasted_iota(jnp.int32, (tt, tt), 0)
    c_idx = lax.broadcasted_iota(jnp.int32, (tt, tt), 1)
    to_scan = jnp.where(c_idx == (r_idx % S5_SEG) * steps + r_idx // S5_SEG, 1.0, 0.0).astype(BF16)
    to_time = jnp.where(r_idx == (c_idx % S5_SEG) * steps + c_idx // S5_SEG, 1.0, 0.0).astype(BF16)

    u = u_ref[...]
    us = jnp.dot(to_scan, u.astype(BF16), preferred_element_type=F32).astype(BF16)
    kw = D_SSM // SSM_KC
    sw = STATE_W // SSM_KC
    for kc in range(SSM_KC):
        uk = us[:, kc * kw:(kc + 1) * kw]
        bur[:, kc * sw:(kc + 1) * sw] = jnp.dot(uk, wbr_ref[kc], preferred_element_type=F32)
        bui[:, kc * sw:(kc + 1) * sw] = jnp.dot(uk, wbi_ref[kc], preferred_element_type=F32)

    row = lax.broadcasted_iota(jnp.int32, (SUBLANES, SCAN_LW), 0)
    for lc in range(STATE_W // SCAN_LW):
        sl = slice(lc * SCAN_LW, (lc + 1) * SCAN_LW)
        lam_re = jnp.broadcast_to(pwj_ref[0, 0:1, sl], (SUBLANES, SCAN_LW))
        lam_im = jnp.broadcast_to(pwj_ref[1, 0:1, sl], (SUBLANES, SCAN_LW))

        def local(j, carry, sl=sl, lam_re=lam_re, lam_im=lam_im):
            h_re, h_im = carry
            rows = pl.ds(pl.multiple_of(j * SUBLANES, SUBLANES), SUBLANES)
            h_re, h_im = (lam_re * h_re - lam_im * h_im + bur[rows, sl],
                          lam_re * h_im + lam_im * h_re + bui[rows, sl])
            bur[rows, sl] = h_re
            bui[rows, sl] = h_im
            return h_re, h_im

        zero = jnp.zeros((SUBLANES, SCAN_LW), F32)
        e_re, e_im = lax.fori_loop(0, steps, local, (zero, zero), unroll=2)

        x_re = jnp.where(row == 0, cre[:, sl], pltpu.roll(e_re, 1, 0))
        x_im = jnp.where(row == 0, cim[:, sl], pltpu.roll(e_im, 1, 0))
        for idx, k in enumerate((1, 2, 4)):
            m_re = seg_ref[2 * idx, :, sl]
            m_im = seg_ref[2 * idx + 1, :, sl]
            s_re = pltpu.roll(x_re, k, 0)
            s_im = pltpu.roll(x_im, k, 0)
            x_re, x_im = (x_re + m_re * s_re - m_im * s_im,
                          x_im + m_re * s_im + m_im * s_re)
        pj_re = jnp.broadcast_to(pwj_ref[0, steps - 1:steps, sl], (SUBLANES, SCAN_LW))
        pj_im = jnp.broadcast_to(pwj_ref[1, steps - 1:steps, sl], (SUBLANES, SCAN_LW))
        n_re = pj_re * x_re - pj_im * x_im + e_re
        n_im = pj_re * x_im + pj_im * x_re + e_im
        cre[:, sl] = jnp.broadcast_to(n_re[SUBLANES - 1:SUBLANES, :], (SUBLANES, SCAN_LW))
        cim[:, sl] = jnp.broadcast_to(n_im[SUBLANES - 1:SUBLANES, :], (SUBLANES, SCAN_LW))

        def stitch(j, carry, sl=sl, c_re=x_re, c_im=x_im):
            rows = pl.ds(pl.multiple_of(j * SUBLANES, SUBLANES), SUBLANES)
            p_re = jnp.broadcast_to(pwj_ref[0, pl.ds(j, 1), sl], (SUBLANES, SCAN_LW))
            p_im = jnp.broadcast_to(pwj_ref[1, pl.ds(j, 1), sl], (SUBLANES, SCAN_LW))
            bur[rows, sl] = bur[rows, sl] + p_re * c_re - p_im * c_im
            bui[rows, sl] = bui[rows, sl] + p_re * c_im + p_im * c_re
            return carry

        lax.fori_loop(0, steps, stitch, 0, unroll=2)

    htr_ref[...] = cre[0:1, :]
    hti_ref[...] = cim[0:1, :]

    d = d_ref[...]
    for kc in range(SSM_KC):
        h_re = bur[:, kc * sw:(kc + 1) * sw].astype(BF16)
        h_im = bui[:, kc * sw:(kc + 1) * sw].astype(BF16)
        ys = (jnp.dot(h_re, wcr_ref[kc], preferred_element_type=F32)
              - jnp.dot(h_im, wci_ref[kc], preferred_element_type=F32))
        ys_hi = ys.astype(BF16)
        ys_lo = (ys - ys_hi.astype(F32)).astype(BF16)
        y = (jnp.dot(to_time, ys_hi, preferred_element_type=F32)
             + jnp.dot(to_time, ys_lo, preferred_element_type=F32))
        cols = slice(kc * kw, (kc + 1) * kw)
        y = y + d[:, cols] * u[:, cols]
        z_scr[:, cols] = 0.5 * y * (1.0 + jnp.tanh(math.sqrt(2.0 / math.pi) * (y + 0.044715 * (y * y * y))))
    z = z_scr[...]
    gate = _sigmoid(jnp.dot(z.astype(BF16), wglu_ref[...], preferred_element_type=F32) + bglu_ref[...])
    o_ref[...] = (z * gate).astype(o_ref.dtype)


def _ssm_call(proj, row0, n_seq, t_len, tt, h0_re, h0_im, l, wb_re, wb_im, pw_re, pw_im, wc_re, wc_im, d_skip,
              w_glu, b_glu, *, name):
    steps = tt // S5_SEG
    power = lambda n: S5_POWERS.index(n)
    pwj = jnp.stack([pw[:, power(1):power(steps) + 1] for pw in (pw_re, pw_im)], axis=1)
    seg_rows = []
    for k in (1, 2, 4):
        keep = (jnp.arange(SUBLANES) >= k)[None, :, None]
        seg_rows += [jnp.where(keep, pw[:, power(k * steps)][:, None, :], 0.0) for pw in (pw_re, pw_im)]
    seg = jnp.stack(seg_rows, axis=1)
    nt = t_len // tt
    rb0 = row0 // tt
    state_spec = pl.BlockSpec((None, 1, STATE_W), lambda b, t: (b, 0, 0))
    return pl.pallas_call(
        functools.partial(_ssm_kernel, tt=tt),
        out_shape=(jax.ShapeDtypeStruct((n_seq * t_len, D_SSM), BF16),
                   jax.ShapeDtypeStruct((n_seq, 1, STATE_W), F32),
                   jax.ShapeDtypeStruct((n_seq, 1, STATE_W), F32)),
        grid=(n_seq, nt),
        in_specs=[
            pl.BlockSpec((tt, D_SSM), lambda b, t: (rb0 + b * nt + t, 0)),
            state_spec, state_spec,
            _layer_spec(l, wb_re.shape[1:]), _layer_spec(l, wb_im.shape[1:]),
            _layer_spec(l, pwj.shape[1:]), _layer_spec(l, seg.shape[1:]),
            _layer_spec(l, wc_re.shape[1:]), _layer_spec(l, wc_im.shape[1:]),
            _layer_spec(l, (1, D_SSM)),
            _layer_spec(l, (D_SSM, D_SSM)),
            _layer_spec(l, (1, D_SSM)),
        ],
        out_specs=(pl.BlockSpec((tt, D_SSM), lambda b, t: (b * nt + t, 0)), state_spec, state_spec),
        scratch_shapes=[pltpu.VMEM((tt, STATE_W), F32), pltpu.VMEM((tt, STATE_W), F32),
                        pltpu.VMEM((SUBLANES, STATE_W), F32), pltpu.VMEM((SUBLANES, STATE_W), F32),
                        pltpu.VMEM((tt, D_SSM), F32)],
        compiler_params=_cparams(("arbitrary", "arbitrary")),
        name=name,
    )(proj, h0_re, h0_im, wb_re, wb_im, pwj, seg, wc_re, wc_im, d_skip.reshape(DEPTH, 1, D_SSM), w_glu,
      b_glu.reshape(DEPTH, 1, D_SSM))


RING = 3
N_KEYS = RING * CHUNK
ATT_CPS = 4


def _stack_blocks(x):
    return jnp.concatenate([x[:, j * LANES:(j + 1) * LANES] for j in range(x.shape[1] // LANES)], axis=0)


def _pair_norm_rope(xs, gain, tab, ones_bd):
    reps = xs.shape[0] // CHUNK
    cos, s_lo, s_hi = (jnp.concatenate([tab[i]] * reps, axis=0) for i in range(3))
    half = ROPE_DIM // 2
    sq = xs * xs
    sq_hi = sq.astype(BF16)
    sq_lo = (sq - sq_hi.astype(F32)).astype(BF16)
    ss = (jnp.dot(sq_hi, ones_bd, preferred_element_type=F32)
          + jnp.dot(sq_lo, ones_bd, preferred_element_type=F32))
    xg = xs * gain
    xr = xg * cos + pltpu.roll(xg, LANES - half, 1) * s_lo + pltpu.roll(xg, half, 1) * s_hi
    return xr * lax.rsqrt(ss * (1.0 / HEAD_DIM) + EPS)


def _attn_kernel(sink_ref, q_ref, k_ref, v_ref, ck_ref, cv_ref, tab_ref, gq_ref, gk_ref, o_ref, kn_ref, kd, vd,
                 *, l, n_prompt_steps, steps_per_seq):
    step = pl.program_id(0)
    is_sample = step >= n_prompt_steps
    c_base = lax.rem(step, steps_per_seq) * ATT_CPS
    first = lax.broadcasted_iota(jnp.int32, (1, LANES), 1) < HEAD_DIM
    ones_bd = jnp.where(lax.broadcasted_iota(jnp.int32, (LANES, LANES), 0) // HEAD_DIM
                        == lax.broadcasted_iota(jnp.int32, (LANES, LANES), 1) // HEAD_DIM, 1.0, 0.0).astype(BF16)
    key_slot = lax.broadcasted_iota(jnp.int32, (1, N_KEYS), 1) // CHUNK
    gq = gq_ref[...]
    gk = gk_ref[...]

    def store_dup(dst, x, rows):
        for pair in range(D_KV // LANES):
            blk = x[:, pair * LANES:(pair + 1) * LANES]
            swapped = pltpu.roll(blk, HEAD_DIM, 1)
            dst[2 * pair, rows, 0:LANES] = jnp.where(first, blk, swapped).astype(BF16)
            dst[2 * pair + 1, rows, 0:LANES] = jnp.where(first, swapped, blk).astype(BF16)

    @pl.when(step == 0)
    def _():
        vd[:, :, LANES:] = jnp.ones((N_KV_HEADS, N_KEYS, LANES), BF16)

    def chunk(ci, carry):
        c = jnp.where(is_sample, 0, c_base + ci)
        rows = pl.ds(pl.multiple_of(ci * CHUNK, CHUNK), CHUNK)
        prev = slice(CHUNK, N_KEYS)

        @pl.when(jnp.logical_and(c == 0, jnp.logical_not(is_sample)))
        def _():
            kd[:, prev, :] = jnp.zeros((N_KV_HEADS, N_KEYS - CHUNK, LANES), BF16)
            vd[:, prev, 0:LANES] = jnp.zeros((N_KV_HEADS, N_KEYS - CHUNK, LANES), BF16)

        @pl.when(is_sample)
        def _():
            cached = pl.ds(pl.multiple_of(ci * WINDOW, WINDOW), WINDOW)
            store_dup(kd, ck_ref[cached, :], prev)
            store_dup(vd, cv_ref[cached, :], prev)

        tab = tab_ref[:, rows, :]
        own = pl.ds(pl.multiple_of(lax.rem(c, RING) * CHUNK, CHUNK), CHUNK)
        kn = _pair_norm_rope(_stack_blocks(k_ref[rows, :]), gk, tab, ones_bd)
        kn = jnp.concatenate([kn[0:CHUNK], kn[CHUNK:2 * CHUNK]], axis=1)
        kn_ref[rows, :] = kn
        store_dup(kd, kn, own)
        store_dup(vd, v_ref[rows, :], own)

        qn = _pair_norm_rope(_stack_blocks(q_ref[rows, :]), gq, tab, ones_bd) * (HEAD_DIM ** -0.5)
        q_lo = jnp.where(first, qn, 0.0).astype(BF16)
        q_hi = jnp.where(first, 0.0, qn).astype(BF16)

        valid = key_slot <= jnp.where(is_sample, RING, c)
        for kh in range(N_KV_HEADS):
            r0 = kh * 2 * CHUNK
            lhs = jnp.concatenate([q_lo[r0:r0 + CHUNK], q_hi[r0:r0 + CHUNK],
                                   q_lo[r0 + CHUNK:r0 + 2 * CHUNK], q_hi[r0 + CHUNK:r0 + 2 * CHUNK]], axis=0)
            s = lax.dot_general(lhs, kd[kh], (((1,), (1,)), ((), ())), preferred_element_type=F32)
            s = jnp.where(valid, s, -1e30)
            sink = jnp.concatenate([jnp.full((CHUNK, 1), sink_ref[l, kh * KV_REP + r], F32)
                                    for r in range(KV_REP)], axis=0)
            m = jnp.maximum(jnp.max(s, axis=-1, keepdims=True), sink)
            p = jnp.exp(s - m).astype(BF16)
            od = jnp.dot(p, vd[kh], preferred_element_type=F32)
            o = od[:, 0:LANES] / (od[:, LANES:] + jnp.exp(sink - m))
            for pair in range(2):
                half = pair * 2 * CHUNK
                blk = jnp.where(first, o[half:half + CHUNK], o[half + CHUNK:half + 2 * CHUNK])
                col = (2 * kh + pair) * LANES
                o_ref[rows, col:col + LANES] = blk.astype(o_ref.dtype)
        return carry

    lax.fori_loop(0, ATT_CPS, chunk, 0)


def _attn_call(proj, l, n_prompt, t_prompt, sink, rope_tab, gq, gk, cache_k, cache_v):
    n_tok = proj.shape[0]
    tr = ATT_CPS * CHUNK
    qcol = D_SSM // D_ATTN
    kcol = (D_SSM + D_ATTN) // D_KV
    n_prompt_steps = n_prompt // tr
    cache_spec = pl.BlockSpec((None, ATT_CPS * WINDOW, D_KV),
                              lambda s: (l, jnp.maximum(s - n_prompt_steps, 0), 0))
    return pl.pallas_call(
        functools.partial(_attn_kernel, l=l, n_prompt_steps=n_prompt_steps, steps_per_seq=t_prompt // tr),
        out_shape=(jax.ShapeDtypeStruct((n_tok, D_ATTN), BF16),
                   jax.ShapeDtypeStruct((n_tok, D_KV), F32)),
        grid=(n_tok // tr,),
        in_specs=[
            pl.BlockSpec(memory_space=pltpu.SMEM),
            pl.BlockSpec((tr, D_ATTN), lambda s: (s, qcol)),
            pl.BlockSpec((tr, D_KV), lambda s: (s, kcol)),
            pl.BlockSpec((tr, D_KV), lambda s: (s, kcol + 1)),
            cache_spec, cache_spec,
            pl.BlockSpec((3, tr, LANES), lambda s: (0, s, 0)),
            _layer_spec(l, (1, LANES)), _layer_spec(l, (1, LANES)),
        ],
        out_specs=(pl.BlockSpec((tr, D_ATTN), lambda s: (s, 0)),
                   pl.BlockSpec((tr, D_KV), lambda s: (s, 0))),
        scratch_shapes=[pltpu.VMEM((N_KV_HEADS, N_KEYS, LANES), BF16),
                        pltpu.VMEM((N_KV_HEADS, N_KEYS, 2 * LANES), BF16)],
        compiler_params=_cparams(("arbitrary",)),
        name="banded_attn",
    )(sink, proj, proj, proj, cache_k, cache_v, rope_tab, gq, gk)


def _block_diag_rows(x, n):
    *lead, r, width = x.shape
    keep = jnp.arange(n)[:, None, None] == (jnp.arange(width) // (width // n))[None, None, :]
    return jnp.where(keep, x[..., None, :, :], 0.0).reshape(*lead, n * r, width)


def _rope_table(pos):
    half = ROPE_DIM // 2
    n_pos = pos.shape[0]
    inv_freq = ROPE_THETA ** (-jnp.arange(half, dtype=F32) / half)
    ang = pos.astype(F32)[:, None] * inv_freq[None, :]
    cos, sin = jnp.cos(ang), jnp.sin(ang)
    ones = jnp.ones((n_pos, HEAD_DIM - ROPE_DIM), F32)
    zeros = jnp.zeros((n_pos, HEAD_DIM - half), F32)
    c_tab = jnp.concatenate([cos, cos, ones], axis=1)
    lo_tab = jnp.concatenate([-sin, zeros], axis=1)
    hi_tab = jnp.concatenate([jnp.zeros((n_pos, half), F32), sin, jnp.zeros((n_pos, HEAD_DIM - ROPE_DIM), F32)],
                             axis=1)
    tab = jnp.stack([c_tab, lo_tab, hi_tab])
    return jnp.concatenate([tab, tab], axis=2)


def kernel(x_prompt, x_sample, cache_k, cache_v, state_ssm_re, state_ssm_im, c_prompt, c_sample, w_mod, b_mod, norm1_g, norm2_g, w_in, ssm_a_re, ssm_a_im, ssm_log_dt, ssm_b_re, ssm_b_im, ssm_c_re, ssm_c_im, ssm_d, w_glu, b_glu, q_norm_g, k_norm_g, attn_sink, w_gate, b_gate, w_proj_ssm, w_proj_attn, w_out, w_ffn_gate, w_ffn_up, w_ffn_down):
    bp, tp, _ = x_prompt.shape
    bs, ts, _ = x_sample.shape
    assert ts == CHUNK and tp % (ATT_CPS * CHUNK) == 0 and bs % ATT_CPS == 0
    n_p, n_s = bp * tp, bs * ts
    tm = 1024
    x = jnp.concatenate([x_prompt.reshape(n_p, D_MODEL), x_sample.reshape(n_s, D_MODEL)], axis=0)

    n_cond = bp + bs
    pad = (-n_cond) % SUBLANES
    c_all = jnp.concatenate([c_prompt, c_sample, jnp.zeros((pad, D_MODEL), F32)], axis=0)
    mod = _mod_call(c_all, w_mod, b_mod)
    n_mod = mod.shape[2]

    def per_block(m, n_seq, t_len):
        reps = t_len // MOD_BLOCK
        return jnp.broadcast_to(m[:, :, None, :], (DEPTH, n_seq, reps, n_mod)).reshape(DEPTH, n_seq * reps, n_mod)

    modx = jnp.concatenate([per_block(mod[:, :bp], bp, tp), per_block(mod[:, bp:n_cond], bs, ts)],
                           axis=1)

    g, p = N_SSM_GROUPS, SSM_STATE
    gpc = g // SSM_KC
    pw_re, pw_im, bb_re, bb_im = _s5_prep_call(ssm_a_re, ssm_a_im, ssm_log_dt,
                                               ssm_b_re.transpose(0, 3, 1, 2), ssm_b_im.transpose(0, 3, 1, 2))
    pw_re = pw_re.reshape(DEPTH, len(S5_POWERS), STATE_W)
    pw_im = pw_im.reshape(DEPTH, len(S5_POWERS), STATE_W)

    def wb_blocks(bb):
        per_chunk = bb.reshape(DEPTH, SSM_GROUP, SSM_KC, gpc * p).transpose(0, 2, 1, 3)
        return _block_diag_rows(per_chunk, gpc).astype(BF16)

    def wc_blocks(cc):
        per_chunk = cc.reshape(DEPTH, SSM_KC, gpc, SSM_GROUP, p).transpose(0, 1, 3, 2, 4)
        transposed = _block_diag_rows(per_chunk.reshape(DEPTH, SSM_KC, SSM_GROUP, gpc * p), gpc)
        return jnp.swapaxes(transposed.astype(BF16), -1, -2)

    ssm_w = (wb_blocks(bb_re), wb_blocks(bb_im), pw_re, pw_im, wc_blocks(ssm_c_re), wc_blocks(ssm_c_im), ssm_d,
             w_glu.astype(BF16), b_glu)
    zeros_state = jnp.zeros((bp, 1, STATE_W), F32)
    h0_re = state_ssm_re.reshape(DEPTH, bs, 1, STATE_W)
    h0_im = state_ssm_im.reshape(DEPTH, bs, 1, STATE_W)

    pos = jnp.concatenate([jnp.tile(jnp.arange(tp), bp), jnp.tile(PAST_LEN + jnp.arange(ts), bs)])
    rope_tab = _rope_table(pos)
    gq = jnp.tile(q_norm_g, (1, LANES // HEAD_DIM)).reshape(DEPTH, 1, LANES)
    gk = jnp.tile(k_norm_g, (1, LANES // HEAD_DIM)).reshape(DEPTH, 1, LANES)
    cache_k2 = cache_k.reshape(DEPTH, bs * WINDOW, D_KV)
    cache_v2 = cache_v.reshape(DEPTH, bs * WINDOW, D_KV)

    w_in_b, w_gate_b = w_in.astype(BF16), w_gate.astype(BF16)
    w_ps_b, w_pa_b, w_out_b = w_proj_ssm.astype(BF16), w_proj_attn.astype(BF16), w_out.astype(BF16)
    w_fg_b, w_fu_b, w_fd_b = w_ffn_gate.astype(BF16), w_ffn_up.astype(BF16), w_ffn_down.astype(BF16)
    v0 = D_SSM + D_ATTN + D_KV

    def heads(t):
        return t.reshape(*t.shape[:-1], N_KV_HEADS, HEAD_DIM)

    def last_window(t, col0, col1):
        return jnp.stack([t[(b + 1) * tp - WINDOW:(b + 1) * tp, col0:col1] for b in range(bp)])

    outs = {k: [] for k in ("pk", "pv", "pre", "pim", "sk", "sv", "sre", "sim")}
    for l in range(DEPTH):
        proj, gates = _in_gate_call(x, l, norm1_g, modx, w_in_b, w_gate_b, b_gate, tm=tm)

        ssm_p, pre, pim = _ssm_call(proj, 0, bp, tp, 256, zeros_state, zeros_state, l, *ssm_w, name="s5_prompt")
        ssm_s, sre, sim = _ssm_call(proj, n_p, bs, ts, ts, h0_re[l], h0_im[l], l, *ssm_w, name="s5_sample")
        attn_out, kn = _attn_call(proj, l, n_p, tp, attn_sink, rope_tab, gq, gk, cache_k2, cache_v2)

        mixed = _mix_call(ssm_p, ssm_s, attn_out, gates, l, w_ps_b, w_pa_b, tm=tm, tn=1024)
        x = _resid_mm_call(mixed, l, w_out_b, x, modx, 2, tm=tm, tn=1024, name="out_proj")

        act = _ffn_up_call(x, l, norm2_g, modx, w_fg_b, w_fu_b, tm=tm)
        last = l == DEPTH - 1
        x = _resid_mm_call(act, l, w_fd_b, x, modx, 5, n_split=n_p // tm if last else None, tm=tm,
                           name="ffn_down_split" if last else "ffn_down")

        outs["pk"].append(heads(last_window(kn, 0, D_KV)))
        outs["pv"].append(heads(last_window(proj, v0, IN_WIDTH)))
        outs["pre"].append(pre.reshape(bp, g, p))
        outs["pim"].append(pim.reshape(bp, g, p))
        outs["sk"].append(jnp.concatenate([cache_k[l][:, ts:], heads(kn[n_p:].reshape(bs, ts, D_KV))], axis=1))
        outs["sv"].append(jnp.concatenate([cache_v[l][:, ts:], heads(proj[n_p:, v0:].reshape(bs, ts, D_KV))],
                                          axis=1))
        outs["sre"].append(sre.reshape(bs, g, p))
        outs["sim"].append(sim.reshape(bs, g, p))

    y_p, y_s = x
    return (y_p.reshape(bp, tp, D_MODEL), y_s.reshape(bs, ts, D_MODEL),
            jnp.stack(outs["pk"]), jnp.stack(outs["pv"]), jnp.stack(outs["pre"]), jnp.stack(outs["pim"]),
            jnp.stack(outs["sk"]), jnp.stack(outs["sv"]), jnp.stack(outs["sre"]), jnp.stack(outs["sim"]))
```

```python
import functools
import math

import jax
import jax.numpy as jnp
from jax import lax
from jax.experimental import pallas as pl
from jax.experimental.pallas import tpu as pltpu

D_MODEL = 2048
DEPTH = 4
CHUNK = 64
D_SSM = 1024
SSM_GROUP = 16
N_SSM_GROUPS = 64
SSM_STATE = 64
HEAD_DIM = 64
N_HEADS = 16
N_KV_HEADS = 4
KV_REP = N_HEADS // N_KV_HEADS
D_ATTN = N_HEADS * HEAD_DIM
D_KV = N_KV_HEADS * HEAD_DIM
IN_WIDTH = D_SSM + D_ATTN + 2 * D_KV
WINDOW = 128
ROPE_DIM = 16
ROPE_THETA = 500000.0
D_FF = 5632
EPS = 1e-6
PAST_LEN = 2048

LANES = 128
SUBLANES = 8
MOD_BLOCK = CHUNK
STATE_W = N_SSM_GROUPS * SSM_STATE
SSM_KC = 4
VMEM_LIMIT = 56 * 1024 * 1024

F32 = jnp.float32
BF16 = jnp.bfloat16


def _cparams(sem):
    return pltpu.CompilerParams(dimension_semantics=sem, vmem_limit_bytes=VMEM_LIMIT)


def _sigmoid(x):
    return 0.5 + 0.5 * jnp.tanh(0.5 * x)


def _layer_spec(l, shape):
    zeros = (0,) * len(shape)
    return pl.BlockSpec((None, *shape), lambda *_: (l, *zeros))


def _mod_kernel(c_ref, w_ref, b_ref, o_ref):
    c = c_ref[...].astype(BF16)
    w = w_ref[...].astype(BF16)
    o_ref[...] = jnp.dot(c, w, preferred_element_type=F32) + b_ref[...]


def _mod_call(c_all, w_mod, b_mod):
    nb = c_all.shape[0]
    tn = 1024
    n_out = w_mod.shape[2]
    return pl.pallas_call(
        _mod_kernel,
        out_shape=jax.ShapeDtypeStruct((DEPTH, nb, n_out), F32),
        grid=(DEPTH, n_out // tn),
        in_specs=[
            pl.BlockSpec((nb, D_MODEL), lambda l, j: (0, 0)),
            pl.BlockSpec((None, D_MODEL, tn), lambda l, j: (l, 0, j)),
            pl.BlockSpec((None, 1, tn), lambda l, j: (l, 0, j)),
        ],
        out_specs=pl.BlockSpec((None, nb, tn), lambda l, j: (l, 0, j)),
        compiler_params=_cparams(("arbitrary", "arbitrary")),
        name="adaln_mod",
    )(c_all, w_mod, b_mod.reshape(DEPTH, 1, n_out))


S5_TILES = (256, 64)
S5_SEG = SUBLANES
S5_POWERS = tuple(sorted({n for t in S5_TILES for n in (*range(1, t // S5_SEG + 1),
                                                         *(k * t // S5_SEG for k in (1, 2, 4)))}))


def _s5_prep_kernel(are_ref, aim_ref, ldt_ref, bre_ref, bim_ref, pwr_ref, pwi_ref, bbr_ref, bbi_ref):
    a_re = are_ref[...]
    a_im = aim_ref[...]
    dt = jnp.exp(ldt_ref[...])
    z_re = a_re * dt
    z_im = a_im * dt
    for i, n in enumerate(S5_POWERS):
        mag = jnp.exp(z_re * float(n))
        pwr_ref[i] = mag * jnp.cos(z_im * float(n))
        pwi_ref[i] = mag * jnp.sin(z_im * float(n))
    l_re = pwr_ref[S5_POWERS.index(1)]
    l_im = pwi_ref[S5_POWERS.index(1)]
    den = a_re * a_re + a_im * a_im
    n_re = l_re - 1.0
    f_re = (n_re * a_re + l_im * a_im) / den
    f_im = (l_im * a_re - n_re * a_im) / den
    for c in range(SSM_GROUP):
        b_re = bre_ref[c]
        b_im = bim_ref[c]
        bbr_ref[c] = f_re * b_re - f_im * b_im
        bbi_ref[c] = f_re * b_im + f_im * b_re


def _s5_prep_call(a_re, a_im, log_dt, bt_re, bt_im):
    g, p = N_SSM_GROUPS, SSM_STATE
    mat = pl.BlockSpec((None, g, p), lambda l: (l, 0, 0))
    stack_c = pl.BlockSpec((None, SSM_GROUP, g, p), lambda l: (l, 0, 0, 0))
    n_pow = len(S5_POWERS)
    stack_n = pl.BlockSpec((None, n_pow, g, p), lambda l: (l, 0, 0, 0))
    return pl.pallas_call(
        _s5_prep_kernel,
        out_shape=(jax.ShapeDtypeStruct((DEPTH, n_pow, g, p), F32),
                   jax.ShapeDtypeStruct((DEPTH, n_pow, g, p), F32),
                   jax.ShapeDtypeStruct((DEPTH, SSM_GROUP, g, p), F32),
                   jax.ShapeDtypeStruct((DEPTH, SSM_GROUP, g, p), F32)),
        grid=(DEPTH,),
        in_specs=[mat, mat, pl.BlockSpec((None, g, 1), lambda l: (l, 0, 0)), stack_c, stack_c],
        out_specs=(stack_n, stack_n, stack_c, stack_c),
        compiler_params=_cparams(("arbitrary",)),
        name="s5_discretize",
    )(a_re, a_im, log_dt.reshape(DEPTH, g, 1), bt_re, bt_im)


def _norm_mod_tile(x_ref, g_ref, sc_ref, sh_ref, h_scr, nsub):
    gain = g_ref[...]
    for s in range(nsub):
        rows = slice(s * MOD_BLOCK, (s + 1) * MOD_BLOCK)
        x = x_ref[rows, :]
        ms = jnp.mean(x * x, axis=-1, keepdims=True)
        y = x * lax.rsqrt(ms + EPS) * gain
        h = y * (1.0 + sc_ref[s:s + 1, :]) + sh_ref[s:s + 1, :]
        h_scr[rows, :] = h.astype(BF16)


def _norm_specs(l, tm, sc_idx, sh_idx):
    nsub = tm // MOD_BLOCK
    return [
        pl.BlockSpec((tm, D_MODEL), lambda i, j: (i, 0)),
        _layer_spec(l, (1, D_MODEL)),
        pl.BlockSpec((None, nsub, D_MODEL), lambda i, j: (l, i, sc_idx)),
        pl.BlockSpec((None, nsub, D_MODEL), lambda i, j: (l, i, sh_idx)),
    ]


def _in_gate_kernel(x_ref, g_ref, sc_ref, sh_ref, wi_ref, wg_ref, b_ref, proj_ref, gate_ref, h_scr, *, nsub, n_in):
    j = pl.program_id(1)
    pl.when(j == 0)(lambda: _norm_mod_tile(x_ref, g_ref, sc_ref, sh_ref, h_scr, nsub))

    @pl.when(j < n_in)
    def _():
        proj_ref[...] = jnp.dot(h_scr[...], wi_ref[...], preferred_element_type=F32)

    @pl.when(j >= n_in)
    def _():
        acc = jnp.dot(h_scr[...], wg_ref[...], preferred_element_type=F32)
        gate_ref[...] = _sigmoid(acc + b_ref[...])


def _in_gate_call(x, l, gain, modx, w_in, w_gate, b_gate, *, tm=1024, tn=512):
    n_tok = x.shape[0]
    n_in, n_gate = w_in.shape[2] // tn, w_gate.shape[2] // tn
    in_col = lambda i, j: jnp.minimum(j, n_in - 1)
    gate_col = lambda i, j: jnp.maximum(j - n_in, 0)
    return pl.pallas_call(
        functools.partial(_in_gate_kernel, nsub=tm // MOD_BLOCK, n_in=n_in),
        out_shape=(jax.ShapeDtypeStruct((n_tok, w_in.shape[2]), F32),
                   jax.ShapeDtypeStruct((n_tok, w_gate.shape[2]), F32)),
        grid=(n_tok // tm, n_in + n_gate),
        in_specs=_norm_specs(l, tm, 1, 0) + [
            pl.BlockSpec((None, D_MODEL, tn), lambda i, j: (l, 0, in_col(i, j))),
            pl.BlockSpec((None, D_MODEL, tn), lambda i, j: (l, 0, gate_col(i, j))),
            pl.BlockSpec((None, 1, tn), lambda i, j: (l, 0, gate_col(i, j))),
        ],
        out_specs=(pl.BlockSpec((tm, tn), lambda i, j: (i, in_col(i, j))),
                   pl.BlockSpec((tm, tn), lambda i, j: (i, gate_col(i, j)))),
        scratch_shapes=[pltpu.VMEM((tm, D_MODEL), BF16)],
        compiler_params=_cparams(("arbitrary", "arbitrary")),
        name="in_proj_gates",
    )(x, gain.reshape(DEPTH, 1, D_MODEL), modx, modx, w_in, w_gate, b_gate.reshape(DEPTH, 1, w_gate.shape[2]))


def _ffn_up_kernel(x_ref, g_ref, sc_ref, sh_ref, wg_ref, wu_ref, o_ref, h_scr, *, nsub):
    pl.when(pl.program_id(1) == 0)(lambda: _norm_mod_tile(x_ref, g_ref, sc_ref, sh_ref, h_scr, nsub))
    h = h_scr[...]
    gate = jnp.dot(h, wg_ref[...], preferred_element_type=F32)
    up = jnp.dot(h, wu_ref[...], preferred_element_type=F32)
    o_ref[...] = (gate * _sigmoid(gate) * up).astype(o_ref.dtype)


def _ffn_up_call(x, l, gain, modx, w_gate, w_up, *, tm=1024, tn=512):
    n_tok = x.shape[0]
    n_out = w_gate.shape[2]
    w_spec = pl.BlockSpec((None, D_MODEL, tn), lambda i, j: (l, 0, j))
    return pl.pallas_call(
        functools.partial(_ffn_up_kernel, nsub=tm // MOD_BLOCK),
        out_shape=jax.ShapeDtypeStruct((n_tok, n_out), BF16),
        grid=(n_tok // tm, n_out // tn),
        in_specs=_norm_specs(l, tm, 4, 3) + [w_spec, w_spec],
        out_specs=pl.BlockSpec((tm, tn), lambda i, j: (i, j)),
        scratch_shapes=[pltpu.VMEM((tm, D_MODEL), BF16)],
        compiler_params=_cparams(("arbitrary", "arbitrary")),
        name="ffn_up",
    )(x, gain.reshape(DEPTH, 1, D_MODEL), modx, modx, w_gate, w_up)


def _mix_kernel(sp_ref, ss_ref, a_ref, ga_ref, gb_ref, ws_ref, wa_ref, o_ref, *, n_split):
    def body(s_ref):
        ps = jnp.dot(s_ref[...], ws_ref[...], preferred_element_type=F32)
        pa = jnp.dot(a_ref[...], wa_ref[...], preferred_element_type=F32)
        o_ref[...] = (ga_ref[...] * ps + gb_ref[...] * pa).astype(o_ref.dtype)

    i = pl.program_id(0)
    pl.when(i < n_split)(lambda: body(sp_ref))
    pl.when(i >= n_split)(lambda: body(ss_ref))


def _mix_call(ssm_p, ssm_s, attn_out, gates, l, w_ps, w_pa, *, tm=1024, tn=512):
    n_tok = attn_out.shape[0]
    nj = D_MODEL // tn
    n_split = ssm_p.shape[0] // tm
    return pl.pallas_call(
        functools.partial(_mix_kernel, n_split=n_split),
        out_shape=jax.ShapeDtypeStruct((n_tok, D_MODEL), BF16),
        grid=(n_tok // tm, nj),
        in_specs=[
            pl.BlockSpec((tm, D_SSM), lambda i, j: (jnp.minimum(i, n_split - 1), 0)),
            pl.BlockSpec((tm, D_SSM), lambda i, j: (jnp.maximum(i - n_split, 0), 0)),
            pl.BlockSpec((tm, D_ATTN), lambda i, j: (i, 0)),
            pl.BlockSpec((tm, tn), lambda i, j: (i, j)),
            pl.BlockSpec((tm, tn), lambda i, j: (i, j + nj)),
            pl.BlockSpec((None, D_SSM, tn), lambda i, j: (l, 0, j)),
            pl.BlockSpec((None, D_ATTN, tn), lambda i, j: (l, 0, j)),
        ],
        out_specs=pl.BlockSpec((tm, tn), lambda i, j: (i, j)),
        compiler_params=_cparams(("arbitrary", "arbitrary")),
        name="branch_merge",
    )(ssm_p, ssm_s, attn_out, gates, gates, w_ps, w_pa)


def _resid_mm_kernel(a_ref, w_ref, x_ref, g_ref, *o_refs, nsub, n_split):
    acc = jnp.dot(a_ref[...], w_ref[...], preferred_element_type=F32)

    def write(o_ref):
        for s in range(nsub):
            rows = slice(s * MOD_BLOCK, (s + 1) * MOD_BLOCK)
            o_ref[rows, :] = x_ref[rows, :] + g_ref[s:s + 1, :] * acc[rows, :]

    if n_split is None:
        write(o_refs[0])
    else:
        i = pl.program_id(0)
        pl.when(i < n_split)(lambda: write(o_refs[0]))
        pl.when(i >= n_split)(lambda: write(o_refs[1]))


def _resid_mm_call(a, l, w, x, modx, g_idx, *, n_split=None, tm=1024, tn=512, name):
    n_tok, k = a.shape
    nsub = tm // MOD_BLOCK
    nj = D_MODEL // tn
    n_tiles = n_tok // tm
    if n_split is None:
        out_shape = jax.ShapeDtypeStruct((n_tok, D_MODEL), F32)
        out_specs = pl.BlockSpec((tm, tn), lambda i, j: (i, j))
    else:
        out_shape = (jax.ShapeDtypeStruct((n_split * tm, D_MODEL), F32),
                     jax.ShapeDtypeStruct(((n_tiles - n_split) * tm, D_MODEL), F32))
        out_specs = (
            pl.BlockSpec((tm, tn), lambda i, j: (jnp.minimum(i, n_split - 1), jnp.where(i < n_split, j, nj - 1))),
            pl.BlockSpec((tm, tn), lambda i, j: (jnp.maximum(i - n_split, 0), jnp.where(i < n_split, 0, j))),
        )
    return pl.pallas_call(
        functools.partial(_resid_mm_kernel, nsub=nsub, n_split=n_split),
        out_shape=out_shape,
        grid=(n_tiles, nj),
        in_specs=[
            pl.BlockSpec((tm, k), lambda i, j: (i, 0)),
            pl.BlockSpec((None, k, tn), lambda i, j: (l, 0, j)),
            pl.BlockSpec((tm, tn), lambda i, j: (i, j)),
            pl.BlockSpec((None, nsub, tn), lambda i, j: (l, i, g_idx * nj + j)),
        ],
        out_specs=out_specs,
        compiler_params=_cparams(("arbitrary", "arbitrary")),
        name=name,
    )(a, w, x, modx)


SCAN_LW = 512


def _ssm_kernel(u_ref, h0r_ref, h0i_ref, wbr_ref, wbi_ref, pwj_ref, seg_ref, wcr_ref, wci_ref, d_ref, wglu_ref,
                bglu_ref, o_ref, htr_ref, hti_ref, bur, bui, cre, cim, z_scr, *, tt):
    steps = tt // S5_SEG

    @pl.when(pl.program_id(1) == 0)
    def _():
        cre[...] = jnp.broadcast_to(h0r_ref[...], (SUBLANES, STATE_W))
        cim[...] = jnp.broadcast_to(h0i_ref[...], (SUBLANES, STATE_W))

    r_idx = lax.broadcasted_iota(jnp.int32, (tt, tt), 0)
    c_idx = lax.broadcasted_iota(jnp.int32, (tt, tt), 1)
    to_scan = jnp.where(c_idx == (r_idx % S5_SEG) * steps + r_idx // S5_SEG, 1.0, 0.0).astype(BF16)
    to_time = jnp.where(r_idx == (c_idx % S5_SEG) * steps + c_idx // S5_SEG, 1.0, 0.0).astype(BF16)

    u = u_ref[...]
    us = jnp.dot(to_scan, u.astype(BF16), preferred_element_type=F32).astype(BF16)
    kw = D_SSM // SSM_KC
    sw = STATE_W // SSM_KC
    for kc in range(SSM_KC):
        uk = us[:, kc * kw:(kc + 1) * kw]
        bur[:, kc * sw:(kc + 1) * sw] = jnp.dot(uk, wbr_ref[kc], preferred_element_type=F32)
        bui[:, kc * sw:(kc + 1) * sw] = jnp.dot(uk, wbi_ref[kc], preferred_element_type=F32)

    row = lax.broadcasted_iota(jnp.int32, (SUBLANES, SCAN_LW), 0)
    for lc in range(STATE_W // SCAN_LW):
        sl = slice(lc * SCAN_LW, (lc + 1) * SCAN_LW)
        lam_re = jnp.broadcast_to(pwj_ref[0, 0:1, sl], (SUBLANES, SCAN_LW))
        lam_im = jnp.broadcast_to(pwj_ref[1, 0:1, sl], (SUBLANES, SCAN_LW))

        def local(j, carry, sl=sl, lam_re=lam_re, lam_im=lam_im):
            h_re, h_im = carry
            rows = pl.ds(pl.multiple_of(j * SUBLANES, SUBLANES), SUBLANES)
            h_re, h_im = (lam_re * h_re - lam_im * h_im + bur[rows, sl],
                          lam_re * h_im + lam_im * h_re + bui[rows, sl])
            bur[rows, sl] = h_re
            bui[rows, sl] = h_im
            return h_re, h_im

        zero = jnp.zeros((SUBLANES, SCAN_LW), F32)
        e_re, e_im = lax.fori_loop(0, steps, local, (zero, zero), unroll=2)

        x_re = jnp.where(row == 0, cre[:, sl], pltpu.roll(e_re, 1, 0))
        x_im = jnp.where(row == 0, cim[:, sl], pltpu.roll(e_im, 1, 0))
        for idx, k in enumerate((1, 2, 4)):
            m_re = seg_ref[2 * idx, :, sl]
            m_im = seg_ref[2 * idx + 1, :, sl]
            s_re = pltpu.roll(x_re, k, 0)
            s_im = pltpu.roll(x_im, k, 0)
            x_re, x_im = (x_re + m_re * s_re - m_im * s_im,
                          x_im + m_re * s_im + m_im * s_re)
        pj_re = jnp.broadcast_to(pwj_ref[0, steps - 1:steps, sl], (SUBLANES, SCAN_LW))
        pj_im = jnp.broadcast_to(pwj_ref[1, steps - 1:steps, sl], (SUBLANES, SCAN_LW))
        n_re = pj_re * x_re - pj_im * x_im + e_re
        n_im = pj_re * x_im + pj_im * x_re + e_im
        cre[:, sl] = jnp.broadcast_to(n_re[SUBLANES - 1:SUBLANES, :], (SUBLANES, SCAN_LW))
        cim[:, sl] = jnp.broadcast_to(n_im[SUBLANES - 1:SUBLANES, :], (SUBLANES, SCAN_LW))

        def stitch(j, carry, sl=sl, c_re=x_re, c_im=x_im):
            rows = pl.ds(pl.multiple_of(j * SUBLANES, SUBLANES), SUBLANES)
            p_re = jnp.broadcast_to(pwj_ref[0, pl.ds(j, 1), sl], (SUBLANES, SCAN_LW))
            p_im = jnp.broadcast_to(pwj_ref[1, pl.ds(j, 1), sl], (SUBLANES, SCAN_LW))
            bur[rows, sl] = bur[rows, sl] + p_re * c_re - p_im * c_im
            bui[rows, sl] = bui[rows, sl] + p_re * c_im + p_im * c_re
            return carry

        lax.fori_loop(0, steps, stitch, 0, unroll=2)

    htr_ref[...] = cre[0:1, :]
    hti_ref[...] = cim[0:1, :]

    d = d_ref[...]
    for kc in range(SSM_KC):
        h_re = bur[:, kc * sw:(kc + 1) * sw].astype(BF16)
        h_im = bui[:, kc * sw:(kc + 1) * sw].astype(BF16)
        ys = (jnp.dot(h_re, wcr_ref[kc], preferred_element_type=F32)
              - jnp.dot(h_im, wci_ref[kc], preferred_element_type=F32))
        ys_hi = ys.astype(BF16)
        ys_lo = (ys - ys_hi.astype(F32)).astype(BF16)
        y = (jnp.dot(to_time, ys_hi, preferred_element_type=F32)
             + jnp.dot(to_time, ys_lo, preferred_element_type=F32))
        cols = slice(kc * kw, (kc + 1) * kw)
        y = y + d[:, cols] * u[:, cols]
        z_scr[:, cols] = 0.5 * y * (1.0 + jnp.tanh(math.sqrt(2.0 / math.pi) * (y + 0.044715 * (y * y * y))))
    z = z_scr[...]
    gate = _sigmoid(jnp.dot(z.astype(BF16), wglu_ref[...], preferred_element_type=F32) + bglu_ref[...])
    o_ref[...] = (z * gate).astype(o_ref.dtype)


def _ssm_call(proj, row0, n_seq, t_len, tt, h0_re, h0_im, l, wb_re, wb_im, pw_re, pw_im, wc_re, wc_im, d_skip,
              w_glu, b_glu, *, name):
    steps = tt // S5_SEG
    power = lambda n: S5_POWERS.index(n)
    pwj = jnp.stack([pw[:, power(1):power(steps) + 1] for pw in (pw_re, pw_im)], axis=1)
    seg_rows = []
    for k in (1, 2, 4):
        keep = (jnp.arange(SUBLANES) >= k)[None, :, None]
        seg_rows += [jnp.where(keep, pw[:, power(k * steps)][:, None, :], 0.0) for pw in (pw_re, pw_im)]
    seg = jnp.stack(seg_rows, axis=1)
    nt = t_len // tt
    rb0 = row0 // tt
    state_spec = pl.BlockSpec((None, 1, STATE_W), lambda b, t: (b, 0, 0))
    return pl.pallas_call(
        functools.partial(_ssm_kernel, tt=tt),
        out_shape=(jax.ShapeDtypeStruct((n_seq * t_len, D_SSM), BF16),
                   jax.ShapeDtypeStruct((n_seq, 1, STATE_W), F32),
                   jax.ShapeDtypeStruct((n_seq, 1, STATE_W), F32)),
        grid=(n_seq, nt),
        in_specs=[
            pl.BlockSpec((tt, D_SSM), lambda b, t: (rb0 + b * nt + t, 0)),
            state_spec, state_spec,
            _layer_spec(l, wb_re.shape[1:]), _layer_spec(l, wb_im.shape[1:]),
            _layer_spec(l, pwj.shape[1:]), _layer_spec(l, seg.shape[1:]),
            _layer_spec(l, wc_re.shape[1:]), _layer_spec(l, wc_im.shape[1:]),
            _layer_spec(l, (1, D_SSM)),
            _layer_spec(l, (D_SSM, D_SSM)),
            _layer_spec(l, (1, D_SSM)),
        ],
        out_specs=(pl.BlockSpec((tt, D_SSM), lambda b, t: (b * nt + t, 0)), state_spec, state_spec),
        scratch_shapes=[pltpu.VMEM((tt, STATE_W), F32), pltpu.VMEM((tt, STATE_W), F32),
                        pltpu.VMEM((SUBLANES, STATE_W), F32), pltpu.VMEM((SUBLANES, STATE_W), F32),
                        pltpu.VMEM((tt, D_SSM), F32)],
        compiler_params=_cparams(("arbitrary", "arbitrary")),
        name=name,
    )(proj, h0_re, h0_im, wb_re, wb_im, pwj, seg, wc_re, wc_im, d_skip.reshape(DEPTH, 1, D_SSM), w_glu,
      b_glu.reshape(DEPTH, 1, D_SSM))


RING = 3
N_KEYS = RING * CHUNK
ATT_CPS = 4


def _stack_blocks(x):
    return jnp.concatenate([x[:, j * LANES:(j + 1) * LANES] for j in range(x.shape[1] // LANES)], axis=0)


def _pair_norm_rope(xs, gain, tab, ones_bd):
    reps = xs.shape[0] // CHUNK
    cos, s_lo, s_hi = (jnp.concatenate([tab[i]] * reps, axis=0) for i in range(3))
    half = ROPE_DIM // 2
    sq = xs * xs
    sq_hi = sq.astype(BF16)
    sq_lo = (sq - sq_hi.astype(F32)).astype(BF16)
    ss = (jnp.dot(sq_hi, ones_bd, preferred_element_type=F32)
          + jnp.dot(sq_lo, ones_bd, preferred_element_type=F32))
    xg = xs * gain
    xr = xg * cos + pltpu.roll(xg, LANES - half, 1) * s_lo + pltpu.roll(xg, half, 1) * s_hi
    return xr * lax.rsqrt(ss * (1.0 / HEAD_DIM) + EPS)


def _attn_kernel(sink_ref, q_ref, k_ref, v_ref, ck_ref, cv_ref, tab_ref, gq_ref, gk_ref, o_ref, kn_ref, kd, vd,
                 *, l, n_prompt_steps, steps_per_seq):
    step = pl.program_id(0)
    is_sample = step >= n_prompt_steps
    c_base = lax.rem(step, steps_per_seq) * ATT_CPS
    first = lax.broadcasted_iota(jnp.int32, (1, LANES), 1) < HEAD_DIM
    ones_bd = jnp.where(lax.broadcasted_iota(jnp.int32, (LANES, LANES), 0) // HEAD_DIM
                        == lax.broadcasted_iota(jnp.int32, (LANES, LANES), 1) // HEAD_DIM, 1.0, 0.0).astype(BF16)
    key_slot = lax.broadcasted_iota(jnp.int32, (1, N_KEYS), 1) // CHUNK
    gq = gq_ref[...]
    gk = gk_ref[...]

    def store_dup(dst, x, rows):
        for pair in range(D_KV // LANES):
            blk = x[:, pair * LANES:(pair + 1) * LANES]
            swapped = pltpu.roll(blk, HEAD_DIM, 1)
            dst[2 * pair, rows, 0:LANES] = jnp.where(first, blk, swapped).astype(BF16)
            dst[2 * pair + 1, rows, 0:LANES] = jnp.where(first, swapped, blk).astype(BF16)

    @pl.when(step == 0)
    def _():
        vd[:, :, LANES:] = jnp.ones((N_KV_HEADS, N_KEYS, LANES), BF16)

    def chunk(ci, carry):
        c = jnp.where(is_sample, 0, c_base + ci)
        rows = pl.ds(pl.multiple_of(ci * CHUNK, CHUNK), CHUNK)
        prev = slice(CHUNK, N_KEYS)

        @pl.when(jnp.logical_and(c == 0, jnp.logical_not(is_sample)))
        def _():
            kd[:, prev, :] = jnp.zeros((N_KV_HEADS, N_KEYS - CHUNK, LANES), BF16)
            vd[:, prev, 0:LANES] = jnp.zeros((N_KV_HEADS, N_KEYS - CHUNK, LANES), BF16)

        @pl.when(is_sample)
        def _():
            cached = pl.ds(pl.multiple_of(ci * WINDOW, WINDOW), WINDOW)
            store_dup(kd, ck_ref[cached, :], prev)
            store_dup(vd, cv_ref[cached, :], prev)

        tab = tab_ref[:, rows, :]
        own = pl.ds(pl.multiple_of(lax.rem(c, RING) * CHUNK, CHUNK), CHUNK)
        kn = _pair_norm_rope(_stack_blocks(k_ref[rows, :]), gk, tab, ones_bd)
        kn = jnp.concatenate([kn[0:CHUNK], kn[CHUNK:2 * CHUNK]], axis=1)
        kn_ref[rows, :] = kn
        store_dup(kd, kn, own)
        store_dup(vd, v_ref[rows, :], own)

        qn = _pair_norm_rope(_stack_blocks(q_ref[rows, :]), gq, tab, ones_bd) * (HEAD_DIM ** -0.5)
        q_lo = jnp.where(first, qn, 0.0).astype(BF16)
        q_hi = jnp.where(first, 0.0, qn).astype(BF16)

        valid = key_slot <= jnp.where(is_sample, RING, c)
        scores = []
        for kh in range(N_KV_HEADS):
            r0 = kh * 2 * CHUNK
            lhs = jnp.concatenate([q_lo[r0:r0 + CHUNK], q_hi[r0:r0 + CHUNK],
                                   q_lo[r0 + CHUNK:r0 + 2 * CHUNK], q_hi[r0 + CHUNK:r0 + 2 * CHUNK]], axis=0)
            scores.append(lax.dot_general(lhs, kd[kh], (((1,), (1,)), ((), ())),
                                          preferred_element_type=F32))
        for kh in range(N_KV_HEADS):
            s = jnp.where(valid, scores[kh], -1e30)
            sink = jnp.concatenate([jnp.full((CHUNK, 1), sink_ref[l, kh * KV_REP + r], F32)
                                    for r in range(KV_REP)], axis=0)
            m = jnp.maximum(jnp.max(s, axis=-1, keepdims=True), sink)
            p = jnp.exp(s - m).astype(BF16)
            od = jnp.dot(p, vd[kh], preferred_element_type=F32)
            o = od[:, 0:LANES] / (od[:, LANES:] + jnp.exp(sink - m))
            for pair in range(2):
                half = pair * 2 * CHUNK
                blk = jnp.where(first, o[half:half + CHUNK], o[half + CHUNK:half + 2 * CHUNK])
                col = (2 * kh + pair) * LANES
                o_ref[rows, col:col + LANES] = blk.astype(o_ref.dtype)
        return carry

    lax.fori_loop(0, ATT_CPS, chunk, 0)


def _attn_call(proj, l, n_prompt, t_prompt, sink, rope_tab, gq, gk, cache_k, cache_v):
    n_tok = proj.shape[0]
    tr = ATT_CPS * CHUNK
    qcol = D_SSM // D_ATTN
    kcol = (D_SSM + D_ATTN) // D_KV
    n_prompt_steps = n_prompt // tr
    cache_spec = pl.BlockSpec((None, ATT_CPS * WINDOW, D_KV),
                              lambda s: (l, jnp.maximum(s - n_prompt_steps, 0), 0))
    return pl.pallas_call(
        functools.partial(_attn_kernel, l=l, n_prompt_steps=n_prompt_steps, steps_per_seq=t_prompt // tr),
        out_shape=(jax.ShapeDtypeStruct((n_tok, D_ATTN), BF16),
                   jax.ShapeDtypeStruct((n_tok, D_KV), F32)),
        grid=(n_tok // tr,),
        in_specs=[
            pl.BlockSpec(memory_space=pltpu.SMEM),
            pl.BlockSpec((tr, D_ATTN), lambda s: (s, qcol)),
            pl.BlockSpec((tr, D_KV), lambda s: (s, kcol)),
            pl.BlockSpec((tr, D_KV), lambda s: (s, kcol + 1)),
            cache_spec, cache_spec,
            pl.BlockSpec((3, tr, LANES), lambda s: (0, s, 0)),
            _layer_spec(l, (1, LANES)), _layer_spec(l, (1, LANES)),
        ],
        out_specs=(pl.BlockSpec((tr, D_ATTN), lambda s: (s, 0)),
                   pl.BlockSpec((tr, D_KV), lambda s: (s, 0))),
        scratch_shapes=[pltpu.VMEM((N_KV_HEADS, N_KEYS, LANES), BF16),
                        pltpu.VMEM((N_KV_HEADS, N_KEYS, 2 * LANES), BF16)],
        compiler_params=_cparams(("arbitrary",)),
        name="banded_attn",
    )(sink, proj, proj, proj, cache_k, cache_v, rope_tab, gq, gk)


def _block_diag_rows(x, n):
    *lead, r, width = x.shape
    keep = jnp.arange(n)[:, None, None] == (jnp.arange(width) // (width // n))[None, None, :]
    return jnp.where(keep, x[..., None, :, :], 0.0).reshape(*lead, n * r, width)


def _rope_table(pos):
    half = ROPE_DIM // 2
    n_pos = pos.shape[0]
    inv_freq = ROPE_THETA ** (-jnp.arange(half, dtype=F32) / half)
    ang = pos.astype(F32)[:, None] * inv_freq[None, :]
    cos, sin = jnp.cos(ang), jnp.sin(ang)
    ones = jnp.ones((n_pos, HEAD_DIM - ROPE_DIM), F32)
    zeros = jnp.zeros((n_pos, HEAD_DIM - half), F32)
    c_tab = jnp.concatenate([cos, cos, ones], axis=1)
    lo_tab = jnp.concatenate([-sin, zeros], axis=1)
    hi_tab = jnp.concatenate([jnp.zeros((n_pos, half), F32), sin, jnp.zeros((n_pos, HEAD_DIM - ROPE_DIM), F32)],
                             axis=1)
    tab = jnp.stack([c_tab, lo_tab, hi_tab])
    return jnp.concatenate([tab, tab], axis=2)


def kernel(x_prompt, x_sample, cache_k, cache_v, state_ssm_re, state_ssm_im, c_prompt, c_sample, w_mod, b_mod, norm1_g, norm2_g, w_in, ssm_a_re, ssm_a_im, ssm_log_dt, ssm_b_re, ssm_b_im, ssm_c_re, ssm_c_im, ssm_d, w_glu, b_glu, q_norm_g, k_norm_g, attn_sink, w_gate, b_gate, w_proj_ssm, w_proj_attn, w_out, w_ffn_gate, w_ffn_up, w_ffn_down):
    bp, tp, _ = x_prompt.shape
    bs, ts, _ = x_sample.shape
    assert ts == CHUNK and tp % (ATT_CPS * CHUNK) == 0 and bs % ATT_CPS == 0
    n_p, n_s = bp * tp, bs * ts
    tm = 1024
    x = jnp.concatenate([x_prompt.reshape(n_p, D_MODEL), x_sample.reshape(n_s, D_MODEL)], axis=0)

    n_cond = bp + bs
    pad = (-n_cond) % SUBLANES
    c_all = jnp.concatenate([c_prompt, c_sample, jnp.zeros((pad, D_MODEL), F32)], axis=0)
    mod = _mod_call(c_all, w_mod, b_mod)
    n_mod = mod.shape[2]

    def per_block(m, n_seq, t_len):
        reps = t_len // MOD_BLOCK
        return jnp.broadcast_to(m[:, :, None, :], (DEPTH, n_seq, reps, n_mod)).reshape(DEPTH, n_seq * reps, n_mod)

    modx = jnp.concatenate([per_block(mod[:, :bp], bp, tp), per_block(mod[:, bp:n_cond], bs, ts)],
                           axis=1)

    g, p = N_SSM_GROUPS, SSM_STATE
    gpc = g // SSM_KC
    pw_re, pw_im, bb_re, bb_im = _s5_prep_call(ssm_a_re, ssm_a_im, ssm_log_dt,
                                               ssm_b_re.transpose(0, 3, 1, 2), ssm_b_im.transpose(0, 3, 1, 2))
    pw_re = pw_re.reshape(DEPTH, len(S5_POWERS), STATE_W)
    pw_im = pw_im.reshape(DEPTH, len(S5_POWERS), STATE_W)

    def wb_blocks(bb):
        per_chunk = bb.reshape(DEPTH, SSM_GROUP, SSM_KC, gpc * p).transpose(0, 2, 1, 3)
        return _block_diag_rows(per_chunk, gpc).astype(BF16)

    def wc_blocks(cc):
        per_chunk = cc.reshape(DEPTH, SSM_KC, gpc, SSM_GROUP, p).transpose(0, 1, 3, 2, 4)
        transposed = _block_diag_rows(per_chunk.reshape(DEPTH, SSM_KC, SSM_GROUP, gpc * p), gpc)
        return jnp.swapaxes(transposed.astype(BF16), -1, -2)

    ssm_w = (wb_blocks(bb_re), wb_blocks(bb_im), pw_re, pw_im, wc_blocks(ssm_c_re), wc_blocks(ssm_c_im), ssm_d,
             w_glu.astype(BF16), b_glu)
    zeros_state = jnp.zeros((bp, 1, STATE_W), F32)
    h0_re = state_ssm_re.reshape(DEPTH, bs, 1, STATE_W)
    h0_im = state_ssm_im.reshape(DEPTH, bs, 1, STATE_W)

    pos = jnp.concatenate([jnp.tile(jnp.arange(tp), bp), jnp.tile(PAST_LEN + jnp.arange(ts), bs)])
    rope_tab = _rope_table(pos)
    gq = jnp.tile(q_norm_g, (1, LANES // HEAD_DIM)).reshape(DEPTH, 1, LANES)
    gk = jnp.tile(k_norm_g, (1, LANES // HEAD_DIM)).reshape(DEPTH, 1, LANES)
    cache_k2 = cache_k.reshape(DEPTH, bs * WINDOW, D_KV)
    cache_v2 = cache_v.reshape(DEPTH, bs * WINDOW, D_KV)

    w_in_b, w_gate_b = w_in.astype(BF16), w_gate.astype(BF16)
    w_ps_b, w_pa_b, w_out_b = w_proj_ssm.astype(BF16), w_proj_attn.astype(BF16), w_out.astype(BF16)
    w_fg_b, w_fu_b, w_fd_b = w_ffn_gate.astype(BF16), w_ffn_up.astype(BF16), w_ffn_down.astype(BF16)
    v0 = D_SSM + D_ATTN + D_KV

    def heads(t):
        return t.reshape(*t.shape[:-1], N_KV_HEADS, HEAD_DIM)

    def last_window(t, col0, col1):
        return jnp.stack([t[(b + 1) * tp - WINDOW:(b + 1) * tp, col0:col1] for b in range(bp)])

    outs = {k: [] for k in ("pk", "pv", "pre", "pim", "sk", "sv", "sre", "sim")}
    for l in range(DEPTH):
        proj, gates = _in_gate_call(x, l, norm1_g, modx, w_in_b, w_gate_b, b_gate, tm=tm)

        ssm_p, pre, pim = _ssm_call(proj, 0, bp, tp, 256, zeros_state, zeros_state, l, *ssm_w, name="s5_prompt")
        ssm_s, sre, sim = _ssm_call(proj, n_p, bs, ts, ts, h0_re[l], h0_im[l], l, *ssm_w, name="s5_sample")
        attn_out, kn = _attn_call(proj, l, n_p, tp, attn_sink, rope_tab, gq, gk, cache_k2, cache_v2)

        mixed = _mix_call(ssm_p, ssm_s, attn_out, gates, l, w_ps_b, w_pa_b, tm=tm, tn=1024)
        x = _resid_mm_call(mixed, l, w_out_b, x, modx, 2, tm=tm, tn=1024, name="out_proj")

        act = _ffn_up_call(x, l, norm2_g, modx, w_fg_b, w_fu_b, tm=tm)
        last = l == DEPTH - 1
        x = _resid_mm_call(act, l, w_fd_b, x, modx, 5, n_split=n_p // tm if last else None, tm=tm,
                           name="ffn_down_split" if last else "ffn_down")

        outs["pk"].append(heads(last_window(kn, 0, D_KV)))
        outs["pv"].append(heads(last_window(proj, v0, IN_WIDTH)))
        outs["pre"].append(pre.reshape(bp, g, p))
        outs["pim"].append(pim.reshape(bp, g, p))
        outs["sk"].append(jnp.concatenate([cache_k[l][:, ts:], heads(kn[n_p:].reshape(bs, ts, D_KV))], axis=1))
        outs["sv"].append(jnp.concatenate([cache_v[l][:, ts:], heads(proj[n_p:, v0:].reshape(bs, ts, D_KV))],
                                          axis=1))
        outs["sre"].append(sre.reshape(bs, g, p))
        outs["sim"].append(sim.reshape(bs, g, p))

    y_p, y_s = x
    return (y_p.reshape(bp, tp, D_MODEL), y_s.reshape(bs, ts, D_MODEL),
            jnp.stack(outs["pk"]), jnp.stack(outs["pv"]), jnp.stack(outs["pre"]), jnp.stack(outs["pim"]),
            jnp.stack(outs["sk"]), jnp.stack(outs["sv"]), jnp.stack(outs["sre"]), jnp.stack(outs["sim"]))
```

```python
import functools
import math

import jax
import jax.numpy as jnp
from jax import lax
from jax.experimental import pallas as pl
from jax.experimental.pallas import tpu as pltpu

D_MODEL = 2048
DEPTH = 4
CHUNK = 64
D_SSM = 1024
SSM_GROUP = 16
N_SSM_GROUPS = 64
SSM_STATE = 64
HEAD_DIM = 64
N_HEADS = 16
N_KV_HEADS = 4
KV_REP = N_HEADS // N_KV_HEADS
D_ATTN = N_HEADS * HEAD_DIM
D_KV = N_KV_HEADS * HEAD_DIM
IN_WIDTH = D_SSM + D_ATTN + 2 * D_KV
WINDOW = 128
ROPE_DIM = 16
ROPE_THETA = 500000.0
D_FF = 5632
EPS = 1e-6
PAST_LEN = 2048

LANES = 128
SUBLANES = 8
MOD_BLOCK = CHUNK
STATE_W = N_SSM_GROUPS * SSM_STATE
SSM_KC = 4
VMEM_LIMIT = 56 * 1024 * 1024

F32 = jnp.float32
BF16 = jnp.bfloat16


def _cparams(sem):
    return pltpu.CompilerParams(dimension_semantics=sem, vmem_limit_bytes=VMEM_LIMIT)


def _sigmoid(x):
    return 0.5 + 0.5 * jnp.tanh(0.5 * x)


def _layer_spec(l, shape):
    zeros = (0,) * len(shape)
    return pl.BlockSpec((None, *shape), lambda *_: (l, *zeros))


def _mod_kernel(c_ref, w_ref, b_ref, o_ref):
    c = c_ref[...].astype(BF16)
    w = w_ref[...].astype(BF16)
    o_ref[...] = jnp.dot(c, w, preferred_element_type=F32) + b_ref[...]


def _mod_call(c_all, w_mod, b_mod):
    nb = c_all.shape[0]
    tn = 1024
    n_out = w_mod.shape[2]
    return pl.pallas_call(
        _mod_kernel,
        out_shape=jax.ShapeDtypeStruct((DEPTH, nb, n_out), F32),
        grid=(DEPTH, n_out // tn),
        in_specs=[
            pl.BlockSpec((nb, D_MODEL), lambda l, j: (0, 0)),
            pl.BlockSpec((None, D_MODEL, tn), lambda l, j: (l, 0, j)),
            pl.BlockSpec((None, 1, tn), lambda l, j: (l, 0, j)),
        ],
        out_specs=pl.BlockSpec((None, nb, tn), lambda l, j: (l, 0, j)),
        compiler_params=_cparams(("arbitrary", "arbitrary")),
        name="adaln_mod",
    )(c_all, w_mod, b_mod.reshape(DEPTH, 1, n_out))


S5_TILES = (256, 64)
S5_SEG = SUBLANES
S5_POWERS = tuple(sorted({n for t in S5_TILES for n in (*range(1, t // S5_SEG + 1),
                                                         *(k * t // S5_SEG for k in (1, 2, 4)))}))


def _s5_prep_kernel(are_ref, aim_ref, ldt_ref, bre_ref, bim_ref, pwr_ref, pwi_ref, bbr_ref, bbi_ref):
    a_re = are_ref[...]
    a_im = aim_ref[...]
    dt = jnp.exp(ldt_ref[...])
    z_re = a_re * dt
    z_im = a_im * dt
    for i, n in enumerate(S5_POWERS):
        mag = jnp.exp(z_re * float(n))
        pwr_ref[i] = mag * jnp.cos(z_im * float(n))
        pwi_ref[i] = mag * jnp.sin(z_im * float(n))
    l_re = pwr_ref[S5_POWERS.index(1)]
    l_im = pwi_ref[S5_POWERS.index(1)]
    den = a_re * a_re + a_im * a_im
    n_re = l_re - 1.0
    f_re = (n_re * a_re + l_im * a_im) / den
    f_im = (l_im * a_re - n_re * a_im) / den
    for c in range(SSM_GROUP):
        b_re = bre_ref[c]
        b_im = bim_ref[c]
        bbr_ref[c] = f_re * b_re - f_im * b_im
        bbi_ref[c] = f_re * b_im + f_im * b_re


def _s5_prep_call(a_re, a_im, log_dt, bt_re, bt_im):
    g, p = N_SSM_GROUPS, SSM_STATE
    mat = pl.BlockSpec((None, g, p), lambda l: (l, 0, 0))
    stack_c = pl.BlockSpec((None, SSM_GROUP, g, p), lambda l: (l, 0, 0, 0))
    n_pow = len(S5_POWERS)
    stack_n = pl.BlockSpec((None, n_pow, g, p), lambda l: (l, 0, 0, 0))
    return pl.pallas_call(
        _s5_prep_kernel,
        out_shape=(jax.ShapeDtypeStruct((DEPTH, n_pow, g, p), F32),
                   jax.ShapeDtypeStruct((DEPTH, n_pow, g, p), F32),
                   jax.ShapeDtypeStruct((DEPTH, SSM_GROUP, g, p), F32),
                   jax.ShapeDtypeStruct((DEPTH, SSM_GROUP, g, p), F32)),
        grid=(DEPTH,),
        in_specs=[mat, mat, pl.BlockSpec((None, g, 1), lambda l: (l, 0, 0)), stack_c, stack_c],
        out_specs=(stack_n, stack_n, stack_c, stack_c),
        compiler_params=_cparams(("arbitrary",)),
        name="s5_discretize",
    )(a_re, a_im, log_dt.reshape(DEPTH, g, 1), bt_re, bt_im)


NORM_CHUNK = 2 * MOD_BLOCK


def _norm_mod_chunk(x_ref, g_ref, sc_ref, sh_ref, h_scr, slot, chunk):
    gain = g_ref[...]
    per_chunk = NORM_CHUNK // MOD_BLOCK
    for s in range(per_chunk):
        x = x_ref[s * MOD_BLOCK:(s + 1) * MOD_BLOCK, :]
        ms = jnp.mean(x * x, axis=-1, keepdims=True)
        y = x * lax.rsqrt(ms + EPS) * gain
        mod_row = pl.ds(chunk * per_chunk + s, 1)
        h = y * (1.0 + sc_ref[mod_row, :]) + sh_ref[mod_row, :]
        rows = pl.ds(pl.multiple_of((chunk * per_chunk + s) * MOD_BLOCK, MOD_BLOCK), MOD_BLOCK)
        h_scr[slot, rows, :] = h.astype(BF16)


def _norm_pipeline(l, n_tiles, tm, sc_idx, sh_idx):
    n_chunks = tm // NORM_CHUNK
    nsub = tm // MOD_BLOCK
    tile = lambda i: jnp.minimum(i, n_tiles - 1)
    specs = [
        pl.BlockSpec((NORM_CHUNK, D_MODEL), lambda i, j: (tile(i) * n_chunks + jnp.minimum(j, n_chunks - 1), 0)),
        _layer_spec(l, (1, D_MODEL)),
        pl.BlockSpec((None, nsub, D_MODEL), lambda i, j: (l, tile(i), sc_idx)),
        pl.BlockSpec((None, nsub, D_MODEL), lambda i, j: (l, tile(i), sh_idx)),
    ]
    scratch = pltpu.VMEM((2, tm, D_MODEL), BF16)
    out_row = lambda i: jnp.maximum(i - 1, 0)
    out_col = lambda i, col: jnp.where(i == 0, 0, col)
    return specs, scratch, n_chunks, out_row, out_col


def _in_gate_kernel(x_ref, g_ref, sc_ref, sh_ref, wi_ref, wg_ref, b_ref, proj_ref, gate_ref, h_scr, *, n_in,
                    n_chunks):
    i, j = pl.program_id(0), pl.program_id(1)
    ready = lax.rem(i + 1, 2)

    filling = j < n_chunks

    def fill():
        _norm_mod_chunk(x_ref, g_ref, sc_ref, sh_ref, h_scr, lax.rem(i, 2), j)

    def gates():
        acc = jnp.dot(h_scr[ready], wg_ref[...], preferred_element_type=F32)
        gate_ref[...] = _sigmoid(acc + b_ref[...])

    pl.when(jnp.logical_and(i == 0, filling))(fill)

    @pl.when(jnp.logical_and(i > 0, j < n_in))
    def _():
        proj_ref[...] = jnp.dot(h_scr[ready], wi_ref[...], preferred_element_type=F32)
        fill()

    @pl.when(jnp.logical_and(i > 0, jnp.logical_and(j >= n_in, filling)))
    def _():
        gates()
        fill()

    pl.when(jnp.logical_and(i > 0, jnp.logical_not(filling)))(gates)


def _in_gate_call(x, l, gain, modx, w_in, w_gate, b_gate, *, tm=1024, tn=512):
    n_tiles = x.shape[0] // tm
    n_in, n_gate = w_in.shape[2] // tn, w_gate.shape[2] // tn
    norm_specs, h_scratch, n_chunks, out_row, out_col = _norm_pipeline(l, n_tiles, tm, 1, 0)
    assert n_in <= n_chunks <= n_in + n_gate
    in_col = lambda j: jnp.minimum(j, n_in - 1)
    gate_col = lambda j: jnp.maximum(j - n_in, 0)
    return pl.pallas_call(
        functools.partial(_in_gate_kernel, n_in=n_in, n_chunks=n_chunks),
        out_shape=(jax.ShapeDtypeStruct((x.shape[0], w_in.shape[2]), F32),
                   jax.ShapeDtypeStruct((x.shape[0], w_gate.shape[2]), F32)),
        grid=(n_tiles + 1, n_in + n_gate),
        in_specs=norm_specs + [
            pl.BlockSpec((None, D_MODEL, tn), lambda i, j: (l, 0, in_col(j))),
            pl.BlockSpec((None, D_MODEL, tn), lambda i, j: (l, 0, gate_col(j))),
            pl.BlockSpec((None, 1, tn), lambda i, j: (l, 0, gate_col(j))),
        ],
        out_specs=(pl.BlockSpec((tm, tn), lambda i, j: (out_row(i), out_col(i, in_col(j)))),
                   pl.BlockSpec((tm, tn), lambda i, j: (out_row(i), out_col(i, gate_col(j))))),
        scratch_shapes=[h_scratch],
        compiler_params=_cparams(("arbitrary", "arbitrary")),
        name="in_proj_gates",
    )(x, gain.reshape(DEPTH, 1, D_MODEL), modx, modx, w_in, w_gate, b_gate.reshape(DEPTH, 1, w_gate.shape[2]))


def _ffn_up_kernel(x_ref, g_ref, sc_ref, sh_ref, wg_ref, wu_ref, o_ref, h_scr, *, n_chunks):
    i, j = pl.program_id(0), pl.program_id(1)

    filling = j < n_chunks

    def fill():
        _norm_mod_chunk(x_ref, g_ref, sc_ref, sh_ref, h_scr, lax.rem(i, 2), j)

    def swiglu():
        h = h_scr[lax.rem(i + 1, 2)]
        gate = jnp.dot(h, wg_ref[...], preferred_element_type=F32)
        up = jnp.dot(h, wu_ref[...], preferred_element_type=F32)
        o_ref[...] = (gate * _sigmoid(gate) * up).astype(o_ref.dtype)

    pl.when(jnp.logical_and(i == 0, filling))(fill)

    @pl.when(jnp.logical_and(i > 0, filling))
    def _():
        swiglu()
        fill()

    pl.when(jnp.logical_and(i > 0, jnp.logical_not(filling)))(swiglu)


def _ffn_up_call(x, l, gain, modx, w_gate, w_up, *, tm=1024, tn=512):
    n_tiles = x.shape[0] // tm
    n_out = w_gate.shape[2]
    norm_specs, h_scratch, n_chunks, out_row, out_col = _norm_pipeline(l, n_tiles, tm, 4, 3)
    assert n_out // tn >= n_chunks
    w_spec = pl.BlockSpec((None, D_MODEL, tn), lambda i, j: (l, 0, j))
    return pl.pallas_call(
        functools.partial(_ffn_up_kernel, n_chunks=n_chunks),
        out_shape=jax.ShapeDtypeStruct((x.shape[0], n_out), BF16),
        grid=(n_tiles + 1, n_out // tn),
        in_specs=norm_specs + [w_spec, w_spec],
        out_specs=pl.BlockSpec((tm, tn), lambda i, j: (out_row(i), out_col(i, j))),
        scratch_shapes=[h_scratch],
        compiler_params=_cparams(("arbitrary", "arbitrary")),
        name="ffn_up",
    )(x, gain.reshape(DEPTH, 1, D_MODEL), modx, modx, w_gate, w_up)


def _mix_kernel(sp_ref, ss_ref, a_ref, ga_ref, gb_ref, ws_ref, wa_ref, o_ref, *, n_split):
    def body(s_ref):
        ps = jnp.dot(s_ref[...], ws_ref[...], preferred_element_type=F32)
        pa = jnp.dot(a_ref[...], wa_ref[...], preferred_element_type=F32)
        o_ref[...] = (ga_ref[...] * ps + gb_ref[...] * pa).astype(o_ref.dtype)

    i = pl.program_id(0)
    pl.when(i < n_split)(lambda: body(sp_ref))
    pl.when(i >= n_split)(lambda: body(ss_ref))


def _mix_call(ssm_p, ssm_s, attn_out, gates, l, w_ps, w_pa, *, tm=1024, tn=512):
    n_tok = attn_out.shape[0]
    nj = D_MODEL // tn
    n_split = ssm_p.shape[0] // tm
    return pl.pallas_call(
        functools.partial(_mix_kernel, n_split=n_split),
        out_shape=jax.ShapeDtypeStruct((n_tok, D_MODEL), BF16),
        grid=(n_tok // tm, nj),
        in_specs=[
            pl.BlockSpec((tm, D_SSM), lambda i, j: (jnp.minimum(i, n_split - 1), 0)),
            pl.BlockSpec((tm, D_SSM), lambda i, j: (jnp.maximum(i - n_split, 0), 0)),
            pl.BlockSpec((tm, D_ATTN), lambda i, j: (i, 0)),
            pl.BlockSpec((tm, tn), lambda i, j: (i, j)),
            pl.BlockSpec((tm, tn), lambda i, j: (i, j + nj)),
            pl.BlockSpec((None, D_SSM, tn), lambda i, j: (l, 0, j)),
            pl.BlockSpec((None, D_ATTN, tn), lambda i, j: (l, 0, j)),
        ],
        out_specs=pl.BlockSpec((tm, tn), lambda i, j: (i, j)),
        compiler_params=_cparams(("arbitrary", "arbitrary")),
        name="branch_merge",
    )(ssm_p, ssm_s, attn_out, gates, gates, w_ps, w_pa)


def _resid_mm_kernel(a_ref, w_ref, x_ref, g_ref, *o_refs, nsub, n_split):
    acc = jnp.dot(a_ref[...], w_ref[...], preferred_element_type=F32)

    def write(o_ref):
        for s in range(nsub):
            rows = slice(s * MOD_BLOCK, (s + 1) * MOD_BLOCK)
            o_ref[rows, :] = x_ref[rows, :] + g_ref[s:s + 1, :] * acc[rows, :]

    if n_split is None:
        write(o_refs[0])
    else:
        i = pl.program_id(0)
        pl.when(i < n_split)(lambda: write(o_refs[0]))
        pl.when(i >= n_split)(lambda: write(o_refs[1]))


def _resid_mm_call(a, l, w, x, modx, g_idx, *, n_split=None, tm=1024, tn=512, name):
    n_tok, k = a.shape
    nsub = tm // MOD_BLOCK
    nj = D_MODEL // tn
    n_tiles = n_tok // tm
    if n_split is None:
        out_shape = jax.ShapeDtypeStruct((n_tok, D_MODEL), F32)
        out_specs = pl.BlockSpec((tm, tn), lambda i, j: (i, j))
    else:
        out_shape = (jax.ShapeDtypeStruct((n_split * tm, D_MODEL), F32),
                     jax.ShapeDtypeStruct(((n_tiles - n_split) * tm, D_MODEL), F32))
        out_specs = (
            pl.BlockSpec((tm, tn), lambda i, j: (jnp.minimum(i, n_split - 1), jnp.where(i < n_split, j, nj - 1))),
            pl.BlockSpec((tm, tn), lambda i, j: (jnp.maximum(i - n_split, 0), jnp.where(i < n_split, 0, j))),
        )
    return pl.pallas_call(
        functools.partial(_resid_mm_kernel, nsub=nsub, n_split=n_split),
        out_shape=out_shape,
        grid=(n_tiles, nj),
        in_specs=[
            pl.BlockSpec((tm, k), lambda i, j: (i, 0)),
            pl.BlockSpec((None, k, tn), lambda i, j: (l, 0, j)),
            pl.BlockSpec((tm, tn), lambda i, j: (i, j)),
            pl.BlockSpec((None, nsub, tn), lambda i, j: (l, i, g_idx * nj + j)),
        ],
        out_specs=out_specs,
        compiler_params=_cparams(("arbitrary", "arbitrary")),
        name=name,
    )(a, w, x, modx)


SCAN_LW = 512


def _ssm_kernel(u_ref, h0r_ref, h0i_ref, wbr_ref, wbi_ref, pwj_ref, seg_ref, wcr_ref, wci_ref, d_ref, wglu_ref,
                bglu_ref, o_ref, htr_ref, hti_ref, bur, bui, cre, cim, z_scr, *, tt):
    steps = tt // S5_SEG

    @pl.when(pl.program_id(1) == 0)
    def _():
        cre[...] = jnp.broadcast_to(h0r_ref[...], (SUBLANES, STATE_W))
        cim[...] = jnp.broadcast_to(h0i_ref[...], (SUBLANES, STATE_W))

    r_idx = lax.broadcasted_iota(jnp.int32, (tt, tt), 0)
    c_idx = lax.broadcasted_iota(jnp.int32, (tt, tt), 1)
    to_scan = jnp.where(c_idx == (r_idx % S5_SEG) * steps + r_idx // S5_SEG, 1.0, 0.0).astype(BF16)
    to_time = jnp.where(r_idx == (c_idx % S5_SEG) * steps + c_idx // S5_SEG, 1.0, 0.0).astype(BF16)

    u = u_ref[...]
    us = jnp.dot(to_scan, u.astype(BF16), preferred_element_type=F32).astype(BF16)
    kw = D_SSM // SSM_KC
    sw = STATE_W // SSM_KC
    for kc in range(SSM_KC):
        uk = us[:, kc * kw:(kc + 1) * kw]
        bur[:, kc * sw:(kc + 1) * sw] = jnp.dot(uk, wbr_ref[kc], preferred_element_type=F32)
        bui[:, kc * sw:(kc + 1) * sw] = jnp.dot(uk, wbi_ref[kc], preferred_element_type=F32)

    row = lax.broadcasted_iota(jnp.int32, (SUBLANES, SCAN_LW), 0)
    for lc in range(STATE_W // SCAN_LW):
        sl = slice(lc * SCAN_LW, (lc + 1) * SCAN_LW)
        lam_re = jnp.broadcast_to(pwj_ref[0, 0:1, sl], (SUBLANES, SCAN_LW))
        lam_im = jnp.broadcast_to(pwj_ref[1, 0:1, sl], (SUBLANES, SCAN_LW))

        def local(j, carry, sl=sl, lam_re=lam_re, lam_im=lam_im):
            h_re, h_im = carry
            rows = pl.ds(pl.multiple_of(j * SUBLANES, SUBLANES), SUBLANES)
            h_re, h_im = (lam_re * h_re - lam_im * h_im + bur[rows, sl],
                          lam_re * h_im + lam_im * h_re + bui[rows, sl])
            bur[rows, sl] = h_re
            bui[rows, sl] = h_im
            return h_re, h_im

        zero = jnp.zeros((SUBLANES, SCAN_LW), F32)
        e_re, e_im = lax.fori_loop(0, steps, local, (zero, zero), unroll=2)

        x_re = jnp.where(row == 0, cre[:, sl], pltpu.roll(e_re, 1, 0))
        x_im = jnp.where(row == 0, cim[:, sl], pltpu.roll(e_im, 1, 0))
        for idx, k in enumerate((1, 2, 4)):
            m_re = seg_ref[2 * idx, :, sl]
            m_im = seg_ref[2 * idx + 1, :, sl]
            s_re = pltpu.roll(x_re, k, 0)
            s_im = pltpu.roll(x_im, k, 0)
            x_re, x_im = (x_re + m_re * s_re - m_im * s_im,
                          x_im + m_re * s_im + m_im * s_re)
        pj_re = jnp.broadcast_to(pwj_ref[0, steps - 1:steps, sl], (SUBLANES, SCAN_LW))
        pj_im = jnp.broadcast_to(pwj_ref[1, steps - 1:steps, sl], (SUBLANES, SCAN_LW))
        n_re = pj_re * x_re - pj_im * x_im + e_re
        n_im = pj_re * x_im + pj_im * x_re + e_im
        cre[:, sl] = jnp.broadcast_to(n_re[SUBLANES - 1:SUBLANES, :], (SUBLANES, SCAN_LW))
        cim[:, sl] = jnp.broadcast_to(n_im[SUBLANES - 1:SUBLANES, :], (SUBLANES, SCAN_LW))

        def stitch(j, carry, sl=sl, c_re=x_re, c_im=x_im):
            rows = pl.ds(pl.multiple_of(j * SUBLANES, SUBLANES), SUBLANES)
            p_re = jnp.broadcast_to(pwj_ref[0, pl.ds(j, 1), sl], (SUBLANES, SCAN_LW))
            p_im = jnp.broadcast_to(pwj_ref[1, pl.ds(j, 1), sl], (SUBLANES, SCAN_LW))
            bur[rows, sl] = bur[rows, sl] + p_re * c_re - p_im * c_im
            bui[rows, sl] = bui[rows, sl] + p_re * c_im + p_im * c_re
            return carry

        lax.fori_loop(0, steps, stitch, 0, unroll=2)

    htr_ref[...] = cre[0:1, :]
    hti_ref[...] = cim[0:1, :]

    d = d_ref[...]
    for kc in range(SSM_KC):
        h_re = bur[:, kc * sw:(kc + 1) * sw].astype(BF16)
        h_im = bui[:, kc * sw:(kc + 1) * sw].astype(BF16)
        ys = (jnp.dot(h_re, wcr_ref[kc], preferred_element_type=F32)
              - jnp.dot(h_im, wci_ref[kc], preferred_element_type=F32))
        ys_hi = ys.astype(BF16)
        ys_lo = (ys - ys_hi.astype(F32)).astype(BF16)
        y = (jnp.dot(to_time, ys_hi, preferred_element_type=F32)
             + jnp.dot(to_time, ys_lo, preferred_element_type=F32))
        cols = slice(kc * kw, (kc + 1) * kw)
        y = y + d[:, cols] * u[:, cols]
        z_scr[:, cols] = 0.5 * y * (1.0 + jnp.tanh(math.sqrt(2.0 / math.pi) * (y + 0.044715 * (y * y * y))))
    z = z_scr[...]
    gate = _sigmoid(jnp.dot(z.astype(BF16), wglu_ref[...], preferred_element_type=F32) + bglu_ref[...])
    o_ref[...] = (z * gate).astype(o_ref.dtype)


def _ssm_call(proj, row0, n_seq, t_len, tt, h0_re, h0_im, l, wb_re, wb_im, pw_re, pw_im, wc_re, wc_im, d_skip,
              w_glu, b_glu, *, name):
    steps = tt // S5_SEG
    power = lambda n: S5_POWERS.index(n)
    pwj = jnp.stack([pw[:, power(1):power(steps) + 1] for pw in (pw_re, pw_im)], axis=1)
    seg_rows = []
    for k in (1, 2, 4):
        keep = (jnp.arange(SUBLANES) >= k)[None, :, None]
        seg_rows += [jnp.where(keep, pw[:, power(k * steps)][:, None, :], 0.0) for pw in (pw_re, pw_im)]
    seg = jnp.stack(seg_rows, axis=1)
    nt = t_len // tt
    rb0 = row0 // tt
    state_spec = pl.BlockSpec((None, 1, STATE_W), lambda b, t: (b, 0, 0))
    return pl.pallas_call(
        functools.partial(_ssm_kernel, tt=tt),
        out_shape=(jax.ShapeDtypeStruct((n_seq * t_len, D_SSM), BF16),
                   jax.ShapeDtypeStruct((n_seq, 1, STATE_W), F32),
                   jax.ShapeDtypeStruct((n_seq, 1, STATE_W), F32)),
        grid=(n_seq, nt),
        in_specs=[
            pl.BlockSpec((tt, D_SSM), lambda b, t: (rb0 + b * nt + t, 0)),
            state_spec, state_spec,
            _layer_spec(l, wb_re.shape[1:]), _layer_spec(l, wb_im.shape[1:]),
            _layer_spec(l, pwj.shape[1:]), _layer_spec(l, seg.shape[1:]),
            _layer_spec(l, wc_re.shape[1:]), _layer_spec(l, wc_im.shape[1:]),
            _layer_spec(l, (1, D_SSM)),
            _layer_spec(l, (D_SSM, D_SSM)),
            _layer_spec(l, (1, D_SSM)),
        ],
        out_specs=(pl.BlockSpec((tt, D_SSM), lambda b, t: (b * nt + t, 0)), state_spec, state_spec),
        scratch_shapes=[pltpu.VMEM((tt, STATE_W), F32), pltpu.VMEM((tt, STATE_W), F32),
                        pltpu.VMEM((SUBLANES, STATE_W), F32), pltpu.VMEM((SUBLANES, STATE_W), F32),
                        pltpu.VMEM((tt, D_SSM), F32)],
        compiler_params=_cparams(("arbitrary", "arbitrary")),
        name=name,
    )(proj, h0_re, h0_im, wb_re, wb_im, pwj, seg, wc_re, wc_im, d_skip.reshape(DEPTH, 1, D_SSM), w_glu,
      b_glu.reshape(DEPTH, 1, D_SSM))


RING = 3
N_KEYS = RING * CHUNK
ATT_CPS = 4


def _stack_blocks(x):
    return jnp.concatenate([x[:, j * LANES:(j + 1) * LANES] for j in range(x.shape[1] // LANES)], axis=0)


def _pair_norm_rope(xs, gain, tab, ones_bd):
    reps = xs.shape[0] // CHUNK
    cos, s_lo, s_hi = (jnp.concatenate([tab[i]] * reps, axis=0) for i in range(3))
    half = ROPE_DIM // 2
    sq = xs * xs
    sq_hi = sq.astype(BF16)
    sq_lo = (sq - sq_hi.astype(F32)).astype(BF16)
    ss = (jnp.dot(sq_hi, ones_bd, preferred_element_type=F32)
          + jnp.dot(sq_lo, ones_bd, preferred_element_type=F32))
    xg = xs * gain
    xr = xg * cos + pltpu.roll(xg, LANES - half, 1) * s_lo + pltpu.roll(xg, half, 1) * s_hi
    return xr * lax.rsqrt(ss * (1.0 / HEAD_DIM) + EPS)


def _attn_kernel(sink_ref, q_ref, k_ref, v_ref, ck_ref, cv_ref, tab_ref, gq_ref, gk_ref, o_ref, kn_ref, kd, vd,
                 *, l, n_prompt_steps, steps_per_seq):
    step = pl.program_id(0)
    is_sample = step >= n_prompt_steps
    c_base = lax.rem(step, steps_per_seq) * ATT_CPS
    first = lax.broadcasted_iota(jnp.int32, (1, LANES), 1) < HEAD_DIM
    ones_bd = jnp.where(lax.broadcasted_iota(jnp.int32, (LANES, LANES), 0) // HEAD_DIM
                        == lax.broadcasted_iota(jnp.int32, (LANES, LANES), 1) // HEAD_DIM, 1.0, 0.0).astype(BF16)
    key_slot = lax.broadcasted_iota(jnp.int32, (1, N_KEYS), 1) // CHUNK
    gq = gq_ref[...]
    gk = gk_ref[...]

    def store_dup(dst, x, rows):
        for pair in range(D_KV // LANES):
            blk = x[:, pair * LANES:(pair + 1) * LANES]
            swapped = pltpu.roll(blk, HEAD_DIM, 1)
            dst[2 * pair, rows, 0:LANES] = jnp.where(first, blk, swapped).astype(BF16)
            dst[2 * pair + 1, rows, 0:LANES] = jnp.where(first, swapped, blk).astype(BF16)

    @pl.when(step == 0)
    def _():
        vd[:, :, LANES:] = jnp.ones((N_KV_HEADS, N_KEYS, LANES), BF16)

    def chunk(ci, carry):
        c = jnp.where(is_sample, 0, c_base + ci)
        rows = pl.ds(pl.multiple_of(ci * CHUNK, CHUNK), CHUNK)
        prev = slice(CHUNK, N_KEYS)

        @pl.when(jnp.logical_and(c == 0, jnp.logical_not(is_sample)))
        def _():
            kd[:, prev, :] = jnp.zeros((N_KV_HEADS, N_KEYS - CHUNK, LANES), BF16)
            vd[:, prev, 0:LANES] = jnp.zeros((N_KV_HEADS, N_KEYS - CHUNK, LANES), BF16)

        @pl.when(is_sample)
        def _():
            cached = pl.ds(pl.multiple_of(ci * WINDOW, WINDOW), WINDOW)
            store_dup(kd, ck_ref[cached, :], prev)
            store_dup(vd, cv_ref[cached, :], prev)

        tab = tab_ref[:, rows, :]
        own = pl.ds(pl.multiple_of(lax.rem(c, RING) * CHUNK, CHUNK), CHUNK)
        kn = _pair_norm_rope(_stack_blocks(k_ref[rows, :]), gk, tab, ones_bd)
        kn = jnp.concatenate([kn[0:CHUNK], kn[CHUNK:2 * CHUNK]], axis=1)
        kn_ref[rows, :] = kn
        store_dup(kd, kn, own)
        store_dup(vd, v_ref[rows, :], own)

        qn = _pair_norm_rope(_stack_blocks(q_ref[rows, :]), gq, tab, ones_bd) * (HEAD_DIM ** -0.5)
        q_lo = jnp.where(first, qn, 0.0).astype(BF16)
        q_hi = jnp.where(first, 0.0, qn).astype(BF16)

        valid = key_slot <= jnp.where(is_sample, RING, c)
        scores = []
        for kh in range(N_KV_HEADS):
            r0 = kh * 2 * CHUNK
            lhs = jnp.concatenate([q_lo[r0:r0 + CHUNK], q_hi[r0:r0 + CHUNK],
                                   q_lo[r0 + CHUNK:r0 + 2 * CHUNK], q_hi[r0 + CHUNK:r0 + 2 * CHUNK]], axis=0)
            scores.append(lax.dot_general(lhs, kd[kh], (((1,), (1,)), ((), ())),
                                          preferred_element_type=F32))
        for kh in range(N_KV_HEADS):
            s = jnp.where(valid, scores[kh], -1e30)
            sink = jnp.concatenate([jnp.full((CHUNK, 1), sink_ref[l, kh * KV_REP + r], F32)
                                    for r in range(KV_REP)], axis=0)
            m = jnp.maximum(jnp.max(s, axis=-1, keepdims=True), sink)
            p = jnp.exp(s - m).astype(BF16)
            od = jnp.dot(p, vd[kh], preferred_element_type=F32)
            o = od[:, 0:LANES] / (od[:, LANES:] + jnp.exp(sink - m))
            for pair in range(2):
                half = pair * 2 * CHUNK
                blk = jnp.where(first, o[half:half + CHUNK], o[half + CHUNK:half + 2 * CHUNK])
                col = (2 * kh + pair) * LANES
                o_ref[rows, col:col + LANES] = blk.astype(o_ref.dtype)
        return carry

    lax.fori_loop(0, ATT_CPS, chunk, 0)


def _attn_call(proj, l, n_prompt, t_prompt, sink, rope_tab, gq, gk, cache_k, cache_v):
    n_tok = proj.shape[0]
    tr = ATT_CPS * CHUNK
    qcol = D_SSM // D_ATTN
    kcol = (D_SSM + D_ATTN) // D_KV
    n_prompt_steps = n_prompt // tr
    cache_spec = pl.BlockSpec((None, ATT_CPS * WINDOW, D_KV),
                              lambda s: (l, jnp.maximum(s - n_prompt_steps, 0), 0))
    return pl.pallas_call(
        functools.partial(_attn_kernel, l=l, n_prompt_steps=n_prompt_steps, steps_per_seq=t_prompt // tr),
        out_shape=(jax.ShapeDtypeStruct((n_tok, D_ATTN), BF16),
                   jax.ShapeDtypeStruct((n_tok, D_KV), F32)),
        grid=(n_tok // tr,),
        in_specs=[
            pl.BlockSpec(memory_space=pltpu.SMEM),
            pl.BlockSpec((tr, D_ATTN), lambda s: (s, qcol)),
            pl.BlockSpec((tr, D_KV), lambda s: (s, kcol)),
            pl.BlockSpec((tr, D_KV), lambda s: (s, kcol + 1)),
            cache_spec, cache_spec,
            pl.BlockSpec((3, tr, LANES), lambda s: (0, s, 0)),
            _layer_spec(l, (1, LANES)), _layer_spec(l, (1, LANES)),
        ],
        out_specs=(pl.BlockSpec((tr, D_ATTN), lambda s: (s, 0)),
                   pl.BlockSpec((tr, D_KV), lambda s: (s, 0))),
        scratch_shapes=[pltpu.VMEM((N_KV_HEADS, N_KEYS, LANES), BF16),
                        pltpu.VMEM((N_KV_HEADS, N_KEYS, 2 * LANES), BF16)],
        compiler_params=_cparams(("arbitrary",)),
        name="banded_attn",
    )(sink, proj, proj, proj, cache_k, cache_v, rope_tab, gq, gk)


def _block_diag_rows(x, n):
    *lead, r, width = x.shape
    keep = jnp.arange(n)[:, None, None] == (jnp.arange(width) // (width // n))[None, None, :]
    return jnp.where(keep, x[..., None, :, :], 0.0).reshape(*lead, n * r, width)


def _rope_table(pos):
    half = ROPE_DIM // 2
    n_pos = pos.shape[0]
    inv_freq = ROPE_THETA ** (-jnp.arange(half, dtype=F32) / half)
    ang = pos.astype(F32)[:, None] * inv_freq[None, :]
    cos, sin = jnp.cos(ang), jnp.sin(ang)
    ones = jnp.ones((n_pos, HEAD_DIM - ROPE_DIM), F32)
    zeros = jnp.zeros((n_pos, HEAD_DIM - half), F32)
    c_tab = jnp.concatenate([cos, cos, ones], axis=1)
    lo_tab = jnp.concatenate([-sin, zeros], axis=1)
    hi_tab = jnp.concatenate([jnp.zeros((n_pos, half), F32), sin, jnp.zeros((n_pos, HEAD_DIM - ROPE_DIM), F32)],
                             axis=1)
    tab = jnp.stack([c_tab, lo_tab, hi_tab])
    return jnp.concatenate([tab, tab], axis=2)


def kernel(x_prompt, x_sample, cache_k, cache_v, state_ssm_re, state_ssm_im, c_prompt, c_sample, w_mod, b_mod, norm1_g, norm2_g, w_in, ssm_a_re, ssm_a_im, ssm_log_dt, ssm_b_re, ssm_b_im, ssm_c_re, ssm_c_im, ssm_d, w_glu, b_glu, q_norm_g, k_norm_g, attn_sink, w_gate, b_gate, w_proj_ssm, w_proj_attn, w_out, w_ffn_gate, w_ffn_up, w_ffn_down):
    bp, tp, _ = x_prompt.shape
    bs, ts, _ = x_sample.shape
    assert ts == CHUNK and tp % (ATT_CPS * CHUNK) == 0 and bs % ATT_CPS == 0
    n_p, n_s = bp * tp, bs * ts
    tm = 1024
    x = jnp.concatenate([x_prompt.reshape(n_p, D_MODEL), x_sample.reshape(n_s, D_MODEL)], axis=0)

    n_cond = bp + bs
    pad = (-n_cond) % SUBLANES
    c_all = jnp.concatenate([c_prompt, c_sample, jnp.zeros((pad, D_MODEL), F32)], axis=0)
    mod = _mod_call(c_all, w_mod, b_mod)
    n_mod = mod.shape[2]

    def per_block(m, n_seq, t_len):
        reps = t_len // MOD_BLOCK
        return jnp.broadcast_to(m[:, :, None, :], (DEPTH, n_seq, reps, n_mod)).reshape(DEPTH, n_seq * reps, n_mod)

    modx = jnp.concatenate([per_block(mod[:, :bp], bp, tp), per_block(mod[:, bp:n_cond], bs, ts)],
                           axis=1)

    g, p = N_SSM_GROUPS, SSM_STATE
    gpc = g // SSM_KC
    pw_re, pw_im, bb_re, bb_im = _s5_prep_call(ssm_a_re, ssm_a_im, ssm_log_dt,
                                               ssm_b_re.transpose(0, 3, 1, 2), ssm_b_im.transpose(0, 3, 1, 2))
    pw_re = pw_re.reshape(DEPTH, len(S5_POWERS), STATE_W)
    pw_im = pw_im.reshape(DEPTH, len(S5_POWERS), STATE_W)

    def wb_blocks(bb):
        per_chunk = bb.reshape(DEPTH, SSM_GROUP, SSM_KC, gpc * p).transpose(0, 2, 1, 3)
        return _block_diag_rows(per_chunk, gpc).astype(BF16)

    def wc_blocks(cc):
        per_chunk = cc.reshape(DEPTH, SSM_KC, gpc, SSM_GROUP, p).transpose(0, 1, 3, 2, 4)
        transposed = _block_diag_rows(per_chunk.reshape(DEPTH, SSM_KC, SSM_GROUP, gpc * p), gpc)
        return jnp.swapaxes(transposed.astype(BF16), -1, -2)

    ssm_w = (wb_blocks(bb_re), wb_blocks(bb_im), pw_re, pw_im, wc_blocks(ssm_c_re), wc_blocks(ssm_c_im), ssm_d,
             w_glu.astype(BF16), b_glu)
    zeros_state = jnp.zeros((bp, 1, STATE_W), F32)
    h0_re = state_ssm_re.reshape(DEPTH, bs, 1, STATE_W)
    h0_im = state_ssm_im.reshape(DEPTH, bs, 1, STATE_W)

    pos = jnp.concatenate([jnp.tile(jnp.arange(tp), bp), jnp.tile(PAST_LEN + jnp.arange(ts), bs)])
    rope_tab = _rope_table(pos)
    gq = jnp.tile(q_norm_g, (1, LANES // HEAD_DIM)).reshape(DEPTH, 1, LANES)
    gk = jnp.tile(k_norm_g, (1, LANES // HEAD_DIM)).reshape(DEPTH, 1, LANES)
    cache_k2 = cache_k.reshape(DEPTH, bs * WINDOW, D_KV)
    cache_v2 = cache_v.reshape(DEPTH, bs * WINDOW, D_KV)

    w_in_b, w_gate_b = w_in.astype(BF16), w_gate.astype(BF16)
    w_ps_b, w_pa_b, w_out_b = w_proj_ssm.astype(BF16), w_proj_attn.astype(BF16), w_out.astype(BF16)
    w_fg_b, w_fu_b, w_fd_b = w_ffn_gate.astype(BF16), w_ffn_up.astype(BF16), w_ffn_down.astype(BF16)
    v0 = D_SSM + D_ATTN + D_KV

    def heads(t):
        return t.reshape(*t.shape[:-1], N_KV_HEADS, HEAD_DIM)

    def last_window(t, col0, col1):
        return jnp.stack([t[(b + 1) * tp - WINDOW:(b + 1) * tp, col0:col1] for b in range(bp)])

    outs = {k: [] for k in ("pk", "pv", "pre", "pim", "sk", "sv", "sre", "sim")}
    for l in range(DEPTH):
        proj, gates = _in_gate_call(x, l, norm1_g, modx, w_in_b, w_gate_b, b_gate, tm=tm)

        ssm_p, pre, pim = _ssm_call(proj, 0, bp, tp, 256, zeros_state, zeros_state, l, *ssm_w, name="s5_prompt")
        ssm_s, sre, sim = _ssm_call(proj, n_p, bs, ts, ts, h0_re[l], h0_im[l], l, *ssm_w, name="s5_sample")
        attn_out, kn = _attn_call(proj, l, n_p, tp, attn_sink, rope_tab, gq, gk, cache_k2, cache_v2)

        mixed = _mix_call(ssm_p, ssm_s, attn_out, gates, l, w_ps_b, w_pa_b, tm=tm, tn=1024)
        x = _resid_mm_call(mixed, l, w_out_b, x, modx, 2, tm=tm, tn=1024, name="out_proj")

        act = _ffn_up_call(x, l, norm2_g, modx, w_fg_b, w_fu_b, tm=tm)
        last = l == DEPTH - 1
        x = _resid_mm_call(act, l, w_fd_b, x, modx, 5, n_split=n_p // tm if last else None, tm=tm,
                           name="ffn_down_split" if last else "ffn_down")

        outs["pk"].append(heads(last_window(kn, 0, D_KV)))
        outs["pv"].append(heads(last_window(proj, v0, IN_WIDTH)))
        outs["pre"].append(pre.reshape(bp, g, p))
        outs["pim"].append(pim.reshape(bp, g, p))
        outs["sk"].append(jnp.concatenate([cache_k[l][:, ts:], heads(kn[n_p:].reshape(bs, ts, D_KV))], axis=1))
        outs["sv"].append(jnp.concatenate([cache_v[l][:, ts:], heads(proj[n_p:, v0:].reshape(bs, ts, D_KV))],
                                          axis=1))
        outs["sre"].append(sre.reshape(bs, g, p))
        outs["sim"].append(sim.reshape(bs, g, p))

    y_p, y_s = x
    return (y_p.reshape(bp, tp, D_MODEL), y_s.reshape(bs, ts, D_MODEL),
            jnp.stack(outs["pk"]), jnp.stack(outs["pv"]), jnp.stack(outs["pre"]), jnp.stack(outs["pim"]),
            jnp.stack(outs["sk"]), jnp.stack(outs["sv"]), jnp.stack(outs["sre"]), jnp.stack(outs["sim"]))
```

```python
import functools
import math

import jax
import jax.numpy as jnp
from jax import lax
from jax.experimental import pallas as pl
from jax.experimental.pallas import tpu as pltpu

D_MODEL = 2048
DEPTH = 4
CHUNK = 64
D_SSM = 1024
SSM_GROUP = 16
N_SSM_GROUPS = 64
SSM_STATE = 64
HEAD_DIM = 64
N_HEADS = 16
N_KV_HEADS = 4
KV_REP = N_HEADS // N_KV_HEADS
D_ATTN = N_HEADS * HEAD_DIM
D_KV = N_KV_HEADS * HEAD_DIM
IN_WIDTH = D_SSM + D_ATTN + 2 * D_KV
WINDOW = 128
ROPE_DIM = 16
ROPE_THETA = 500000.0
D_FF = 5632
EPS = 1e-6
PAST_LEN = 2048

LANES = 128
SUBLANES = 8
MOD_BLOCK = CHUNK
STATE_W = N_SSM_GROUPS * SSM_STATE
SSM_KC = 4
VMEM_LIMIT = 56 * 1024 * 1024

F32 = jnp.float32
BF16 = jnp.bfloat16


def _cparams(sem):
    return pltpu.CompilerParams(dimension_semantics=sem, vmem_limit_bytes=VMEM_LIMIT)


def _sigmoid(x):
    return 0.5 + 0.5 * jnp.tanh(0.5 * x)


def _layer_spec(l, shape):
    zeros = (0,) * len(shape)
    return pl.BlockSpec((None, *shape), lambda *_: (l, *zeros))


def _mod_kernel(c_ref, w_ref, b_ref, o_ref):
    c = c_ref[...].astype(BF16)
    w = w_ref[...].astype(BF16)
    o_ref[...] = jnp.dot(c, w, preferred_element_type=F32) + b_ref[...]


def _mod_call(c_all, w_mod, b_mod):
    nb = c_all.shape[0]
    tn = 1024
    n_out = w_mod.shape[2]
    return pl.pallas_call(
        _mod_kernel,
        out_shape=jax.ShapeDtypeStruct((DEPTH, nb, n_out), F32),
        grid=(DEPTH, n_out // tn),
        in_specs=[
            pl.BlockSpec((nb, D_MODEL), lambda l, j: (0, 0)),
            pl.BlockSpec((None, D_MODEL, tn), lambda l, j: (l, 0, j)),
            pl.BlockSpec((None, 1, tn), lambda l, j: (l, 0, j)),
        ],
        out_specs=pl.BlockSpec((None, nb, tn), lambda l, j: (l, 0, j)),
        compiler_params=_cparams(("arbitrary", "arbitrary")),
        name="adaln_mod",
    )(c_all, w_mod, b_mod.reshape(DEPTH, 1, n_out))


S5_TILES = (256, 64)
S5_SEG = SUBLANES
S5_POWERS = tuple(sorted({n for t in S5_TILES for n in (1, *(k * t // S5_SEG for k in (1, 2, 4)))}))


def _s5_prep_kernel(are_ref, aim_ref, ldt_ref, bre_ref, bim_ref, pwr_ref, pwi_ref, bbr_ref, bbi_ref):
    a_re = are_ref[...]
    a_im = aim_ref[...]
    dt = jnp.exp(ldt_ref[...])
    z_re = a_re * dt
    z_im = a_im * dt
    for i, n in enumerate(S5_POWERS):
        mag = jnp.exp(z_re * float(n))
        pwr_ref[i] = mag * jnp.cos(z_im * float(n))
        pwi_ref[i] = mag * jnp.sin(z_im * float(n))
    l_re = pwr_ref[S5_POWERS.index(1)]
    l_im = pwi_ref[S5_POWERS.index(1)]
    den = a_re * a_re + a_im * a_im
    n_re = l_re - 1.0
    f_re = (n_re * a_re + l_im * a_im) / den
    f_im = (l_im * a_re - n_re * a_im) / den
    for c in range(SSM_GROUP):
        b_re = bre_ref[c]
        b_im = bim_ref[c]
        bbr_ref[c] = f_re * b_re - f_im * b_im
        bbi_ref[c] = f_re * b_im + f_im * b_re


def _s5_prep_call(a_re, a_im, log_dt, bt_re, bt_im):
    g, p = N_SSM_GROUPS, SSM_STATE
    mat = pl.BlockSpec((None, g, p), lambda l: (l, 0, 0))
    stack_c = pl.BlockSpec((None, SSM_GROUP, g, p), lambda l: (l, 0, 0, 0))
    n_pow = len(S5_POWERS)
    stack_n = pl.BlockSpec((None, n_pow, g, p), lambda l: (l, 0, 0, 0))
    return pl.pallas_call(
        _s5_prep_kernel,
        out_shape=(jax.ShapeDtypeStruct((DEPTH, n_pow, g, p), F32),
                   jax.ShapeDtypeStruct((DEPTH, n_pow, g, p), F32),
                   jax.ShapeDtypeStruct((DEPTH, SSM_GROUP, g, p), F32),
                   jax.ShapeDtypeStruct((DEPTH, SSM_GROUP, g, p), F32)),
        grid=(DEPTH,),
        in_specs=[mat, mat, pl.BlockSpec((None, g, 1), lambda l: (l, 0, 0)), stack_c, stack_c],
        out_specs=(stack_n, stack_n, stack_c, stack_c),
        compiler_params=_cparams(("arbitrary",)),
        name="s5_discretize",
    )(a_re, a_im, log_dt.reshape(DEPTH, g, 1), bt_re, bt_im)


NORM_CHUNK = 2 * MOD_BLOCK


def _norm_mod_chunk(x_ref, g_ref, sc_ref, sh_ref, h_scr, slot, chunk):
    gain = g_ref[...]
    per_chunk = NORM_CHUNK // MOD_BLOCK
    for s in range(per_chunk):
        x = x_ref[s * MOD_BLOCK:(s + 1) * MOD_BLOCK, :]
        ms = jnp.mean(x * x, axis=-1, keepdims=True)
        y = x * lax.rsqrt(ms + EPS) * gain
        mod_row = pl.ds(chunk * per_chunk + s, 1)
        h = y * (1.0 + sc_ref[mod_row, :]) + sh_ref[mod_row, :]
        rows = pl.ds(pl.multiple_of((chunk * per_chunk + s) * MOD_BLOCK, MOD_BLOCK), MOD_BLOCK)
        h_scr[slot, rows, :] = h.astype(BF16)


def _norm_pipeline(l, n_tiles, tm, sc_idx, sh_idx):
    n_chunks = tm // NORM_CHUNK
    nsub = tm // MOD_BLOCK
    tile = lambda i: jnp.minimum(i, n_tiles - 1)
    specs = [
        pl.BlockSpec((NORM_CHUNK, D_MODEL), lambda i, j: (tile(i) * n_chunks + jnp.minimum(j, n_chunks - 1), 0)),
        _layer_spec(l, (1, D_MODEL)),
        pl.BlockSpec((None, nsub, D_MODEL), lambda i, j: (l, tile(i), sc_idx)),
        pl.BlockSpec((None, nsub, D_MODEL), lambda i, j: (l, tile(i), sh_idx)),
    ]
    scratch = pltpu.VMEM((2, tm, D_MODEL), BF16)
    out_row = lambda i: jnp.maximum(i - 1, 0)
    out_col = lambda i, col: jnp.where(i == 0, 0, col)
    return specs, scratch, n_chunks, out_row, out_col


def _in_gate_kernel(x_ref, g_ref, sc_ref, sh_ref, wi_ref, wg_ref, b_ref, proj_ref, gate_ref, h_scr, *, n_in,
                    n_chunks):
    i, j = pl.program_id(0), pl.program_id(1)
    ready = lax.rem(i + 1, 2)

    filling = j < n_chunks

    def fill():
        _norm_mod_chunk(x_ref, g_ref, sc_ref, sh_ref, h_scr, lax.rem(i, 2), j)

    def gates():
        acc = jnp.dot(h_scr[ready], wg_ref[...], preferred_element_type=F32)
        gate_ref[...] = _sigmoid(acc + b_ref[...])

    pl.when(jnp.logical_and(i == 0, filling))(fill)

    @pl.when(jnp.logical_and(i > 0, j < n_in))
    def _():
        proj_ref[...] = jnp.dot(h_scr[ready], wi_ref[...], preferred_element_type=F32)
        fill()

    @pl.when(jnp.logical_and(i > 0, jnp.logical_and(j >= n_in, filling)))
    def _():
        gates()
        fill()

    pl.when(jnp.logical_and(i > 0, jnp.logical_not(filling)))(gates)


def _in_gate_call(x, l, gain, modx, w_in, w_gate, b_gate, *, tm=1024, tn=512):
    n_tiles = x.shape[0] // tm
    n_in, n_gate = w_in.shape[2] // tn, w_gate.shape[2] // tn
    norm_specs, h_scratch, n_chunks, out_row, out_col = _norm_pipeline(l, n_tiles, tm, 1, 0)
    assert n_in <= n_chunks <= n_in + n_gate
    in_col = lambda j: jnp.minimum(j, n_in - 1)
    gate_col = lambda j: jnp.maximum(j - n_in, 0)
    return pl.pallas_call(
        functools.partial(_in_gate_kernel, n_in=n_in, n_chunks=n_chunks),
        out_shape=(jax.ShapeDtypeStruct((x.shape[0], w_in.shape[2]), F32),
                   jax.ShapeDtypeStruct((x.shape[0], w_gate.shape[2]), F32)),
        grid=(n_tiles + 1, n_in + n_gate),
        in_specs=norm_specs + [
            pl.BlockSpec((None, D_MODEL, tn), lambda i, j: (l, 0, in_col(j))),
            pl.BlockSpec((None, D_MODEL, tn), lambda i, j: (l, 0, gate_col(j))),
            pl.BlockSpec((None, 1, tn), lambda i, j: (l, 0, gate_col(j))),
        ],
        out_specs=(pl.BlockSpec((tm, tn), lambda i, j: (out_row(i), out_col(i, in_col(j)))),
                   pl.BlockSpec((tm, tn), lambda i, j: (out_row(i), out_col(i, gate_col(j))))),
        scratch_shapes=[h_scratch],
        compiler_params=_cparams(("arbitrary", "arbitrary")),
        name="in_proj_gates",
    )(x, gain.reshape(DEPTH, 1, D_MODEL), modx, modx, w_in, w_gate, b_gate.reshape(DEPTH, 1, w_gate.shape[2]))


def _ffn_up_kernel(x_ref, g_ref, sc_ref, sh_ref, wg_ref, wu_ref, o_ref, h_scr, *, n_chunks):
    i, j = pl.program_id(0), pl.program_id(1)

    filling = j < n_chunks

    def fill():
        _norm_mod_chunk(x_ref, g_ref, sc_ref, sh_ref, h_scr, lax.rem(i, 2), j)

    def swiglu():
        h = h_scr[lax.rem(i + 1, 2)]
        gate = jnp.dot(h, wg_ref[...], preferred_element_type=F32)
        up = jnp.dot(h, wu_ref[...], preferred_element_type=F32)
        o_ref[...] = (gate * _sigmoid(gate) * up).astype(o_ref.dtype)

    pl.when(jnp.logical_and(i == 0, filling))(fill)

    @pl.when(jnp.logical_and(i > 0, filling))
    def _():
        swiglu()
        fill()

    pl.when(jnp.logical_and(i > 0, jnp.logical_not(filling)))(swiglu)


def _ffn_up_call(x, l, gain, modx, w_gate, w_up, *, tm=1024, tn=512):
    n_tiles = x.shape[0] // tm
    n_out = w_gate.shape[2]
    norm_specs, h_scratch, n_chunks, out_row, out_col = _norm_pipeline(l, n_tiles, tm, 4, 3)
    assert n_out // tn >= n_chunks
    w_spec = pl.BlockSpec((None, D_MODEL, tn), lambda i, j: (l, 0, j))
    return pl.pallas_call(
        functools.partial(_ffn_up_kernel, n_chunks=n_chunks),
        out_shape=jax.ShapeDtypeStruct((x.shape[0], n_out), BF16),
        grid=(n_tiles + 1, n_out // tn),
        in_specs=norm_specs + [w_spec, w_spec],
        out_specs=pl.BlockSpec((tm, tn), lambda i, j: (out_row(i), out_col(i, j))),
        scratch_shapes=[h_scratch],
        compiler_params=_cparams(("arbitrary", "arbitrary")),
        name="ffn_up",
    )(x, gain.reshape(DEPTH, 1, D_MODEL), modx, modx, w_gate, w_up)


def _mix_kernel(sp_ref, ss_ref, a_ref, ga_ref, gb_ref, ws_ref, wa_ref, o_ref, *, n_split):
    def body(s_ref):
        ps = jnp.dot(s_ref[...], ws_ref[...], preferred_element_type=F32)
        pa = jnp.dot(a_ref[...], wa_ref[...], preferred_element_type=F32)
        o_ref[...] = (ga_ref[...] * ps + gb_ref[...] * pa).astype(o_ref.dtype)

    i = pl.program_id(0)
    pl.when(i < n_split)(lambda: body(sp_ref))
    pl.when(i >= n_split)(lambda: body(ss_ref))


def _mix_call(ssm_p, ssm_s, attn_out, gates, l, w_ps, w_pa, *, tm=1024, tn=512):
    n_tok = attn_out.shape[0]
    nj = D_MODEL // tn
    n_split = ssm_p.shape[0] // tm
    return pl.pallas_call(
        functools.partial(_mix_kernel, n_split=n_split),
        out_shape=jax.ShapeDtypeStruct((n_tok, D_MODEL), BF16),
        grid=(n_tok // tm, nj),
        in_specs=[
            pl.BlockSpec((tm, D_SSM), lambda i, j: (jnp.minimum(i, n_split - 1), 0)),
            pl.BlockSpec((tm, D_SSM), lambda i, j: (jnp.maximum(i - n_split, 0), 0)),
            pl.BlockSpec((tm, D_ATTN), lambda i, j: (i, 0)),
            pl.BlockSpec((tm, tn), lambda i, j: (i, j)),
            pl.BlockSpec((tm, tn), lambda i, j: (i, j + nj)),
            pl.BlockSpec((None, D_SSM, tn), lambda i, j: (l, 0, j)),
            pl.BlockSpec((None, D_ATTN, tn), lambda i, j: (l, 0, j)),
        ],
        out_specs=pl.BlockSpec((tm, tn), lambda i, j: (i, j)),
        compiler_params=_cparams(("arbitrary", "arbitrary")),
        name="branch_merge",
    )(ssm_p, ssm_s, attn_out, gates, gates, w_ps, w_pa)


def _resid_mm_kernel(a_ref, w_ref, x_ref, g_ref, *o_refs, nsub, n_split):
    acc = jnp.dot(a_ref[...], w_ref[...], preferred_element_type=F32)

    def write(o_ref):
        for s in range(nsub):
            rows = slice(s * MOD_BLOCK, (s + 1) * MOD_BLOCK)
            o_ref[rows, :] = x_ref[rows, :] + g_ref[s:s + 1, :] * acc[rows, :]

    if n_split is None:
        write(o_refs[0])
    else:
        i = pl.program_id(0)
        pl.when(i < n_split)(lambda: write(o_refs[0]))
        pl.when(i >= n_split)(lambda: write(o_refs[1]))


def _resid_mm_call(a, l, w, x, modx, g_idx, *, n_split=None, tm=1024, tn=512, name):
    n_tok, k = a.shape
    nsub = tm // MOD_BLOCK
    nj = D_MODEL // tn
    n_tiles = n_tok // tm
    if n_split is None:
        out_shape = jax.ShapeDtypeStruct((n_tok, D_MODEL), F32)
        out_specs = pl.BlockSpec((tm, tn), lambda i, j: (i, j))
    else:
        out_shape = (jax.ShapeDtypeStruct((n_split * tm, D_MODEL), F32),
                     jax.ShapeDtypeStruct(((n_tiles - n_split) * tm, D_MODEL), F32))
        out_specs = (
            pl.BlockSpec((tm, tn), lambda i, j: (jnp.minimum(i, n_split - 1), jnp.where(i < n_split, j, nj - 1))),
            pl.BlockSpec((tm, tn), lambda i, j: (jnp.maximum(i - n_split, 0), jnp.where(i < n_split, 0, j))),
        )
    return pl.pallas_call(
        functools.partial(_resid_mm_kernel, nsub=nsub, n_split=n_split),
        out_shape=out_shape,
        grid=(n_tiles, nj),
        in_specs=[
            pl.BlockSpec((tm, k), lambda i, j: (i, 0)),
            pl.BlockSpec((None, k, tn), lambda i, j: (l, 0, j)),
            pl.BlockSpec((tm, tn), lambda i, j: (i, j)),
            pl.BlockSpec((None, nsub, tn), lambda i, j: (l, i, g_idx * nj + j)),
        ],
        out_specs=out_specs,
        compiler_params=_cparams(("arbitrary", "arbitrary")),
        name=name,
    )(a, w, x, modx)


SCAN_LW = 512


def _ssm_kernel(u_ref, h0r_ref, h0i_ref, wbr_ref, wbi_ref, pwj_ref, seg_ref, wcr_ref, wci_ref, d_ref, wglu_ref,
                bglu_ref, o_ref, htr_ref, hti_ref, bur, bui, cre, cim, z_scr, *, tt):
    steps = tt // S5_SEG

    @pl.when(pl.program_id(1) == 0)
    def _():
        cre[...] = jnp.broadcast_to(h0r_ref[...], (SUBLANES, STATE_W))
        cim[...] = jnp.broadcast_to(h0i_ref[...], (SUBLANES, STATE_W))

    r_idx = lax.broadcasted_iota(jnp.int32, (tt, tt), 0)
    c_idx = lax.broadcasted_iota(jnp.int32, (tt, tt), 1)
    to_scan = jnp.where(c_idx == (r_idx % S5_SEG) * steps + r_idx // S5_SEG, 1.0, 0.0).astype(BF16)
    to_time = jnp.where(r_idx == (c_idx % S5_SEG) * steps + c_idx // S5_SEG, 1.0, 0.0).astype(BF16)

    u = u_ref[...]
    us = jnp.dot(to_scan, u.astype(BF16), preferred_element_type=F32).astype(BF16)
    kw = D_SSM // SSM_KC
    sw = STATE_W // SSM_KC
    for kc in range(SSM_KC):
        uk = us[:, kc * kw:(kc + 1) * kw]
        bur[:, kc * sw:(kc + 1) * sw] = jnp.dot(uk, wbr_ref[kc], preferred_element_type=F32)
        bui[:, kc * sw:(kc + 1) * sw] = jnp.dot(uk, wbi_ref[kc], preferred_element_type=F32)

    row = lax.broadcasted_iota(jnp.int32, (SUBLANES, SCAN_LW), 0)
    for lc in range(STATE_W // SCAN_LW):
        sl = slice(lc * SCAN_LW, (lc + 1) * SCAN_LW)
        lam_re = jnp.broadcast_to(pwj_ref[0, 0:1, sl], (SUBLANES, SCAN_LW))
        lam_im = jnp.broadcast_to(pwj_ref[1, 0:1, sl], (SUBLANES, SCAN_LW))

        def step(j, carry, sl=sl, lam_re=lam_re, lam_im=lam_im):
            h_re, h_im = carry
            rows = pl.ds(pl.multiple_of(j * SUBLANES, SUBLANES), SUBLANES)
            return (lam_re * h_re - lam_im * h_im + bur[rows, sl],
                    lam_re * h_im + lam_im * h_re + bui[rows, sl])

        zero = jnp.zeros((SUBLANES, SCAN_LW), F32)
        e_re, e_im = lax.fori_loop(0, steps, step, (zero, zero), unroll=4)

        x_re = jnp.where(row == 0, cre[:, sl], pltpu.roll(e_re, 1, 0))
        x_im = jnp.where(row == 0, cim[:, sl], pltpu.roll(e_im, 1, 0))
        for idx, k in enumerate((1, 2, 4)):
            m_re = seg_ref[2 * idx, :, sl]
            m_im = seg_ref[2 * idx + 1, :, sl]
            s_re = pltpu.roll(x_re, k, 0)
            s_im = pltpu.roll(x_im, k, 0)
            x_re, x_im = (x_re + m_re * s_re - m_im * s_im,
                          x_im + m_re * s_im + m_im * s_re)
        pj_re = jnp.broadcast_to(pwj_ref[0, 1:2, sl], (SUBLANES, SCAN_LW))
        pj_im = jnp.broadcast_to(pwj_ref[1, 1:2, sl], (SUBLANES, SCAN_LW))
        n_re = pj_re * x_re - pj_im * x_im + e_re
        n_im = pj_re * x_im + pj_im * x_re + e_im
        cre[:, sl] = jnp.broadcast_to(n_re[SUBLANES - 1:SUBLANES, :], (SUBLANES, SCAN_LW))
        cim[:, sl] = jnp.broadcast_to(n_im[SUBLANES - 1:SUBLANES, :], (SUBLANES, SCAN_LW))

        def scan(j, carry, sl=sl, step=step):
            h_re, h_im = step(j, carry)
            rows = pl.ds(pl.multiple_of(j * SUBLANES, SUBLANES), SUBLANES)
            bur[rows, sl] = h_re
            bui[rows, sl] = h_im
            return h_re, h_im

        lax.fori_loop(0, steps, scan, (x_re, x_im), unroll=2)

    htr_ref[...] = cre[0:1, :]
    hti_ref[...] = cim[0:1, :]

    d = d_ref[...]
    for kc in range(SSM_KC):
        h_re = bur[:, kc * sw:(kc + 1) * sw].astype(BF16)
        h_im = bui[:, kc * sw:(kc + 1) * sw].astype(BF16)
        ys = (jnp.dot(h_re, wcr_ref[kc], preferred_element_type=F32)
              - jnp.dot(h_im, wci_ref[kc], preferred_element_type=F32))
        ys_hi = ys.astype(BF16)
        ys_lo = (ys - ys_hi.astype(F32)).astype(BF16)
        y = (jnp.dot(to_time, ys_hi, preferred_element_type=F32)
             + jnp.dot(to_time, ys_lo, preferred_element_type=F32))
        cols = slice(kc * kw, (kc + 1) * kw)
        y = y + d[:, cols] * u[:, cols]
        z_scr[:, cols] = 0.5 * y * (1.0 + jnp.tanh(math.sqrt(2.0 / math.pi) * (y + 0.044715 * (y * y * y))))
    z = z_scr[...]
    gate = _sigmoid(jnp.dot(z.astype(BF16), wglu_ref[...], preferred_element_type=F32) + bglu_ref[...])
    o_ref[...] = (z * gate).astype(o_ref.dtype)


def _ssm_call(proj, row0, n_seq, t_len, tt, h0_re, h0_im, l, wb_re, wb_im, pw_re, pw_im, wc_re, wc_im, d_skip,
              w_glu, b_glu, *, name):
    steps = tt // S5_SEG
    power = lambda n: S5_POWERS.index(n)
    pwj = jnp.stack([pw[:, (power(1), power(steps)), :] for pw in (pw_re, pw_im)], axis=1)
    seg_rows = []
    for k in (1, 2, 4):
        keep = (jnp.arange(SUBLANES) >= k)[None, :, None]
        seg_rows += [jnp.where(keep, pw[:, power(k * steps)][:, None, :], 0.0) for pw in (pw_re, pw_im)]
    seg = jnp.stack(seg_rows, axis=1)
    nt = t_len // tt
    rb0 = row0 // tt
    state_spec = pl.BlockSpec((None, 1, STATE_W), lambda b, t: (b, 0, 0))
    return pl.pallas_call(
        functools.partial(_ssm_kernel, tt=tt),
        out_shape=(jax.ShapeDtypeStruct((n_seq * t_len, D_SSM), BF16),
                   jax.ShapeDtypeStruct((n_seq, 1, STATE_W), F32),
                   jax.ShapeDtypeStruct((n_seq, 1, STATE_W), F32)),
        grid=(n_seq, nt),
        in_specs=[
            pl.BlockSpec((tt, D_SSM), lambda b, t: (rb0 + b * nt + t, 0)),
            state_spec, state_spec,
            _layer_spec(l, wb_re.shape[1:]), _layer_spec(l, wb_im.shape[1:]),
            _layer_spec(l, pwj.shape[1:]), _layer_spec(l, seg.shape[1:]),
            _layer_spec(l, wc_re.shape[1:]), _layer_spec(l, wc_im.shape[1:]),
            _layer_spec(l, (1, D_SSM)),
            _layer_spec(l, (D_SSM, D_SSM)),
            _layer_spec(l, (1, D_SSM)),
        ],
        out_specs=(pl.BlockSpec((tt, D_SSM), lambda b, t: (b * nt + t, 0)), state_spec, state_spec),
        scratch_shapes=[pltpu.VMEM((tt, STATE_W), F32), pltpu.VMEM((tt, STATE_W), F32),
                        pltpu.VMEM((SUBLANES, STATE_W), F32), pltpu.VMEM((SUBLANES, STATE_W), F32),
                        pltpu.VMEM((tt, D_SSM), F32)],
        compiler_params=_cparams(("arbitrary", "arbitrary")),
        name=name,
    )(proj, h0_re, h0_im, wb_re, wb_im, pwj, seg, wc_re, wc_im, d_skip.reshape(DEPTH, 1, D_SSM), w_glu,
      b_glu.reshape(DEPTH, 1, D_SSM))


RING = 3
N_KEYS = RING * CHUNK
ATT_CPS = 4


def _stack_blocks(x):
    return jnp.concatenate([x[:, j * LANES:(j + 1) * LANES] for j in range(x.shape[1] // LANES)], axis=0)


def _pair_norm_rope(xs, gain, tab, ones_bd):
    reps = xs.shape[0] // CHUNK
    cos, s_lo, s_hi = (jnp.concatenate([tab[i]] * reps, axis=0) for i in range(3))
    half = ROPE_DIM // 2
    sq = xs * xs
    sq_hi = sq.astype(BF16)
    sq_lo = (sq - sq_hi.astype(F32)).astype(BF16)
    ss = (jnp.dot(sq_hi, ones_bd, preferred_element_type=F32)
          + jnp.dot(sq_lo, ones_bd, preferred_element_type=F32))
    xg = xs * gain
    xr = xg * cos + pltpu.roll(xg, LANES - half, 1) * s_lo + pltpu.roll(xg, half, 1) * s_hi
    return xr * lax.rsqrt(ss * (1.0 / HEAD_DIM) + EPS)


def _attn_kernel(sink_ref, q_ref, k_ref, v_ref, ck_ref, cv_ref, tab_ref, gq_ref, gk_ref, o_ref, kn_ref, kd, vd,
                 *, l, n_prompt_steps, steps_per_seq):
    step = pl.program_id(0)
    is_sample = step >= n_prompt_steps
    c_base = lax.rem(step, steps_per_seq) * ATT_CPS
    first = lax.broadcasted_iota(jnp.int32, (1, LANES), 1) < HEAD_DIM
    ones_bd = jnp.where(lax.broadcasted_iota(jnp.int32, (LANES, LANES), 0) // HEAD_DIM
                        == lax.broadcasted_iota(jnp.int32, (LANES, LANES), 1) // HEAD_DIM, 1.0, 0.0).astype(BF16)
    key_slot = lax.broadcasted_iota(jnp.int32, (1, N_KEYS), 1) // CHUNK
    gq = gq_ref[...]
    gk = gk_ref[...]

    def store_dup(dst, x, rows):
        for pair in range(D_KV // LANES):
            blk = x[:, pair * LANES:(pair + 1) * LANES]
            swapped = pltpu.roll(blk, HEAD_DIM, 1)
            dst[2 * pair, rows, 0:LANES] = jnp.where(first, blk, swapped).astype(BF16)
            dst[2 * pair + 1, rows, 0:LANES] = jnp.where(first, swapped, blk).astype(BF16)

    @pl.when(step == 0)
    def _():
        vd[:, :, LANES:] = jnp.ones((N_KV_HEADS, N_KEYS, LANES), BF16)

    def chunk(ci, carry):
        c = jnp.where(is_sample, 0, c_base + ci)
        rows = pl.ds(pl.multiple_of(ci * CHUNK, CHUNK), CHUNK)
        prev = slice(CHUNK, N_KEYS)

        @pl.when(jnp.logical_and(c == 0, jnp.logical_not(is_sample)))
        def _():
            kd[:, prev, :] = jnp.zeros((N_KV_HEADS, N_KEYS - CHUNK, LANES), BF16)
            vd[:, prev, 0:LANES] = jnp.zeros((N_KV_HEADS, N_KEYS - CHUNK, LANES), BF16)

        @pl.when(is_sample)
        def _():
            cached = pl.ds(pl.multiple_of(ci * WINDOW, WINDOW), WINDOW)
            store_dup(kd, ck_ref[cached, :], prev)
            store_dup(vd, cv_ref[cached, :], prev)

        tab = tab_ref[:, rows, :]
        own = pl.ds(pl.multiple_of(lax.rem(c, RING) * CHUNK, CHUNK), CHUNK)
        kn = _pair_norm_rope(_stack_blocks(k_ref[rows, :]), gk, tab, ones_bd)
        kn = jnp.concatenate([kn[0:CHUNK], kn[CHUNK:2 * CHUNK]], axis=1)
        kn_ref[rows, :] = kn
        store_dup(kd, kn, own)
        store_dup(vd, v_ref[rows, :], own)

        qn = _pair_norm_rope(_stack_blocks(q_ref[rows, :]), gq, tab, ones_bd) * (HEAD_DIM ** -0.5)
        q_lo = jnp.where(first, qn, 0.0).astype(BF16)
        q_hi = jnp.where(first, 0.0, qn).astype(BF16)

        valid = key_slot <= jnp.where(is_sample, RING, c)
        scores = []
        for kh in range(N_KV_HEADS):
            r0 = kh * 2 * CHUNK
            lhs = jnp.concatenate([q_lo[r0:r0 + CHUNK], q_hi[r0:r0 + CHUNK],
                                   q_lo[r0 + CHUNK:r0 + 2 * CHUNK], q_hi[r0 + CHUNK:r0 + 2 * CHUNK]], axis=0)
            scores.append(lax.dot_general(lhs, kd[kh], (((1,), (1,)), ((), ())),
                                          preferred_element_type=F32))
        weighted, sink_terms = [], []
        for kh in range(N_KV_HEADS):
            s = jnp.where(valid, scores[kh], -1e30)
            sink = jnp.concatenate([jnp.full((CHUNK, 1), sink_ref[l, kh * KV_REP + r], F32)
                                    for r in range(KV_REP)], axis=0)
            m = jnp.maximum(jnp.max(s, axis=-1, keepdims=True), sink)
            p = jnp.exp(s - m).astype(BF16)
            sink_terms.append(jnp.exp(sink - m))
            weighted.append(jnp.dot(p, vd[kh], preferred_element_type=F32))
        for kh in range(N_KV_HEADS):
            od = weighted[kh]
            o = od[:, 0:LANES] / (od[:, LANES:] + sink_terms[kh])
            for pair in range(2):
                half = pair * 2 * CHUNK
                blk = jnp.where(first, o[half:half + CHUNK], o[half + CHUNK:half + 2 * CHUNK])
                col = (2 * kh + pair) * LANES
                o_ref[rows, col:col + LANES] = blk.astype(o_ref.dtype)
        return carry

    lax.fori_loop(0, ATT_CPS, chunk, 0)


def _attn_call(proj, l, n_prompt, t_prompt, sink, rope_tab, gq, gk, cache_k, cache_v):
    n_tok = proj.shape[0]
    tr = ATT_CPS * CHUNK
    qcol = D_SSM // D_ATTN
    kcol = (D_SSM + D_ATTN) // D_KV
    n_prompt_steps = n_prompt // tr
    cache_spec = pl.BlockSpec((None, ATT_CPS * WINDOW, D_KV),
                              lambda s: (l, jnp.maximum(s - n_prompt_steps, 0), 0))
    return pl.pallas_call(
        functools.partial(_attn_kernel, l=l, n_prompt_steps=n_prompt_steps, steps_per_seq=t_prompt // tr),
        out_shape=(jax.ShapeDtypeStruct((n_tok, D_ATTN), BF16),
                   jax.ShapeDtypeStruct((n_tok, D_KV), F32)),
        grid=(n_tok // tr,),
        in_specs=[
            pl.BlockSpec(memory_space=pltpu.SMEM),
            pl.BlockSpec((tr, D_ATTN), lambda s: (s, qcol)),
            pl.BlockSpec((tr, D_KV), lambda s: (s, kcol)),
            pl.BlockSpec((tr, D_KV), lambda s: (s, kcol + 1)),
            cache_spec, cache_spec,
            pl.BlockSpec((3, tr, LANES), lambda s: (0, s, 0)),
            _layer_spec(l, (1, LANES)), _layer_spec(l, (1, LANES)),
        ],
        out_specs=(pl.BlockSpec((tr, D_ATTN), lambda s: (s, 0)),
                   pl.BlockSpec((tr, D_KV), lambda s: (s, 0))),
        scratch_shapes=[pltpu.VMEM((N_KV_HEADS, N_KEYS, LANES), BF16),
                        pltpu.VMEM((N_KV_HEADS, N_KEYS, 2 * LANES), BF16)],
        compiler_params=_cparams(("arbitrary",)),
        name="banded_attn",
    )(sink, proj, proj, proj, cache_k, cache_v, rope_tab, gq, gk)


def _block_diag_rows(x, n):
    *lead, r, width = x.shape
    keep = jnp.arange(n)[:, None, None] == (jnp.arange(width) // (width // n))[None, None, :]
    return jnp.where(keep, x[..., None, :, :], 0.0).reshape(*lead, n * r, width)


def _rope_table(pos):
    half = ROPE_DIM // 2
    n_pos = pos.shape[0]
    inv_freq = ROPE_THETA ** (-jnp.arange(half, dtype=F32) / half)
    ang = pos.astype(F32)[:, None] * inv_freq[None, :]
    cos, sin = jnp.cos(ang), jnp.sin(ang)
    ones = jnp.ones((n_pos, HEAD_DIM - ROPE_DIM), F32)
    zeros = jnp.zeros((n_pos, HEAD_DIM - half), F32)
    c_tab = jnp.concatenate([cos, cos, ones], axis=1)
    lo_tab = jnp.concatenate([-sin, zeros], axis=1)
    hi_tab = jnp.concatenate([jnp.zeros((n_pos, half), F32), sin, jnp.zeros((n_pos, HEAD_DIM - ROPE_DIM), F32)],
                             axis=1)
    tab = jnp.stack([c_tab, lo_tab, hi_tab])
    return jnp.concatenate([tab, tab], axis=2)


def kernel(x_prompt, x_sample, cache_k, cache_v, state_ssm_re, state_ssm_im, c_prompt, c_sample, w_mod, b_mod, norm1_g, norm2_g, w_in, ssm_a_re, ssm_a_im, ssm_log_dt, ssm_b_re, ssm_b_im, ssm_c_re, ssm_c_im, ssm_d, w_glu, b_glu, q_norm_g, k_norm_g, attn_sink, w_gate, b_gate, w_proj_ssm, w_proj_attn, w_out, w_ffn_gate, w_ffn_up, w_ffn_down):
    bp, tp, _ = x_prompt.shape
    bs, ts, _ = x_sample.shape
    assert ts == CHUNK and tp % (ATT_CPS * CHUNK) == 0 and bs % ATT_CPS == 0
    n_p, n_s = bp * tp, bs * ts
    tm = 1024
    x = jnp.concatenate([x_prompt.reshape(n_p, D_MODEL), x_sample.reshape(n_s, D_MODEL)], axis=0)

    n_cond = bp + bs
    pad = (-n_cond) % SUBLANES
    c_all = jnp.concatenate([c_prompt, c_sample, jnp.zeros((pad, D_MODEL), F32)], axis=0)
    mod = _mod_call(c_all, w_mod, b_mod)
    n_mod = mod.shape[2]

    def per_block(m, n_seq, t_len):
        reps = t_len // MOD_BLOCK
        return jnp.broadcast_to(m[:, :, None, :], (DEPTH, n_seq, reps, n_mod)).reshape(DEPTH, n_seq * reps, n_mod)

    modx = jnp.concatenate([per_block(mod[:, :bp], bp, tp), per_block(mod[:, bp:n_cond], bs, ts)],
                           axis=1)

    g, p = N_SSM_GROUPS, SSM_STATE
    gpc = g // SSM_KC
    pw_re, pw_im, bb_re, bb_im = _s5_prep_call(ssm_a_re, ssm_a_im, ssm_log_dt,
                                               ssm_b_re.transpose(0, 3, 1, 2), ssm_b_im.transpose(0, 3, 1, 2))
    pw_re = pw_re.reshape(DEPTH, len(S5_POWERS), STATE_W)
    pw_im = pw_im.reshape(DEPTH, len(S5_POWERS), STATE_W)

    def wb_blocks(bb):
        per_chunk = bb.reshape(DEPTH, SSM_GROUP, SSM_KC, gpc * p).transpose(0, 2, 1, 3)
        return _block_diag_rows(per_chunk, gpc).astype(BF16)

    def wc_blocks(cc):
        per_chunk = cc.reshape(DEPTH, SSM_KC, gpc, SSM_GROUP, p).transpose(0, 1, 3, 2, 4)
        transposed = _block_diag_rows(per_chunk.reshape(DEPTH, SSM_KC, SSM_GROUP, gpc * p), gpc)
        return jnp.swapaxes(transposed.astype(BF16), -1, -2)

    ssm_w = (wb_blocks(bb_re), wb_blocks(bb_im), pw_re, pw_im, wc_blocks(ssm_c_re), wc_blocks(ssm_c_im), ssm_d,
             w_glu.astype(BF16), b_glu)
    zeros_state = jnp.zeros((bp, 1, STATE_W), F32)
    h0_re = state_ssm_re.reshape(DEPTH, bs, 1, STATE_W)
    h0_im = state_ssm_im.reshape(DEPTH, bs, 1, STATE_W)

    pos = jnp.concatenate([jnp.tile(jnp.arange(tp), bp), jnp.tile(PAST_LEN + jnp.arange(ts), bs)])
    rope_tab = _rope_table(pos)
    gq = jnp.tile(q_norm_g, (1, LANES // HEAD_DIM)).reshape(DEPTH, 1, LANES)
    gk = jnp.tile(k_norm_g, (1, LANES // HEAD_DIM)).reshape(DEPTH, 1, LANES)
    cache_k2 = cache_k.reshape(DEPTH, bs * WINDOW, D_KV)
    cache_v2 = cache_v.reshape(DEPTH, bs * WINDOW, D_KV)

    w_in_b, w_gate_b = w_in.astype(BF16), w_gate.astype(BF16)
    w_ps_b, w_pa_b, w_out_b = w_proj_ssm.astype(BF16), w_proj_attn.astype(BF16), w_out.astype(BF16)
    w_fg_b, w_fu_b, w_fd_b = w_ffn_gate.astype(BF16), w_ffn_up.astype(BF16), w_ffn_down.astype(BF16)
    v0 = D_SSM + D_ATTN + D_KV

    def heads(t):
        return t.reshape(*t.shape[:-1], N_KV_HEADS, HEAD_DIM)

    def last_window(t, col0, col1):
        return jnp.stack([t[(b + 1) * tp - WINDOW:(b + 1) * tp, col0:col1] for b in range(bp)])

    outs = {k: [] for k in ("pk", "pv", "pre", "pim", "sk", "sv", "sre", "sim")}
    for l in range(DEPTH):
        proj, gates = _in_gate_call(x, l, norm1_g, modx, w_in_b, w_gate_b, b_gate, tm=tm)

        ssm_p, pre, pim = _ssm_call(proj, 0, bp, tp, 256, zeros_state, zeros_state, l, *ssm_w, name="s5_prompt")
        ssm_s, sre, sim = _ssm_call(proj, n_p, bs, ts, ts, h0_re[l], h0_im[l], l, *ssm_w, name="s5_sample")
        attn_out, kn = _attn_call(proj, l, n_p, tp, attn_sink, rope_tab, gq, gk, cache_k2, cache_v2)

        mixed = _mix_call(ssm_p, ssm_s, attn_out, gates, l, w_ps_b, w_pa_b, tm=tm, tn=1024)
        x = _resid_mm_call(mixed, l, w_out_b, x, modx, 2, tm=tm, tn=1024, name="out_proj")

        act = _ffn_up_call(x, l, norm2_g, modx, w_fg_b, w_fu_b, tm=tm)
        last = l == DEPTH - 1
        x = _resid_mm_call(act, l, w_fd_b, x, modx, 5, n_split=n_p // tm if last else None, tm=tm,
                           name="ffn_down_split" if last else "ffn_down")

        outs["pk"].append(heads(last_window(kn, 0, D_KV)))
        outs["pv"].append(heads(last_window(proj, v0, IN_WIDTH)))
        outs["pre"].append(pre.reshape(bp, g, p))
        outs["pim"].append(pim.reshape(bp, g, p))
        outs["sk"].append(jnp.concatenate([cache_k[l][:, ts:], heads(kn[n_p:].reshape(bs, ts, D_KV))], axis=1))
        outs["sv"].append(jnp.concatenate([cache_v[l][:, ts:], heads(proj[n_p:, v0:].reshape(bs, ts, D_KV))],
                                          axis=1))
        outs["sre"].append(sre.reshape(bs, g, p))
        outs["sim"].append(sim.reshape(bs, g, p))

    y_p, y_s = x
    return (y_p.reshape(bp, tp, D_MODEL), y_s.reshape(bs, ts, D_MODEL),
            jnp.stack(outs["pk"]), jnp.stack(outs["pv"]), jnp.stack(outs["pre"]), jnp.stack(outs["pim"]),
            jnp.stack(outs["sk"]), jnp.stack(outs["sv"]), jnp.stack(outs["sre"]), jnp.stack(outs["sim"]))
```

```python
import functools
import math

import jax
import jax.numpy as jnp
from jax import lax
from jax.experimental import pallas as pl
from jax.experimental.pallas import tpu as pltpu

D_MODEL = 2048
DEPTH = 4
CHUNK = 64
D_SSM = 1024
SSM_GROUP = 16
N_SSM_GROUPS = 64
SSM_STATE = 64
HEAD_DIM = 64
N_HEADS = 16
N_KV_HEADS = 4
KV_REP = N_HEADS // N_KV_HEADS
D_ATTN = N_HEADS * HEAD_DIM
D_KV = N_KV_HEADS * HEAD_DIM
IN_WIDTH = D_SSM + D_ATTN + 2 * D_KV
WINDOW = 128
ROPE_DIM = 16
ROPE_THETA = 500000.0
D_FF = 5632
EPS = 1e-6
PAST_LEN = 2048

LANES = 128
SUBLANES = 8
MOD_BLOCK = CHUNK
STATE_W = N_SSM_GROUPS * SSM_STATE
SSM_KC = 4
VMEM_LIMIT = 56 * 1024 * 1024

F32 = jnp.float32
BF16 = jnp.bfloat16


def _cparams(sem):
    return pltpu.CompilerParams(dimension_semantics=sem, vmem_limit_bytes=VMEM_LIMIT)


def _sigmoid(x):
    return 0.5 + 0.5 * jnp.tanh(0.5 * x)


def _layer_spec(l, shape):
    zeros = (0,) * len(shape)
    return pl.BlockSpec((None, *shape), lambda *_: (l, *zeros))


def _mod_kernel(c_ref, w_ref, b_ref, o_ref):
    c = c_ref[...].astype(BF16)
    w = w_ref[...].astype(BF16)
    o_ref[...] = jnp.dot(c, w, preferred_element_type=F32) + b_ref[...]


def _mod_call(c_all, w_mod, b_mod):
    nb = c_all.shape[0]
    tn = 1024
    n_out = w_mod.shape[2]
    return pl.pallas_call(
        _mod_kernel,
        out_shape=jax.ShapeDtypeStruct((DEPTH, nb, n_out), F32),
        grid=(DEPTH, n_out // tn),
        in_specs=[
            pl.BlockSpec((nb, D_MODEL), lambda l, j: (0, 0)),
            pl.BlockSpec((None, D_MODEL, tn), lambda l, j: (l, 0, j)),
            pl.BlockSpec((None, 1, tn), lambda l, j: (l, 0, j)),
        ],
        out_specs=pl.BlockSpec((None, nb, tn), lambda l, j: (l, 0, j)),
        compiler_params=_cparams(("arbitrary", "arbitrary")),
        name="adaln_mod",
    )(c_all, w_mod, b_mod.reshape(DEPTH, 1, n_out))


S5_TILES = (256, 64)
S5_SEG = SUBLANES
S5_POWERS = tuple(sorted({n for t in S5_TILES for n in (1, *(k * t // S5_SEG for k in (1, 2, 4)))}))


def _s5_prep_kernel(are_ref, aim_ref, ldt_ref, bre_ref, bim_ref, pwr_ref, pwi_ref, bbr_ref, bbi_ref):
    a_re = are_ref[...]
    a_im = aim_ref[...]
    dt = jnp.exp(ldt_ref[...])
    z_re = a_re * dt
    z_im = a_im * dt
    for i, n in enumerate(S5_POWERS):
        mag = jnp.exp(z_re * float(n))
        pwr_ref[i] = mag * jnp.cos(z_im * float(n))
        pwi_ref[i] = mag * jnp.sin(z_im * float(n))
    l_re = pwr_ref[S5_POWERS.index(1)]
    l_im = pwi_ref[S5_POWERS.index(1)]
    den = a_re * a_re + a_im * a_im
    n_re = l_re - 1.0
    f_re = (n_re * a_re + l_im * a_im) / den
    f_im = (l_im * a_re - n_re * a_im) / den
    for c in range(SSM_GROUP):
        b_re = bre_ref[c]
        b_im = bim_ref[c]
        bbr_ref[c] = f_re * b_re - f_im * b_im
        bbi_ref[c] = f_re * b_im + f_im * b_re


def _s5_prep_call(a_re, a_im, log_dt, bt_re, bt_im):
    g, p = N_SSM_GROUPS, SSM_STATE
    mat = pl.BlockSpec((None, g, p), lambda l: (l, 0, 0))
    stack_c = pl.BlockSpec((None, SSM_GROUP, g, p), lambda l: (l, 0, 0, 0))
    n_pow = len(S5_POWERS)
    stack_n = pl.BlockSpec((None, n_pow, g, p), lambda l: (l, 0, 0, 0))
    return pl.pallas_call(
        _s5_prep_kernel,
        out_shape=(jax.ShapeDtypeStruct((DEPTH, n_pow, g, p), F32),
                   jax.ShapeDtypeStruct((DEPTH, n_pow, g, p), F32),
                   jax.ShapeDtypeStruct((DEPTH, SSM_GROUP, g, p), F32),
                   jax.ShapeDtypeStruct((DEPTH, SSM_GROUP, g, p), F32)),
        grid=(DEPTH,),
        in_specs=[mat, mat, pl.BlockSpec((None, g, 1), lambda l: (l, 0, 0)), stack_c, stack_c],
        out_specs=(stack_n, stack_n, stack_c, stack_c),
        compiler_params=_cparams(("arbitrary",)),
        name="s5_discretize",
    )(a_re, a_im, log_dt.reshape(DEPTH, g, 1), bt_re, bt_im)


NORM_CHUNK = 2 * MOD_BLOCK


def _norm_mod_chunk(x_ref, g_ref, sc_ref, sh_ref, h_scr, slot, chunk):
    gain = g_ref[...]
    per_chunk = NORM_CHUNK // MOD_BLOCK
    for s in range(per_chunk):
        x = x_ref[s * MOD_BLOCK:(s + 1) * MOD_BLOCK, :]
        ms = jnp.mean(x * x, axis=-1, keepdims=True)
        y = x * lax.rsqrt(ms + EPS) * gain
        mod_row = pl.ds(chunk * per_chunk + s, 1)
        h = y * (1.0 + sc_ref[mod_row, :]) + sh_ref[mod_row, :]
        rows = pl.ds(pl.multiple_of((chunk * per_chunk + s) * MOD_BLOCK, MOD_BLOCK), MOD_BLOCK)
        h_scr[slot, rows, :] = h.astype(BF16)


def _norm_pipeline(l, n_tiles, tm, sc_idx, sh_idx):
    n_chunks = tm // NORM_CHUNK
    nsub = tm // MOD_BLOCK
    tile = lambda i: jnp.minimum(i, n_tiles - 1)
    specs = [
        pl.BlockSpec((NORM_CHUNK, D_MODEL), lambda i, j: (tile(i) * n_chunks + jnp.minimum(j, n_chunks - 1), 0)),
        _layer_spec(l, (1, D_MODEL)),
        pl.BlockSpec((None, nsub, D_MODEL), lambda i, j: (l, tile(i), sc_idx)),
        pl.BlockSpec((None, nsub, D_MODEL), lambda i, j: (l, tile(i), sh_idx)),
    ]
    scratch = pltpu.VMEM((2, tm, D_MODEL), BF16)
    out_row = lambda i: jnp.maximum(i - 1, 0)
    out_col = lambda i, col: jnp.where(i == 0, 0, col)
    return specs, scratch, n_chunks, out_row, out_col


def _in_gate_kernel(x_ref, g_ref, sc_ref, sh_ref, wi_ref, wg_ref, b_ref, proj_ref, gate_ref, h_scr, *, n_in,
                    n_chunks):
    i, j = pl.program_id(0), pl.program_id(1)
    ready = lax.rem(i + 1, 2)

    filling = j < n_chunks

    def fill():
        _norm_mod_chunk(x_ref, g_ref, sc_ref, sh_ref, h_scr, lax.rem(i, 2), j)

    def gates():
        acc = jnp.dot(h_scr[ready], wg_ref[...], preferred_element_type=F32)
        gate_ref[...] = _sigmoid(acc + b_ref[...]).astype(gate_ref.dtype)

    pl.when(jnp.logical_and(i == 0, filling))(fill)

    @pl.when(jnp.logical_and(i > 0, j < n_in))
    def _():
        proj_ref[...] = jnp.dot(h_scr[ready], wi_ref[...], preferred_element_type=F32)
        fill()

    @pl.when(jnp.logical_and(i > 0, jnp.logical_and(j >= n_in, filling)))
    def _():
        gates()
        fill()

    pl.when(jnp.logical_and(i > 0, jnp.logical_not(filling)))(gates)


def _in_gate_call(x, l, gain, modx, w_in, w_gate, b_gate, *, tm=1024, tn=512):
    n_tiles = x.shape[0] // tm
    n_in, n_gate = w_in.shape[2] // tn, w_gate.shape[2] // tn
    norm_specs, h_scratch, n_chunks, out_row, out_col = _norm_pipeline(l, n_tiles, tm, 1, 0)
    assert n_in <= n_chunks <= n_in + n_gate
    in_col = lambda j: jnp.minimum(j, n_in - 1)
    gate_col = lambda j: jnp.maximum(j - n_in, 0)
    return pl.pallas_call(
        functools.partial(_in_gate_kernel, n_in=n_in, n_chunks=n_chunks),
        out_shape=(jax.ShapeDtypeStruct((x.shape[0], w_in.shape[2]), F32),
                   jax.ShapeDtypeStruct((x.shape[0], w_gate.shape[2]), BF16)),
        grid=(n_tiles + 1, n_in + n_gate),
        in_specs=norm_specs + [
            pl.BlockSpec((None, D_MODEL, tn), lambda i, j: (l, 0, in_col(j))),
            pl.BlockSpec((None, D_MODEL, tn), lambda i, j: (l, 0, gate_col(j))),
            pl.BlockSpec((None, 1, tn), lambda i, j: (l, 0, gate_col(j))),
        ],
        out_specs=(pl.BlockSpec((tm, tn), lambda i, j: (out_row(i), out_col(i, in_col(j)))),
                   pl.BlockSpec((tm, tn), lambda i, j: (out_row(i), out_col(i, gate_col(j))))),
        scratch_shapes=[h_scratch],
        compiler_params=_cparams(("arbitrary", "arbitrary")),
        name="in_proj_gates",
    )(x, gain.reshape(DEPTH, 1, D_MODEL), modx, modx, w_in, w_gate, b_gate.reshape(DEPTH, 1, w_gate.shape[2]))


def _ffn_up_kernel(x_ref, g_ref, sc_ref, sh_ref, wg_ref, wu_ref, o_ref, h_scr, *, n_chunks):
    i, j = pl.program_id(0), pl.program_id(1)

    filling = j < n_chunks

    def fill():
        _norm_mod_chunk(x_ref, g_ref, sc_ref, sh_ref, h_scr, lax.rem(i, 2), j)

    def swiglu():
        h = h_scr[lax.rem(i + 1, 2)]
        gate = jnp.dot(h, wg_ref[...], preferred_element_type=F32)
        up = jnp.dot(h, wu_ref[...], preferred_element_type=F32)
        o_ref[...] = (gate * _sigmoid(gate) * up).astype(o_ref.dtype)

    pl.when(jnp.logical_and(i == 0, filling))(fill)

    @pl.when(jnp.logical_and(i > 0, filling))
    def _():
        swiglu()
        fill()

    pl.when(jnp.logical_and(i > 0, jnp.logical_not(filling)))(swiglu)


def _ffn_up_call(x, l, gain, modx, w_gate, w_up, *, tm=1024, tn=512):
    n_tiles = x.shape[0] // tm
    n_out = w_gate.shape[2]
    norm_specs, h_scratch, n_chunks, out_row, out_col = _norm_pipeline(l, n_tiles, tm, 4, 3)
    assert n_out // tn >= n_chunks
    w_spec = pl.BlockSpec((None, D_MODEL, tn), lambda i, j: (l, 0, j))
    return pl.pallas_call(
        functools.partial(_ffn_up_kernel, n_chunks=n_chunks),
        out_shape=jax.ShapeDtypeStruct((x.shape[0], n_out), BF16),
        grid=(n_tiles + 1, n_out // tn),
        in_specs=norm_specs + [w_spec, w_spec],
        out_specs=pl.BlockSpec((tm, tn), lambda i, j: (out_row(i), out_col(i, j))),
        scratch_shapes=[h_scratch],
        compiler_params=_cparams(("arbitrary", "arbitrary")),
        name="ffn_up",
    )(x, gain.reshape(DEPTH, 1, D_MODEL), modx, modx, w_gate, w_up)


def _mix_kernel(sp_ref, ss_ref, a_ref, ga_ref, gb_ref, ws_ref, wa_ref, o_ref, *, n_split):
    def body(s_ref):
        ps = jnp.dot(s_ref[...], ws_ref[...], preferred_element_type=F32)
        pa = jnp.dot(a_ref[...], wa_ref[...], preferred_element_type=F32)
        o_ref[...] = (ga_ref[...] * ps + gb_ref[...] * pa).astype(o_ref.dtype)

    i = pl.program_id(0)
    pl.when(i < n_split)(lambda: body(sp_ref))
    pl.when(i >= n_split)(lambda: body(ss_ref))


def _mix_call(ssm_p, ssm_s, attn_out, gates, l, w_ps, w_pa, *, tm=1024, tn=512):
    n_tok = attn_out.shape[0]
    nj = D_MODEL // tn
    n_split = ssm_p.shape[0] // tm
    return pl.pallas_call(
        functools.partial(_mix_kernel, n_split=n_split),
        out_shape=jax.ShapeDtypeStruct((n_tok, D_MODEL), BF16),
        grid=(n_tok // tm, nj),
        in_specs=[
            pl.BlockSpec((tm, D_SSM), lambda i, j: (jnp.minimum(i, n_split - 1), 0)),
            pl.BlockSpec((tm, D_SSM), lambda i, j: (jnp.maximum(i - n_split, 0), 0)),
            pl.BlockSpec((tm, D_ATTN), lambda i, j: (i, 0)),
            pl.BlockSpec((tm, tn), lambda i, j: (i, j)),
            pl.BlockSpec((tm, tn), lambda i, j: (i, j + nj)),
            pl.BlockSpec((None, D_SSM, tn), lambda i, j: (l, 0, j)),
            pl.BlockSpec((None, D_ATTN, tn), lambda i, j: (l, 0, j)),
        ],
        out_specs=pl.BlockSpec((tm, tn), lambda i, j: (i, j)),
        compiler_params=_cparams(("arbitrary", "arbitrary")),
        name="branch_merge",
    )(ssm_p, ssm_s, attn_out, gates, gates, w_ps, w_pa)


def _resid_mm_kernel(a_ref, w_ref, x_ref, g_ref, *o_refs, nsub, n_split):
    acc = jnp.dot(a_ref[...], w_ref[...], preferred_element_type=F32)

    def write(o_ref):
        for s in range(nsub):
            rows = slice(s * MOD_BLOCK, (s + 1) * MOD_BLOCK)
            o_ref[rows, :] = x_ref[rows, :] + g_ref[s:s + 1, :] * acc[rows, :]

    if n_split is None:
        write(o_refs[0])
    else:
        i = pl.program_id(0)
        pl.when(i < n_split)(lambda: write(o_refs[0]))
        pl.when(i >= n_split)(lambda: write(o_refs[1]))


def _resid_mm_call(a, l, w, x, modx, g_idx, *, n_split=None, tm=1024, tn=512, name):
    n_tok, k = a.shape
    nsub = tm // MOD_BLOCK
    nj = D_MODEL // tn
    n_tiles = n_tok // tm
    if n_split is None:
        out_shape = jax.ShapeDtypeStruct((n_tok, D_MODEL), F32)
        out_specs = pl.BlockSpec((tm, tn), lambda i, j: (i, j))
    else:
        out_shape = (jax.ShapeDtypeStruct((n_split * tm, D_MODEL), F32),
                     jax.ShapeDtypeStruct(((n_tiles - n_split) * tm, D_MODEL), F32))
        out_specs = (
            pl.BlockSpec((tm, tn), lambda i, j: (jnp.minimum(i, n_split - 1), jnp.where(i < n_split, j, nj - 1))),
            pl.BlockSpec((tm, tn), lambda i, j: (jnp.maximum(i - n_split, 0), jnp.where(i < n_split, 0, j))),
        )
    return pl.pallas_call(
        functools.partial(_resid_mm_kernel, nsub=nsub, n_split=n_split),
        out_shape=out_shape,
        grid=(n_tiles, nj),
        in_specs=[
            pl.BlockSpec((tm, k), lambda i, j: (i, 0)),
            pl.BlockSpec((None, k, tn), lambda i, j: (l, 0, j)),
            pl.BlockSpec((tm, tn), lambda i, j: (i, j)),
            pl.BlockSpec((None, nsub, tn), lambda i, j: (l, i, g_idx * nj + j)),
        ],
        out_specs=out_specs,
        compiler_params=_cparams(("arbitrary", "arbitrary")),
        name=name,
    )(a, w, x, modx)


SCAN_LW = 512


def _ssm_kernel(u_ref, h0r_ref, h0i_ref, wbr_ref, wbi_ref, pwj_ref, seg_ref, wcr_ref, wci_ref, d_ref, wglu_ref,
                bglu_ref, o_ref, htr_ref, hti_ref, bur, bui, cre, cim, z_scr, *, tt):
    steps = tt // S5_SEG

    @pl.when(pl.program_id(1) == 0)
    def _():
        cre[...] = jnp.broadcast_to(h0r_ref[...], (SUBLANES, STATE_W))
        cim[...] = jnp.broadcast_to(h0i_ref[...], (SUBLANES, STATE_W))

    r_idx = lax.broadcasted_iota(jnp.int32, (tt, tt), 0)
    c_idx = lax.broadcasted_iota(jnp.int32, (tt, tt), 1)
    to_scan = jnp.where(c_idx == (r_idx % S5_SEG) * steps + r_idx // S5_SEG, 1.0, 0.0).astype(BF16)
    to_time = jnp.where(r_idx == (c_idx % S5_SEG) * steps + c_idx // S5_SEG, 1.0, 0.0).astype(BF16)

    u = u_ref[...]
    us = jnp.dot(to_scan, u.astype(BF16), preferred_element_type=F32).astype(BF16)
    kw = D_SSM // SSM_KC
    sw = STATE_W // SSM_KC
    for kc in range(SSM_KC):
        uk = us[:, kc * kw:(kc + 1) * kw]
        bur[:, kc * sw:(kc + 1) * sw] = jnp.dot(uk, wbr_ref[kc], preferred_element_type=F32)
        bui[:, kc * sw:(kc + 1) * sw] = jnp.dot(uk, wbi_ref[kc], preferred_element_type=F32)

    row = lax.broadcasted_iota(jnp.int32, (SUBLANES, SCAN_LW), 0)
    for lc in range(STATE_W // SCAN_LW):
        sl = slice(lc * SCAN_LW, (lc + 1) * SCAN_LW)
        lam_re = jnp.broadcast_to(pwj_ref[0, 0:1, sl], (SUBLANES, SCAN_LW))
        lam_im = jnp.broadcast_to(pwj_ref[1, 0:1, sl], (SUBLANES, SCAN_LW))

        def step(j, carry, sl=sl, lam_re=lam_re, lam_im=lam_im):
            h_re, h_im = carry
            rows = pl.ds(pl.multiple_of(j * SUBLANES, SUBLANES), SUBLANES)
            return (lam_re * h_re - lam_im * h_im + bur[rows, sl],
                    lam_re * h_im + lam_im * h_re + bui[rows, sl])

        zero = jnp.zeros((SUBLANES, SCAN_LW), F32)
        e_re, e_im = lax.fori_loop(0, steps, step, (zero, zero), unroll=4)

        x_re = jnp.where(row == 0, cre[:, sl], pltpu.roll(e_re, 1, 0))
        x_im = jnp.where(row == 0, cim[:, sl], pltpu.roll(e_im, 1, 0))
        for idx, k in enumerate((1, 2, 4)):
            m_re = seg_ref[2 * idx, :, sl]
            m_im = seg_ref[2 * idx + 1, :, sl]
            s_re = pltpu.roll(x_re, k, 0)
            s_im = pltpu.roll(x_im, k, 0)
            x_re, x_im = (x_re + m_re * s_re - m_im * s_im,
                          x_im + m_re * s_im + m_im * s_re)
        pj_re = jnp.broadcast_to(pwj_ref[0, 1:2, sl], (SUBLANES, SCAN_LW))
        pj_im = jnp.broadcast_to(pwj_ref[1, 1:2, sl], (SUBLANES, SCAN_LW))
        n_re = pj_re * x_re - pj_im * x_im + e_re
        n_im = pj_re * x_im + pj_im * x_re + e_im
        cre[:, sl] = jnp.broadcast_to(n_re[SUBLANES - 1:SUBLANES, :], (SUBLANES, SCAN_LW))
        cim[:, sl] = jnp.broadcast_to(n_im[SUBLANES - 1:SUBLANES, :], (SUBLANES, SCAN_LW))

        def scan(j, carry, sl=sl, step=step):
            h_re, h_im = step(j, carry)
            rows = pl.ds(pl.multiple_of(j * SUBLANES, SUBLANES), SUBLANES)
            bur[rows, sl] = h_re
            bui[rows, sl] = h_im
            return h_re, h_im

        lax.fori_loop(0, steps, scan, (x_re, x_im), unroll=2)

    htr_ref[...] = cre[0:1, :]
    hti_ref[...] = cim[0:1, :]

    d = d_ref[...]
    for kc in range(SSM_KC):
        h_re = bur[:, kc * sw:(kc + 1) * sw].astype(BF16)
        h_im = bui[:, kc * sw:(kc + 1) * sw].astype(BF16)
        ys = (jnp.dot(h_re, wcr_ref[kc], preferred_element_type=F32)
              - jnp.dot(h_im, wci_ref[kc], preferred_element_type=F32))
        ys_hi = ys.astype(BF16)
        ys_lo = (ys - ys_hi.astype(F32)).astype(BF16)
        y = (jnp.dot(to_time, ys_hi, preferred_element_type=F32)
             + jnp.dot(to_time, ys_lo, preferred_element_type=F32))
        cols = slice(kc * kw, (kc + 1) * kw)
        y = y + d[:, cols] * u[:, cols]
        z_scr[:, cols] = 0.5 * y * (1.0 + jnp.tanh(math.sqrt(2.0 / math.pi) * (y + 0.044715 * (y * y * y))))
    z = z_scr[...]
    gate = _sigmoid(jnp.dot(z.astype(BF16), wglu_ref[...], preferred_element_type=F32) + bglu_ref[...])
    o_ref[...] = (z * gate).astype(o_ref.dtype)


def _ssm_call(proj, row0, n_seq, t_len, tt, h0_re, h0_im, l, wb_re, wb_im, pw_re, pw_im, wc_re, wc_im, d_skip,
              w_glu, b_glu, *, name):
    steps = tt // S5_SEG
    power = lambda n: S5_POWERS.index(n)
    pwj = jnp.stack([pw[:, (power(1), power(steps)), :] for pw in (pw_re, pw_im)], axis=1)
    seg_rows = []
    for k in (1, 2, 4):
        keep = (jnp.arange(SUBLANES) >= k)[None, :, None]
        seg_rows += [jnp.where(keep, pw[:, power(k * steps)][:, None, :], 0.0) for pw in (pw_re, pw_im)]
    seg = jnp.stack(seg_rows, axis=1)
    nt = t_len // tt
    rb0 = row0 // tt
    state_spec = pl.BlockSpec((None, 1, STATE_W), lambda b, t: (b, 0, 0))
    return pl.pallas_call(
        functools.partial(_ssm_kernel, tt=tt),
        out_shape=(jax.ShapeDtypeStruct((n_seq * t_len, D_SSM), BF16),
                   jax.ShapeDtypeStruct((n_seq, 1, STATE_W), F32),
                   jax.ShapeDtypeStruct((n_seq, 1, STATE_W), F32)),
        grid=(n_seq, nt),
        in_specs=[
            pl.BlockSpec((tt, D_SSM), lambda b, t: (rb0 + b * nt + t, 0)),
            state_spec, state_spec,
            _layer_spec(l, wb_re.shape[1:]), _layer_spec(l, wb_im.shape[1:]),
            _layer_spec(l, pwj.shape[1:]), _layer_spec(l, seg.shape[1:]),
            _layer_spec(l, wc_re.shape[1:]), _layer_spec(l, wc_im.shape[1:]),
            _layer_spec(l, (1, D_SSM)),
            _layer_spec(l, (D_SSM, D_SSM)),
            _layer_spec(l, (1, D_SSM)),
        ],
        out_specs=(pl.BlockSpec((tt, D_SSM), lambda b, t: (b * nt + t, 0)), state_spec, state_spec),
        scratch_shapes=[pltpu.VMEM((tt, STATE_W), F32), pltpu.VMEM((tt, STATE_W), F32),
                        pltpu.VMEM((SUBLANES, STATE_W), F32), pltpu.VMEM((SUBLANES, STATE_W), F32),
                        pltpu.VMEM((tt, D_SSM), F32)],
        compiler_params=_cparams(("arbitrary", "arbitrary")),
        name=name,
    )(proj, h0_re, h0_im, wb_re, wb_im, pwj, seg, wc_re, wc_im, d_skip.reshape(DEPTH, 1, D_SSM), w_glu,
      b_glu.reshape(DEPTH, 1, D_SSM))


RING = 3
N_KEYS = RING * CHUNK
ATT_CPS = 4


def _stack_blocks(x):
    return jnp.concatenate([x[:, j * LANES:(j + 1) * LANES] for j in range(x.shape[1] // LANES)], axis=0)


def _pair_norm_rope(xs, gain, tab, ones_bd):
    reps = xs.shape[0] // CHUNK
    cos, s_lo, s_hi = (jnp.concatenate([tab[i]] * reps, axis=0) for i in range(3))
    half = ROPE_DIM // 2
    sq = xs * xs
    sq_hi = sq.astype(BF16)
    sq_lo = (sq - sq_hi.astype(F32)).astype(BF16)
    ss = (jnp.dot(sq_hi, ones_bd, preferred_element_type=F32)
          + jnp.dot(sq_lo, ones_bd, preferred_element_type=F32))
    xg = xs * gain
    xr = xg * cos + pltpu.roll(xg, LANES - half, 1) * s_lo + pltpu.roll(xg, half, 1) * s_hi
    return xr * lax.rsqrt(ss * (1.0 / HEAD_DIM) + EPS)


def _attn_kernel(sink_ref, q_ref, k_ref, v_ref, ck_ref, cv_ref, tab_ref, gq_ref, gk_ref, o_ref, kn_ref, kd, vd,
                 *, l, n_prompt_steps, steps_per_seq):
    step = pl.program_id(0)
    is_sample = step >= n_prompt_steps
    c_base = lax.rem(step, steps_per_seq) * ATT_CPS
    first = lax.broadcasted_iota(jnp.int32, (1, LANES), 1) < HEAD_DIM
    ones_bd = jnp.where(lax.broadcasted_iota(jnp.int32, (LANES, LANES), 0) // HEAD_DIM
                        == lax.broadcasted_iota(jnp.int32, (LANES, LANES), 1) // HEAD_DIM, 1.0, 0.0).astype(BF16)
    key_slot = lax.broadcasted_iota(jnp.int32, (1, N_KEYS), 1) // CHUNK
    gq = gq_ref[...]
    gk = gk_ref[...]

    def store_dup(dst, x, rows):
        for pair in range(D_KV // LANES):
            blk = x[:, pair * LANES:(pair + 1) * LANES]
            swapped = pltpu.roll(blk, HEAD_DIM, 1)
            dst[2 * pair, rows, 0:LANES] = jnp.where(first, blk, swapped).astype(BF16)
            dst[2 * pair + 1, rows, 0:LANES] = jnp.where(first, swapped, blk).astype(BF16)

    @pl.when(step == 0)
    def _():
        vd[:, :, LANES:] = jnp.ones((N_KV_HEADS, N_KEYS, LANES), BF16)

    def chunk(ci, carry):
        c = jnp.where(is_sample, 0, c_base + ci)
        rows = pl.ds(pl.multiple_of(ci * CHUNK, CHUNK), CHUNK)
        prev = slice(CHUNK, N_KEYS)

        @pl.when(jnp.logical_and(c == 0, jnp.logical_not(is_sample)))
        def _():
            kd[:, prev, :] = jnp.zeros((N_KV_HEADS, N_KEYS - CHUNK, LANES), BF16)
            vd[:, prev, 0:LANES] = jnp.zeros((N_KV_HEADS, N_KEYS - CHUNK, LANES), BF16)

        @pl.when(is_sample)
        def _():
            cached = pl.ds(pl.multiple_of(ci * WINDOW, WINDOW), WINDOW)
            store_dup(kd, ck_ref[cached, :], prev)
            store_dup(vd, cv_ref[cached, :], prev)

        tab = tab_ref[:, rows, :]
        own = pl.ds(pl.multiple_of(lax.rem(c, RING) * CHUNK, CHUNK), CHUNK)
        kn = _pair_norm_rope(_stack_blocks(k_ref[rows, :]), gk, tab, ones_bd)
        kn = jnp.concatenate([kn[0:CHUNK], kn[CHUNK:2 * CHUNK]], axis=1)
        kn_ref[rows, :] = kn
        store_dup(kd, kn, own)
        store_dup(vd, v_ref[rows, :], own)

        qn = _pair_norm_rope(_stack_blocks(q_ref[rows, :]), gq, tab, ones_bd) * (HEAD_DIM ** -0.5)
        q_lo = jnp.where(first, qn, 0.0).astype(BF16)
        q_hi = jnp.where(first, 0.0, qn).astype(BF16)

        valid = key_slot <= jnp.where(is_sample, RING, c)
        scores = []
        for kh in range(N_KV_HEADS):
            r0 = kh * 2 * CHUNK
            lhs = jnp.concatenate([q_lo[r0:r0 + CHUNK], q_hi[r0:r0 + CHUNK],
                                   q_lo[r0 + CHUNK:r0 + 2 * CHUNK], q_hi[r0 + CHUNK:r0 + 2 * CHUNK]], axis=0)
            scores.append(lax.dot_general(lhs, kd[kh], (((1,), (1,)), ((), ())),
                                          preferred_element_type=F32))
        weighted, sink_terms = [], []
        for kh in range(N_KV_HEADS):
            s = jnp.where(valid, scores[kh], -1e30)
            sink = jnp.concatenate([jnp.full((CHUNK, 1), sink_ref[l, kh * KV_REP + r], F32)
                                    for r in range(KV_REP)], axis=0)
            m = jnp.maximum(jnp.max(s, axis=-1, keepdims=True), sink)
            p = jnp.exp(s - m).astype(BF16)
            sink_terms.append(jnp.exp(sink - m))
            weighted.append(jnp.dot(p, vd[kh], preferred_element_type=F32))
        for kh in range(N_KV_HEADS):
            od = weighted[kh]
            o = od[:, 0:LANES] / (od[:, LANES:] + sink_terms[kh])
            for pair in range(2):
                half = pair * 2 * CHUNK
                blk = jnp.where(first, o[half:half + CHUNK], o[half + CHUNK:half + 2 * CHUNK])
                col = (2 * kh + pair) * LANES
                o_ref[rows, col:col + LANES] = blk.astype(o_ref.dtype)
        return carry

    lax.fori_loop(0, ATT_CPS, chunk, 0)


def _attn_call(proj, l, n_prompt, t_prompt, sink, rope_tab, gq, gk, cache_k, cache_v):
    n_tok = proj.shape[0]
    tr = ATT_CPS * CHUNK
    qcol = D_SSM // D_ATTN
    kcol = (D_SSM + D_ATTN) // D_KV
    n_prompt_steps = n_prompt // tr
    cache_spec = pl.BlockSpec((None, ATT_CPS * WINDOW, D_KV),
                              lambda s: (l, jnp.maximum(s - n_prompt_steps, 0), 0))
    return pl.pallas_call(
        functools.partial(_attn_kernel, l=l, n_prompt_steps=n_prompt_steps, steps_per_seq=t_prompt // tr),
        out_shape=(jax.ShapeDtypeStruct((n_tok, D_ATTN), BF16),
                   jax.ShapeDtypeStruct((n_tok, D_KV), F32)),
        grid=(n_tok // tr,),
        in_specs=[
            pl.BlockSpec(memory_space=pltpu.SMEM),
            pl.BlockSpec((tr, D_ATTN), lambda s: (s, qcol)),
            pl.BlockSpec((tr, D_KV), lambda s: (s, kcol)),
            pl.BlockSpec((tr, D_KV), lambda s: (s, kcol + 1)),
            cache_spec, cache_spec,
            pl.BlockSpec((3, tr, LANES), lambda s: (0, s, 0)),
            _layer_spec(l, (1, LANES)), _layer_spec(l, (1, LANES)),
        ],
        out_specs=(pl.BlockSpec((tr, D_ATTN), lambda s: (s, 0)),
                   pl.BlockSpec((tr, D_KV), lambda s: (s, 0))),
        scratch_shapes=[pltpu.VMEM((N_KV_HEADS, N_KEYS, LANES), BF16),
                        pltpu.VMEM((N_KV_HEADS, N_KEYS, 2 * LANES), BF16)],
        compiler_params=_cparams(("arbitrary",)),
        name="banded_attn",
    )(sink, proj, proj, proj, cache_k, cache_v, rope_tab, gq, gk)


def _block_diag_rows(x, n):
    *lead, r, width = x.shape
    keep = jnp.arange(n)[:, None, None] == (jnp.arange(width) // (width // n))[None, None, :]
    return jnp.where(keep, x[..., None, :, :], 0.0).reshape(*lead, n * r, width)


def _rope_table(pos):
    half = ROPE_DIM // 2
    n_pos = pos.shape[0]
    inv_freq = ROPE_THETA ** (-jnp.arange(half, dtype=F32) / half)
    ang = pos.astype(F32)[:, None] * inv_freq[None, :]
    cos, sin = jnp.cos(ang), jnp.sin(ang)
    ones = jnp.ones((n_pos, HEAD_DIM - ROPE_DIM), F32)
    zeros = jnp.zeros((n_pos, HEAD_DIM - half), F32)
    c_tab = jnp.concatenate([cos, cos, ones], axis=1)
    lo_tab = jnp.concatenate([-sin, zeros], axis=1)
    hi_tab = jnp.concatenate([jnp.zeros((n_pos, half), F32), sin, jnp.zeros((n_pos, HEAD_DIM - ROPE_DIM), F32)],
                             axis=1)
    tab = jnp.stack([c_tab, lo_tab, hi_tab])
    return jnp.concatenate([tab, tab], axis=2)


def kernel(x_prompt, x_sample, cache_k, cache_v, state_ssm_re, state_ssm_im, c_prompt, c_sample, w_mod, b_mod, norm1_g, norm2_g, w_in, ssm_a_re, ssm_a_im, ssm_log_dt, ssm_b_re, ssm_b_im, ssm_c_re, ssm_c_im, ssm_d, w_glu, b_glu, q_norm_g, k_norm_g, attn_sink, w_gate, b_gate, w_proj_ssm, w_proj_attn, w_out, w_ffn_gate, w_ffn_up, w_ffn_down):
    bp, tp, _ = x_prompt.shape
    bs, ts, _ = x_sample.shape
    assert ts == CHUNK and tp % (ATT_CPS * CHUNK) == 0 and bs % ATT_CPS == 0
    n_p, n_s = bp * tp, bs * ts
    tm = 1024
    x = jnp.concatenate([x_prompt.reshape(n_p, D_MODEL), x_sample.reshape(n_s, D_MODEL)], axis=0)

    n_cond = bp + bs
    pad = (-n_cond) % SUBLANES
    c_all = jnp.concatenate([c_prompt, c_sample, jnp.zeros((pad, D_MODEL), F32)], axis=0)
    mod = _mod_call(c_all, w_mod, b_mod)
    n_mod = mod.shape[2]

    def per_block(m, n_seq, t_len):
        reps = t_len // MOD_BLOCK
        return jnp.broadcast_to(m[:, :, None, :], (DEPTH, n_seq, reps, n_mod)).reshape(DEPTH, n_seq * reps, n_mod)

    modx = jnp.concatenate([per_block(mod[:, :bp], bp, tp), per_block(mod[:, bp:n_cond], bs, ts)],
                           axis=1)

    g, p = N_SSM_GROUPS, SSM_STATE
    gpc = g // SSM_KC
    pw_re, pw_im, bb_re, bb_im = _s5_prep_call(ssm_a_re, ssm_a_im, ssm_log_dt,
                                               ssm_b_re.transpose(0, 3, 1, 2), ssm_b_im.transpose(0, 3, 1, 2))
    pw_re = pw_re.reshape(DEPTH, len(S5_POWERS), STATE_W)
    pw_im = pw_im.reshape(DEPTH, len(S5_POWERS), STATE_W)

    def wb_blocks(bb):
        per_chunk = bb.reshape(DEPTH, SSM_GROUP, SSM_KC, gpc * p).transpose(0, 2, 1, 3)
        return _block_diag_rows(per_chunk, gpc).astype(BF16)

    def wc_blocks(cc):
        per_chunk = cc.reshape(DEPTH, SSM_KC, gpc, SSM_GROUP, p).transpose(0, 1, 3, 2, 4)
        transposed = _block_diag_rows(per_chunk.reshape(DEPTH, SSM_KC, SSM_GROUP, gpc * p), gpc)
        return jnp.swapaxes(transposed.astype(BF16), -1, -2)

    ssm_w = (wb_blocks(bb_re), wb_blocks(bb_im), pw_re, pw_im, wc_blocks(ssm_c_re), wc_blocks(ssm_c_im), ssm_d,
             w_glu.astype(BF16), b_glu)
    zeros_state = jnp.zeros((bp, 1, STATE_W), F32)
    h0_re = state_ssm_re.reshape(DEPTH, bs, 1, STATE_W)
    h0_im = state_ssm_im.reshape(DEPTH, bs, 1, STATE_W)

    pos = jnp.concatenate([jnp.tile(jnp.arange(tp), bp), jnp.tile(PAST_LEN + jnp.arange(ts), bs)])
    rope_tab = _rope_table(pos)
    gq = jnp.tile(q_norm_g, (1, LANES // HEAD_DIM)).reshape(DEPTH, 1, LANES)
    gk = jnp.tile(k_norm_g, (1, LANES // HEAD_DIM)).reshape(DEPTH, 1, LANES)
    cache_k2 = cache_k.reshape(DEPTH, bs * WINDOW, D_KV)
    cache_v2 = cache_v.reshape(DEPTH, bs * WINDOW, D_KV)

    w_in_b, w_gate_b = w_in.astype(BF16), w_gate.astype(BF16)
    w_ps_b, w_pa_b, w_out_b = w_proj_ssm.astype(BF16), w_proj_attn.astype(BF16), w_out.astype(BF16)
    w_fg_b, w_fu_b, w_fd_b = w_ffn_gate.astype(BF16), w_ffn_up.astype(BF16), w_ffn_down.astype(BF16)
    v0 = D_SSM + D_ATTN + D_KV

    def heads(t):
        return t.reshape(*t.shape[:-1], N_KV_HEADS, HEAD_DIM)

    def last_window(t, col0, col1):
        return jnp.stack([t[(b + 1) * tp - WINDOW:(b + 1) * tp, col0:col1] for b in range(bp)])

    outs = {k: [] for k in ("pk", "pv", "pre", "pim", "sk", "sv", "sre", "sim")}
    for l in range(DEPTH):
        proj, gates = _in_gate_call(x, l, norm1_g, modx, w_in_b, w_gate_b, b_gate, tm=tm)

        ssm_p, pre, pim = _ssm_call(proj, 0, bp, tp, 256, zeros_state, zeros_state, l, *ssm_w, name="s5_prompt")
        ssm_s, sre, sim = _ssm_call(proj, n_p, bs, ts, ts, h0_re[l], h0_im[l], l, *ssm_w, name="s5_sample")
        attn_out, kn = _attn_call(proj, l, n_p, tp, attn_sink, rope_tab, gq, gk, cache_k2, cache_v2)

        mixed = _mix_call(ssm_p, ssm_s, attn_out, gates, l, w_ps_b, w_pa_b, tm=tm, tn=1024)
        x = _resid_mm_call(mixed, l, w_out_b, x, modx, 2, tm=tm, tn=1024, name="out_proj")

        act = _ffn_up_call(x, l, norm2_g, modx, w_fg_b, w_fu_b, tm=tm)
        last = l == DEPTH - 1
        x = _resid_mm_call(act, l, w_fd_b, x, modx, 5, n_split=n_p // tm if last else None, tm=tm,
                           name="ffn_down_split" if last else "ffn_down")

        outs["pk"].append(heads(last_window(kn, 0, D_KV)))
        outs["pv"].append(heads(last_window(proj, v0, IN_WIDTH)))
        outs["pre"].append(pre.reshape(bp, g, p))
        outs["pim"].append(pim.reshape(bp, g, p))
        outs["sk"].append(jnp.concatenate([cache_k[l][:, ts:], heads(kn[n_p:].reshape(bs, ts, D_KV))], axis=1))
        outs["sv"].append(jnp.concatenate([cache_v[l][:, ts:], heads(proj[n_p:, v0:].reshape(bs, ts, D_KV))],
                                          axis=1))
        outs["sre"].append(sre.reshape(bs, g, p))
        outs["sim"].append(sim.reshape(bs, g, p))

    y_p, y_s = x
    return (y_p.reshape(bp, tp, D_MODEL), y_s.reshape(bs, ts, D_MODEL),
            jnp.stack(outs["pk"]), jnp.stack(outs["pv"]), jnp.stack(outs["pre"]), jnp.stack(outs["pim"]),
            jnp.stack(outs["sk"]), jnp.stack(outs["sv"]), jnp.stack(outs["sre"]), jnp.stack(outs["sim"]))
```

```python
import functools
import math

import jax
import jax.numpy as jnp
from jax import lax
from jax.experimental import pallas as pl
from jax.experimental.pallas import tpu as pltpu

D_MODEL = 2048
DEPTH = 4
CHUNK = 64
D_SSM = 1024
SSM_GROUP = 16
N_SSM_GROUPS = 64
SSM_STATE = 64
HEAD_DIM = 64
N_HEADS = 16
N_KV_HEADS = 4
KV_REP = N_HEADS // N_KV_HEADS
D_ATTN = N_HEADS * HEAD_DIM
D_KV = N_KV_HEADS * HEAD_DIM
IN_WIDTH = D_SSM + D_ATTN + 2 * D_KV
WINDOW = 128
ROPE_DIM = 16
ROPE_THETA = 500000.0
D_FF = 5632
EPS = 1e-6
PAST_LEN = 2048

LANES = 128
SUBLANES = 8
MOD_BLOCK = CHUNK
STATE_W = N_SSM_GROUPS * SSM_STATE
SSM_KC = 4
VMEM_LIMIT = 56 * 1024 * 1024

F32 = jnp.float32
BF16 = jnp.bfloat16


def _cparams(sem):
    return pltpu.CompilerParams(dimension_semantics=sem, vmem_limit_bytes=VMEM_LIMIT)


def _sigmoid(x):
    return 0.5 + 0.5 * jnp.tanh(0.5 * x)


def _layer_spec(l, shape):
    zeros = (0,) * len(shape)
    return pl.BlockSpec((None, *shape), lambda *_: (l, *zeros))


def _mod_kernel(c_ref, w_ref, b_ref, o_ref):
    c = c_ref[...].astype(BF16)
    w = w_ref[...].astype(BF16)
    o_ref[...] = jnp.dot(c, w, preferred_element_type=F32) + b_ref[...]


def _mod_call(c_all, w_mod, b_mod):
    nb = c_all.shape[0]
    tn = 1024
    n_out = w_mod.shape[2]
    return pl.pallas_call(
        _mod_kernel,
        out_shape=jax.ShapeDtypeStruct((DEPTH, nb, n_out), F32),
        grid=(DEPTH, n_out // tn),
        in_specs=[
            pl.BlockSpec((nb, D_MODEL), lambda l, j: (0, 0)),
            pl.BlockSpec((None, D_MODEL, tn), lambda l, j: (l, 0, j)),
            pl.BlockSpec((None, 1, tn), lambda l, j: (l, 0, j)),
        ],
        out_specs=pl.BlockSpec((None, nb, tn), lambda l, j: (l, 0, j)),
        compiler_params=_cparams(("arbitrary", "arbitrary")),
        name="adaln_mod",
    )(c_all, w_mod, b_mod.reshape(DEPTH, 1, n_out))


S5_TILES = (256, 64)
S5_SEG = SUBLANES
S5_POWERS = tuple(sorted({n for t in S5_TILES for n in (1, *(k * t // S5_SEG for k in (1, 2, 4)))}))


def _s5_prep_kernel(are_ref, aim_ref, ldt_ref, bre_ref, bim_ref, pwr_ref, pwi_ref, bbr_ref, bbi_ref):
    a_re = are_ref[...]
    a_im = aim_ref[...]
    dt = jnp.exp(ldt_ref[...])
    z_re = a_re * dt
    z_im = a_im * dt
    for i, n in enumerate(S5_POWERS):
        mag = jnp.exp(z_re * float(n))
        pwr_ref[i] = mag * jnp.cos(z_im * float(n))
        pwi_ref[i] = mag * jnp.sin(z_im * float(n))
    l_re = pwr_ref[S5_POWERS.index(1)]
    l_im = pwi_ref[S5_POWERS.index(1)]
    den = a_re * a_re + a_im * a_im
    n_re = l_re - 1.0
    f_re = (n_re * a_re + l_im * a_im) / den
    f_im = (l_im * a_re - n_re * a_im) / den
    for c in range(SSM_GROUP):
        b_re = bre_ref[c]
        b_im = bim_ref[c]
        bbr_ref[c] = f_re * b_re - f_im * b_im
        bbi_ref[c] = f_re * b_im + f_im * b_re


def _s5_prep_call(a_re, a_im, log_dt, bt_re, bt_im):
    g, p = N_SSM_GROUPS, SSM_STATE
    mat = pl.BlockSpec((None, g, p), lambda l: (l, 0, 0))
    stack_c = pl.BlockSpec((None, SSM_GROUP, g, p), lambda l: (l, 0, 0, 0))
    n_pow = len(S5_POWERS)
    stack_n = pl.BlockSpec((None, n_pow, g, p), lambda l: (l, 0, 0, 0))
    return pl.pallas_call(
        _s5_prep_kernel,
        out_shape=(jax.ShapeDtypeStruct((DEPTH, n_pow, g, p), F32),
                   jax.ShapeDtypeStruct((DEPTH, n_pow, g, p), F32),
                   jax.ShapeDtypeStruct((DEPTH, SSM_GROUP, g, p), F32),
                   jax.ShapeDtypeStruct((DEPTH, SSM_GROUP, g, p), F32)),
        grid=(DEPTH,),
        in_specs=[mat, mat, pl.BlockSpec((None, g, 1), lambda l: (l, 0, 0)), stack_c, stack_c],
        out_specs=(stack_n, stack_n, stack_c, stack_c),
        compiler_params=_cparams(("arbitrary",)),
        name="s5_discretize",
    )(a_re, a_im, log_dt.reshape(DEPTH, g, 1), bt_re, bt_im)


NORM_CHUNK = 2 * MOD_BLOCK


def _norm_mod_chunk(x_ref, g_ref, sc_ref, sh_ref, h_scr, slot, chunk):
    gain = g_ref[...]
    per_chunk = NORM_CHUNK // MOD_BLOCK
    for s in range(per_chunk):
        x = x_ref[s * MOD_BLOCK:(s + 1) * MOD_BLOCK, :]
        ms = jnp.mean(x * x, axis=-1, keepdims=True)
        y = x * lax.rsqrt(ms + EPS) * gain
        mod_row = pl.ds(chunk * per_chunk + s, 1)
        h = y * (1.0 + sc_ref[mod_row, :]) + sh_ref[mod_row, :]
        rows = pl.ds(pl.multiple_of((chunk * per_chunk + s) * MOD_BLOCK, MOD_BLOCK), MOD_BLOCK)
        h_scr[slot, rows, :] = h.astype(BF16)


def _norm_pipeline(l, n_tiles, tm, sc_idx, sh_idx):
    n_chunks = tm // NORM_CHUNK
    nsub = tm // MOD_BLOCK
    tile = lambda i: jnp.minimum(i, n_tiles - 1)
    specs = [
        pl.BlockSpec((NORM_CHUNK, D_MODEL), lambda i, j: (tile(i) * n_chunks + jnp.minimum(j, n_chunks - 1), 0)),
        _layer_spec(l, (1, D_MODEL)),
        pl.BlockSpec((None, nsub, D_MODEL), lambda i, j: (l, tile(i), sc_idx)),
        pl.BlockSpec((None, nsub, D_MODEL), lambda i, j: (l, tile(i), sh_idx)),
    ]
    scratch = pltpu.VMEM((2, tm, D_MODEL), BF16)
    out_row = lambda i: jnp.maximum(i - 1, 0)
    out_col = lambda i, col: jnp.where(i == 0, 0, col)
    return specs, scratch, n_chunks, out_row, out_col


def _in_gate_kernel(x_ref, g_ref, sc_ref, sh_ref, wi_ref, wg_ref, b_ref, proj_ref, gate_ref, h_scr, *, n_in,
                    n_chunks):
    i, j = pl.program_id(0), pl.program_id(1)
    ready = lax.rem(i + 1, 2)

    filling = j < n_chunks

    def fill():
        _norm_mod_chunk(x_ref, g_ref, sc_ref, sh_ref, h_scr, lax.rem(i, 2), j)

    def gates():
        acc = jnp.dot(h_scr[ready], wg_ref[...], preferred_element_type=F32)
        gate_ref[...] = _sigmoid(acc + b_ref[...]).astype(gate_ref.dtype)

    pl.when(jnp.logical_and(i == 0, filling))(fill)

    @pl.when(jnp.logical_and(i > 0, j < n_in))
    def _():
        proj_ref[...] = jnp.dot(h_scr[ready], wi_ref[...], preferred_element_type=F32)
        fill()

    @pl.when(jnp.logical_and(i > 0, jnp.logical_and(j >= n_in, filling)))
    def _():
        gates()
        fill()

    pl.when(jnp.logical_and(i > 0, jnp.logical_not(filling)))(gates)


def _in_gate_call(x, l, gain, modx, w_in, w_gate, b_gate, *, tm=1024, tn=512):
    n_tiles = x.shape[0] // tm
    n_in, n_gate = w_in.shape[2] // tn, w_gate.shape[2] // tn
    norm_specs, h_scratch, n_chunks, out_row, out_col = _norm_pipeline(l, n_tiles, tm, 1, 0)
    assert n_in <= n_chunks <= n_in + n_gate
    in_col = lambda j: jnp.minimum(j, n_in - 1)
    gate_col = lambda j: jnp.maximum(j - n_in, 0)
    return pl.pallas_call(
        functools.partial(_in_gate_kernel, n_in=n_in, n_chunks=n_chunks),
        out_shape=(jax.ShapeDtypeStruct((x.shape[0], w_in.shape[2]), F32),
                   jax.ShapeDtypeStruct((x.shape[0], w_gate.shape[2]), BF16)),
        grid=(n_tiles + 1, n_in + n_gate),
        in_specs=norm_specs + [
            pl.BlockSpec((None, D_MODEL, tn), lambda i, j: (0, 0, in_col(j))),
            pl.BlockSpec((None, D_MODEL, tn), lambda i, j: (0, 0, gate_col(j))),
            pl.BlockSpec((None, 1, tn), lambda i, j: (l, 0, gate_col(j))),
        ],
        out_specs=(pl.BlockSpec((tm, tn), lambda i, j: (out_row(i), out_col(i, in_col(j)))),
                   pl.BlockSpec((tm, tn), lambda i, j: (out_row(i), out_col(i, gate_col(j))))),
        scratch_shapes=[h_scratch],
        compiler_params=_cparams(("arbitrary", "arbitrary")),
        name="in_proj_gates",
    )(x, gain.reshape(DEPTH, 1, D_MODEL), modx, modx, w_in, w_gate, b_gate.reshape(DEPTH, 1, w_gate.shape[2]))


def _ffn_up_kernel(x_ref, g_ref, sc_ref, sh_ref, wg_ref, wu_ref, o_ref, h_scr, *, n_chunks):
    i, j = pl.program_id(0), pl.program_id(1)

    filling = j < n_chunks

    def fill():
        _norm_mod_chunk(x_ref, g_ref, sc_ref, sh_ref, h_scr, lax.rem(i, 2), j)

    def swiglu():
        h = h_scr[lax.rem(i + 1, 2)]
        gate = jnp.dot(h, wg_ref[...], preferred_element_type=F32)
        up = jnp.dot(h, wu_ref[...], preferred_element_type=F32)
        o_ref[...] = (gate * _sigmoid(gate) * up).astype(o_ref.dtype)

    pl.when(jnp.logical_and(i == 0, filling))(fill)

    @pl.when(jnp.logical_and(i > 0, filling))
    def _():
        swiglu()
        fill()

    pl.when(jnp.logical_and(i > 0, jnp.logical_not(filling)))(swiglu)


def _ffn_up_call(x, l, gain, modx, w_gate, w_up, *, tm=1024, tn=512):
    n_tiles = x.shape[0] // tm
    n_out = w_gate.shape[2]
    norm_specs, h_scratch, n_chunks, out_row, out_col = _norm_pipeline(l, n_tiles, tm, 4, 3)
    assert n_out // tn >= n_chunks
    w_spec = pl.BlockSpec((None, D_MODEL, tn), lambda i, j: (0, 0, j))
    return pl.pallas_call(
        functools.partial(_ffn_up_kernel, n_chunks=n_chunks),
        out_shape=jax.ShapeDtypeStruct((x.shape[0], n_out), BF16),
        grid=(n_tiles + 1, n_out // tn),
        in_specs=norm_specs + [w_spec, w_spec],
        out_specs=pl.BlockSpec((tm, tn), lambda i, j: (out_row(i), out_col(i, j))),
        scratch_shapes=[h_scratch],
        compiler_params=_cparams(("arbitrary", "arbitrary")),
        name="ffn_up",
    )(x, gain.reshape(DEPTH, 1, D_MODEL), modx, modx, w_gate, w_up)


def _mix_kernel(sp_ref, ss_ref, a_ref, ga_ref, gb_ref, ws_ref, wa_ref, o_ref, *, n_split):
    def body(s_ref):
        ps = jnp.dot(s_ref[...], ws_ref[...], preferred_element_type=F32)
        pa = jnp.dot(a_ref[...], wa_ref[...], preferred_element_type=F32)
        o_ref[...] = (ga_ref[...] * ps + gb_ref[...] * pa).astype(o_ref.dtype)

    i = pl.program_id(0)
    pl.when(i < n_split)(lambda: body(sp_ref))
    pl.when(i >= n_split)(lambda: body(ss_ref))


def _mix_call(ssm_p, ssm_s, attn_out, gates, w_ps, w_pa, *, tm=1024, tn=512):
    n_tok = attn_out.shape[0]
    nj = D_MODEL // tn
    n_split = ssm_p.shape[0] // tm
    return pl.pallas_call(
        functools.partial(_mix_kernel, n_split=n_split),
        out_shape=jax.ShapeDtypeStruct((n_tok, D_MODEL), BF16),
        grid=(n_tok // tm, nj),
        in_specs=[
            pl.BlockSpec((tm, D_SSM), lambda i, j: (jnp.minimum(i, n_split - 1), 0)),
            pl.BlockSpec((tm, D_SSM), lambda i, j: (jnp.maximum(i - n_split, 0), 0)),
            pl.BlockSpec((tm, D_ATTN), lambda i, j: (i, 0)),
            pl.BlockSpec((tm, tn), lambda i, j: (i, j)),
            pl.BlockSpec((tm, tn), lambda i, j: (i, j + nj)),
            pl.BlockSpec((None, D_SSM, tn), lambda i, j: (0, 0, j)),
            pl.BlockSpec((None, D_ATTN, tn), lambda i, j: (0, 0, j)),
        ],
        out_specs=pl.BlockSpec((tm, tn), lambda i, j: (i, j)),
        compiler_params=_cparams(("arbitrary", "arbitrary")),
        name="branch_merge",
    )(ssm_p, ssm_s, attn_out, gates, gates, w_ps, w_pa)


def _resid_mm_kernel(a_ref, w_ref, x_ref, g_ref, *o_refs, nsub, n_split):
    acc = jnp.dot(a_ref[...], w_ref[...], preferred_element_type=F32)

    def write(o_ref):
        for s in range(nsub):
            rows = slice(s * MOD_BLOCK, (s + 1) * MOD_BLOCK)
            o_ref[rows, :] = x_ref[rows, :] + g_ref[s:s + 1, :] * acc[rows, :]

    if n_split is None:
        write(o_refs[0])
    else:
        i = pl.program_id(0)
        pl.when(i < n_split)(lambda: write(o_refs[0]))
        pl.when(i >= n_split)(lambda: write(o_refs[1]))


def _resid_mm_call(a, l, w, x, modx, g_idx, *, n_split=None, tm=1024, tn=512, name):
    n_tok, k = a.shape
    nsub = tm // MOD_BLOCK
    nj = D_MODEL // tn
    n_tiles = n_tok // tm
    if n_split is None:
        out_shape = jax.ShapeDtypeStruct((n_tok, D_MODEL), F32)
        out_specs = pl.BlockSpec((tm, tn), lambda i, j: (i, j))
    else:
        out_shape = (jax.ShapeDtypeStruct((n_split * tm, D_MODEL), F32),
                     jax.ShapeDtypeStruct(((n_tiles - n_split) * tm, D_MODEL), F32))
        out_specs = (
            pl.BlockSpec((tm, tn), lambda i, j: (jnp.minimum(i, n_split - 1), jnp.where(i < n_split, j, nj - 1))),
            pl.BlockSpec((tm, tn), lambda i, j: (jnp.maximum(i - n_split, 0), jnp.where(i < n_split, 0, j))),
        )
    return pl.pallas_call(
        functools.partial(_resid_mm_kernel, nsub=nsub, n_split=n_split),
        out_shape=out_shape,
        grid=(n_tiles, nj),
        in_specs=[
            pl.BlockSpec((tm, k), lambda i, j: (i, 0)),
            pl.BlockSpec((None, k, tn), lambda i, j: (0, 0, j)),
            pl.BlockSpec((tm, tn), lambda i, j: (i, j)),
            pl.BlockSpec((None, nsub, tn), lambda i, j: (l, i, g_idx * nj + j)),
        ],
        out_specs=out_specs,
        compiler_params=_cparams(("arbitrary", "arbitrary")),
        name=name,
    )(a, w, x, modx)


SCAN_LW = 512


def _ssm_kernel(*refs, tt, n_cast):
    (u_ref, h0r_ref, h0i_ref, wbr_ref, wbi_ref, pwj_ref, seg_ref, wcr_ref, wci_ref, d_ref, wglu_ref,
     bglu_ref) = refs[:12]
    cast_in = refs[12:12 + n_cast]
    o_ref, htr_ref, hti_ref = refs[12 + n_cast:15 + n_cast]
    cast_out = refs[15 + n_cast:15 + 2 * n_cast]
    bur, bui, cre, cim, z_scr = refs[15 + 2 * n_cast:]
    steps = tt // S5_SEG
    for w_ref, wb_ref in zip(cast_in, cast_out):
        wb_ref[...] = w_ref[...].astype(BF16)

    @pl.when(pl.program_id(1) == 0)
    def _():
        cre[...] = jnp.broadcast_to(h0r_ref[...], (SUBLANES, STATE_W))
        cim[...] = jnp.broadcast_to(h0i_ref[...], (SUBLANES, STATE_W))

    r_idx = lax.broadcasted_iota(jnp.int32, (tt, tt), 0)
    c_idx = lax.broadcasted_iota(jnp.int32, (tt, tt), 1)
    to_scan = jnp.where(c_idx == (r_idx % S5_SEG) * steps + r_idx // S5_SEG, 1.0, 0.0).astype(BF16)
    to_time = jnp.where(r_idx == (c_idx % S5_SEG) * steps + c_idx // S5_SEG, 1.0, 0.0).astype(BF16)

    u = u_ref[...]
    us = jnp.dot(to_scan, u.astype(BF16), preferred_element_type=F32).astype(BF16)
    kw = D_SSM // SSM_KC
    sw = STATE_W // SSM_KC
    for kc in range(SSM_KC):
        uk = us[:, kc * kw:(kc + 1) * kw]
        bur[:, kc * sw:(kc + 1) * sw] = jnp.dot(uk, wbr_ref[kc], preferred_element_type=F32)
        bui[:, kc * sw:(kc + 1) * sw] = jnp.dot(uk, wbi_ref[kc], preferred_element_type=F32)

    row = lax.broadcasted_iota(jnp.int32, (SUBLANES, SCAN_LW), 0)
    for lc in range(STATE_W // SCAN_LW):
        sl = slice(lc * SCAN_LW, (lc + 1) * SCAN_LW)
        lam_re = jnp.broadcast_to(pwj_ref[0, 0:1, sl], (SUBLANES, SCAN_LW))
        lam_im = jnp.broadcast_to(pwj_ref[1, 0:1, sl], (SUBLANES, SCAN_LW))

        def step(j, carry, sl=sl, lam_re=lam_re, lam_im=lam_im):
            h_re, h_im = carry
            rows = pl.ds(pl.multiple_of(j * SUBLANES, SUBLANES), SUBLANES)
            return (lam_re * h_re - lam_im * h_im + bur[rows, sl],
                    lam_re * h_im + lam_im * h_re + bui[rows, sl])

        zero = jnp.zeros((SUBLANES, SCAN_LW), F32)
        e_re, e_im = lax.fori_loop(0, steps, step, (zero, zero), unroll=4)

        x_re = jnp.where(row == 0, cre[:, sl], pltpu.roll(e_re, 1, 0))
        x_im = jnp.where(row == 0, cim[:, sl], pltpu.roll(e_im, 1, 0))
        for idx, k in enumerate((1, 2, 4)):
            m_re = seg_ref[2 * idx, :, sl]
            m_im = seg_ref[2 * idx + 1, :, sl]
            s_re = pltpu.roll(x_re, k, 0)
            s_im = pltpu.roll(x_im, k, 0)
            x_re, x_im = (x_re + m_re * s_re - m_im * s_im,
                          x_im + m_re * s_im + m_im * s_re)
        pj_re = jnp.broadcast_to(pwj_ref[0, 1:2, sl], (SUBLANES, SCAN_LW))
        pj_im = jnp.broadcast_to(pwj_ref[1, 1:2, sl], (SUBLANES, SCAN_LW))
        n_re = pj_re * x_re - pj_im * x_im + e_re
        n_im = pj_re * x_im + pj_im * x_re + e_im
        cre[:, sl] = jnp.broadcast_to(n_re[SUBLANES - 1:SUBLANES, :], (SUBLANES, SCAN_LW))
        cim[:, sl] = jnp.broadcast_to(n_im[SUBLANES - 1:SUBLANES, :], (SUBLANES, SCAN_LW))

        def scan(j, carry, sl=sl, step=step):
            h_re, h_im = step(j, carry)
            rows = pl.ds(pl.multiple_of(j * SUBLANES, SUBLANES), SUBLANES)
            bur[rows, sl] = h_re
            bui[rows, sl] = h_im
            return h_re, h_im

        lax.fori_loop(0, steps, scan, (x_re, x_im), unroll=2)

    htr_ref[...] = cre[0:1, :]
    hti_ref[...] = cim[0:1, :]

    d = d_ref[...]
    for kc in range(SSM_KC):
        h_re = bur[:, kc * sw:(kc + 1) * sw].astype(BF16)
        h_im = bui[:, kc * sw:(kc + 1) * sw].astype(BF16)
        ys = (jnp.dot(h_re, wcr_ref[kc], preferred_element_type=F32)
              - jnp.dot(h_im, wci_ref[kc], preferred_element_type=F32))
        ys_hi = ys.astype(BF16)
        ys_lo = (ys - ys_hi.astype(F32)).astype(BF16)
        y = (jnp.dot(to_time, ys_hi, preferred_element_type=F32)
             + jnp.dot(to_time, ys_lo, preferred_element_type=F32))
        cols = slice(kc * kw, (kc + 1) * kw)
        y = y + d[:, cols] * u[:, cols]
        z_scr[:, cols] = 0.5 * y * (1.0 + jnp.tanh(math.sqrt(2.0 / math.pi) * (y + 0.044715 * (y * y * y))))
    z = z_scr[...]
    gate = _sigmoid(jnp.dot(z.astype(BF16), wglu_ref[...], preferred_element_type=F32) + bglu_ref[...])
    o_ref[...] = (z * gate).astype(o_ref.dtype)


def _ssm_call(proj, row0, n_seq, t_len, tt, h0_re, h0_im, l, wb_re, wb_im, pw_re, pw_im, wc_re, wc_im, d_skip,
              w_glu, b_glu, *, cast=(), name):
    steps = tt // S5_SEG
    power = lambda n: S5_POWERS.index(n)
    pwj = jnp.stack([pw[:, (power(1), power(steps)), :] for pw in (pw_re, pw_im)], axis=1)
    seg_rows = []
    for k in (1, 2, 4):
        keep = (jnp.arange(SUBLANES) >= k)[None, :, None]
        seg_rows += [jnp.where(keep, pw[:, power(k * steps)][:, None, :], 0.0) for pw in (pw_re, pw_im)]
    seg = jnp.stack(seg_rows, axis=1)
    nt = t_len // tt
    rb0 = row0 // tt
    n_steps = n_seq * nt
    state_spec = pl.BlockSpec((None, 1, STATE_W), lambda b, t: (b, 0, 0))
    cast_in, cast_out, cast_shapes = [], [], []
    for w, wl in cast:
        _, k, n = w.shape
        rows = k // n_steps
        cast_in.append(pl.BlockSpec((None, rows, n), lambda b, t, wl=wl: (wl, b * nt + t, 0)))
        cast_out.append(pl.BlockSpec((None, rows, n), lambda b, t: (0, b * nt + t, 0)))
        cast_shapes.append(jax.ShapeDtypeStruct((1, k, n), BF16))
    resident = lambda shape: pl.BlockSpec((None, *shape), lambda b, t: (l, *(0,) * len(shape)),
                                          pipeline_mode=pl.Buffered(1))
    return pl.pallas_call(
        functools.partial(_ssm_kernel, tt=tt, n_cast=len(cast)),
        out_shape=(jax.ShapeDtypeStruct((n_seq * t_len, D_SSM), BF16),
                   jax.ShapeDtypeStruct((n_seq, 1, STATE_W), F32),
                   jax.ShapeDtypeStruct((n_seq, 1, STATE_W), F32),
                   *cast_shapes),
        grid=(n_seq, nt),
        in_specs=[
            pl.BlockSpec((tt, D_SSM), lambda b, t: (rb0 + b * nt + t, 0)),
            state_spec, state_spec,
            resident(wb_re.shape[1:]), resident(wb_im.shape[1:]),
            _layer_spec(l, pwj.shape[1:]), _layer_spec(l, seg.shape[1:]),
            resident(wc_re.shape[1:]), resident(wc_im.shape[1:]),
            _layer_spec(l, (1, D_SSM)),
            resident((D_SSM, D_SSM)),
            _layer_spec(l, (1, D_SSM)),
            *cast_in,
        ],
        out_specs=(pl.BlockSpec((tt, D_SSM), lambda b, t: (b * nt + t, 0)), state_spec, state_spec, *cast_out),
        scratch_shapes=[pltpu.VMEM((tt, STATE_W), F32), pltpu.VMEM((tt, STATE_W), F32),
                        pltpu.VMEM((SUBLANES, STATE_W), F32), pltpu.VMEM((SUBLANES, STATE_W), F32),
                        pltpu.VMEM((tt, D_SSM), F32)],
        compiler_params=_cparams(("arbitrary", "arbitrary")),
        name=name,
    )(proj, h0_re, h0_im, wb_re, wb_im, pwj, seg, wc_re, wc_im, d_skip.reshape(DEPTH, 1, D_SSM), w_glu,
      b_glu.reshape(DEPTH, 1, D_SSM), *[w for w, _ in cast])


RING = 3
N_KEYS = RING * CHUNK
ATT_CPS = 4


def _stack_blocks(x):
    return jnp.concatenate([x[:, j * LANES:(j + 1) * LANES] for j in range(x.shape[1] // LANES)], axis=0)


def _pair_norm_rope(xs, gain, tab, ones_bd):
    reps = xs.shape[0] // CHUNK
    cos, s_lo, s_hi = (jnp.concatenate([tab[i]] * reps, axis=0) for i in range(3))
    half = ROPE_DIM // 2
    sq = xs * xs
    sq_hi = sq.astype(BF16)
    sq_lo = (sq - sq_hi.astype(F32)).astype(BF16)
    ss = (jnp.dot(sq_hi, ones_bd, preferred_element_type=F32)
          + jnp.dot(sq_lo, ones_bd, preferred_element_type=F32))
    xg = xs * gain
    xr = xg * cos + pltpu.roll(xg, LANES - half, 1) * s_lo + pltpu.roll(xg, half, 1) * s_hi
    return xr * lax.rsqrt(ss * (1.0 / HEAD_DIM) + EPS)


def _attn_kernel(sink_ref, q_ref, k_ref, v_ref, ck_ref, cv_ref, tab_ref, gq_ref, gk_ref, o_ref, kn_ref, kd, vd,
                 *, l, n_prompt_steps, steps_per_seq):
    step = pl.program_id(0)
    is_sample = step >= n_prompt_steps
    c_base = lax.rem(step, steps_per_seq) * ATT_CPS
    first = lax.broadcasted_iota(jnp.int32, (1, LANES), 1) < HEAD_DIM
    ones_bd = jnp.where(lax.broadcasted_iota(jnp.int32, (LANES, LANES), 0) // HEAD_DIM
                        == lax.broadcasted_iota(jnp.int32, (LANES, LANES), 1) // HEAD_DIM, 1.0, 0.0).astype(BF16)
    key_slot = lax.broadcasted_iota(jnp.int32, (1, N_KEYS), 1) // CHUNK
    gq = gq_ref[...]
    gk = gk_ref[...]

    def store_dup(dst, x, rows):
        for pair in range(D_KV // LANES):
            blk = x[:, pair * LANES:(pair + 1) * LANES]
            swapped = pltpu.roll(blk, HEAD_DIM, 1)
            dst[2 * pair, rows, 0:LANES] = jnp.where(first, blk, swapped).astype(BF16)
            dst[2 * pair + 1, rows, 0:LANES] = jnp.where(first, swapped, blk).astype(BF16)

    @pl.when(step == 0)
    def _():
        vd[:, :, LANES:] = jnp.ones((N_KV_HEADS, N_KEYS, LANES), BF16)

    def chunk(ci, carry):
        c = jnp.where(is_sample, 0, c_base + ci)
        rows = pl.ds(pl.multiple_of(ci * CHUNK, CHUNK), CHUNK)
        prev = slice(CHUNK, N_KEYS)

        @pl.when(jnp.logical_and(c == 0, jnp.logical_not(is_sample)))
        def _():
            kd[:, prev, :] = jnp.zeros((N_KV_HEADS, N_KEYS - CHUNK, LANES), BF16)
            vd[:, prev, 0:LANES] = jnp.zeros((N_KV_HEADS, N_KEYS - CHUNK, LANES), BF16)

        @pl.when(is_sample)
        def _():
            cached = pl.ds(pl.multiple_of(ci * WINDOW, WINDOW), WINDOW)
            store_dup(kd, ck_ref[cached, :], prev)
            store_dup(vd, cv_ref[cached, :], prev)

        tab = tab_ref[:, rows, :]
        own = pl.ds(pl.multiple_of(lax.rem(c, RING) * CHUNK, CHUNK), CHUNK)
        kn = _pair_norm_rope(_stack_blocks(k_ref[rows, :]), gk, tab, ones_bd)
        kn = jnp.concatenate([kn[0:CHUNK], kn[CHUNK:2 * CHUNK]], axis=1)
        kn_ref[rows, :] = kn
        store_dup(kd, kn, own)
        store_dup(vd, v_ref[rows, :], own)

        qn = _pair_norm_rope(_stack_blocks(q_ref[rows, :]), gq, tab, ones_bd) * (HEAD_DIM ** -0.5)
        q_lo = jnp.where(first, qn, 0.0).astype(BF16)
        q_hi = jnp.where(first, 0.0, qn).astype(BF16)

        valid = key_slot <= jnp.where(is_sample, RING, c)
        scores = []
        for kh in range(N_KV_HEADS):
            r0 = kh * 2 * CHUNK
            lhs = jnp.concatenate([q_lo[r0:r0 + CHUNK], q_hi[r0:r0 + CHUNK],
                                   q_lo[r0 + CHUNK:r0 + 2 * CHUNK], q_hi[r0 + CHUNK:r0 + 2 * CHUNK]], axis=0)
            scores.append(lax.dot_general(lhs, kd[kh], (((1,), (1,)), ((), ())),
                                          preferred_element_type=F32))
        weighted, sink_terms = [], []
        for kh in range(N_KV_HEADS):
            s = jnp.where(valid, scores[kh], -1e30)
            sink = jnp.concatenate([jnp.full((CHUNK, 1), sink_ref[l, kh * KV_REP + r], F32)
                                    for r in range(KV_REP)], axis=0)
            m = jnp.maximum(jnp.max(s, axis=-1, keepdims=True), sink)
            p = jnp.exp(s - m).astype(BF16)
            sink_terms.append(jnp.exp(sink - m))
            weighted.append(jnp.dot(p, vd[kh], preferred_element_type=F32))
        for kh in range(N_KV_HEADS):
            od = weighted[kh]
            o = od[:, 0:LANES] / (od[:, LANES:] + sink_terms[kh])
            for pair in range(2):
                half = pair * 2 * CHUNK
                blk = jnp.where(first, o[half:half + CHUNK], o[half + CHUNK:half + 2 * CHUNK])
                col = (2 * kh + pair) * LANES
                o_ref[rows, col:col + LANES] = blk.astype(o_ref.dtype)
        return carry

    lax.fori_loop(0, ATT_CPS, chunk, 0)


def _attn_call(proj, l, n_prompt, t_prompt, sink, rope_tab, gq, gk, cache_k, cache_v):
    n_tok = proj.shape[0]
    tr = ATT_CPS * CHUNK
    qcol = D_SSM // D_ATTN
    kcol = (D_SSM + D_ATTN) // D_KV
    n_prompt_steps = n_prompt // tr
    cache_spec = pl.BlockSpec((None, ATT_CPS * WINDOW, D_KV),
                              lambda s: (l, jnp.maximum(s - n_prompt_steps, 0), 0))
    return pl.pallas_call(
        functools.partial(_attn_kernel, l=l, n_prompt_steps=n_prompt_steps, steps_per_seq=t_prompt // tr),
        out_shape=(jax.ShapeDtypeStruct((n_tok, D_ATTN), BF16),
                   jax.ShapeDtypeStruct((n_tok, D_KV), F32)),
        grid=(n_tok // tr,),
        in_specs=[
            pl.BlockSpec(memory_space=pltpu.SMEM),
            pl.BlockSpec((tr, D_ATTN), lambda s: (s, qcol)),
            pl.BlockSpec((tr, D_KV), lambda s: (s, kcol)),
            pl.BlockSpec((tr, D_KV), lambda s: (s, kcol + 1)),
            cache_spec, cache_spec,
            pl.BlockSpec((3, tr, LANES), lambda s: (0, s, 0)),
            _layer_spec(l, (1, LANES)), _layer_spec(l, (1, LANES)),
        ],
        out_specs=(pl.BlockSpec((tr, D_ATTN), lambda s: (s, 0)),
                   pl.BlockSpec((tr, D_KV), lambda s: (s, 0))),
        scratch_shapes=[pltpu.VMEM((N_KV_HEADS, N_KEYS, LANES), BF16),
                        pltpu.VMEM((N_KV_HEADS, N_KEYS, 2 * LANES), BF16)],
        compiler_params=_cparams(("arbitrary",)),
        name="banded_attn",
    )(sink, proj, proj, proj, cache_k, cache_v, rope_tab, gq, gk)


def _block_diag_rows(x, n):
    *lead, r, width = x.shape
    keep = jnp.arange(n)[:, None, None] == (jnp.arange(width) // (width // n))[None, None, :]
    return jnp.where(keep, x[..., None, :, :], 0.0).reshape(*lead, n * r, width)


def _rope_table(pos):
    half = ROPE_DIM // 2
    n_pos = pos.shape[0]
    inv_freq = ROPE_THETA ** (-jnp.arange(half, dtype=F32) / half)
    ang = pos.astype(F32)[:, None] * inv_freq[None, :]
    cos, sin = jnp.cos(ang), jnp.sin(ang)
    ones = jnp.ones((n_pos, HEAD_DIM - ROPE_DIM), F32)
    zeros = jnp.zeros((n_pos, HEAD_DIM - half), F32)
    c_tab = jnp.concatenate([cos, cos, ones], axis=1)
    lo_tab = jnp.concatenate([-sin, zeros], axis=1)
    hi_tab = jnp.concatenate([jnp.zeros((n_pos, half), F32), sin, jnp.zeros((n_pos, HEAD_DIM - ROPE_DIM), F32)],
                             axis=1)
    tab = jnp.stack([c_tab, lo_tab, hi_tab])
    return jnp.concatenate([tab, tab], axis=2)


def kernel(x_prompt, x_sample, cache_k, cache_v, state_ssm_re, state_ssm_im, c_prompt, c_sample, w_mod, b_mod, norm1_g, norm2_g, w_in, ssm_a_re, ssm_a_im, ssm_log_dt, ssm_b_re, ssm_b_im, ssm_c_re, ssm_c_im, ssm_d, w_glu, b_glu, q_norm_g, k_norm_g, attn_sink, w_gate, b_gate, w_proj_ssm, w_proj_attn, w_out, w_ffn_gate, w_ffn_up, w_ffn_down):
    bp, tp, _ = x_prompt.shape
    bs, ts, _ = x_sample.shape
    assert ts == CHUNK and tp % (ATT_CPS * CHUNK) == 0 and bs % ATT_CPS == 0
    n_p, n_s = bp * tp, bs * ts
    tm = 1024
    x = jnp.concatenate([x_prompt.reshape(n_p, D_MODEL), x_sample.reshape(n_s, D_MODEL)], axis=0)

    n_cond = bp + bs
    pad = (-n_cond) % SUBLANES
    c_all = jnp.concatenate([c_prompt, c_sample, jnp.zeros((pad, D_MODEL), F32)], axis=0)
    mod = _mod_call(c_all, w_mod, b_mod)
    n_mod = mod.shape[2]

    def per_block(m, n_seq, t_len):
        reps = t_len // MOD_BLOCK
        return jnp.broadcast_to(m[:, :, None, :], (DEPTH, n_seq, reps, n_mod)).reshape(DEPTH, n_seq * reps, n_mod)

    modx = jnp.concatenate([per_block(mod[:, :bp], bp, tp), per_block(mod[:, bp:n_cond], bs, ts)],
                           axis=1)

    g, p = N_SSM_GROUPS, SSM_STATE
    gpc = g // SSM_KC
    pw_re, pw_im, bb_re, bb_im = _s5_prep_call(ssm_a_re, ssm_a_im, ssm_log_dt,
                                               ssm_b_re.transpose(0, 3, 1, 2), ssm_b_im.transpose(0, 3, 1, 2))
    pw_re = pw_re.reshape(DEPTH, len(S5_POWERS), STATE_W)
    pw_im = pw_im.reshape(DEPTH, len(S5_POWERS), STATE_W)

    def wb_blocks(bb):
        per_chunk = bb.reshape(DEPTH, SSM_GROUP, SSM_KC, gpc * p).transpose(0, 2, 1, 3)
        return _block_diag_rows(per_chunk, gpc).astype(BF16)

    def wc_blocks(cc):
        per_chunk = cc.reshape(DEPTH, SSM_KC, gpc, SSM_GROUP, p).transpose(0, 1, 3, 2, 4)
        transposed = _block_diag_rows(per_chunk.reshape(DEPTH, SSM_KC, SSM_GROUP, gpc * p), gpc)
        return jnp.swapaxes(transposed.astype(BF16), -1, -2)

    ssm_w = (wb_blocks(bb_re), wb_blocks(bb_im), pw_re, pw_im, wc_blocks(ssm_c_re), wc_blocks(ssm_c_im), ssm_d,
             w_glu.astype(BF16), b_glu)
    zeros_state = jnp.zeros((bp, 1, STATE_W), F32)
    h0_re = state_ssm_re.reshape(DEPTH, bs, 1, STATE_W)
    h0_im = state_ssm_im.reshape(DEPTH, bs, 1, STATE_W)

    pos = jnp.concatenate([jnp.tile(jnp.arange(tp), bp), jnp.tile(PAST_LEN + jnp.arange(ts), bs)])
    rope_tab = _rope_table(pos)
    gq = jnp.tile(q_norm_g, (1, LANES // HEAD_DIM)).reshape(DEPTH, 1, LANES)
    gk = jnp.tile(k_norm_g, (1, LANES // HEAD_DIM)).reshape(DEPTH, 1, LANES)
    cache_k2 = cache_k.reshape(DEPTH, bs * WINDOW, D_KV)
    cache_v2 = cache_v.reshape(DEPTH, bs * WINDOW, D_KV)

    w_in_b, w_gate_b = w_in[:1].astype(BF16), w_gate[:1].astype(BF16)
    v0 = D_SSM + D_ATTN + D_KV

    def heads(t):
        return t.reshape(*t.shape[:-1], N_KV_HEADS, HEAD_DIM)

    def last_window(t, col0, col1):
        return jnp.stack([t[(b + 1) * tp - WINDOW:(b + 1) * tp, col0:col1] for b in range(bp)])

    outs = {k: [] for k in ("pk", "pv", "pre", "pim", "sk", "sv", "sre", "sim")}
    for l in range(DEPTH):
        proj, gates = _in_gate_call(x, l, norm1_g, modx, w_in_b, w_gate_b, b_gate, tm=tm)

        nxt = min(l + 1, DEPTH - 1)
        cast = [(w, l) for w in (w_proj_ssm, w_proj_attn, w_out, w_ffn_gate, w_ffn_up, w_ffn_down)]
        cast += [(w_in, nxt), (w_gate, nxt)]
        ssm_p, pre, pim, w_ps_b, w_pa_b, w_out_b, w_fg_b, w_fu_b, w_fd_b, w_in_b, w_gate_b = _ssm_call(
            proj, 0, bp, tp, S5_TILES[0], zeros_state, zeros_state, l, *ssm_w, cast=cast, name="s5_prompt")
        ssm_s, sre, sim = _ssm_call(proj, n_p, bs, ts, S5_TILES[1], h0_re[l], h0_im[l], l, *ssm_w,
                                    name="s5_sample")
        attn_out, kn = _attn_call(proj, l, n_p, tp, attn_sink, rope_tab, gq, gk, cache_k2, cache_v2)

        mixed = _mix_call(ssm_p, ssm_s, attn_out, gates, w_ps_b, w_pa_b, tm=tm, tn=1024)
        x = _resid_mm_call(mixed, l, w_out_b, x, modx, 2, tm=tm, tn=1024, name="out_proj")

        act = _ffn_up_call(x, l, norm2_g, modx, w_fg_b, w_fu_b, tm=tm)
        last = l == DEPTH - 1
        x = _resid_mm_call(act, l, w_fd_b, x, modx, 5, n_split=n_p // tm if last else None, tm=tm,
                           name="ffn_down_split" if last else "ffn_down")

        outs["pk"].append(heads(last_window(kn, 0, D_KV)))
        outs["pv"].append(heads(last_window(proj, v0, IN_WIDTH)))
        outs["pre"].append(pre.reshape(bp, g, p))
        outs["pim"].append(pim.reshape(bp, g, p))
        outs["sk"].append(jnp.concatenate([cache_k[l][:, ts:], heads(kn[n_p:].reshape(bs, ts, D_KV))], axis=1))
        outs["sv"].append(jnp.concatenate([cache_v[l][:, ts:], heads(proj[n_p:, v0:].reshape(bs, ts, D_KV))],
                                          axis=1))
        outs["sre"].append(sre.reshape(bs, g, p))
        outs["sim"].append(sim.reshape(bs, g, p))

    y_p, y_s = x
    return (y_p.reshape(bp, tp, D_MODEL), y_s.reshape(bs, ts, D_MODEL),
            jnp.stack(outs["pk"]), jnp.stack(outs["pv"]), jnp.stack(outs["pre"]), jnp.stack(outs["pim"]),
            jnp.stack(outs["sk"]), jnp.stack(outs["sv"]), jnp.stack(outs["sre"]), jnp.stack(outs["sim"]))
```

```python
import functools
import math

import jax
import jax.numpy as jnp
from jax import lax
from jax.experimental import pallas as pl
from jax.experimental.pallas import tpu as pltpu

D_MODEL = 2048
DEPTH = 4
CHUNK = 64
D_SSM = 1024
SSM_GROUP = 16
N_SSM_GROUPS = 64
SSM_STATE = 64
HEAD_DIM = 64
N_HEADS = 16
N_KV_HEADS = 4
KV_REP = N_HEADS // N_KV_HEADS
D_ATTN = N_HEADS * HEAD_DIM
D_KV = N_KV_HEADS * HEAD_DIM
IN_WIDTH = D_SSM + D_ATTN + 2 * D_KV
WINDOW = 128
ROPE_DIM = 16
ROPE_THETA = 500000.0
D_FF = 5632
EPS = 1e-6
PAST_LEN = 2048

LANES = 128
SUBLANES = 8
MOD_BLOCK = CHUNK
STATE_W = N_SSM_GROUPS * SSM_STATE
SSM_KC = 4
VMEM_LIMIT = 56 * 1024 * 1024

F32 = jnp.float32
BF16 = jnp.bfloat16


def _cparams(sem):
    return pltpu.CompilerParams(dimension_semantics=sem, vmem_limit_bytes=VMEM_LIMIT)


def _sigmoid(x):
    return 0.5 + 0.5 * jnp.tanh(0.5 * x)


def _layer_spec(l, shape):
    zeros = (0,) * len(shape)
    return pl.BlockSpec((None, *shape), lambda *_: (l, *zeros))


def _mod_kernel(c_ref, w_ref, b_ref, o_ref):
    c = c_ref[...].astype(BF16)
    w = w_ref[...].astype(BF16)
    o_ref[...] = jnp.dot(c, w, preferred_element_type=F32) + b_ref[...]


def _mod_call(c_all, w_mod, b_mod):
    nb = c_all.shape[0]
    tn = 1024
    n_out = w_mod.shape[2]
    return pl.pallas_call(
        _mod_kernel,
        out_shape=jax.ShapeDtypeStruct((DEPTH, nb, n_out), F32),
        grid=(DEPTH, n_out // tn),
        in_specs=[
            pl.BlockSpec((nb, D_MODEL), lambda l, j: (0, 0)),
            pl.BlockSpec((None, D_MODEL, tn), lambda l, j: (l, 0, j)),
            pl.BlockSpec((None, 1, tn), lambda l, j: (l, 0, j)),
        ],
        out_specs=pl.BlockSpec((None, nb, tn), lambda l, j: (l, 0, j)),
        compiler_params=_cparams(("arbitrary", "arbitrary")),
        name="adaln_mod",
    )(c_all, w_mod, b_mod.reshape(DEPTH, 1, n_out))


S5_TILE = 256
S5_SEG = SUBLANES
S5_POWERS = (1, *(k * S5_TILE // S5_SEG for k in (1, 2, 4)))


def _s5_prep_kernel(are_ref, aim_ref, ldt_ref, bre_ref, bim_ref, pwr_ref, pwi_ref, bbr_ref, bbi_ref):
    a_re = are_ref[...]
    a_im = aim_ref[...]
    dt = jnp.exp(ldt_ref[...])
    z_re = a_re * dt
    z_im = a_im * dt
    for i, n in enumerate(S5_POWERS):
        mag = jnp.exp(z_re * float(n))
        pwr_ref[i] = mag * jnp.cos(z_im * float(n))
        pwi_ref[i] = mag * jnp.sin(z_im * float(n))
    l_re = pwr_ref[S5_POWERS.index(1)]
    l_im = pwi_ref[S5_POWERS.index(1)]
    den = a_re * a_re + a_im * a_im
    n_re = l_re - 1.0
    f_re = (n_re * a_re + l_im * a_im) / den
    f_im = (l_im * a_re - n_re * a_im) / den
    for c in range(SSM_GROUP):
        b_re = bre_ref[c]
        b_im = bim_ref[c]
        bbr_ref[c] = f_re * b_re - f_im * b_im
        bbi_ref[c] = f_re * b_im + f_im * b_re


def _s5_prep_call(a_re, a_im, log_dt, bt_re, bt_im):
    g, p = N_SSM_GROUPS, SSM_STATE
    mat = pl.BlockSpec((None, g, p), lambda l: (l, 0, 0))
    stack_c = pl.BlockSpec((None, SSM_GROUP, g, p), lambda l: (l, 0, 0, 0))
    n_pow = len(S5_POWERS)
    stack_n = pl.BlockSpec((None, n_pow, g, p), lambda l: (l, 0, 0, 0))
    return pl.pallas_call(
        _s5_prep_kernel,
        out_shape=(jax.ShapeDtypeStruct((DEPTH, n_pow, g, p), F32),
                   jax.ShapeDtypeStruct((DEPTH, n_pow, g, p), F32),
                   jax.ShapeDtypeStruct((DEPTH, SSM_GROUP, g, p), F32),
                   jax.ShapeDtypeStruct((DEPTH, SSM_GROUP, g, p), F32)),
        grid=(DEPTH,),
        in_specs=[mat, mat, pl.BlockSpec((None, g, 1), lambda l: (l, 0, 0)), stack_c, stack_c],
        out_specs=(stack_n, stack_n, stack_c, stack_c),
        compiler_params=_cparams(("arbitrary",)),
        name="s5_discretize",
    )(a_re, a_im, log_dt.reshape(DEPTH, g, 1), bt_re, bt_im)


NORM_CHUNK = 2 * MOD_BLOCK


def _norm_mod_chunk(x_ref, g_ref, sc_ref, sh_ref, h_scr, slot, chunk):
    gain = g_ref[...]
    per_chunk = NORM_CHUNK // MOD_BLOCK
    for s in range(per_chunk):
        x = x_ref[s * MOD_BLOCK:(s + 1) * MOD_BLOCK, :]
        ms = jnp.mean(x * x, axis=-1, keepdims=True)
        y = x * lax.rsqrt(ms + EPS) * gain
        mod_row = pl.ds(chunk * per_chunk + s, 1)
        h = y * (1.0 + sc_ref[mod_row, :]) + sh_ref[mod_row, :]
        rows = pl.ds(pl.multiple_of((chunk * per_chunk + s) * MOD_BLOCK, MOD_BLOCK), MOD_BLOCK)
        h_scr[slot, rows, :] = h.astype(BF16)


def _norm_pipeline(l, n_tiles, tm, sc_idx, sh_idx):
    n_chunks = tm // NORM_CHUNK
    nsub = tm // MOD_BLOCK
    tile = lambda i: jnp.minimum(i, n_tiles - 1)
    specs = [
        pl.BlockSpec((NORM_CHUNK, D_MODEL), lambda i, j: (tile(i) * n_chunks + jnp.minimum(j, n_chunks - 1), 0)),
        _layer_spec(l, (1, D_MODEL)),
        pl.BlockSpec((None, nsub, D_MODEL), lambda i, j: (l, tile(i), sc_idx)),
        pl.BlockSpec((None, nsub, D_MODEL), lambda i, j: (l, tile(i), sh_idx)),
    ]
    scratch = pltpu.VMEM((2, tm, D_MODEL), BF16)
    out_row = lambda i: jnp.maximum(i - 1, 0)
    out_col = lambda i, col: jnp.where(i == 0, 0, col)
    return specs, scratch, n_chunks, out_row, out_col


def _in_gate_kernel(x_ref, g_ref, sc_ref, sh_ref, wi_ref, wg_ref, b_ref, proj_ref, gate_ref, h_scr, *, n_in,
                    n_chunks):
    i, j = pl.program_id(0), pl.program_id(1)
    ready = lax.rem(i + 1, 2)

    filling = j < n_chunks

    def fill():
        _norm_mod_chunk(x_ref, g_ref, sc_ref, sh_ref, h_scr, lax.rem(i, 2), j)

    def gates():
        acc = jnp.dot(h_scr[ready], wg_ref[...], preferred_element_type=F32)
        gate_ref[...] = _sigmoid(acc + b_ref[...]).astype(gate_ref.dtype)

    pl.when(jnp.logical_and(i == 0, filling))(fill)

    @pl.when(jnp.logical_and(i > 0, j < n_in))
    def _():
        proj_ref[...] = jnp.dot(h_scr[ready], wi_ref[...], preferred_element_type=F32)
        fill()

    @pl.when(jnp.logical_and(i > 0, jnp.logical_and(j >= n_in, filling)))
    def _():
        gates()
        fill()

    pl.when(jnp.logical_and(i > 0, jnp.logical_not(filling)))(gates)


def _in_gate_call(x, l, gain, modx, w_in, w_gate, b_gate, *, tm=1024, tn=512):
    n_tiles = x.shape[0] // tm
    n_in, n_gate = w_in.shape[2] // tn, w_gate.shape[2] // tn
    norm_specs, h_scratch, n_chunks, out_row, out_col = _norm_pipeline(l, n_tiles, tm, 1, 0)
    assert n_in <= n_chunks <= n_in + n_gate
    in_col = lambda j: jnp.minimum(j, n_in - 1)
    gate_col = lambda j: jnp.maximum(j - n_in, 0)
    return pl.pallas_call(
        functools.partial(_in_gate_kernel, n_in=n_in, n_chunks=n_chunks),
        out_shape=(jax.ShapeDtypeStruct((x.shape[0], w_in.shape[2]), F32),
                   jax.ShapeDtypeStruct((x.shape[0], w_gate.shape[2]), BF16)),
        grid=(n_tiles + 1, n_in + n_gate),
        in_specs=norm_specs + [
            pl.BlockSpec((None, D_MODEL, tn), lambda i, j: (0, 0, in_col(j))),
            pl.BlockSpec((None, D_MODEL, tn), lambda i, j: (0, 0, gate_col(j))),
            pl.BlockSpec((None, 1, tn), lambda i, j: (l, 0, gate_col(j))),
        ],
        out_specs=(pl.BlockSpec((tm, tn), lambda i, j: (out_row(i), out_col(i, in_col(j)))),
                   pl.BlockSpec((tm, tn), lambda i, j: (out_row(i), out_col(i, gate_col(j))))),
        scratch_shapes=[h_scratch],
        compiler_params=_cparams(("arbitrary", "arbitrary")),
        name="in_proj_gates",
    )(x, gain.reshape(DEPTH, 1, D_MODEL), modx, modx, w_in, w_gate, b_gate.reshape(DEPTH, 1, w_gate.shape[2]))


def _ffn_up_kernel(x_ref, g_ref, sc_ref, sh_ref, wg_ref, wu_ref, o_ref, h_scr, *, n_chunks):
    i, j = pl.program_id(0), pl.program_id(1)

    filling = j < n_chunks

    def fill():
        _norm_mod_chunk(x_ref, g_ref, sc_ref, sh_ref, h_scr, lax.rem(i, 2), j)

    def swiglu():
        h = h_scr[lax.rem(i + 1, 2)]
        gate = jnp.dot(h, wg_ref[...], preferred_element_type=F32)
        up = jnp.dot(h, wu_ref[...], preferred_element_type=F32)
        o_ref[...] = (gate * _sigmoid(gate) * up).astype(o_ref.dtype)

    pl.when(jnp.logical_and(i == 0, filling))(fill)

    @pl.when(jnp.logical_and(i > 0, filling))
    def _():
        swiglu()
        fill()

    pl.when(jnp.logical_and(i > 0, jnp.logical_not(filling)))(swiglu)


def _ffn_up_call(x, l, gain, modx, w_gate, w_up, *, tm=1024, tn=512):
    n_tiles = x.shape[0] // tm
    n_out = w_gate.shape[2]
    norm_specs, h_scratch, n_chunks, out_row, out_col = _norm_pipeline(l, n_tiles, tm, 4, 3)
    assert n_out // tn >= n_chunks
    w_spec = pl.BlockSpec((None, D_MODEL, tn), lambda i, j: (0, 0, j))
    return pl.pallas_call(
        functools.partial(_ffn_up_kernel, n_chunks=n_chunks),
        out_shape=jax.ShapeDtypeStruct((x.shape[0], n_out), BF16),
        grid=(n_tiles + 1, n_out // tn),
        in_specs=norm_specs + [w_spec, w_spec],
        out_specs=pl.BlockSpec((tm, tn), lambda i, j: (out_row(i), out_col(i, j))),
        scratch_shapes=[h_scratch],
        compiler_params=_cparams(("arbitrary", "arbitrary")),
        name="ffn_up",
    )(x, gain.reshape(DEPTH, 1, D_MODEL), modx, modx, w_gate, w_up)


def _mix_kernel(sp_ref, ss_ref, a_ref, ga_ref, gb_ref, ws_ref, wa_ref, o_ref, *, n_split):
    def body(s_ref):
        ps = jnp.dot(s_ref[...], ws_ref[...], preferred_element_type=F32)
        pa = jnp.dot(a_ref[...], wa_ref[...], preferred_element_type=F32)
        o_ref[...] = (ga_ref[...] * ps + gb_ref[...] * pa).astype(o_ref.dtype)

    i = pl.program_id(0)
    pl.when(i < n_split)(lambda: body(sp_ref))
    pl.when(i >= n_split)(lambda: body(ss_ref))


def _mix_call(ssm_p, ssm_s, attn_out, gates, w_ps, w_pa, *, tm=1024, tn=512):
    n_tok = attn_out.shape[0]
    nj = D_MODEL // tn
    n_split = ssm_p.shape[0] // tm
    return pl.pallas_call(
        functools.partial(_mix_kernel, n_split=n_split),
        out_shape=jax.ShapeDtypeStruct((n_tok, D_MODEL), BF16),
        grid=(n_tok // tm, nj),
        in_specs=[
            pl.BlockSpec((tm, D_SSM), lambda i, j: (jnp.minimum(i, n_split - 1), 0)),
            pl.BlockSpec((tm, D_SSM), lambda i, j: (jnp.maximum(i - n_split, 0), 0)),
            pl.BlockSpec((tm, D_ATTN), lambda i, j: (i, 0)),
            pl.BlockSpec((tm, tn), lambda i, j: (i, j)),
            pl.BlockSpec((tm, tn), lambda i, j: (i, j + nj)),
            pl.BlockSpec((None, D_SSM, tn), lambda i, j: (0, 0, j)),
            pl.BlockSpec((None, D_ATTN, tn), lambda i, j: (0, 0, j)),
        ],
        out_specs=pl.BlockSpec((tm, tn), lambda i, j: (i, j)),
        compiler_params=_cparams(("arbitrary", "arbitrary")),
        name="branch_merge",
    )(ssm_p, ssm_s, attn_out, gates, gates, w_ps, w_pa)


def _resid_mm_kernel(a_ref, w_ref, x_ref, g_ref, *o_refs, nsub, n_split):
    acc = jnp.dot(a_ref[...], w_ref[...], preferred_element_type=F32)

    def write(o_ref):
        for s in range(nsub):
            rows = slice(s * MOD_BLOCK, (s + 1) * MOD_BLOCK)
            o_ref[rows, :] = x_ref[rows, :] + g_ref[s:s + 1, :] * acc[rows, :]

    if n_split is None:
        write(o_refs[0])
    else:
        i = pl.program_id(0)
        pl.when(i < n_split)(lambda: write(o_refs[0]))
        pl.when(i >= n_split)(lambda: write(o_refs[1]))


def _resid_mm_call(a, l, w, x, modx, g_idx, *, n_split=None, tm=1024, tn=512, name):
    n_tok, k = a.shape
    nsub = tm // MOD_BLOCK
    nj = D_MODEL // tn
    n_tiles = n_tok // tm
    if n_split is None:
        out_shape = jax.ShapeDtypeStruct((n_tok, D_MODEL), F32)
        out_specs = pl.BlockSpec((tm, tn), lambda i, j: (i, j))
    else:
        out_shape = (jax.ShapeDtypeStruct((n_split * tm, D_MODEL), F32),
                     jax.ShapeDtypeStruct(((n_tiles - n_split) * tm, D_MODEL), F32))
        out_specs = (
            pl.BlockSpec((tm, tn), lambda i, j: (jnp.minimum(i, n_split - 1), jnp.where(i < n_split, j, nj - 1))),
            pl.BlockSpec((tm, tn), lambda i, j: (jnp.maximum(i - n_split, 0), jnp.where(i < n_split, 0, j))),
        )
    return pl.pallas_call(
        functools.partial(_resid_mm_kernel, nsub=nsub, n_split=n_split),
        out_shape=out_shape,
        grid=(n_tiles, nj),
        in_specs=[
            pl.BlockSpec((tm, k), lambda i, j: (i, 0)),
            pl.BlockSpec((None, k, tn), lambda i, j: (0, 0, j)),
            pl.BlockSpec((tm, tn), lambda i, j: (i, j)),
            pl.BlockSpec((None, nsub, tn), lambda i, j: (l, i, g_idx * nj + j)),
        ],
        out_specs=out_specs,
        compiler_params=_cparams(("arbitrary", "arbitrary")),
        name=name,
    )(a, w, x, modx)


SCAN_LW = 512


def _ssm_kernel(*refs, tt, n_cast, sps):
    (u_ref, h0r_ref, h0i_ref, wbr_ref, wbi_ref, pwj_ref, seg_ref, wcr_ref, wci_ref, d_ref, wglu_ref,
     bglu_ref) = refs[:12]
    cast_in = refs[12:12 + n_cast]
    o_ref, htr_ref, hti_ref = refs[12 + n_cast:15 + n_cast]
    cast_out = refs[15 + n_cast:15 + 2 * n_cast]
    bur, bui, cre, cim, z_scr = refs[15 + 2 * n_cast:]
    steps = tt // S5_SEG
    for w_ref, wb_ref in zip(cast_in, cast_out):
        wb_ref[...] = w_ref[...].astype(BF16)

    carried = sps == S5_SEG
    if carried:
        @pl.when(pl.program_id(1) == 0)
        def _():
            cre[...] = jnp.broadcast_to(h0r_ref[...], (SUBLANES, STATE_W))
            cim[...] = jnp.broadcast_to(h0i_ref[...], (SUBLANES, STATE_W))

    r_idx = lax.broadcasted_iota(jnp.int32, (tt, tt), 0)
    c_idx = lax.broadcasted_iota(jnp.int32, (tt, tt), 1)
    to_scan = jnp.where(c_idx == (r_idx % S5_SEG) * steps + r_idx // S5_SEG, 1.0, 0.0).astype(BF16)
    to_time = jnp.where(r_idx == (c_idx % S5_SEG) * steps + c_idx // S5_SEG, 1.0, 0.0).astype(BF16)

    u = u_ref[...]
    us = jnp.dot(to_scan, u.astype(BF16), preferred_element_type=F32).astype(BF16)
    kw = D_SSM // SSM_KC
    sw = STATE_W // SSM_KC
    for kc in range(SSM_KC):
        uk = us[:, kc * kw:(kc + 1) * kw]
        bur[:, kc * sw:(kc + 1) * sw] = jnp.dot(uk, wbr_ref[kc], preferred_element_type=F32)
        bui[:, kc * sw:(kc + 1) * sw] = jnp.dot(uk, wbi_ref[kc], preferred_element_type=F32)

    row = lax.broadcasted_iota(jnp.int32, (SUBLANES, SCAN_LW), 0)
    for lc in range(STATE_W // SCAN_LW):
        sl = slice(lc * SCAN_LW, (lc + 1) * SCAN_LW)
        lam_re = jnp.broadcast_to(pwj_ref[0, 0:1, sl], (SUBLANES, SCAN_LW))
        lam_im = jnp.broadcast_to(pwj_ref[1, 0:1, sl], (SUBLANES, SCAN_LW))

        def step(j, carry, sl=sl, lam_re=lam_re, lam_im=lam_im):
            h_re, h_im = carry
            rows = pl.ds(pl.multiple_of(j * SUBLANES, SUBLANES), SUBLANES)
            return (lam_re * h_re - lam_im * h_im + bur[rows, sl],
                    lam_re * h_im + lam_im * h_re + bui[rows, sl])

        zero = jnp.zeros((SUBLANES, SCAN_LW), F32)
        e_re, e_im = lax.fori_loop(0, steps, step, (zero, zero), unroll=4)

        first = row % sps == 0
        in_re, in_im = (cre, cim) if carried else (h0r_ref, h0i_ref)
        x_re = jnp.where(first, in_re[:, sl], pltpu.roll(e_re, 1, 0))
        x_im = jnp.where(first, in_im[:, sl], pltpu.roll(e_im, 1, 0))
        for idx, k in enumerate((1, 2, 4)):
            if k >= sps:
                continue
            m_re = seg_ref[2 * idx, :, sl]
            m_im = seg_ref[2 * idx + 1, :, sl]
            s_re = pltpu.roll(x_re, k, 0)
            s_im = pltpu.roll(x_im, k, 0)
            x_re, x_im = (x_re + m_re * s_re - m_im * s_im,
                          x_im + m_re * s_im + m_im * s_re)
        pj_re = jnp.broadcast_to(pwj_ref[0, 1:2, sl], (SUBLANES, SCAN_LW))
        pj_im = jnp.broadcast_to(pwj_ref[1, 1:2, sl], (SUBLANES, SCAN_LW))
        n_re = pj_re * x_re - pj_im * x_im + e_re
        n_im = pj_re * x_im + pj_im * x_re + e_im
        if carried:
            cre[:, sl] = jnp.broadcast_to(n_re[SUBLANES - 1:SUBLANES, :], (SUBLANES, SCAN_LW))
            cim[:, sl] = jnp.broadcast_to(n_im[SUBLANES - 1:SUBLANES, :], (SUBLANES, SCAN_LW))
        else:
            htr_ref[:, sl] = n_re
            hti_ref[:, sl] = n_im

        def scan(j, carry, sl=sl, step=step):
            h_re, h_im = step(j, carry)
            rows = pl.ds(pl.multiple_of(j * SUBLANES, SUBLANES), SUBLANES)
            bur[rows, sl] = h_re
            bui[rows, sl] = h_im
            return h_re, h_im

        lax.fori_loop(0, steps, scan, (x_re, x_im), unroll=2)

    if carried:
        htr_ref[...] = cre[0:1, :]
        hti_ref[...] = cim[0:1, :]

    d = d_ref[...]
    for kc in range(SSM_KC):
        h_re = bur[:, kc * sw:(kc + 1) * sw].astype(BF16)
        h_im = bui[:, kc * sw:(kc + 1) * sw].astype(BF16)
        ys = (jnp.dot(h_re, wcr_ref[kc], preferred_element_type=F32)
              - jnp.dot(h_im, wci_ref[kc], preferred_element_type=F32))
        ys_hi = ys.astype(BF16)
        ys_lo = (ys - ys_hi.astype(F32)).astype(BF16)
        y = (jnp.dot(to_time, ys_hi, preferred_element_type=F32)
             + jnp.dot(to_time, ys_lo, preferred_element_type=F32))
        cols = slice(kc * kw, (kc + 1) * kw)
        y = y + d[:, cols] * u[:, cols]
        z_scr[:, cols] = 0.5 * y * (1.0 + jnp.tanh(math.sqrt(2.0 / math.pi) * (y + 0.044715 * (y * y * y))))
    z = z_scr[...]
    gate = _sigmoid(jnp.dot(z.astype(BF16), wglu_ref[...], preferred_element_type=F32) + bglu_ref[...])
    o_ref[...] = (z * gate).astype(o_ref.dtype)


def _ssm_call(proj, row0, n_seq, nt, h0_re, h0_im, l, wb_re, wb_im, pw_re, pw_im, wc_re, wc_im, d_skip,
              w_glu, b_glu, *, sps, cast=(), name):
    tt = S5_TILE
    steps = tt // S5_SEG
    power = lambda n: S5_POWERS.index(n)
    pwj = jnp.stack([pw[:, (power(1), power(steps)), :] for pw in (pw_re, pw_im)], axis=1)
    seg_rows = []
    for k in (1, 2, 4):
        keep = (jnp.arange(SUBLANES) % sps >= k)[None, :, None]
        seg_rows += [jnp.where(keep, pw[:, power(k * steps)][:, None, :], 0.0) for pw in (pw_re, pw_im)]
    seg = jnp.stack(seg_rows, axis=1)
    state_rows = h0_re.shape[1]
    assert state_rows == (1 if sps == S5_SEG else S5_SEG) and (sps == S5_SEG or nt == 1)
    t_len = nt * tt
    rb0 = row0 // tt
    n_steps = n_seq * nt
    state_spec = pl.BlockSpec((None, state_rows, STATE_W), lambda b, t: (b, 0, 0))
    cast_in, cast_out, cast_shapes = [], [], []
    for w, wl in cast:
        _, k, n = w.shape
        rows = k // n_steps
        cast_in.append(pl.BlockSpec((None, rows, n), lambda b, t, wl=wl: (wl, b * nt + t, 0)))
        cast_out.append(pl.BlockSpec((None, rows, n), lambda b, t: (0, b * nt + t, 0)))
        cast_shapes.append(jax.ShapeDtypeStruct((1, k, n), BF16))
    resident = lambda shape: pl.BlockSpec((None, *shape), lambda b, t: (l, *(0,) * len(shape)),
                                          pipeline_mode=pl.Buffered(1))
    return pl.pallas_call(
        functools.partial(_ssm_kernel, tt=tt, n_cast=len(cast), sps=sps),
        out_shape=(jax.ShapeDtypeStruct((n_seq * t_len, D_SSM), BF16),
                   jax.ShapeDtypeStruct((n_seq, state_rows, STATE_W), F32),
                   jax.ShapeDtypeStruct((n_seq, state_rows, STATE_W), F32),
                   *cast_shapes),
        grid=(n_seq, nt),
        in_specs=[
            pl.BlockSpec((tt, D_SSM), lambda b, t: (rb0 + b * nt + t, 0)),
            state_spec, state_spec,
            resident(wb_re.shape[1:]), resident(wb_im.shape[1:]),
            _layer_spec(l, pwj.shape[1:]), _layer_spec(l, seg.shape[1:]),
            resident(wc_re.shape[1:]), resident(wc_im.shape[1:]),
            _layer_spec(l, (1, D_SSM)),
            resident((D_SSM, D_SSM)),
            _layer_spec(l, (1, D_SSM)),
            *cast_in,
        ],
        out_specs=(pl.BlockSpec((tt, D_SSM), lambda b, t: (b * nt + t, 0)), state_spec, state_spec, *cast_out),
        scratch_shapes=[pltpu.VMEM((tt, STATE_W), F32), pltpu.VMEM((tt, STATE_W), F32),
                        pltpu.VMEM((SUBLANES, STATE_W), F32), pltpu.VMEM((SUBLANES, STATE_W), F32),
                        pltpu.VMEM((tt, D_SSM), F32)],
        compiler_params=_cparams(("arbitrary", "arbitrary")),
        name=name,
    )(proj, h0_re, h0_im, wb_re, wb_im, pwj, seg, wc_re, wc_im, d_skip.reshape(DEPTH, 1, D_SSM), w_glu,
      b_glu.reshape(DEPTH, 1, D_SSM), *[w for w, _ in cast])


RING = 3
N_KEYS = RING * CHUNK
ATT_CPS = 4


def _stack_blocks(x):
    return jnp.concatenate([x[:, j * LANES:(j + 1) * LANES] for j in range(x.shape[1] // LANES)], axis=0)


def _pair_norm_rope(xs, gain, tab, ones_bd):
    reps = xs.shape[0] // CHUNK
    cos, s_lo, s_hi = (jnp.concatenate([tab[i]] * reps, axis=0) for i in range(3))
    half = ROPE_DIM // 2
    sq = xs * xs
    sq_hi = sq.astype(BF16)
    sq_lo = (sq - sq_hi.astype(F32)).astype(BF16)
    ss = (jnp.dot(sq_hi, ones_bd, preferred_element_type=F32)
          + jnp.dot(sq_lo, ones_bd, preferred_element_type=F32))
    xg = xs * gain
    xr = xg * cos + pltpu.roll(xg, LANES - half, 1) * s_lo + pltpu.roll(xg, half, 1) * s_hi
    return xr * lax.rsqrt(ss * (1.0 / HEAD_DIM) + EPS)


def _attn_kernel(sink_ref, q_ref, k_ref, v_ref, ck_ref, cv_ref, tab_ref, gq_ref, gk_ref, o_ref, kn_ref, kd, vd,
                 *, l, n_prompt_steps, steps_per_seq):
    step = pl.program_id(0)
    is_sample = step >= n_prompt_steps
    c_base = lax.rem(step, steps_per_seq) * ATT_CPS
    first = lax.broadcasted_iota(jnp.int32, (1, LANES), 1) < HEAD_DIM
    ones_bd = jnp.where(lax.broadcasted_iota(jnp.int32, (LANES, LANES), 0) // HEAD_DIM
                        == lax.broadcasted_iota(jnp.int32, (LANES, LANES), 1) // HEAD_DIM, 1.0, 0.0).astype(BF16)
    key_slot = lax.broadcasted_iota(jnp.int32, (1, N_KEYS), 1) // CHUNK
    gq = gq_ref[...]
    gk = gk_ref[...]

    def store_dup(dst, x, rows):
        for pair in range(D_KV // LANES):
            blk = x[:, pair * LANES:(pair + 1) * LANES]
            swapped = pltpu.roll(blk, HEAD_DIM, 1)
            dst[2 * pair, rows, 0:LANES] = jnp.where(first, blk, swapped).astype(BF16)
            dst[2 * pair + 1, rows, 0:LANES] = jnp.where(first, swapped, blk).astype(BF16)

    @pl.when(step == 0)
    def _():
        vd[:, :, LANES:] = jnp.ones((N_KV_HEADS, N_KEYS, LANES), BF16)

    def chunk(ci, carry):
        c = jnp.where(is_sample, 0, c_base + ci)
        rows = pl.ds(pl.multiple_of(ci * CHUNK, CHUNK), CHUNK)
        prev = slice(CHUNK, N_KEYS)

        @pl.when(jnp.logical_and(c == 0, jnp.logical_not(is_sample)))
        def _():
            kd[:, prev, :] = jnp.zeros((N_KV_HEADS, N_KEYS - CHUNK, LANES), BF16)
            vd[:, prev, 0:LANES] = jnp.zeros((N_KV_HEADS, N_KEYS - CHUNK, LANES), BF16)

        @pl.when(is_sample)
        def _():
            cached = pl.ds(pl.multiple_of(ci * WINDOW, WINDOW), WINDOW)
            store_dup(kd, ck_ref[cached, :], prev)
            store_dup(vd, cv_ref[cached, :], prev)

        tab = tab_ref[:, rows, :]
        own = pl.ds(pl.multiple_of(lax.rem(c, RING) * CHUNK, CHUNK), CHUNK)
        kn = _pair_norm_rope(_stack_blocks(k_ref[rows, :]), gk, tab, ones_bd)
        kn = jnp.concatenate([kn[0:CHUNK], kn[CHUNK:2 * CHUNK]], axis=1)
        kn_ref[rows, :] = kn
        store_dup(kd, kn, own)
        store_dup(vd, v_ref[rows, :], own)

        qn = _pair_norm_rope(_stack_blocks(q_ref[rows, :]), gq, tab, ones_bd) * (HEAD_DIM ** -0.5)
        q_lo = jnp.where(first, qn, 0.0).astype(BF16)
        q_hi = jnp.where(first, 0.0, qn).astype(BF16)

        valid = key_slot <= jnp.where(is_sample, RING, c)
        scores = []
        for kh in range(N_KV_HEADS):
            r0 = kh * 2 * CHUNK
            lhs = jnp.concatenate([q_lo[r0:r0 + CHUNK], q_hi[r0:r0 + CHUNK],
                                   q_lo[r0 + CHUNK:r0 + 2 * CHUNK], q_hi[r0 + CHUNK:r0 + 2 * CHUNK]], axis=0)
            scores.append(lax.dot_general(lhs, kd[kh], (((1,), (1,)), ((), ())),
                                          preferred_element_type=F32))
        weighted, sink_terms = [], []
        for kh in range(N_KV_HEADS):
            s = jnp.where(valid, scores[kh], -1e30)
            sink = jnp.concatenate([jnp.full((CHUNK, 1), sink_ref[l, kh * KV_REP + r], F32)
                                    for r in range(KV_REP)], axis=0)
            m = jnp.maximum(jnp.max(s, axis=-1, keepdims=True), sink)
            p = jnp.exp(s - m).astype(BF16)
            sink_terms.append(jnp.exp(sink - m))
            weighted.append(jnp.dot(p, vd[kh], preferred_element_type=F32))
        for kh in range(N_KV_HEADS):
            od = weighted[kh]
            o = od[:, 0:LANES] / (od[:, LANES:] + sink_terms[kh])
            for pair in range(2):
                half = pair * 2 * CHUNK
                blk = jnp.where(first, o[half:half + CHUNK], o[half + CHUNK:half + 2 * CHUNK])
                col = (2 * kh + pair) * LANES
                o_ref[rows, col:col + LANES] = blk.astype(o_ref.dtype)
        return carry

    lax.fori_loop(0, ATT_CPS, chunk, 0)


def _attn_call(proj, l, n_prompt, t_prompt, sink, rope_tab, gq, gk, cache_k, cache_v):
    n_tok = proj.shape[0]
    tr = ATT_CPS * CHUNK
    qcol = D_SSM // D_ATTN
    kcol = (D_SSM + D_ATTN) // D_KV
    n_prompt_steps = n_prompt // tr
    cache_spec = pl.BlockSpec((None, ATT_CPS * WINDOW, D_KV),
                              lambda s: (l, jnp.maximum(s - n_prompt_steps, 0), 0))
    return pl.pallas_call(
        functools.partial(_attn_kernel, l=l, n_prompt_steps=n_prompt_steps, steps_per_seq=t_prompt // tr),
        out_shape=(jax.ShapeDtypeStruct((n_tok, D_ATTN), BF16),
                   jax.ShapeDtypeStruct((n_tok, D_KV), F32)),
        grid=(n_tok // tr,),
        in_specs=[
            pl.BlockSpec(memory_space=pltpu.SMEM),
            pl.BlockSpec((tr, D_ATTN), lambda s: (s, qcol)),
            pl.BlockSpec((tr, D_KV), lambda s: (s, kcol)),
            pl.BlockSpec((tr, D_KV), lambda s: (s, kcol + 1)),
            cache_spec, cache_spec,
            pl.BlockSpec((3, tr, LANES), lambda s: (0, s, 0)),
            _layer_spec(l, (1, LANES)), _layer_spec(l, (1, LANES)),
        ],
        out_specs=(pl.BlockSpec((tr, D_ATTN), lambda s: (s, 0)),
                   pl.BlockSpec((tr, D_KV), lambda s: (s, 0))),
        scratch_shapes=[pltpu.VMEM((N_KV_HEADS, N_KEYS, LANES), BF16),
                        pltpu.VMEM((N_KV_HEADS, N_KEYS, 2 * LANES), BF16)],
        compiler_params=_cparams(("arbitrary",)),
        name="banded_attn",
    )(sink, proj, proj, proj, cache_k, cache_v, rope_tab, gq, gk)


def _block_diag_rows(x, n):
    *lead, r, width = x.shape
    keep = jnp.arange(n)[:, None, None] == (jnp.arange(width) // (width // n))[None, None, :]
    return jnp.where(keep, x[..., None, :, :], 0.0).reshape(*lead, n * r, width)


def _rope_table(pos):
    half = ROPE_DIM // 2
    n_pos = pos.shape[0]
    inv_freq = ROPE_THETA ** (-jnp.arange(half, dtype=F32) / half)
    ang = pos.astype(F32)[:, None] * inv_freq[None, :]
    cos, sin = jnp.cos(ang), jnp.sin(ang)
    ones = jnp.ones((n_pos, HEAD_DIM - ROPE_DIM), F32)
    zeros = jnp.zeros((n_pos, HEAD_DIM - half), F32)
    c_tab = jnp.concatenate([cos, cos, ones], axis=1)
    lo_tab = jnp.concatenate([-sin, zeros], axis=1)
    hi_tab = jnp.concatenate([jnp.zeros((n_pos, half), F32), sin, jnp.zeros((n_pos, HEAD_DIM - ROPE_DIM), F32)],
                             axis=1)
    tab = jnp.stack([c_tab, lo_tab, hi_tab])
    return jnp.concatenate([tab, tab], axis=2)


def kernel(x_prompt, x_sample, cache_k, cache_v, state_ssm_re, state_ssm_im, c_prompt, c_sample, w_mod, b_mod, norm1_g, norm2_g, w_in, ssm_a_re, ssm_a_im, ssm_log_dt, ssm_b_re, ssm_b_im, ssm_c_re, ssm_c_im, ssm_d, w_glu, b_glu, q_norm_g, k_norm_g, attn_sink, w_gate, b_gate, w_proj_ssm, w_proj_attn, w_out, w_ffn_gate, w_ffn_up, w_ffn_down):
    bp, tp, _ = x_prompt.shape
    bs, ts, _ = x_sample.shape
    assert ts == CHUNK and tp % (ATT_CPS * CHUNK) == 0 and bs % ATT_CPS == 0
    assert tp % S5_TILE == 0 and S5_TILE % ts == 0 and (bs * ts) % S5_TILE == 0
    n_p, n_s = bp * tp, bs * ts
    tm = 1024
    x = jnp.concatenate([x_prompt.reshape(n_p, D_MODEL), x_sample.reshape(n_s, D_MODEL)], axis=0)

    n_cond = bp + bs
    pad = (-n_cond) % SUBLANES
    c_all = jnp.concatenate([c_prompt, c_sample, jnp.zeros((pad, D_MODEL), F32)], axis=0)
    mod = _mod_call(c_all, w_mod, b_mod)
    n_mod = mod.shape[2]

    def per_block(m, n_seq, t_len):
        reps = t_len // MOD_BLOCK
        return jnp.broadcast_to(m[:, :, None, :], (DEPTH, n_seq, reps, n_mod)).reshape(DEPTH, n_seq * reps, n_mod)

    modx = jnp.concatenate([per_block(mod[:, :bp], bp, tp), per_block(mod[:, bp:n_cond], bs, ts)],
                           axis=1)

    g, p = N_SSM_GROUPS, SSM_STATE
    gpc = g // SSM_KC
    pw_re, pw_im, bb_re, bb_im = _s5_prep_call(ssm_a_re, ssm_a_im, ssm_log_dt,
                                               ssm_b_re.transpose(0, 3, 1, 2), ssm_b_im.transpose(0, 3, 1, 2))
    pw_re = pw_re.reshape(DEPTH, len(S5_POWERS), STATE_W)
    pw_im = pw_im.reshape(DEPTH, len(S5_POWERS), STATE_W)

    def wb_blocks(bb):
        per_chunk = bb.reshape(DEPTH, SSM_GROUP, SSM_KC, gpc * p).transpose(0, 2, 1, 3)
        return _block_diag_rows(per_chunk, gpc).astype(BF16)

    def wc_blocks(cc):
        per_chunk = cc.reshape(DEPTH, SSM_KC, gpc, SSM_GROUP, p).transpose(0, 1, 3, 2, 4)
        transposed = _block_diag_rows(per_chunk.reshape(DEPTH, SSM_KC, SSM_GROUP, gpc * p), gpc)
        return jnp.swapaxes(transposed.astype(BF16), -1, -2)

    ssm_w = (wb_blocks(bb_re), wb_blocks(bb_im), pw_re, pw_im, wc_blocks(ssm_c_re), wc_blocks(ssm_c_im), ssm_d,
             w_glu.astype(BF16), b_glu)
    zeros_state = jnp.zeros((bp, 1, STATE_W), F32)
    sps = ts * S5_SEG // S5_TILE
    n_sample_tiles = n_s // S5_TILE

    def per_segment(state):
        rows = jnp.repeat(state.reshape(DEPTH, bs, STATE_W), sps, axis=1)
        return rows.reshape(DEPTH, n_sample_tiles, S5_SEG, STATE_W)

    def last_segment(state):
        return state.reshape(bs, sps, STATE_W)[:, sps - 1].reshape(bs, g, p)

    h0_re, h0_im = per_segment(state_ssm_re), per_segment(state_ssm_im)

    pos = jnp.concatenate([jnp.tile(jnp.arange(tp), bp), jnp.tile(PAST_LEN + jnp.arange(ts), bs)])
    rope_tab = _rope_table(pos)
    gq = jnp.tile(q_norm_g, (1, LANES // HEAD_DIM)).reshape(DEPTH, 1, LANES)
    gk = jnp.tile(k_norm_g, (1, LANES // HEAD_DIM)).reshape(DEPTH, 1, LANES)
    cache_k2 = cache_k.reshape(DEPTH, bs * WINDOW, D_KV)
    cache_v2 = cache_v.reshape(DEPTH, bs * WINDOW, D_KV)

    w_in_b, w_gate_b = w_in[:1].astype(BF16), w_gate[:1].astype(BF16)
    v0 = D_SSM + D_ATTN + D_KV

    def heads(t):
        return t.reshape(*t.shape[:-1], N_KV_HEADS, HEAD_DIM)

    def last_window(t, col0, col1):
        return jnp.stack([t[(b + 1) * tp - WINDOW:(b + 1) * tp, col0:col1] for b in range(bp)])

    outs = {k: [] for k in ("pk", "pv", "pre", "pim", "sk", "sv", "sre", "sim")}
    for l in range(DEPTH):
        proj, gates = _in_gate_call(x, l, norm1_g, modx, w_in_b, w_gate_b, b_gate, tm=tm)

        nxt = min(l + 1, DEPTH - 1)
        cast = [(w, l) for w in (w_proj_ssm, w_proj_attn, w_out, w_ffn_gate, w_ffn_up, w_ffn_down)]
        cast += [(w_in, nxt), (w_gate, nxt)]
        ssm_p, pre, pim, w_ps_b, w_pa_b, w_out_b, w_fg_b, w_fu_b, w_fd_b, w_in_b, w_gate_b = _ssm_call(
            proj, 0, bp, tp // S5_TILE, zeros_state, zeros_state, l, *ssm_w, sps=S5_SEG, cast=cast, name="s5_prompt")
        ssm_s, sre, sim = _ssm_call(proj, n_p, n_sample_tiles, 1, h0_re[l], h0_im[l], l, *ssm_w, sps=sps,
                                    name="s5_sample")
        attn_out, kn = _attn_call(proj, l, n_p, tp, attn_sink, rope_tab, gq, gk, cache_k2, cache_v2)

        mixed = _mix_call(ssm_p, ssm_s, attn_out, gates, w_ps_b, w_pa_b, tm=tm, tn=1024)
        x = _resid_mm_call(mixed, l, w_out_b, x, modx, 2, tm=tm, tn=1024, name="out_proj")

        act = _ffn_up_call(x, l, norm2_g, modx, w_fg_b, w_fu_b, tm=tm)
        last = l == DEPTH - 1
        x = _resid_mm_call(act, l, w_fd_b, x, modx, 5, n_split=n_p // tm if last else None, tm=tm,
                           name="ffn_down_split" if last else "ffn_down")

        outs["pk"].append(heads(last_window(kn, 0, D_KV)))
        outs["pv"].append(heads(last_window(proj, v0, IN_WIDTH)))
        outs["pre"].append(pre.reshape(bp, g, p))
        outs["pim"].append(pim.reshape(bp, g, p))
        outs["sk"].append(jnp.concatenate([cache_k[l][:, ts:], heads(kn[n_p:].reshape(bs, ts, D_KV))], axis=1))
        outs["sv"].append(jnp.concatenate([cache_v[l][:, ts:], heads(proj[n_p:, v0:].reshape(bs, ts, D_KV))],
                                          axis=1))
        outs["sre"].append(last_segment(sre))
        outs["sim"].append(last_segment(sim))

    y_p, y_s = x
    return (y_p.reshape(bp, tp, D_MODEL), y_s.reshape(bs, ts, D_MODEL),
            jnp.stack(outs["pk"]), jnp.stack(outs["pv"]), jnp.stack(outs["pre"]), jnp.stack(outs["pim"]),
            jnp.stack(outs["sk"]), jnp.stack(outs["sv"]), jnp.stack(outs["sre"]), jnp.stack(outs["sim"]))
```

```python
import functools
import math

import jax
import jax.numpy as jnp
from jax import lax
from jax.experimental import pallas as pl
from jax.experimental.pallas import tpu as pltpu

D_MODEL = 2048
DEPTH = 4
CHUNK = 64
D_SSM = 1024
SSM_GROUP = 16
N_SSM_GROUPS = 64
SSM_STATE = 64
HEAD_DIM = 64
N_HEADS = 16
N_KV_HEADS = 4
KV_REP = N_HEADS // N_KV_HEADS
D_ATTN = N_HEADS * HEAD_DIM
D_KV = N_KV_HEADS * HEAD_DIM
IN_WIDTH = D_SSM + D_ATTN + 2 * D_KV
WINDOW = 128
ROPE_DIM = 16
ROPE_THETA = 500000.0
D_FF = 5632
EPS = 1e-6
PAST_LEN = 2048

LANES = 128
SUBLANES = 8
MOD_BLOCK = CHUNK
STATE_W = N_SSM_GROUPS * SSM_STATE
SSM_KC = 4
VMEM_LIMIT = 56 * 1024 * 1024

F32 = jnp.float32
BF16 = jnp.bfloat16


def _cparams(sem):
    return pltpu.CompilerParams(dimension_semantics=sem, vmem_limit_bytes=VMEM_LIMIT)


def _sigmoid(x):
    return 0.5 + 0.5 * jnp.tanh(0.5 * x)


def _layer_spec(l, shape):
    zeros = (0,) * len(shape)
    return pl.BlockSpec((None, *shape), lambda *_: (l, *zeros))


def _mod_kernel(c_ref, w_ref, b_ref, o_ref):
    c = c_ref[...].astype(BF16)
    w = w_ref[...].astype(BF16)
    o_ref[...] = jnp.dot(c, w, preferred_element_type=F32) + b_ref[...]


def _mod_call(c_all, w_mod, b_mod):
    nb = c_all.shape[0]
    tn = 1024
    n_out = w_mod.shape[2]
    return pl.pallas_call(
        _mod_kernel,
        out_shape=jax.ShapeDtypeStruct((DEPTH, nb, n_out), F32),
        grid=(DEPTH, n_out // tn),
        in_specs=[
            pl.BlockSpec((nb, D_MODEL), lambda l, j: (0, 0)),
            pl.BlockSpec((None, D_MODEL, tn), lambda l, j: (l, 0, j)),
            pl.BlockSpec((None, 1, tn), lambda l, j: (l, 0, j)),
        ],
        out_specs=pl.BlockSpec((None, nb, tn), lambda l, j: (l, 0, j)),
        compiler_params=_cparams(("arbitrary", "arbitrary")),
        name="adaln_mod",
    )(c_all, w_mod, b_mod.reshape(DEPTH, 1, n_out))


S5_TILE = 256
S5_SEG = SUBLANES
S5_POWERS = (1, *(k * S5_TILE // S5_SEG for k in (1, 2, 4)))


def _s5_prep_kernel(are_ref, aim_ref, ldt_ref, bre_ref, bim_ref, pwr_ref, pwi_ref, bbr_ref, bbi_ref):
    a_re = are_ref[...]
    a_im = aim_ref[...]
    dt = jnp.exp(ldt_ref[...])
    z_re = a_re * dt
    z_im = a_im * dt
    for i, n in enumerate(S5_POWERS):
        mag = jnp.exp(z_re * float(n))
        pwr_ref[i] = mag * jnp.cos(z_im * float(n))
        pwi_ref[i] = mag * jnp.sin(z_im * float(n))
    l_re = pwr_ref[S5_POWERS.index(1)]
    l_im = pwi_ref[S5_POWERS.index(1)]
    den = a_re * a_re + a_im * a_im
    n_re = l_re - 1.0
    f_re = (n_re * a_re + l_im * a_im) / den
    f_im = (l_im * a_re - n_re * a_im) / den
    for c in range(SSM_GROUP):
        b_re = bre_ref[c]
        b_im = bim_ref[c]
        bbr_ref[c] = f_re * b_re - f_im * b_im
        bbi_ref[c] = f_re * b_im + f_im * b_re


def _s5_prep_call(a_re, a_im, log_dt, bt_re, bt_im):
    g, p = N_SSM_GROUPS, SSM_STATE
    mat = pl.BlockSpec((None, g, p), lambda l: (l, 0, 0))
    stack_c = pl.BlockSpec((None, SSM_GROUP, g, p), lambda l: (l, 0, 0, 0))
    n_pow = len(S5_POWERS)
    stack_n = pl.BlockSpec((None, n_pow, g, p), lambda l: (l, 0, 0, 0))
    return pl.pallas_call(
        _s5_prep_kernel,
        out_shape=(jax.ShapeDtypeStruct((DEPTH, n_pow, g, p), F32),
                   jax.ShapeDtypeStruct((DEPTH, n_pow, g, p), F32),
                   jax.ShapeDtypeStruct((DEPTH, SSM_GROUP, g, p), F32),
                   jax.ShapeDtypeStruct((DEPTH, SSM_GROUP, g, p), F32)),
        grid=(DEPTH,),
        in_specs=[mat, mat, pl.BlockSpec((None, g, 1), lambda l: (l, 0, 0)), stack_c, stack_c],
        out_specs=(stack_n, stack_n, stack_c, stack_c),
        compiler_params=_cparams(("arbitrary",)),
        name="s5_discretize",
    )(a_re, a_im, log_dt.reshape(DEPTH, g, 1), bt_re, bt_im)


def _norm_mod_chunk(x_ref, g_ref, sc_ref, sh_ref, h_scr, slot, chunk):
    gain = g_ref[...]
    per_chunk = x_ref.shape[0] // MOD_BLOCK
    for s in range(per_chunk):
        x = x_ref[s * MOD_BLOCK:(s + 1) * MOD_BLOCK, :]
        ms = jnp.mean(x * x, axis=-1, keepdims=True)
        y = x * lax.rsqrt(ms + EPS) * gain
        mod_row = pl.ds(chunk * per_chunk + s, 1)
        h = y * (1.0 + sc_ref[mod_row, :]) + sh_ref[mod_row, :]
        rows = pl.ds(pl.multiple_of((chunk * per_chunk + s) * MOD_BLOCK, MOD_BLOCK), MOD_BLOCK)
        h_scr[slot, rows, :] = h.astype(BF16)


def _norm_pipeline(l, n_tiles, tm, sc_idx, sh_idx, chunk_rows):
    n_chunks = tm // chunk_rows
    nsub = tm // MOD_BLOCK
    tile = lambda i: jnp.minimum(i, n_tiles - 1)
    specs = [
        pl.BlockSpec((chunk_rows, D_MODEL), lambda i, j: (tile(i) * n_chunks + jnp.minimum(j, n_chunks - 1), 0)),
        _layer_spec(l, (1, D_MODEL)),
        pl.BlockSpec((None, nsub, D_MODEL), lambda i, j: (l, tile(i), sc_idx)),
        pl.BlockSpec((None, nsub, D_MODEL), lambda i, j: (l, tile(i), sh_idx)),
    ]
    scratch = pltpu.VMEM((2, tm, D_MODEL), BF16)
    out_row = lambda i: jnp.maximum(i - 1, 0)
    out_col = lambda i, col: jnp.where(i == 0, 0, col)
    return specs, scratch, n_chunks, out_row, out_col


def _in_gate_kernel(x_ref, g_ref, sc_ref, sh_ref, wi_ref, wg_ref, b_ref, proj_ref, gate_ref, h_scr, *, n_in,
                    n_chunks):
    i, j = pl.program_id(0), pl.program_id(1)
    ready = lax.rem(i + 1, 2)

    filling = j < n_chunks

    def fill():
        _norm_mod_chunk(x_ref, g_ref, sc_ref, sh_ref, h_scr, lax.rem(i, 2), j)

    def gates():
        acc = jnp.dot(h_scr[ready], wg_ref[...], preferred_element_type=F32)
        gate_ref[...] = _sigmoid(acc + b_ref[...]).astype(gate_ref.dtype)

    pl.when(jnp.logical_and(i == 0, filling))(fill)

    @pl.when(jnp.logical_and(i > 0, j < n_in))
    def _():
        proj_ref[...] = jnp.dot(h_scr[ready], wi_ref[...], preferred_element_type=F32)
        fill()

    @pl.when(jnp.logical_and(i > 0, jnp.logical_and(j >= n_in, filling)))
    def _():
        gates()
        fill()

    pl.when(jnp.logical_and(i > 0, jnp.logical_not(filling)))(gates)


def _in_gate_call(x, l, gain, modx, w_in, w_gate, b_gate, *, tm=1024, tn_in=1280, tn_gate=1024):
    n_tiles = x.shape[0] // tm
    n_in, n_gate = w_in.shape[2] // tn_in, w_gate.shape[2] // tn_gate
    norm_specs, h_scratch, n_chunks, out_row, out_col = _norm_pipeline(l, n_tiles, tm, 1, 0, 4 * MOD_BLOCK)
    assert n_in <= n_chunks <= n_in + n_gate
    in_col = lambda j: jnp.minimum(j, n_in - 1)
    gate_col = lambda j: jnp.maximum(j - n_in, 0)
    return pl.pallas_call(
        functools.partial(_in_gate_kernel, n_in=n_in, n_chunks=n_chunks),
        out_shape=(jax.ShapeDtypeStruct((x.shape[0], w_in.shape[2]), F32),
                   jax.ShapeDtypeStruct((x.shape[0], w_gate.shape[2]), BF16)),
        grid=(n_tiles + 1, n_in + n_gate),
        in_specs=norm_specs + [
            pl.BlockSpec((None, D_MODEL, tn_in), lambda i, j: (0, 0, in_col(j))),
            pl.BlockSpec((None, D_MODEL, tn_gate), lambda i, j: (0, 0, gate_col(j))),
            pl.BlockSpec((None, 1, tn_gate), lambda i, j: (l, 0, gate_col(j))),
        ],
        out_specs=(pl.BlockSpec((tm, tn_in), lambda i, j: (out_row(i), out_col(i, in_col(j)))),
                   pl.BlockSpec((tm, tn_gate), lambda i, j: (out_row(i), out_col(i, gate_col(j))))),
        scratch_shapes=[h_scratch],
        compiler_params=_cparams(("arbitrary", "arbitrary")),
        name="in_proj_gates",
    )(x, gain.reshape(DEPTH, 1, D_MODEL), modx, modx, w_in, w_gate, b_gate.reshape(DEPTH, 1, w_gate.shape[2]))


def _ffn_up_kernel(x_ref, g_ref, sc_ref, sh_ref, wg_ref, wu_ref, o_ref, h_scr, *, n_chunks):
    i, j = pl.program_id(0), pl.program_id(1)

    filling = j < n_chunks

    def fill():
        _norm_mod_chunk(x_ref, g_ref, sc_ref, sh_ref, h_scr, lax.rem(i, 2), j)

    def swiglu():
        h = h_scr[lax.rem(i + 1, 2)]
        gate = jnp.dot(h, wg_ref[...], preferred_element_type=F32)
        up = jnp.dot(h, wu_ref[...], preferred_element_type=F32)
        o_ref[...] = (gate * _sigmoid(gate) * up).astype(o_ref.dtype)

    pl.when(jnp.logical_and(i == 0, filling))(fill)

    @pl.when(jnp.logical_and(i > 0, filling))
    def _():
        swiglu()
        fill()

    pl.when(jnp.logical_and(i > 0, jnp.logical_not(filling)))(swiglu)


def _ffn_up_call(x, l, gain, modx, w_gate, w_up, *, tm=1024, tn=512):
    n_tiles = x.shape[0] // tm
    n_out = w_gate.shape[2]
    norm_specs, h_scratch, n_chunks, out_row, out_col = _norm_pipeline(l, n_tiles, tm, 4, 3, 2 * MOD_BLOCK)
    assert n_out // tn >= n_chunks
    w_spec = pl.BlockSpec((None, D_MODEL, tn), lambda i, j: (0, 0, j))
    return pl.pallas_call(
        functools.partial(_ffn_up_kernel, n_chunks=n_chunks),
        out_shape=jax.ShapeDtypeStruct((x.shape[0], n_out), BF16),
        grid=(n_tiles + 1, n_out // tn),
        in_specs=norm_specs + [w_spec, w_spec],
        out_specs=pl.BlockSpec((tm, tn), lambda i, j: (out_row(i), out_col(i, j))),
        scratch_shapes=[h_scratch],
        compiler_params=_cparams(("arbitrary", "arbitrary")),
        name="ffn_up",
    )(x, gain.reshape(DEPTH, 1, D_MODEL), modx, modx, w_gate, w_up)


def _mix_kernel(sp_ref, ss_ref, a_ref, ga_ref, gb_ref, ws_ref, wa_ref, o_ref, *, n_split):
    def body(s_ref):
        ps = jnp.dot(s_ref[...], ws_ref[...], preferred_element_type=F32)
        pa = jnp.dot(a_ref[...], wa_ref[...], preferred_element_type=F32)
        o_ref[...] = (ga_ref[...] * ps + gb_ref[...] * pa).astype(o_ref.dtype)

    i = pl.program_id(0)
    pl.when(i < n_split)(lambda: body(sp_ref))
    pl.when(i >= n_split)(lambda: body(ss_ref))


def _mix_call(ssm_p, ssm_s, attn_out, gates, w_ps, w_pa, *, tm=1024, tn=512):
    n_tok = attn_out.shape[0]
    nj = D_MODEL // tn
    n_split = ssm_p.shape[0] // tm
    return pl.pallas_call(
        functools.partial(_mix_kernel, n_split=n_split),
        out_shape=jax.ShapeDtypeStruct((n_tok, D_MODEL), BF16),
        grid=(n_tok // tm, nj),
        in_specs=[
            pl.BlockSpec((tm, D_SSM), lambda i, j: (jnp.minimum(i, n_split - 1), 0)),
            pl.BlockSpec((tm, D_SSM), lambda i, j: (jnp.maximum(i - n_split, 0), 0)),
            pl.BlockSpec((tm, D_ATTN), lambda i, j: (i, 0)),
            pl.BlockSpec((tm, tn), lambda i, j: (i, j)),
            pl.BlockSpec((tm, tn), lambda i, j: (i, j + nj)),
            pl.BlockSpec((None, D_SSM, tn), lambda i, j: (0, 0, j)),
            pl.BlockSpec((None, D_ATTN, tn), lambda i, j: (0, 0, j)),
        ],
        out_specs=pl.BlockSpec((tm, tn), lambda i, j: (i, j)),
        compiler_params=_cparams(("arbitrary", "arbitrary")),
        name="branch_merge",
    )(ssm_p, ssm_s, attn_out, gates, gates, w_ps, w_pa)


def _resid_mm_kernel(a_ref, w_ref, x_ref, g_ref, *o_refs, nsub, n_split):
    acc = jnp.dot(a_ref[...], w_ref[...], preferred_element_type=F32)

    def write(o_ref):
        for s in range(nsub):
            rows = slice(s * MOD_BLOCK, (s + 1) * MOD_BLOCK)
            o_ref[rows, :] = x_ref[rows, :] + g_ref[s:s + 1, :] * acc[rows, :]

    if n_split is None:
        write(o_refs[0])
    else:
        i = pl.program_id(0)
        pl.when(i < n_split)(lambda: write(o_refs[0]))
        pl.when(i >= n_split)(lambda: write(o_refs[1]))


def _resid_mm_call(a, l, w, x, modx, g_idx, *, n_split=None, tm=1024, tn=512, name):
    n_tok, k = a.shape
    nsub = tm // MOD_BLOCK
    nj = D_MODEL // tn
    n_tiles = n_tok // tm
    if n_split is None:
        out_shape = jax.ShapeDtypeStruct((n_tok, D_MODEL), F32)
        out_specs = pl.BlockSpec((tm, tn), lambda i, j: (i, j))
    else:
        out_shape = (jax.ShapeDtypeStruct((n_split * tm, D_MODEL), F32),
                     jax.ShapeDtypeStruct(((n_tiles - n_split) * tm, D_MODEL), F32))
        out_specs = (
            pl.BlockSpec((tm, tn), lambda i, j: (jnp.minimum(i, n_split - 1), jnp.where(i < n_split, j, nj - 1))),
            pl.BlockSpec((tm, tn), lambda i, j: (jnp.maximum(i - n_split, 0), jnp.where(i < n_split, 0, j))),
        )
    return pl.pallas_call(
        functools.partial(_resid_mm_kernel, nsub=nsub, n_split=n_split),
        out_shape=out_shape,
        grid=(n_tiles, nj),
        in_specs=[
            pl.BlockSpec((tm, k), lambda i, j: (i, 0)),
            pl.BlockSpec((None, k, tn), lambda i, j: (0, 0, j)),
            pl.BlockSpec((tm, tn), lambda i, j: (i, j)),
            pl.BlockSpec((None, nsub, tn), lambda i, j: (l, i, g_idx * nj + j)),
        ],
        out_specs=out_specs,
        compiler_params=_cparams(("arbitrary", "arbitrary")),
        name=name,
    )(a, w, x, modx)


SCAN_LW = 512


def _ssm_kernel(*refs, tt, n_cast, sps):
    (u_ref, h0r_ref, h0i_ref, wbr_ref, wbi_ref, pwj_ref, seg_ref, wcr_ref, wci_ref, d_ref, wglu_ref,
     bglu_ref) = refs[:12]
    cast_in = refs[12:12 + n_cast]
    o_ref, htr_ref, hti_ref = refs[12 + n_cast:15 + n_cast]
    cast_out = refs[15 + n_cast:15 + 2 * n_cast]
    bur, bui, cre, cim, z_scr = refs[15 + 2 * n_cast:]
    steps = tt // S5_SEG
    for w_ref, wb_ref in zip(cast_in, cast_out):
        wb_ref[...] = w_ref[...].astype(BF16)

    carried = sps == S5_SEG
    if carried:
        @pl.when(pl.program_id(1) == 0)
        def _():
            cre[...] = jnp.broadcast_to(h0r_ref[...], (SUBLANES, STATE_W))
            cim[...] = jnp.broadcast_to(h0i_ref[...], (SUBLANES, STATE_W))

    r_idx = lax.broadcasted_iota(jnp.int32, (tt, tt), 0)
    c_idx = lax.broadcasted_iota(jnp.int32, (tt, tt), 1)
    to_scan = jnp.where(c_idx == (r_idx % S5_SEG) * steps + r_idx // S5_SEG, 1.0, 0.0).astype(BF16)
    to_time = jnp.where(r_idx == (c_idx % S5_SEG) * steps + c_idx // S5_SEG, 1.0, 0.0).astype(BF16)

    u = u_ref[...]
    us = jnp.dot(to_scan, u.astype(BF16), preferred_element_type=F32).astype(BF16)
    kw = D_SSM // SSM_KC
    sw = STATE_W // SSM_KC
    for kc in range(SSM_KC):
        uk = us[:, kc * kw:(kc + 1) * kw]
        bur[:, kc * sw:(kc + 1) * sw] = jnp.dot(uk, wbr_ref[kc], preferred_element_type=F32)
        bui[:, kc * sw:(kc + 1) * sw] = jnp.dot(uk, wbi_ref[kc], preferred_element_type=F32)

    row = lax.broadcasted_iota(jnp.int32, (SUBLANES, SCAN_LW), 0)
    for lc in range(STATE_W // SCAN_LW):
        sl = slice(lc * SCAN_LW, (lc + 1) * SCAN_LW)
        lam_re = jnp.broadcast_to(pwj_ref[0, 0:1, sl], (SUBLANES, SCAN_LW))
        lam_im = jnp.broadcast_to(pwj_ref[1, 0:1, sl], (SUBLANES, SCAN_LW))

        def step(j, carry, sl=sl, lam_re=lam_re, lam_im=lam_im):
            h_re, h_im = carry
            rows = pl.ds(pl.multiple_of(j * SUBLANES, SUBLANES), SUBLANES)
            return (lam_re * h_re - lam_im * h_im + bur[rows, sl],
                    lam_re * h_im + lam_im * h_re + bui[rows, sl])

        zero = jnp.zeros((SUBLANES, SCAN_LW), F32)
        e_re, e_im = lax.fori_loop(0, steps, step, (zero, zero), unroll=4)

        first = row % sps == 0
        in_re, in_im = (cre, cim) if carried else (h0r_ref, h0i_ref)
        x_re = jnp.where(first, in_re[:, sl], pltpu.roll(e_re, 1, 0))
        x_im = jnp.where(first, in_im[:, sl], pltpu.roll(e_im, 1, 0))
        for idx, k in enumerate((1, 2, 4)):
            if k >= sps:
                continue
            m_re = seg_ref[2 * idx, :, sl]
            m_im = seg_ref[2 * idx + 1, :, sl]
            s_re = pltpu.roll(x_re, k, 0)
            s_im = pltpu.roll(x_im, k, 0)
            x_re, x_im = (x_re + m_re * s_re - m_im * s_im,
                          x_im + m_re * s_im + m_im * s_re)
        pj_re = jnp.broadcast_to(pwj_ref[0, 1:2, sl], (SUBLANES, SCAN_LW))
        pj_im = jnp.broadcast_to(pwj_ref[1, 1:2, sl], (SUBLANES, SCAN_LW))
        n_re = pj_re * x_re - pj_im * x_im + e_re
        n_im = pj_re * x_im + pj_im * x_re + e_im
        if carried:
            cre[:, sl] = jnp.broadcast_to(n_re[SUBLANES - 1:SUBLANES, :], (SUBLANES, SCAN_LW))
            cim[:, sl] = jnp.broadcast_to(n_im[SUBLANES - 1:SUBLANES, :], (SUBLANES, SCAN_LW))
        else:
            htr_ref[:, sl] = n_re
            hti_ref[:, sl] = n_im

        def scan(j, carry, sl=sl, step=step):
            h_re, h_im = step(j, carry)
            rows = pl.ds(pl.multiple_of(j * SUBLANES, SUBLANES), SUBLANES)
            bur[rows, sl] = h_re
            bui[rows, sl] = h_im
            return h_re, h_im

        lax.fori_loop(0, steps, scan, (x_re, x_im), unroll=2)

    if carried:
        htr_ref[...] = cre[0:1, :]
        hti_ref[...] = cim[0:1, :]

    d = d_ref[...]
    for kc in range(SSM_KC):
        h_re = bur[:, kc * sw:(kc + 1) * sw].astype(BF16)
        h_im = bui[:, kc * sw:(kc + 1) * sw].astype(BF16)
        ys = (jnp.dot(h_re, wcr_ref[kc], preferred_element_type=F32)
              - jnp.dot(h_im, wci_ref[kc], preferred_element_type=F32))
        ys_hi = ys.astype(BF16)
        ys_lo = (ys - ys_hi.astype(F32)).astype(BF16)
        y = (jnp.dot(to_time, ys_hi, preferred_element_type=F32)
             + jnp.dot(to_time, ys_lo, preferred_element_type=F32))
        cols = slice(kc * kw, (kc + 1) * kw)
        y = y + d[:, cols] * u[:, cols]
        z_scr[:, cols] = 0.5 * y * (1.0 + jnp.tanh(math.sqrt(2.0 / math.pi) * (y + 0.044715 * (y * y * y))))
    z = z_scr[...]
    gate = _sigmoid(jnp.dot(z.astype(BF16), wglu_ref[...], preferred_element_type=F32) + bglu_ref[...])
    o_ref[...] = (z * gate).astype(o_ref.dtype)


def _ssm_call(proj, row0, n_seq, nt, h0_re, h0_im, l, wb_re, wb_im, pw_re, pw_im, wc_re, wc_im, d_skip,
              w_glu, b_glu, *, sps, cast=(), name):
    tt = S5_TILE
    steps = tt // S5_SEG
    power = lambda n: S5_POWERS.index(n)
    pwj = jnp.stack([pw[:, (power(1), power(steps)), :] for pw in (pw_re, pw_im)], axis=1)
    seg_rows = []
    for k in (1, 2, 4):
        keep = (jnp.arange(SUBLANES) % sps >= k)[None, :, None]
        seg_rows += [jnp.where(keep, pw[:, power(k * steps)][:, None, :], 0.0) for pw in (pw_re, pw_im)]
    seg = jnp.stack(seg_rows, axis=1)
    state_rows = h0_re.shape[1]
    assert state_rows == (1 if sps == S5_SEG else S5_SEG) and (sps == S5_SEG or nt == 1)
    t_len = nt * tt
    rb0 = row0 // tt
    n_steps = n_seq * nt
    state_spec = pl.BlockSpec((None, state_rows, STATE_W), lambda b, t: (b, 0, 0))
    cast_in, cast_out, cast_shapes = [], [], []
    for w, wl in cast:
        _, k, n = w.shape
        rows = k // n_steps
        cast_in.append(pl.BlockSpec((None, rows, n), lambda b, t, wl=wl: (wl, b * nt + t, 0)))
        cast_out.append(pl.BlockSpec((None, rows, n), lambda b, t: (0, b * nt + t, 0)))
        cast_shapes.append(jax.ShapeDtypeStruct((1, k, n), BF16))
    resident = lambda shape: pl.BlockSpec((None, *shape), lambda b, t: (l, *(0,) * len(shape)),
                                          pipeline_mode=pl.Buffered(1))
    return pl.pallas_call(
        functools.partial(_ssm_kernel, tt=tt, n_cast=len(cast), sps=sps),
        out_shape=(jax.ShapeDtypeStruct((n_seq * t_len, D_SSM), BF16),
                   jax.ShapeDtypeStruct((n_seq, state_rows, STATE_W), F32),
                   jax.ShapeDtypeStruct((n_seq, state_rows, STATE_W), F32),
                   *cast_shapes),
        grid=(n_seq, nt),
        in_specs=[
            pl.BlockSpec((tt, D_SSM), lambda b, t: (rb0 + b * nt + t, 0)),
            state_spec, state_spec,
            resident(wb_re.shape[1:]), resident(wb_im.shape[1:]),
            _layer_spec(l, pwj.shape[1:]), _layer_spec(l, seg.shape[1:]),
            resident(wc_re.shape[1:]), resident(wc_im.shape[1:]),
            _layer_spec(l, (1, D_SSM)),
            resident((D_SSM, D_SSM)),
            _layer_spec(l, (1, D_SSM)),
            *cast_in,
        ],
        out_specs=(pl.BlockSpec((tt, D_SSM), lambda b, t: (b * nt + t, 0)), state_spec, state_spec, *cast_out),
        scratch_shapes=[pltpu.VMEM((tt, STATE_W), F32), pltpu.VMEM((tt, STATE_W), F32),
                        pltpu.VMEM((SUBLANES, STATE_W), F32), pltpu.VMEM((SUBLANES, STATE_W), F32),
                        pltpu.VMEM((tt, D_SSM), F32)],
        compiler_params=_cparams(("arbitrary", "arbitrary")),
        name=name,
    )(proj, h0_re, h0_im, wb_re, wb_im, pwj, seg, wc_re, wc_im, d_skip.reshape(DEPTH, 1, D_SSM), w_glu,
      b_glu.reshape(DEPTH, 1, D_SSM), *[w for w, _ in cast])


RING = 3
N_KEYS = RING * CHUNK
ATT_CPS = 4


def _stack_blocks(x):
    return jnp.concatenate([x[:, j * LANES:(j + 1) * LANES] for j in range(x.shape[1] // LANES)], axis=0)


def _pair_norm_rope(xs, gain, tab, ones_bd):
    reps = xs.shape[0] // CHUNK
    cos, s_lo, s_hi = (jnp.concatenate([tab[i]] * reps, axis=0) for i in range(3))
    half = ROPE_DIM // 2
    sq = xs * xs
    sq_hi = sq.astype(BF16)
    sq_lo = (sq - sq_hi.astype(F32)).astype(BF16)
    ss = (jnp.dot(sq_hi, ones_bd, preferred_element_type=F32)
          + jnp.dot(sq_lo, ones_bd, preferred_element_type=F32))
    xg = xs * gain
    xr = xg * cos + pltpu.roll(xg, LANES - half, 1) * s_lo + pltpu.roll(xg, half, 1) * s_hi
    return xr * lax.rsqrt(ss * (1.0 / HEAD_DIM) + EPS)


def _attn_kernel(sink_ref, q_ref, k_ref, v_ref, ck_ref, cv_ref, tab_ref, gq_ref, gk_ref, o_ref, kn_ref, kd, vd,
                 *, l, n_prompt_steps, steps_per_seq):
    step = pl.program_id(0)
    is_sample = step >= n_prompt_steps
    c_base = lax.rem(step, steps_per_seq) * ATT_CPS
    first = lax.broadcasted_iota(jnp.int32, (1, LANES), 1) < HEAD_DIM
    ones_bd = jnp.where(lax.broadcasted_iota(jnp.int32, (LANES, LANES), 0) // HEAD_DIM
                        == lax.broadcasted_iota(jnp.int32, (LANES, LANES), 1) // HEAD_DIM, 1.0, 0.0).astype(BF16)
    key_slot = lax.broadcasted_iota(jnp.int32, (1, N_KEYS), 1) // CHUNK
    gq = gq_ref[...]
    gk = gk_ref[...]

    def store_dup(dst, x, rows):
        for pair in range(D_KV // LANES):
            blk = x[:, pair * LANES:(pair + 1) * LANES]
            swapped = pltpu.roll(blk, HEAD_DIM, 1)
            dst[2 * pair, rows, 0:LANES] = jnp.where(first, blk, swapped).astype(BF16)
            dst[2 * pair + 1, rows, 0:LANES] = jnp.where(first, swapped, blk).astype(BF16)

    @pl.when(step == 0)
    def _():
        vd[:, :, LANES:] = jnp.ones((N_KV_HEADS, N_KEYS, LANES), BF16)

    def chunk(ci, carry):
        c = jnp.where(is_sample, 0, c_base + ci)
        rows = pl.ds(pl.multiple_of(ci * CHUNK, CHUNK), CHUNK)
        prev = slice(CHUNK, N_KEYS)

        @pl.when(jnp.logical_and(c == 0, jnp.logical_not(is_sample)))
        def _():
            kd[:, prev, :] = jnp.zeros((N_KV_HEADS, N_KEYS - CHUNK, LANES), BF16)
            vd[:, prev, 0:LANES] = jnp.zeros((N_KV_HEADS, N_KEYS - CHUNK, LANES), BF16)

        @pl.when(is_sample)
        def _():
            cached = pl.ds(pl.multiple_of(ci * WINDOW, WINDOW), WINDOW)
            store_dup(kd, ck_ref[cached, :], prev)
            store_dup(vd, cv_ref[cached, :], prev)

        tab = tab_ref[:, rows, :]
        own = pl.ds(pl.multiple_of(lax.rem(c, RING) * CHUNK, CHUNK), CHUNK)
        kn = _pair_norm_rope(_stack_blocks(k_ref[rows, :]), gk, tab, ones_bd)
        kn = jnp.concatenate([kn[0:CHUNK], kn[CHUNK:2 * CHUNK]], axis=1)
        kn_ref[rows, :] = kn
        store_dup(kd, kn, own)
        store_dup(vd, v_ref[rows, :], own)

        qn = _pair_norm_rope(_stack_blocks(q_ref[rows, :]), gq, tab, ones_bd) * (HEAD_DIM ** -0.5)
        q_lo = jnp.where(first, qn, 0.0).astype(BF16)
        q_hi = jnp.where(first, 0.0, qn).astype(BF16)

        valid = key_slot <= jnp.where(is_sample, RING, c)
        scores = []
        for kh in range(N_KV_HEADS):
            r0 = kh * 2 * CHUNK
            lhs = jnp.concatenate([q_lo[r0:r0 + CHUNK], q_hi[r0:r0 + CHUNK],
                                   q_lo[r0 + CHUNK:r0 + 2 * CHUNK], q_hi[r0 + CHUNK:r0 + 2 * CHUNK]], axis=0)
            scores.append(lax.dot_general(lhs, kd[kh], (((1,), (1,)), ((), ())),
                                          preferred_element_type=F32))
        weighted, sink_terms = [], []
        for kh in range(N_KV_HEADS):
            s = jnp.where(valid, scores[kh], -1e30)
            sink = jnp.concatenate([jnp.full((CHUNK, 1), sink_ref[l, kh * KV_REP + r], F32)
                                    for r in range(KV_REP)], axis=0)
            m = jnp.maximum(jnp.max(s, axis=-1, keepdims=True), sink)
            p = jnp.exp(s - m).astype(BF16)
            sink_terms.append(jnp.exp(sink - m))
            weighted.append(jnp.dot(p, vd[kh], preferred_element_type=F32))
        for kh in range(N_KV_HEADS):
            od = weighted[kh]
            o = od[:, 0:LANES] / (od[:, LANES:] + sink_terms[kh])
            for pair in range(2):
                half = pair * 2 * CHUNK
                blk = jnp.where(first, o[half:half + CHUNK], o[half + CHUNK:half + 2 * CHUNK])
                col = (2 * kh + pair) * LANES
                o_ref[rows, col:col + LANES] = blk.astype(o_ref.dtype)
        return carry

    lax.fori_loop(0, ATT_CPS, chunk, 0)


def _attn_call(proj, l, n_prompt, t_prompt, sink, rope_tab, gq, gk, cache_k, cache_v):
    n_tok = proj.shape[0]
    tr = ATT_CPS * CHUNK
    qcol = D_SSM // D_ATTN
    kcol = (D_SSM + D_ATTN) // D_KV
    n_prompt_steps = n_prompt // tr
    cache_spec = pl.BlockSpec((None, ATT_CPS * WINDOW, D_KV),
                              lambda s: (l, jnp.maximum(s - n_prompt_steps, 0), 0))
    return pl.pallas_call(
        functools.partial(_attn_kernel, l=l, n_prompt_steps=n_prompt_steps, steps_per_seq=t_prompt // tr),
        out_shape=(jax.ShapeDtypeStruct((n_tok, D_ATTN), BF16),
                   jax.ShapeDtypeStruct((n_tok, D_KV), F32)),
        grid=(n_tok // tr,),
        in_specs=[
            pl.BlockSpec(memory_space=pltpu.SMEM),
            pl.BlockSpec((tr, D_ATTN), lambda s: (s, qcol)),
            pl.BlockSpec((tr, D_KV), lambda s: (s, kcol)),
            pl.BlockSpec((tr, D_KV), lambda s: (s, kcol + 1)),
            cache_spec, cache_spec,
            pl.BlockSpec((3, tr, LANES), lambda s: (0, s, 0)),
            _layer_spec(l, (1, LANES)), _layer_spec(l, (1, LANES)),
        ],
        out_specs=(pl.BlockSpec((tr, D_ATTN), lambda s: (s, 0)),
                   pl.BlockSpec((tr, D_KV), lambda s: (s, 0))),
        scratch_shapes=[pltpu.VMEM((N_KV_HEADS, N_KEYS, LANES), BF16),
                        pltpu.VMEM((N_KV_HEADS, N_KEYS, 2 * LANES), BF16)],
        compiler_params=_cparams(("arbitrary",)),
        name="banded_attn",
    )(sink, proj, proj, proj, cache_k, cache_v, rope_tab, gq, gk)


def _block_diag_rows(x, n):
    *lead, r, width = x.shape
    keep = jnp.arange(n)[:, None, None] == (jnp.arange(width) // (width // n))[None, None, :]
    return jnp.where(keep, x[..., None, :, :], 0.0).reshape(*lead, n * r, width)


def _rope_table(pos):
    half = ROPE_DIM // 2
    n_pos = pos.shape[0]
    inv_freq = ROPE_THETA ** (-jnp.arange(half, dtype=F32) / half)
    ang = pos.astype(F32)[:, None] * inv_freq[None, :]
    cos, sin = jnp.cos(ang), jnp.sin(ang)
    ones = jnp.ones((n_pos, HEAD_DIM - ROPE_DIM), F32)
    zeros = jnp.zeros((n_pos, HEAD_DIM - half), F32)
    c_tab = jnp.concatenate([cos, cos, ones], axis=1)
    lo_tab = jnp.concatenate([-sin, zeros], axis=1)
    hi_tab = jnp.concatenate([jnp.zeros((n_pos, half), F32), sin, jnp.zeros((n_pos, HEAD_DIM - ROPE_DIM), F32)],
                             axis=1)
    tab = jnp.stack([c_tab, lo_tab, hi_tab])
    return jnp.concatenate([tab, tab], axis=2)


def kernel(x_prompt, x_sample, cache_k, cache_v, state_ssm_re, state_ssm_im, c_prompt, c_sample, w_mod, b_mod, norm1_g, norm2_g, w_in, ssm_a_re, ssm_a_im, ssm_log_dt, ssm_b_re, ssm_b_im, ssm_c_re, ssm_c_im, ssm_d, w_glu, b_glu, q_norm_g, k_norm_g, attn_sink, w_gate, b_gate, w_proj_ssm, w_proj_attn, w_out, w_ffn_gate, w_ffn_up, w_ffn_down):
    bp, tp, _ = x_prompt.shape
    bs, ts, _ = x_sample.shape
    assert ts == CHUNK and tp % (ATT_CPS * CHUNK) == 0 and bs % ATT_CPS == 0
    assert tp % S5_TILE == 0 and S5_TILE % ts == 0 and (bs * ts) % S5_TILE == 0
    n_p, n_s = bp * tp, bs * ts
    tm = 1024
    x = jnp.concatenate([x_prompt.reshape(n_p, D_MODEL), x_sample.reshape(n_s, D_MODEL)], axis=0)

    n_cond = bp + bs
    pad = (-n_cond) % SUBLANES
    c_all = jnp.concatenate([c_prompt, c_sample, jnp.zeros((pad, D_MODEL), F32)], axis=0)
    mod = _mod_call(c_all, w_mod, b_mod)
    n_mod = mod.shape[2]

    def per_block(m, n_seq, t_len):
        reps = t_len // MOD_BLOCK
        return jnp.broadcast_to(m[:, :, None, :], (DEPTH, n_seq, reps, n_mod)).reshape(DEPTH, n_seq * reps, n_mod)

    modx = jnp.concatenate([per_block(mod[:, :bp], bp, tp), per_block(mod[:, bp:n_cond], bs, ts)],
                           axis=1)

    g, p = N_SSM_GROUPS, SSM_STATE
    gpc = g // SSM_KC
    pw_re, pw_im, bb_re, bb_im = _s5_prep_call(ssm_a_re, ssm_a_im, ssm_log_dt,
                                               ssm_b_re.transpose(0, 3, 1, 2), ssm_b_im.transpose(0, 3, 1, 2))
    pw_re = pw_re.reshape(DEPTH, len(S5_POWERS), STATE_W)
    pw_im = pw_im.reshape(DEPTH, len(S5_POWERS), STATE_W)

    def wb_blocks(bb):
        per_chunk = bb.reshape(DEPTH, SSM_GROUP, SSM_KC, gpc * p).transpose(0, 2, 1, 3)
        return _block_diag_rows(per_chunk, gpc).astype(BF16)

    def wc_blocks(cc):
        per_chunk = cc.reshape(DEPTH, SSM_KC, gpc, SSM_GROUP, p).transpose(0, 1, 3, 2, 4)
        transposed = _block_diag_rows(per_chunk.reshape(DEPTH, SSM_KC, SSM_GROUP, gpc * p), gpc)
        return jnp.swapaxes(transposed.astype(BF16), -1, -2)

    ssm_w = (wb_blocks(bb_re), wb_blocks(bb_im), pw_re, pw_im, wc_blocks(ssm_c_re), wc_blocks(ssm_c_im), ssm_d,
             w_glu.astype(BF16), b_glu)
    zeros_state = jnp.zeros((bp, 1, STATE_W), F32)
    sps = ts * S5_SEG // S5_TILE
    n_sample_tiles = n_s // S5_TILE

    def per_segment(state):
        rows = jnp.repeat(state.reshape(DEPTH, bs, STATE_W), sps, axis=1)
        return rows.reshape(DEPTH, n_sample_tiles, S5_SEG, STATE_W)

    def last_segment(state):
        return state.reshape(bs, sps, STATE_W)[:, sps - 1].reshape(bs, g, p)

    h0_re, h0_im = per_segment(state_ssm_re), per_segment(state_ssm_im)

    pos = jnp.concatenate([jnp.tile(jnp.arange(tp), bp), jnp.tile(PAST_LEN + jnp.arange(ts), bs)])
    rope_tab = _rope_table(pos)
    gq = jnp.tile(q_norm_g, (1, LANES // HEAD_DIM)).reshape(DEPTH, 1, LANES)
    gk = jnp.tile(k_norm_g, (1, LANES // HEAD_DIM)).reshape(DEPTH, 1, LANES)
    cache_k2 = cache_k.reshape(DEPTH, bs * WINDOW, D_KV)
    cache_v2 = cache_v.reshape(DEPTH, bs * WINDOW, D_KV)

    w_in_b, w_gate_b = w_in[:1].astype(BF16), w_gate[:1].astype(BF16)
    v0 = D_SSM + D_ATTN + D_KV

    def heads(t):
        return t.reshape(*t.shape[:-1], N_KV_HEADS, HEAD_DIM)

    def last_window(t, col0, col1):
        return jnp.stack([t[(b + 1) * tp - WINDOW:(b + 1) * tp, col0:col1] for b in range(bp)])

    outs = {k: [] for k in ("pk", "pv", "pre", "pim", "sk", "sv", "sre", "sim")}
    for l in range(DEPTH):
        proj, gates = _in_gate_call(x, l, norm1_g, modx, w_in_b, w_gate_b, b_gate, tm=tm)

        nxt = min(l + 1, DEPTH - 1)
        cast = [(w, l) for w in (w_proj_ssm, w_proj_attn, w_out, w_ffn_gate, w_ffn_up, w_ffn_down)]
        cast += [(w_in, nxt), (w_gate, nxt)]
        ssm_p, pre, pim, w_ps_b, w_pa_b, w_out_b, w_fg_b, w_fu_b, w_fd_b, w_in_b, w_gate_b = _ssm_call(
            proj, 0, bp, tp // S5_TILE, zeros_state, zeros_state, l, *ssm_w, sps=S5_SEG, cast=cast, name="s5_prompt")
        ssm_s, sre, sim = _ssm_call(proj, n_p, n_sample_tiles, 1, h0_re[l], h0_im[l], l, *ssm_w, sps=sps,
                                    name="s5_sample")
        attn_out, kn = _attn_call(proj, l, n_p, tp, attn_sink, rope_tab, gq, gk, cache_k2, cache_v2)

        mixed = _mix_call(ssm_p, ssm_s, attn_out, gates, w_ps_b, w_pa_b, tm=tm, tn=1024)
        x = _resid_mm_call(mixed, l, w_out_b, x, modx, 2, tm=tm, tn=1024, name="out_proj")

        act = _ffn_up_call(x, l, norm2_g, modx, w_fg_b, w_fu_b, tm=tm)
        last = l == DEPTH - 1
        x = _resid_mm_call(act, l, w_fd_b, x, modx, 5, n_split=n_p // tm if last else None, tm=tm,
                           name="ffn_down_split" if last else "ffn_down")

        outs["pk"].append(heads(last_window(kn, 0, D_KV)))
        outs["pv"].append(heads(last_window(proj, v0, IN_WIDTH)))
        outs["pre"].append(pre.reshape(bp, g, p))
        outs["pim"].append(pim.reshape(bp, g, p))
        outs["sk"].append(jnp.concatenate([cache_k[l][:, ts:], heads(kn[n_p:].reshape(bs, ts, D_KV))], axis=1))
        outs["sv"].append(jnp.concatenate([cache_v[l][:, ts:], heads(proj[n_p:, v0:].reshape(bs, ts, D_KV))],
                                          axis=1))
        outs["sre"].append(last_segment(sre))
        outs["sim"].append(last_segment(sim))

    y_p, y_s = x
    return (y_p.reshape(bp, tp, D_MODEL), y_s.reshape(bs, ts, D_MODEL),
            jnp.stack(outs["pk"]), jnp.stack(outs["pv"]), jnp.stack(outs["pre"]), jnp.stack(outs["pim"]),
            jnp.stack(outs["sk"]), jnp.stack(outs["sv"]), jnp.stack(outs["sre"]), jnp.stack(outs["sim"]))
```

```python
import functools
import math

import jax
import jax.numpy as jnp
from jax import lax
from jax.experimental import pallas as pl
from jax.experimental.pallas import tpu as pltpu

D_MODEL = 2048
DEPTH = 4
CHUNK = 64
D_SSM = 1024
SSM_GROUP = 16
N_SSM_GROUPS = 64
SSM_STATE = 64
HEAD_DIM = 64
N_HEADS = 16
N_KV_HEADS = 4
KV_REP = N_HEADS // N_KV_HEADS
D_ATTN = N_HEADS * HEAD_DIM
D_KV = N_KV_HEADS * HEAD_DIM
IN_WIDTH = D_SSM + D_ATTN + 2 * D_KV
WINDOW = 128
ROPE_DIM = 16
ROPE_THETA = 500000.0
D_FF = 5632
EPS = 1e-6
PAST_LEN = 2048

LANES = 128
SUBLANES = 8
MOD_BLOCK = CHUNK
STATE_W = N_SSM_GROUPS * SSM_STATE
SSM_KC = 4
VMEM_LIMIT = 56 * 1024 * 1024

F32 = jnp.float32
BF16 = jnp.bfloat16


def _cparams(sem):
    return pltpu.CompilerParams(dimension_semantics=sem, vmem_limit_bytes=VMEM_LIMIT)


def _sigmoid(x):
    return 0.5 + 0.5 * jnp.tanh(0.5 * x)


def _layer_spec(l, shape):
    zeros = (0,) * len(shape)
    return pl.BlockSpec((None, *shape), lambda *_: (l, *zeros))


def _mod_kernel(c_ref, w_ref, b_ref, o_ref):
    c = c_ref[...].astype(BF16)
    w = w_ref[...].astype(BF16)
    o_ref[...] = jnp.dot(c, w, preferred_element_type=F32) + b_ref[...]


def _mod_call(c_all, w_mod, b_mod):
    nb = c_all.shape[0]
    tn = 1024
    n_out = w_mod.shape[2]
    return pl.pallas_call(
        _mod_kernel,
        out_shape=jax.ShapeDtypeStruct((DEPTH, nb, n_out), F32),
        grid=(DEPTH, n_out // tn),
        in_specs=[
            pl.BlockSpec((nb, D_MODEL), lambda l, j: (0, 0)),
            pl.BlockSpec((None, D_MODEL, tn), lambda l, j: (l, 0, j)),
            pl.BlockSpec((None, 1, tn), lambda l, j: (l, 0, j)),
        ],
        out_specs=pl.BlockSpec((None, nb, tn), lambda l, j: (l, 0, j)),
        compiler_params=_cparams(("arbitrary", "arbitrary")),
        name="adaln_mod",
    )(c_all, w_mod, b_mod.reshape(DEPTH, 1, n_out))


S5_TILE = 256
S5_SEG = SUBLANES
S5_POWERS = (1, *(k * S5_TILE // S5_SEG for k in (1, 2, 4)))


def _s5_prep_kernel(are_ref, aim_ref, ldt_ref, bre_ref, bim_ref, pwr_ref, pwi_ref, bbr_ref, bbi_ref):
    a_re = are_ref[...]
    a_im = aim_ref[...]
    dt = jnp.exp(ldt_ref[...])
    z_re = a_re * dt
    z_im = a_im * dt
    for i, n in enumerate(S5_POWERS):
        mag = jnp.exp(z_re * float(n))
        pwr_ref[i] = mag * jnp.cos(z_im * float(n))
        pwi_ref[i] = mag * jnp.sin(z_im * float(n))
    l_re = pwr_ref[S5_POWERS.index(1)]
    l_im = pwi_ref[S5_POWERS.index(1)]
    den = a_re * a_re + a_im * a_im
    n_re = l_re - 1.0
    f_re = (n_re * a_re + l_im * a_im) / den
    f_im = (l_im * a_re - n_re * a_im) / den
    for c in range(SSM_GROUP):
        b_re = bre_ref[c]
        b_im = bim_ref[c]
        bbr_ref[c] = f_re * b_re - f_im * b_im
        bbi_ref[c] = f_re * b_im + f_im * b_re


def _s5_prep_call(a_re, a_im, log_dt, bt_re, bt_im):
    g, p = N_SSM_GROUPS, SSM_STATE
    mat = pl.BlockSpec((None, g, p), lambda l: (l, 0, 0))
    stack_c = pl.BlockSpec((None, SSM_GROUP, g, p), lambda l: (l, 0, 0, 0))
    n_pow = len(S5_POWERS)
    stack_n = pl.BlockSpec((None, n_pow, g, p), lambda l: (l, 0, 0, 0))
    return pl.pallas_call(
        _s5_prep_kernel,
        out_shape=(jax.ShapeDtypeStruct((DEPTH, n_pow, g, p), F32),
                   jax.ShapeDtypeStruct((DEPTH, n_pow, g, p), F32),
                   jax.ShapeDtypeStruct((DEPTH, SSM_GROUP, g, p), F32),
                   jax.ShapeDtypeStruct((DEPTH, SSM_GROUP, g, p), F32)),
        grid=(DEPTH,),
        in_specs=[mat, mat, pl.BlockSpec((None, g, 1), lambda l: (l, 0, 0)), stack_c, stack_c],
        out_specs=(stack_n, stack_n, stack_c, stack_c),
        compiler_params=_cparams(("arbitrary",)),
        name="s5_discretize",
    )(a_re, a_im, log_dt.reshape(DEPTH, g, 1), bt_re, bt_im)


def _norm_mod_chunk(x_ref, g_ref, sc_ref, sh_ref, h_scr, slot, chunk):
    gain = g_ref[...]
    per_chunk = x_ref.shape[0] // MOD_BLOCK
    for s in range(per_chunk):
        x = x_ref[s * MOD_BLOCK:(s + 1) * MOD_BLOCK, :]
        ms = jnp.mean(x * x, axis=-1, keepdims=True)
        y = x * lax.rsqrt(ms + EPS) * gain
        mod_row = pl.ds(chunk * per_chunk + s, 1)
        h = y * (1.0 + sc_ref[mod_row, :]) + sh_ref[mod_row, :]
        rows = pl.ds(pl.multiple_of((chunk * per_chunk + s) * MOD_BLOCK, MOD_BLOCK), MOD_BLOCK)
        h_scr[slot, rows, :] = h.astype(BF16)


def _norm_pipeline(l, n_tiles, tm, sc_idx, sh_idx, chunk_rows):
    n_chunks = tm // chunk_rows
    nsub = tm // MOD_BLOCK
    tile = lambda i: jnp.minimum(i, n_tiles - 1)
    specs = [
        pl.BlockSpec((chunk_rows, D_MODEL), lambda i, j: (tile(i) * n_chunks + jnp.minimum(j, n_chunks - 1), 0)),
        _layer_spec(l, (1, D_MODEL)),
        pl.BlockSpec((None, nsub, D_MODEL), lambda i, j: (l, tile(i), sc_idx)),
        pl.BlockSpec((None, nsub, D_MODEL), lambda i, j: (l, tile(i), sh_idx)),
    ]
    scratch = pltpu.VMEM((2, tm, D_MODEL), BF16)
    out_row = lambda i: jnp.maximum(i - 1, 0)
    out_col = lambda i, col: jnp.where(i == 0, 0, col)
    return specs, scratch, n_chunks, out_row, out_col


def _in_gate_kernel(x_ref, g_ref, sc_ref, sh_ref, wi_ref, wg_ref, b_ref, proj_ref, gate_ref, h_scr, *, n_in,
                    n_chunks):
    i, j = pl.program_id(0), pl.program_id(1)
    ready = lax.rem(i + 1, 2)

    filling = j < n_chunks

    def fill():
        _norm_mod_chunk(x_ref, g_ref, sc_ref, sh_ref, h_scr, lax.rem(i, 2), j)

    def gates():
        acc = jnp.dot(h_scr[ready], wg_ref[...], preferred_element_type=F32)
        gate_ref[...] = _sigmoid(acc + b_ref[...]).astype(gate_ref.dtype)

    pl.when(jnp.logical_and(i == 0, filling))(fill)

    @pl.when(jnp.logical_and(i > 0, j < n_in))
    def _():
        proj_ref[...] = jnp.dot(h_scr[ready], wi_ref[...], preferred_element_type=F32)
        fill()

    @pl.when(jnp.logical_and(i > 0, jnp.logical_and(j >= n_in, filling)))
    def _():
        gates()
        fill()

    pl.when(jnp.logical_and(i > 0, jnp.logical_not(filling)))(gates)


def _in_gate_call(x, l, gain, modx, w_in, w_gate, b_gate, *, tm=1024, tn_in=1280, tn_gate=1024):
    n_tiles = x.shape[0] // tm
    n_in, n_gate = w_in.shape[2] // tn_in, w_gate.shape[2] // tn_gate
    norm_specs, h_scratch, n_chunks, out_row, out_col = _norm_pipeline(l, n_tiles, tm, 1, 0, 4 * MOD_BLOCK)
    assert n_in <= n_chunks <= n_in + n_gate
    in_col = lambda j: jnp.minimum(j, n_in - 1)
    gate_col = lambda j: jnp.maximum(j - n_in, 0)
    return pl.pallas_call(
        functools.partial(_in_gate_kernel, n_in=n_in, n_chunks=n_chunks),
        out_shape=(jax.ShapeDtypeStruct((x.shape[0], w_in.shape[2]), F32),
                   jax.ShapeDtypeStruct((x.shape[0], w_gate.shape[2]), BF16)),
        grid=(n_tiles + 1, n_in + n_gate),
        in_specs=norm_specs + [
            pl.BlockSpec((None, D_MODEL, tn_in), lambda i, j: (0, 0, in_col(j))),
            pl.BlockSpec((None, D_MODEL, tn_gate), lambda i, j: (0, 0, gate_col(j))),
            pl.BlockSpec((None, 1, tn_gate), lambda i, j: (l, 0, gate_col(j))),
        ],
        out_specs=(pl.BlockSpec((tm, tn_in), lambda i, j: (out_row(i), out_col(i, in_col(j)))),
                   pl.BlockSpec((tm, tn_gate), lambda i, j: (out_row(i), out_col(i, gate_col(j))))),
        scratch_shapes=[h_scratch],
        compiler_params=_cparams(("arbitrary", "arbitrary")),
        name="in_proj_gates",
    )(x, gain.reshape(DEPTH, 1, D_MODEL), modx, modx, w_in, w_gate, b_gate.reshape(DEPTH, 1, w_gate.shape[2]))


def _ffn_up_kernel(x_ref, g_ref, sc_ref, sh_ref, wg_ref, wu_ref, o_ref, h_scr, *, n_chunks):
    i, j = pl.program_id(0), pl.program_id(1)

    filling = j < n_chunks

    def fill():
        _norm_mod_chunk(x_ref, g_ref, sc_ref, sh_ref, h_scr, lax.rem(i, 2), j)

    def swiglu():
        h = h_scr[lax.rem(i + 1, 2)]
        gate = jnp.dot(h, wg_ref[...], preferred_element_type=F32)
        up = jnp.dot(h, wu_ref[...], preferred_element_type=F32)
        o_ref[...] = (gate * _sigmoid(gate) * up).astype(o_ref.dtype)

    pl.when(jnp.logical_and(i == 0, filling))(fill)

    @pl.when(jnp.logical_and(i > 0, filling))
    def _():
        swiglu()
        fill()

    pl.when(jnp.logical_and(i > 0, jnp.logical_not(filling)))(swiglu)


def _ffn_up_call(x, l, gain, modx, w_gate, w_up, *, tm=1024, tn=512):
    n_tiles = x.shape[0] // tm
    n_out = w_gate.shape[2]
    norm_specs, h_scratch, n_chunks, out_row, out_col = _norm_pipeline(l, n_tiles, tm, 4, 3, 2 * MOD_BLOCK)
    assert n_out // tn >= n_chunks
    w_spec = pl.BlockSpec((None, D_MODEL, tn), lambda i, j: (0, 0, j))
    return pl.pallas_call(
        functools.partial(_ffn_up_kernel, n_chunks=n_chunks),
        out_shape=jax.ShapeDtypeStruct((x.shape[0], n_out), BF16),
        grid=(n_tiles + 1, n_out // tn),
        in_specs=norm_specs + [w_spec, w_spec],
        out_specs=pl.BlockSpec((tm, tn), lambda i, j: (out_row(i), out_col(i, j))),
        scratch_shapes=[h_scratch],
        compiler_params=_cparams(("arbitrary", "arbitrary")),
        name="ffn_up",
    )(x, gain.reshape(DEPTH, 1, D_MODEL), modx, modx, w_gate, w_up)


def _mix_kernel(sp_ref, ss_ref, a_ref, ga_ref, gb_ref, ws_ref, wa_ref, o_ref, *, n_split):
    def body(s_ref):
        ps = jnp.dot(s_ref[...], ws_ref[...], preferred_element_type=F32)
        pa = jnp.dot(a_ref[...], wa_ref[...], preferred_element_type=F32)
        o_ref[...] = (ga_ref[...] * ps + gb_ref[...] * pa).astype(o_ref.dtype)

    i = pl.program_id(0)
    pl.when(i < n_split)(lambda: body(sp_ref))
    pl.when(i >= n_split)(lambda: body(ss_ref))


def _mix_call(ssm_p, ssm_s, attn_out, gates, w_ps, w_pa, *, tm=1024, tn=512):
    n_tok = attn_out.shape[0]
    nj = D_MODEL // tn
    n_split = ssm_p.shape[0] // tm
    return pl.pallas_call(
        functools.partial(_mix_kernel, n_split=n_split),
        out_shape=jax.ShapeDtypeStruct((n_tok, D_MODEL), BF16),
        grid=(n_tok // tm, nj),
        in_specs=[
            pl.BlockSpec((tm, D_SSM), lambda i, j: (jnp.minimum(i, n_split - 1), 0)),
            pl.BlockSpec((tm, D_SSM), lambda i, j: (jnp.maximum(i - n_split, 0), 0)),
            pl.BlockSpec((tm, D_ATTN), lambda i, j: (i, 0)),
            pl.BlockSpec((tm, tn), lambda i, j: (i, j)),
            pl.BlockSpec((tm, tn), lambda i, j: (i, j + nj)),
            pl.BlockSpec((None, D_SSM, tn), lambda i, j: (0, 0, j)),
            pl.BlockSpec((None, D_ATTN, tn), lambda i, j: (0, 0, j)),
        ],
        out_specs=pl.BlockSpec((tm, tn), lambda i, j: (i, j)),
        compiler_params=_cparams(("arbitrary", "arbitrary")),
        name="branch_merge",
    )(ssm_p, ssm_s, attn_out, gates, gates, w_ps, w_pa)


def _resid_mm_kernel(a_ref, w_ref, x_ref, g_ref, *o_refs, nsub, n_split):
    acc = jnp.dot(a_ref[...], w_ref[...], preferred_element_type=F32)

    def write(o_ref):
        for s in range(nsub):
            rows = slice(s * MOD_BLOCK, (s + 1) * MOD_BLOCK)
            o_ref[rows, :] = x_ref[rows, :] + g_ref[s:s + 1, :] * acc[rows, :]

    if n_split is None:
        write(o_refs[0])
    else:
        i = pl.program_id(0)
        pl.when(i < n_split)(lambda: write(o_refs[0]))
        pl.when(i >= n_split)(lambda: write(o_refs[1]))


def _resid_mm_call(a, l, w, x, modx, g_idx, *, n_split=None, tm=1024, tn=512, name):
    n_tok, k = a.shape
    nsub = tm // MOD_BLOCK
    nj = D_MODEL // tn
    n_tiles = n_tok // tm
    if n_split is None:
        out_shape = jax.ShapeDtypeStruct((n_tok, D_MODEL), F32)
        out_specs = pl.BlockSpec((tm, tn), lambda i, j: (i, j))
    else:
        out_shape = (jax.ShapeDtypeStruct((n_split * tm, D_MODEL), F32),
                     jax.ShapeDtypeStruct(((n_tiles - n_split) * tm, D_MODEL), F32))
        out_specs = (
            pl.BlockSpec((tm, tn), lambda i, j: (jnp.minimum(i, n_split - 1), jnp.where(i < n_split, j, nj - 1))),
            pl.BlockSpec((tm, tn), lambda i, j: (jnp.maximum(i - n_split, 0), jnp.where(i < n_split, 0, j))),
        )
    return pl.pallas_call(
        functools.partial(_resid_mm_kernel, nsub=nsub, n_split=n_split),
        out_shape=out_shape,
        grid=(n_tiles, nj),
        in_specs=[
            pl.BlockSpec((tm, k), lambda i, j: (i, 0)),
            pl.BlockSpec((None, k, tn), lambda i, j: (0, 0, j)),
            pl.BlockSpec((tm, tn), lambda i, j: (i, j)),
            pl.BlockSpec((None, nsub, tn), lambda i, j: (l, i, g_idx * nj + j)),
        ],
        out_specs=out_specs,
        compiler_params=_cparams(("arbitrary", "arbitrary")),
        name=name,
    )(a, w, x, modx)


SCAN_LW = 512


def _ssm_kernel(*refs, tt, n_cast, sps):
    (u_ref, h0r_ref, h0i_ref, wbr_ref, wbi_ref, pwj_ref, seg_ref, wcr_ref, wci_ref, d_ref, wglu_ref,
     bglu_ref) = refs[:12]
    cast_in = refs[12:12 + n_cast]
    o_ref, htr_ref, hti_ref = refs[12 + n_cast:15 + n_cast]
    cast_out = refs[15 + n_cast:15 + 2 * n_cast]
    bur, bui, cre, cim, z_scr = refs[15 + 2 * n_cast:]
    steps = tt // S5_SEG
    for w_ref, wb_ref in zip(cast_in, cast_out):
        wb_ref[...] = w_ref[...].astype(BF16)

    carried = sps == S5_SEG
    if carried:
        @pl.when(pl.program_id(1) == 0)
        def _():
            cre[...] = jnp.broadcast_to(h0r_ref[...], (SUBLANES, STATE_W))
            cim[...] = jnp.broadcast_to(h0i_ref[...], (SUBLANES, STATE_W))

    r_idx = lax.broadcasted_iota(jnp.int32, (tt, tt), 0)
    c_idx = lax.broadcasted_iota(jnp.int32, (tt, tt), 1)
    to_scan = jnp.where(c_idx == (r_idx % S5_SEG) * steps + r_idx // S5_SEG, 1.0, 0.0).astype(BF16)
    to_time = jnp.where(r_idx == (c_idx % S5_SEG) * steps + c_idx // S5_SEG, 1.0, 0.0).astype(BF16)

    u = u_ref[...]
    us = jnp.dot(to_scan, u.astype(BF16), preferred_element_type=F32).astype(BF16)
    kw = D_SSM // SSM_KC
    sw = STATE_W // SSM_KC
    for kc in range(SSM_KC):
        uk = us[:, kc * kw:(kc + 1) * kw]
        bur[:, kc * sw:(kc + 1) * sw] = jnp.dot(uk, wbr_ref[kc], preferred_element_type=F32)
        bui[:, kc * sw:(kc + 1) * sw] = jnp.dot(uk, wbi_ref[kc], preferred_element_type=F32)

    row = lax.broadcasted_iota(jnp.int32, (SUBLANES, SCAN_LW), 0)
    for lc in range(STATE_W // SCAN_LW):
        sl = slice(lc * SCAN_LW, (lc + 1) * SCAN_LW)
        lam_re = jnp.broadcast_to(pwj_ref[0, 0:1, sl], (SUBLANES, SCAN_LW))
        lam_im = jnp.broadcast_to(pwj_ref[1, 0:1, sl], (SUBLANES, SCAN_LW))

        def step(j, carry, sl=sl, lam_re=lam_re, lam_im=lam_im):
            h_re, h_im = carry
            rows = pl.ds(pl.multiple_of(j * SUBLANES, SUBLANES), SUBLANES)
            return (lam_re * h_re - lam_im * h_im + bur[rows, sl],
                    lam_re * h_im + lam_im * h_re + bui[rows, sl])

        zero = jnp.zeros((SUBLANES, SCAN_LW), F32)
        e_re, e_im = lax.fori_loop(0, steps, step, (zero, zero), unroll=4)

        first = row % sps == 0
        in_re, in_im = (cre, cim) if carried else (h0r_ref, h0i_ref)
        x_re = jnp.where(first, in_re[:, sl], pltpu.roll(e_re, 1, 0))
        x_im = jnp.where(first, in_im[:, sl], pltpu.roll(e_im, 1, 0))
        for idx, k in enumerate((1, 2, 4)):
            if k >= sps:
                continue
            m_re = seg_ref[2 * idx, :, sl]
            m_im = seg_ref[2 * idx + 1, :, sl]
            s_re = pltpu.roll(x_re, k, 0)
            s_im = pltpu.roll(x_im, k, 0)
            x_re, x_im = (x_re + m_re * s_re - m_im * s_im,
                          x_im + m_re * s_im + m_im * s_re)
        pj_re = jnp.broadcast_to(pwj_ref[0, 1:2, sl], (SUBLANES, SCAN_LW))
        pj_im = jnp.broadcast_to(pwj_ref[1, 1:2, sl], (SUBLANES, SCAN_LW))
        n_re = pj_re * x_re - pj_im * x_im + e_re
        n_im = pj_re * x_im + pj_im * x_re + e_im
        if carried:
            cre[:, sl] = jnp.broadcast_to(n_re[SUBLANES - 1:SUBLANES, :], (SUBLANES, SCAN_LW))
            cim[:, sl] = jnp.broadcast_to(n_im[SUBLANES - 1:SUBLANES, :], (SUBLANES, SCAN_LW))
        else:
            htr_ref[:, sl] = n_re
            hti_ref[:, sl] = n_im

        def scan(j, carry, sl=sl, step=step):
            h_re, h_im = step(j, carry)
            rows = pl.ds(pl.multiple_of(j * SUBLANES, SUBLANES), SUBLANES)
            bur[rows, sl] = h_re
            bui[rows, sl] = h_im
            return h_re, h_im

        lax.fori_loop(0, steps, scan, (x_re, x_im), unroll=2)

    if carried:
        htr_ref[...] = cre[0:1, :]
        hti_ref[...] = cim[0:1, :]

    d = d_ref[...]
    for kc in range(SSM_KC):
        h_re = bur[:, kc * sw:(kc + 1) * sw].astype(BF16)
        h_im = bui[:, kc * sw:(kc + 1) * sw].astype(BF16)
        ys = (jnp.dot(h_re, wcr_ref[kc], preferred_element_type=F32)
              - jnp.dot(h_im, wci_ref[kc], preferred_element_type=F32))
        ys_hi = ys.astype(BF16)
        ys_lo = (ys - ys_hi.astype(F32)).astype(BF16)
        y = (jnp.dot(to_time, ys_hi, preferred_element_type=F32)
             + jnp.dot(to_time, ys_lo, preferred_element_type=F32))
        cols = slice(kc * kw, (kc + 1) * kw)
        y = y + d[:, cols] * u[:, cols]
        z_scr[:, cols] = 0.5 * y * (1.0 + jnp.tanh(math.sqrt(2.0 / math.pi) * (y + 0.044715 * (y * y * y))))
    z = z_scr[...]
    gate = _sigmoid(jnp.dot(z.astype(BF16), wglu_ref[...], preferred_element_type=F32) + bglu_ref[...])
    o_ref[...] = (z * gate).astype(o_ref.dtype)


def _ssm_call(proj, row0, n_seq, nt, h0_re, h0_im, l, wb_re, wb_im, pw_re, pw_im, wc_re, wc_im, d_skip,
              w_glu, b_glu, *, sps, cast=(), name):
    tt = S5_TILE
    steps = tt // S5_SEG
    power = lambda n: S5_POWERS.index(n)
    pwj = jnp.stack([pw[:, (power(1), power(steps)), :] for pw in (pw_re, pw_im)], axis=1)
    seg_rows = []
    for k in (1, 2, 4):
        keep = (jnp.arange(SUBLANES) % sps >= k)[None, :, None]
        seg_rows += [jnp.where(keep, pw[:, power(k * steps)][:, None, :], 0.0) for pw in (pw_re, pw_im)]
    seg = jnp.stack(seg_rows, axis=1)
    state_rows = h0_re.shape[1]
    assert state_rows == (1 if sps == S5_SEG else S5_SEG) and (sps == S5_SEG or nt == 1)
    t_len = nt * tt
    rb0 = row0 // tt
    n_steps = n_seq * nt
    state_spec = pl.BlockSpec((None, state_rows, STATE_W), lambda b, t: (b, 0, 0))
    cast_in, cast_out, cast_shapes = [], [], []
    for w, wl in cast:
        _, k, n = w.shape
        rows = k // n_steps
        cast_in.append(pl.BlockSpec((None, rows, n), lambda b, t, wl=wl: (wl, b * nt + t, 0)))
        cast_out.append(pl.BlockSpec((None, rows, n), lambda b, t: (0, b * nt + t, 0)))
        cast_shapes.append(jax.ShapeDtypeStruct((1, k, n), BF16))
    resident = lambda shape: pl.BlockSpec((None, *shape), lambda b, t: (l, *(0,) * len(shape)),
                                          pipeline_mode=pl.Buffered(1))
    return pl.pallas_call(
        functools.partial(_ssm_kernel, tt=tt, n_cast=len(cast), sps=sps),
        out_shape=(jax.ShapeDtypeStruct((n_seq * t_len, D_SSM), BF16),
                   jax.ShapeDtypeStruct((n_seq, state_rows, STATE_W), F32),
                   jax.ShapeDtypeStruct((n_seq, state_rows, STATE_W), F32),
                   *cast_shapes),
        grid=(n_seq, nt),
        in_specs=[
            pl.BlockSpec((tt, D_SSM), lambda b, t: (rb0 + b * nt + t, 0)),
            state_spec, state_spec,
            resident(wb_re.shape[1:]), resident(wb_im.shape[1:]),
            _layer_spec(l, pwj.shape[1:]), _layer_spec(l, seg.shape[1:]),
            resident(wc_re.shape[1:]), resident(wc_im.shape[1:]),
            _layer_spec(l, (1, D_SSM)),
            resident((D_SSM, D_SSM)),
            _layer_spec(l, (1, D_SSM)),
            *cast_in,
        ],
        out_specs=(pl.BlockSpec((tt, D_SSM), lambda b, t: (b * nt + t, 0)), state_spec, state_spec, *cast_out),
        scratch_shapes=[pltpu.VMEM((tt, STATE_W), F32), pltpu.VMEM((tt, STATE_W), F32),
                        pltpu.VMEM((SUBLANES, STATE_W), F32), pltpu.VMEM((SUBLANES, STATE_W), F32),
                        pltpu.VMEM((tt, D_SSM), F32)],
        compiler_params=_cparams(("arbitrary", "arbitrary")),
        name=name,
    )(proj, h0_re, h0_im, wb_re, wb_im, pwj, seg, wc_re, wc_im, d_skip.reshape(DEPTH, 1, D_SSM), w_glu,
      b_glu.reshape(DEPTH, 1, D_SSM), *[w for w, _ in cast])


RING = 3
N_KEYS = RING * CHUNK
ATT_CPS = 4


def _stack_blocks(x):
    return jnp.concatenate([x[:, j * LANES:(j + 1) * LANES] for j in range(x.shape[1] // LANES)], axis=0)


def _pair_norm_rope(xs, gain, tab, ones_bd):
    reps = xs.shape[0] // CHUNK
    cos, s_lo, s_hi = (jnp.concatenate([tab[i]] * reps, axis=0) for i in range(3))
    half = ROPE_DIM // 2
    sq = xs * xs
    sq_hi = sq.astype(BF16)
    sq_lo = (sq - sq_hi.astype(F32)).astype(BF16)
    ss = (jnp.dot(sq_hi, ones_bd, preferred_element_type=F32)
          + jnp.dot(sq_lo, ones_bd, preferred_element_type=F32))
    xg = xs * gain
    xr = xg * cos + pltpu.roll(xg, LANES - half, 1) * s_lo + pltpu.roll(xg, half, 1) * s_hi
    return xr * lax.rsqrt(ss * (1.0 / HEAD_DIM) + EPS)


def _attn_kernel(sink_ref, q_ref, k_ref, v_ref, ck_ref, cv_ref, tab_ref, gq_ref, gk_ref, o_ref, kn_ref, kd, vd,
                 *, l, n_prompt_steps, steps_per_seq):
    step = pl.program_id(0)
    is_sample = step >= n_prompt_steps
    c_base = lax.rem(step, steps_per_seq) * ATT_CPS
    first = lax.broadcasted_iota(jnp.int32, (1, LANES), 1) < HEAD_DIM
    ones_bd = jnp.where(lax.broadcasted_iota(jnp.int32, (LANES, LANES), 0) // HEAD_DIM
                        == lax.broadcasted_iota(jnp.int32, (LANES, LANES), 1) // HEAD_DIM, 1.0, 0.0).astype(BF16)
    key_slot = lax.broadcasted_iota(jnp.int32, (1, N_KEYS), 1) // CHUNK
    gq = gq_ref[...]
    gk = gk_ref[...]

    def store_dup(dst, x, rows):
        for pair in range(D_KV // LANES):
            blk = x[:, pair * LANES:(pair + 1) * LANES]
            swapped = pltpu.roll(blk, HEAD_DIM, 1)
            dst[2 * pair, rows, 0:LANES] = jnp.where(first, blk, swapped).astype(BF16)
            dst[2 * pair + 1, rows, 0:LANES] = jnp.where(first, swapped, blk).astype(BF16)

    @pl.when(step == 0)
    def _():
        vd[:, :, LANES:] = jnp.ones((N_KV_HEADS, N_KEYS, LANES), BF16)

    prev = slice(CHUNK, N_KEYS)

    def chunk(ci, from_cache):
        c = 0 if from_cache else c_base + ci
        rows = pl.ds(pl.multiple_of(ci * CHUNK, CHUNK), CHUNK)
        if from_cache:
            cached = pl.ds(pl.multiple_of(ci * WINDOW, WINDOW), WINDOW)
            store_dup(kd, ck_ref[cached, :], prev)
            store_dup(vd, cv_ref[cached, :], prev)

        tab = tab_ref[:, rows, :]
        own = (slice(0, CHUNK) if from_cache else
               pl.ds(pl.multiple_of(lax.rem(c, RING) * CHUNK, CHUNK), CHUNK))
        kn = _pair_norm_rope(_stack_blocks(k_ref[rows, :]), gk, tab, ones_bd)
        kn = jnp.concatenate([kn[0:CHUNK], kn[CHUNK:2 * CHUNK]], axis=1)
        kn_ref[rows, :] = kn
        store_dup(kd, kn, own)
        store_dup(vd, v_ref[rows, :], own)

        qn = _pair_norm_rope(_stack_blocks(q_ref[rows, :]), gq, tab, ones_bd) * (HEAD_DIM ** -0.5)
        q_lo = jnp.where(first, qn, 0.0).astype(BF16)
        q_hi = jnp.where(first, 0.0, qn).astype(BF16)

        valid = None if from_cache else key_slot <= c
        scores = []
        for kh in range(N_KV_HEADS):
            r0 = kh * 2 * CHUNK
            lhs = jnp.concatenate([q_lo[r0:r0 + CHUNK], q_hi[r0:r0 + CHUNK],
                                   q_lo[r0 + CHUNK:r0 + 2 * CHUNK], q_hi[r0 + CHUNK:r0 + 2 * CHUNK]], axis=0)
            scores.append(lax.dot_general(lhs, kd[kh], (((1,), (1,)), ((), ())),
                                          preferred_element_type=F32))
        weighted, sink_terms = [], []
        for kh in range(N_KV_HEADS):
            s = scores[kh] if from_cache else jnp.where(valid, scores[kh], -1e30)
            sink = jnp.concatenate([jnp.full((CHUNK, 1), sink_ref[l, kh * KV_REP + r], F32)
                                    for r in range(KV_REP)], axis=0)
            m = jnp.maximum(jnp.max(s, axis=-1, keepdims=True), sink)
            p = jnp.exp(s - m).astype(BF16)
            sink_terms.append(jnp.exp(sink - m))
            weighted.append(jnp.dot(p, vd[kh], preferred_element_type=F32))
        for kh in range(N_KV_HEADS):
            od = weighted[kh]
            o = od[:, 0:LANES] / (od[:, LANES:] + sink_terms[kh])
            for pair in range(2):
                half = pair * 2 * CHUNK
                blk = jnp.where(first, o[half:half + CHUNK], o[half + CHUNK:half + 2 * CHUNK])
                col = (2 * kh + pair) * LANES
                o_ref[rows, col:col + LANES] = blk.astype(o_ref.dtype)

    @pl.when(jnp.logical_not(is_sample))
    def _():
        @pl.when(c_base == 0)
        def _():
            kd[:, prev, :] = jnp.zeros((N_KV_HEADS, N_KEYS - CHUNK, LANES), BF16)
            vd[:, prev, 0:LANES] = jnp.zeros((N_KV_HEADS, N_KEYS - CHUNK, LANES), BF16)

        lax.fori_loop(0, ATT_CPS, lambda ci, carry: (chunk(ci, False), carry)[1], 0, unroll=2)

    @pl.when(is_sample)
    def _():
        lax.fori_loop(0, ATT_CPS, lambda ci, carry: (chunk(ci, True), carry)[1], 0)


def _attn_call(proj, l, n_prompt, t_prompt, sink, rope_tab, gq, gk, cache_k, cache_v):
    n_tok = proj.shape[0]
    tr = ATT_CPS * CHUNK
    qcol = D_SSM // D_ATTN
    kcol = (D_SSM + D_ATTN) // D_KV
    n_prompt_steps = n_prompt // tr
    cache_spec = pl.BlockSpec((None, ATT_CPS * WINDOW, D_KV),
                              lambda s: (l, jnp.maximum(s - n_prompt_steps, 0), 0))
    return pl.pallas_call(
        functools.partial(_attn_kernel, l=l, n_prompt_steps=n_prompt_steps, steps_per_seq=t_prompt // tr),
        out_shape=(jax.ShapeDtypeStruct((n_tok, D_ATTN), BF16),
                   jax.ShapeDtypeStruct((n_tok, D_KV), F32)),
        grid=(n_tok // tr,),
        in_specs=[
            pl.BlockSpec(memory_space=pltpu.SMEM),
            pl.BlockSpec((tr, D_ATTN), lambda s: (s, qcol)),
            pl.BlockSpec((tr, D_KV), lambda s: (s, kcol)),
            pl.BlockSpec((tr, D_KV), lambda s: (s, kcol + 1)),
            cache_spec, cache_spec,
            pl.BlockSpec((3, tr, LANES), lambda s: (0, s, 0)),
            _layer_spec(l, (1, LANES)), _layer_spec(l, (1, LANES)),
        ],
        out_specs=(pl.BlockSpec((tr, D_ATTN), lambda s: (s, 0)),
                   pl.BlockSpec((tr, D_KV), lambda s: (s, 0))),
        scratch_shapes=[pltpu.VMEM((N_KV_HEADS, N_KEYS, LANES), BF16),
                        pltpu.VMEM((N_KV_HEADS, N_KEYS, 2 * LANES), BF16)],
        compiler_params=_cparams(("arbitrary",)),
        name="banded_attn",
    )(sink, proj, proj, proj, cache_k, cache_v, rope_tab, gq, gk)


def _block_diag_rows(x, n):
    *lead, r, width = x.shape
    keep = jnp.arange(n)[:, None, None] == (jnp.arange(width) // (width // n))[None, None, :]
    return jnp.where(keep, x[..., None, :, :], 0.0).reshape(*lead, n * r, width)


def _rope_table(pos):
    half = ROPE_DIM // 2
    n_pos = pos.shape[0]
    inv_freq = ROPE_THETA ** (-jnp.arange(half, dtype=F32) / half)
    ang = pos.astype(F32)[:, None] * inv_freq[None, :]
    cos, sin = jnp.cos(ang), jnp.sin(ang)
    ones = jnp.ones((n_pos, HEAD_DIM - ROPE_DIM), F32)
    zeros = jnp.zeros((n_pos, HEAD_DIM - half), F32)
    c_tab = jnp.concatenate([cos, cos, ones], axis=1)
    lo_tab = jnp.concatenate([-sin, zeros], axis=1)
    hi_tab = jnp.concatenate([jnp.zeros((n_pos, half), F32), sin, jnp.zeros((n_pos, HEAD_DIM - ROPE_DIM), F32)],
                             axis=1)
    tab = jnp.stack([c_tab, lo_tab, hi_tab])
    return jnp.concatenate([tab, tab], axis=2)


def kernel(x_prompt, x_sample, cache_k, cache_v, state_ssm_re, state_ssm_im, c_prompt, c_sample, w_mod, b_mod, norm1_g, norm2_g, w_in, ssm_a_re, ssm_a_im, ssm_log_dt, ssm_b_re, ssm_b_im, ssm_c_re, ssm_c_im, ssm_d, w_glu, b_glu, q_norm_g, k_norm_g, attn_sink, w_gate, b_gate, w_proj_ssm, w_proj_attn, w_out, w_ffn_gate, w_ffn_up, w_ffn_down):
    bp, tp, _ = x_prompt.shape
    bs, ts, _ = x_sample.shape
    assert ts == CHUNK and tp % (ATT_CPS * CHUNK) == 0 and bs % ATT_CPS == 0
    assert tp % S5_TILE == 0 and S5_TILE % ts == 0 and (bs * ts) % S5_TILE == 0
    n_p, n_s = bp * tp, bs * ts
    tm = 1024
    x = jnp.concatenate([x_prompt.reshape(n_p, D_MODEL), x_sample.reshape(n_s, D_MODEL)], axis=0)

    n_cond = bp + bs
    pad = (-n_cond) % SUBLANES
    c_all = jnp.concatenate([c_prompt, c_sample, jnp.zeros((pad, D_MODEL), F32)], axis=0)
    mod = _mod_call(c_all, w_mod, b_mod)
    n_mod = mod.shape[2]

    def per_block(m, n_seq, t_len):
        reps = t_len // MOD_BLOCK
        return jnp.broadcast_to(m[:, :, None, :], (DEPTH, n_seq, reps, n_mod)).reshape(DEPTH, n_seq * reps, n_mod)

    modx = jnp.concatenate([per_block(mod[:, :bp], bp, tp), per_block(mod[:, bp:n_cond], bs, ts)],
                           axis=1)

    g, p = N_SSM_GROUPS, SSM_STATE
    gpc = g // SSM_KC
    pw_re, pw_im, bb_re, bb_im = _s5_prep_call(ssm_a_re, ssm_a_im, ssm_log_dt,
                                               ssm_b_re.transpose(0, 3, 1, 2), ssm_b_im.transpose(0, 3, 1, 2))
    pw_re = pw_re.reshape(DEPTH, len(S5_POWERS), STATE_W)
    pw_im = pw_im.reshape(DEPTH, len(S5_POWERS), STATE_W)

    def wb_blocks(bb):
        per_chunk = bb.reshape(DEPTH, SSM_GROUP, SSM_KC, gpc * p).transpose(0, 2, 1, 3)
        return _block_diag_rows(per_chunk, gpc).astype(BF16)

    def wc_blocks(cc):
        per_chunk = cc.reshape(DEPTH, SSM_KC, gpc, SSM_GROUP, p).transpose(0, 1, 3, 2, 4)
        transposed = _block_diag_rows(per_chunk.reshape(DEPTH, SSM_KC, SSM_GROUP, gpc * p), gpc)
        return jnp.swapaxes(transposed.astype(BF16), -1, -2)

    ssm_w = (wb_blocks(bb_re), wb_blocks(bb_im), pw_re, pw_im, wc_blocks(ssm_c_re), wc_blocks(ssm_c_im), ssm_d,
             w_glu.astype(BF16), b_glu)
    zeros_state = jnp.zeros((bp, 1, STATE_W), F32)
    sps = ts * S5_SEG // S5_TILE
    n_sample_tiles = n_s // S5_TILE

    def per_segment(state):
        rows = jnp.repeat(state.reshape(DEPTH, bs, STATE_W), sps, axis=1)
        return rows.reshape(DEPTH, n_sample_tiles, S5_SEG, STATE_W)

    def last_segment(state):
        return state.reshape(bs, sps, STATE_W)[:, sps - 1].reshape(bs, g, p)

    h0_re, h0_im = per_segment(state_ssm_re), per_segment(state_ssm_im)

    rope_tab = jnp.concatenate([jnp.tile(_rope_table(jnp.arange(tp)), (1, bp, 1)),
                                jnp.tile(_rope_table(PAST_LEN + jnp.arange(ts)), (1, bs, 1))], axis=1)
    gq = jnp.tile(q_norm_g, (1, LANES // HEAD_DIM)).reshape(DEPTH, 1, LANES)
    gk = jnp.tile(k_norm_g, (1, LANES // HEAD_DIM)).reshape(DEPTH, 1, LANES)
    cache_k2 = cache_k.reshape(DEPTH, bs * WINDOW, D_KV)
    cache_v2 = cache_v.reshape(DEPTH, bs * WINDOW, D_KV)

    w_in_b, w_gate_b = w_in[:1].astype(BF16), w_gate[:1].astype(BF16)
    v0 = D_SSM + D_ATTN + D_KV

    def heads(t):
        return t.reshape(*t.shape[:-1], N_KV_HEADS, HEAD_DIM)

    def last_window(t, col0, col1):
        return jnp.stack([t[(b + 1) * tp - WINDOW:(b + 1) * tp, col0:col1] for b in range(bp)])

    outs = {k: [] for k in ("pk", "pv", "pre", "pim", "sk", "sv", "sre", "sim")}
    for l in range(DEPTH):
        proj, gates = _in_gate_call(x, l, norm1_g, modx, w_in_b, w_gate_b, b_gate, tm=tm)

        nxt = min(l + 1, DEPTH - 1)
        cast = [(w, l) for w in (w_proj_ssm, w_proj_attn, w_out, w_ffn_gate, w_ffn_up, w_ffn_down)]
        cast += [(w_in, nxt), (w_gate, nxt)]
        ssm_p, pre, pim, w_ps_b, w_pa_b, w_out_b, w_fg_b, w_fu_b, w_fd_b, w_in_b, w_gate_b = _ssm_call(
            proj, 0, bp, tp // S5_TILE, zeros_state, zeros_state, l, *ssm_w, sps=S5_SEG, cast=cast, name="s5_prompt")
        ssm_s, sre, sim = _ssm_call(proj, n_p, n_sample_tiles, 1, h0_re[l], h0_im[l], l, *ssm_w, sps=sps,
                                    name="s5_sample")
        attn_out, kn = _attn_call(proj, l, n_p, tp, attn_sink, rope_tab, gq, gk, cache_k2, cache_v2)

        mixed = _mix_call(ssm_p, ssm_s, attn_out, gates, w_ps_b, w_pa_b, tm=tm, tn=1024)
        x = _resid_mm_call(mixed, l, w_out_b, x, modx, 2, tm=tm, tn=1024, name="out_proj")

        act = _ffn_up_call(x, l, norm2_g, modx, w_fg_b, w_fu_b, tm=tm)
        last = l == DEPTH - 1
        x = _resid_mm_call(act, l, w_fd_b, x, modx, 5, n_split=n_p // tm if last else None, tm=tm,
                           name="ffn_down_split" if last else "ffn_down")

        outs["pk"].append(heads(last_window(kn, 0, D_KV)))
        outs["pv"].append(heads(last_window(proj, v0, IN_WIDTH)))
        outs["pre"].append(pre.reshape(bp, g, p))
        outs["pim"].append(pim.reshape(bp, g, p))
        outs["sk"].append(jnp.concatenate([cache_k[l][:, ts:], heads(kn[n_p:].reshape(bs, ts, D_KV))], axis=1))
        outs["sv"].append(jnp.concatenate([cache_v[l][:, ts:], heads(proj[n_p:, v0:].reshape(bs, ts, D_KV))],
                                          axis=1))
        outs["sre"].append(last_segment(sre))
        outs["sim"].append(last_segment(sim))

    y_p, y_s = x
    return (y_p.reshape(bp, tp, D_MODEL), y_s.reshape(bs, ts, D_MODEL),
            jnp.stack(outs["pk"]), jnp.stack(outs["pv"]), jnp.stack(outs["pre"]), jnp.stack(outs["pim"]),
            jnp.stack(outs["sk"]), jnp.stack(outs["sv"]), jnp.stack(outs["sre"]), jnp.stack(outs["sim"]))
```

```python
import functools
import math

import jax
import jax.numpy as jnp
from jax import lax
from jax.experimental import pallas as pl
from jax.experimental.pallas import tpu as pltpu

D_MODEL = 2048
DEPTH = 4
CHUNK = 64
D_SSM = 1024
SSM_GROUP = 16
N_SSM_GROUPS = 64
SSM_STATE = 64
HEAD_DIM = 64
N_HEADS = 16
N_KV_HEADS = 4
KV_REP = N_HEADS // N_KV_HEADS
D_ATTN = N_HEADS * HEAD_DIM
D_KV = N_KV_HEADS * HEAD_DIM
IN_WIDTH = D_SSM + D_ATTN + 2 * D_KV
WINDOW = 128
ROPE_DIM = 16
ROPE_THETA = 500000.0
D_FF = 5632
EPS = 1e-6
PAST_LEN = 2048

LANES = 128
SUBLANES = 8
MOD_BLOCK = CHUNK
STATE_W = N_SSM_GROUPS * SSM_STATE
SSM_KC = 4
VMEM_LIMIT = 56 * 1024 * 1024

F32 = jnp.float32
BF16 = jnp.bfloat16


def _cparams(sem):
    return pltpu.CompilerParams(dimension_semantics=sem, vmem_limit_bytes=VMEM_LIMIT)


def _sigmoid(x):
    return 0.5 + 0.5 * jnp.tanh(0.5 * x)


def _layer_spec(l, shape):
    zeros = (0,) * len(shape)
    return pl.BlockSpec((None, *shape), lambda *_: (l, *zeros))


def _mod_kernel(c_ref, w_ref, b_ref, o_ref):
    c = c_ref[...].astype(BF16)
    w = w_ref[...].astype(BF16)
    o_ref[...] = jnp.dot(c, w, preferred_element_type=F32) + b_ref[...]


def _mod_call(c_all, w_mod, b_mod):
    nb = c_all.shape[0]
    tn = 1024
    n_out = w_mod.shape[2]
    return pl.pallas_call(
        _mod_kernel,
        out_shape=jax.ShapeDtypeStruct((DEPTH, nb, n_out), F32),
        grid=(DEPTH, n_out // tn),
        in_specs=[
            pl.BlockSpec((nb, D_MODEL), lambda l, j: (0, 0)),
            pl.BlockSpec((None, D_MODEL, tn), lambda l, j: (l, 0, j)),
            pl.BlockSpec((None, 1, tn), lambda l, j: (l, 0, j)),
        ],
        out_specs=pl.BlockSpec((None, nb, tn), lambda l, j: (l, 0, j)),
        compiler_params=_cparams(("arbitrary", "arbitrary")),
        name="adaln_mod",
    )(c_all, w_mod, b_mod.reshape(DEPTH, 1, n_out))


S5_TILE = 256
S5_SEG = SUBLANES
S5_POWERS = (1, *(k * S5_TILE // S5_SEG for k in (1, 2, 4)))


def _s5_prep_kernel(are_ref, aim_ref, ldt_ref, bre_ref, bim_ref, pwr_ref, pwi_ref, bbr_ref, bbi_ref):
    a_re = are_ref[...]
    a_im = aim_ref[...]
    dt = jnp.exp(ldt_ref[...])
    z_re = a_re * dt
    z_im = a_im * dt
    for i, n in enumerate(S5_POWERS):
        mag = jnp.exp(z_re * float(n))
        pwr_ref[i] = mag * jnp.cos(z_im * float(n))
        pwi_ref[i] = mag * jnp.sin(z_im * float(n))
    l_re = pwr_ref[S5_POWERS.index(1)]
    l_im = pwi_ref[S5_POWERS.index(1)]
    den = a_re * a_re + a_im * a_im
    n_re = l_re - 1.0
    f_re = (n_re * a_re + l_im * a_im) / den
    f_im = (l_im * a_re - n_re * a_im) / den
    for c in range(SSM_GROUP):
        b_re = bre_ref[c]
        b_im = bim_ref[c]
        bbr_ref[c] = f_re * b_re - f_im * b_im
        bbi_ref[c] = f_re * b_im + f_im * b_re


def _s5_prep_call(a_re, a_im, log_dt, bt_re, bt_im):
    g, p = N_SSM_GROUPS, SSM_STATE
    mat = pl.BlockSpec((None, g, p), lambda l: (l, 0, 0))
    stack_c = pl.BlockSpec((None, SSM_GROUP, g, p), lambda l: (l, 0, 0, 0))
    n_pow = len(S5_POWERS)
    stack_n = pl.BlockSpec((None, n_pow, g, p), lambda l: (l, 0, 0, 0))
    return pl.pallas_call(
        _s5_prep_kernel,
        out_shape=(jax.ShapeDtypeStruct((DEPTH, n_pow, g, p), F32),
                   jax.ShapeDtypeStruct((DEPTH, n_pow, g, p), F32),
                   jax.ShapeDtypeStruct((DEPTH, SSM_GROUP, g, p), F32),
                   jax.ShapeDtypeStruct((DEPTH, SSM_GROUP, g, p), F32)),
        grid=(DEPTH,),
        in_specs=[mat, mat, pl.BlockSpec((None, g, 1), lambda l: (l, 0, 0)), stack_c, stack_c],
        out_specs=(stack_n, stack_n, stack_c, stack_c),
        compiler_params=_cparams(("arbitrary",)),
        name="s5_discretize",
    )(a_re, a_im, log_dt.reshape(DEPTH, g, 1), bt_re, bt_im)


def _norm_mod_chunk(x_ref, g_ref, sc_ref, sh_ref, h_scr, slot, chunk):
    gain = g_ref[...]
    per_chunk = x_ref.shape[0] // MOD_BLOCK
    for s in range(per_chunk):
        x = x_ref[s * MOD_BLOCK:(s + 1) * MOD_BLOCK, :]
        ms = jnp.mean(x * x, axis=-1, keepdims=True)
        y = x * lax.rsqrt(ms + EPS) * gain
        mod_row = pl.ds(chunk * per_chunk + s, 1)
        h = y * (1.0 + sc_ref[mod_row, :]) + sh_ref[mod_row, :]
        rows = pl.ds(pl.multiple_of((chunk * per_chunk + s) * MOD_BLOCK, MOD_BLOCK), MOD_BLOCK)
        h_scr[slot, rows, :] = h.astype(BF16)


def _norm_pipeline(l, n_tiles, tm, sc_idx, sh_idx, chunk_rows):
    n_chunks = tm // chunk_rows
    nsub = tm // MOD_BLOCK
    tile = lambda i: jnp.minimum(i, n_tiles - 1)
    specs = [
        pl.BlockSpec((chunk_rows, D_MODEL), lambda i, j: (tile(i) * n_chunks + jnp.minimum(j, n_chunks - 1), 0)),
        _layer_spec(l, (1, D_MODEL)),
        pl.BlockSpec((None, nsub, D_MODEL), lambda i, j: (l, tile(i), sc_idx)),
        pl.BlockSpec((None, nsub, D_MODEL), lambda i, j: (l, tile(i), sh_idx)),
    ]
    scratch = pltpu.VMEM((2, tm, D_MODEL), BF16)
    out_row = lambda i: jnp.maximum(i - 1, 0)
    out_col = lambda i, col: jnp.where(i == 0, 0, col)
    return specs, scratch, n_chunks, out_row, out_col


def _in_gate_kernel(x_ref, g_ref, sc_ref, sh_ref, wi_ref, wg_ref, b_ref, proj_ref, gate_ref, h_scr, *, n_in,
                    n_chunks):
    i, j = pl.program_id(0), pl.program_id(1)
    ready = lax.rem(i + 1, 2)

    filling = j < n_chunks

    def fill():
        _norm_mod_chunk(x_ref, g_ref, sc_ref, sh_ref, h_scr, lax.rem(i, 2), j)

    def gates():
        acc = jnp.dot(h_scr[ready], wg_ref[...], preferred_element_type=F32)
        gate_ref[...] = _sigmoid(acc + b_ref[...]).astype(gate_ref.dtype)

    pl.when(jnp.logical_and(i == 0, filling))(fill)

    @pl.when(jnp.logical_and(i > 0, j < n_in))
    def _():
        proj_ref[...] = jnp.dot(h_scr[ready], wi_ref[...], preferred_element_type=F32)
        fill()

    @pl.when(jnp.logical_and(i > 0, jnp.logical_and(j >= n_in, filling)))
    def _():
        gates()
        fill()

    pl.when(jnp.logical_and(i > 0, jnp.logical_not(filling)))(gates)


def _in_gate_call(x, l, gain, modx, w_in, w_gate, b_gate, *, tm=1024, tn_in=1280, tn_gate=1024):
    n_tiles = x.shape[0] // tm
    n_in, n_gate = w_in.shape[2] // tn_in, w_gate.shape[2] // tn_gate
    norm_specs, h_scratch, n_chunks, out_row, out_col = _norm_pipeline(l, n_tiles, tm, 1, 0, 4 * MOD_BLOCK)
    assert n_in <= n_chunks <= n_in + n_gate
    in_col = lambda j: jnp.minimum(j, n_in - 1)
    gate_col = lambda j: jnp.maximum(j - n_in, 0)
    return pl.pallas_call(
        functools.partial(_in_gate_kernel, n_in=n_in, n_chunks=n_chunks),
        out_shape=(jax.ShapeDtypeStruct((x.shape[0], w_in.shape[2]), F32),
                   jax.ShapeDtypeStruct((x.shape[0], w_gate.shape[2]), BF16)),
        grid=(n_tiles + 1, n_in + n_gate),
        in_specs=norm_specs + [
            pl.BlockSpec((None, D_MODEL, tn_in), lambda i, j: (0, 0, in_col(j))),
            pl.BlockSpec((None, D_MODEL, tn_gate), lambda i, j: (0, 0, gate_col(j))),
            pl.BlockSpec((None, 1, tn_gate), lambda i, j: (l, 0, gate_col(j))),
        ],
        out_specs=(pl.BlockSpec((tm, tn_in), lambda i, j: (out_row(i), out_col(i, in_col(j)))),
                   pl.BlockSpec((tm, tn_gate), lambda i, j: (out_row(i), out_col(i, gate_col(j))))),
        scratch_shapes=[h_scratch],
        compiler_params=_cparams(("arbitrary", "arbitrary")),
        name="in_proj_gates",
    )(x, gain.reshape(DEPTH, 1, D_MODEL), modx, modx, w_in, w_gate, b_gate.reshape(DEPTH, 1, w_gate.shape[2]))


def _ffn_up_kernel(x_ref, g_ref, sc_ref, sh_ref, wg_ref, wu_ref, o_ref, h_scr, *, n_chunks):
    i, j = pl.program_id(0), pl.program_id(1)

    filling = j < n_chunks

    def fill():
        _norm_mod_chunk(x_ref, g_ref, sc_ref, sh_ref, h_scr, lax.rem(i, 2), j)

    def swiglu():
        h = h_scr[lax.rem(i + 1, 2)]
        gate = jnp.dot(h, wg_ref[...], preferred_element_type=F32)
        up = jnp.dot(h, wu_ref[...], preferred_element_type=F32)
        o_ref[...] = (gate * _sigmoid(gate) * up).astype(o_ref.dtype)

    pl.when(jnp.logical_and(i == 0, filling))(fill)

    @pl.when(jnp.logical_and(i > 0, filling))
    def _():
        swiglu()
        fill()

    pl.when(jnp.logical_and(i > 0, jnp.logical_not(filling)))(swiglu)


def _ffn_up_call(x, l, gain, modx, w_gate, w_up, *, tm=1024, tn=512):
    n_tiles = x.shape[0] // tm
    n_out = w_gate.shape[2]
    norm_specs, h_scratch, n_chunks, out_row, out_col = _norm_pipeline(l, n_tiles, tm, 4, 3, 2 * MOD_BLOCK)
    assert n_out // tn >= n_chunks
    w_spec = pl.BlockSpec((None, D_MODEL, tn), lambda i, j: (0, 0, j))
    return pl.pallas_call(
        functools.partial(_ffn_up_kernel, n_chunks=n_chunks),
        out_shape=jax.ShapeDtypeStruct((x.shape[0], n_out), BF16),
        grid=(n_tiles + 1, n_out // tn),
        in_specs=norm_specs + [w_spec, w_spec],
        out_specs=pl.BlockSpec((tm, tn), lambda i, j: (out_row(i), out_col(i, j))),
        scratch_shapes=[h_scratch],
        compiler_params=_cparams(("arbitrary", "arbitrary")),
        name="ffn_up",
    )(x, gain.reshape(DEPTH, 1, D_MODEL), modx, modx, w_gate, w_up)


def _mix_kernel(sp_ref, ss_ref, a_ref, ga_ref, gb_ref, ws_ref, wa_ref, o_ref, *, n_split):
    def body(s_ref):
        ps = jnp.dot(s_ref[...], ws_ref[...], preferred_element_type=F32)
        pa = jnp.dot(a_ref[...], wa_ref[...], preferred_element_type=F32)
        o_ref[...] = (ga_ref[...] * ps + gb_ref[...] * pa).astype(o_ref.dtype)

    i = pl.program_id(0)
    pl.when(i < n_split)(lambda: body(sp_ref))
    pl.when(i >= n_split)(lambda: body(ss_ref))


def _mix_call(ssm_p, ssm_s, attn_out, gates, w_ps, w_pa, *, tm=1024, tn=512):
    n_tok = attn_out.shape[0]
    nj = D_MODEL // tn
    n_split = ssm_p.shape[0] // tm
    return pl.pallas_call(
        functools.partial(_mix_kernel, n_split=n_split),
        out_shape=jax.ShapeDtypeStruct((n_tok, D_MODEL), BF16),
        grid=(n_tok // tm, nj),
        in_specs=[
            pl.BlockSpec((tm, D_SSM), lambda i, j: (jnp.minimum(i, n_split - 1), 0)),
            pl.BlockSpec((tm, D_SSM), lambda i, j: (jnp.maximum(i - n_split, 0), 0)),
            pl.BlockSpec((tm, D_ATTN), lambda i, j: (i, 0)),
            pl.BlockSpec((tm, tn), lambda i, j: (i, j)),
            pl.BlockSpec((tm, tn), lambda i, j: (i, j + nj)),
            pl.BlockSpec((None, D_SSM, tn), lambda i, j: (0, 0, j)),
            pl.BlockSpec((None, D_ATTN, tn), lambda i, j: (0, 0, j)),
        ],
        out_specs=pl.BlockSpec((tm, tn), lambda i, j: (i, j)),
        compiler_params=_cparams(("arbitrary", "arbitrary")),
        name="branch_merge",
    )(ssm_p, ssm_s, attn_out, gates, gates, w_ps, w_pa)


def _resid_mm_kernel(a_ref, w_ref, x_ref, g_ref, *o_refs, nsub, n_split):
    acc = jnp.dot(a_ref[...], w_ref[...], preferred_element_type=F32)

    def write(o_ref):
        for s in range(nsub):
            rows = slice(s * MOD_BLOCK, (s + 1) * MOD_BLOCK)
            o_ref[rows, :] = x_ref[rows, :] + g_ref[s:s + 1, :] * acc[rows, :]

    if n_split is None:
        write(o_refs[0])
    else:
        i = pl.program_id(0)
        pl.when(i < n_split)(lambda: write(o_refs[0]))
        pl.when(i >= n_split)(lambda: write(o_refs[1]))


def _resid_mm_call(a, l, w, x, modx, g_idx, *, n_split=None, tm=1024, tn=512, name):
    n_tok, k = a.shape
    nsub = tm // MOD_BLOCK
    nj = D_MODEL // tn
    n_tiles = n_tok // tm
    if n_split is None:
        out_shape = jax.ShapeDtypeStruct((n_tok, D_MODEL), F32)
        out_specs = pl.BlockSpec((tm, tn), lambda i, j: (i, j))
    else:
        out_shape = (jax.ShapeDtypeStruct((n_split * tm, D_MODEL), F32),
                     jax.ShapeDtypeStruct(((n_tiles - n_split) * tm, D_MODEL), F32))
        out_specs = (
            pl.BlockSpec((tm, tn), lambda i, j: (jnp.minimum(i, n_split - 1), jnp.where(i < n_split, j, nj - 1))),
            pl.BlockSpec((tm, tn), lambda i, j: (jnp.maximum(i - n_split, 0), jnp.where(i < n_split, 0, j))),
        )
    return pl.pallas_call(
        functools.partial(_resid_mm_kernel, nsub=nsub, n_split=n_split),
        out_shape=out_shape,
        grid=(n_tiles, nj),
        in_specs=[
            pl.BlockSpec((tm, k), lambda i, j: (i, 0)),
            pl.BlockSpec((None, k, tn), lambda i, j: (0, 0, j)),
            pl.BlockSpec((tm, tn), lambda i, j: (i, j)),
            pl.BlockSpec((None, nsub, tn), lambda i, j: (l, i, g_idx * nj + j)),
        ],
        out_specs=out_specs,
        compiler_params=_cparams(("arbitrary", "arbitrary")),
        name=name,
    )(a, w, x, modx)


SCAN_LW = 512


def _ssm_kernel(*refs, tt, n_cast, sps):
    (u_ref, h0r_ref, h0i_ref, wbr_ref, wbi_ref, pwj_ref, seg_ref, wcr_ref, wci_ref, d_ref, wglu_ref,
     bglu_ref) = refs[:12]
    cast_in = refs[12:12 + n_cast]
    o_ref, htr_ref, hti_ref = refs[12 + n_cast:15 + n_cast]
    cast_out = refs[15 + n_cast:15 + 2 * n_cast]
    bur, bui, cre, cim, z_scr = refs[15 + 2 * n_cast:]
    steps = tt // S5_SEG
    for w_ref, wb_ref in zip(cast_in, cast_out):
        wb_ref[...] = w_ref[...].astype(BF16)

    carried = sps == S5_SEG
    if carried:
        @pl.when(pl.program_id(1) == 0)
        def _():
            cre[...] = jnp.broadcast_to(h0r_ref[...], (SUBLANES, STATE_W))
            cim[...] = jnp.broadcast_to(h0i_ref[...], (SUBLANES, STATE_W))

    r_idx = lax.broadcasted_iota(jnp.int32, (tt, tt), 0)
    c_idx = lax.broadcasted_iota(jnp.int32, (tt, tt), 1)
    to_scan = jnp.where(c_idx == (r_idx % S5_SEG) * steps + r_idx // S5_SEG, 1.0, 0.0).astype(BF16)
    to_time = jnp.where(r_idx == (c_idx % S5_SEG) * steps + c_idx // S5_SEG, 1.0, 0.0).astype(BF16)

    u = u_ref[...]
    us = jnp.dot(to_scan, u.astype(BF16), preferred_element_type=F32).astype(BF16)
    kw = D_SSM // SSM_KC
    sw = STATE_W // SSM_KC
    for kc in range(SSM_KC):
        uk = us[:, kc * kw:(kc + 1) * kw]
        bur[:, kc * sw:(kc + 1) * sw] = jnp.dot(uk, wbr_ref[kc], preferred_element_type=F32)
        bui[:, kc * sw:(kc + 1) * sw] = jnp.dot(uk, wbi_ref[kc], preferred_element_type=F32)

    row = lax.broadcasted_iota(jnp.int32, (SUBLANES, SCAN_LW), 0)
    for lc in range(STATE_W // SCAN_LW):
        sl = slice(lc * SCAN_LW, (lc + 1) * SCAN_LW)
        lam_re = jnp.broadcast_to(pwj_ref[0, 0:1, sl], (SUBLANES, SCAN_LW))
        lam_im = jnp.broadcast_to(pwj_ref[1, 0:1, sl], (SUBLANES, SCAN_LW))

        def step(j, carry, sl=sl, lam_re=lam_re, lam_im=lam_im):
            h_re, h_im = carry
            rows = pl.ds(pl.multiple_of(j * SUBLANES, SUBLANES), SUBLANES)
            return (lam_re * h_re - lam_im * h_im + bur[rows, sl],
                    lam_re * h_im + lam_im * h_re + bui[rows, sl])

        zero = jnp.zeros((SUBLANES, SCAN_LW), F32)
        e_re, e_im = lax.fori_loop(0, steps, step, (zero, zero), unroll=4)

        first = row % sps == 0
        in_re, in_im = (cre, cim) if carried else (h0r_ref, h0i_ref)
        x_re = jnp.where(first, in_re[:, sl], pltpu.roll(e_re, 1, 0))
        x_im = jnp.where(first, in_im[:, sl], pltpu.roll(e_im, 1, 0))
        for idx, k in enumerate((1, 2, 4)):
            if k >= sps:
                continue
            m_re = seg_ref[2 * idx, :, sl]
            m_im = seg_ref[2 * idx + 1, :, sl]
            s_re = pltpu.roll(x_re, k, 0)
            s_im = pltpu.roll(x_im, k, 0)
            x_re, x_im = (x_re + m_re * s_re - m_im * s_im,
                          x_im + m_re * s_im + m_im * s_re)
        pj_re = jnp.broadcast_to(pwj_ref[0, 1:2, sl], (SUBLANES, SCAN_LW))
        pj_im = jnp.broadcast_to(pwj_ref[1, 1:2, sl], (SUBLANES, SCAN_LW))
        n_re = pj_re * x_re - pj_im * x_im + e_re
        n_im = pj_re * x_im + pj_im * x_re + e_im
        if carried:
            cre[:, sl] = jnp.broadcast_to(n_re[SUBLANES - 1:SUBLANES, :], (SUBLANES, SCAN_LW))
            cim[:, sl] = jnp.broadcast_to(n_im[SUBLANES - 1:SUBLANES, :], (SUBLANES, SCAN_LW))
        else:
            htr_ref[:, sl] = n_re
            hti_ref[:, sl] = n_im

        def scan(j, carry, sl=sl, step=step):
            h_re, h_im = step(j, carry)
            rows = pl.ds(pl.multiple_of(j * SUBLANES, SUBLANES), SUBLANES)
            bur[rows, sl] = h_re
            bui[rows, sl] = h_im
            return h_re, h_im

        lax.fori_loop(0, steps, scan, (x_re, x_im), unroll=2)

    if carried:
        htr_ref[...] = cre[0:1, :]
        hti_ref[...] = cim[0:1, :]

    d = d_ref[...]
    for kc in range(SSM_KC):
        h_re = bur[:, kc * sw:(kc + 1) * sw].astype(BF16)
        h_im = bui[:, kc * sw:(kc + 1) * sw].astype(BF16)
        ys = (jnp.dot(h_re, wcr_ref[kc], preferred_element_type=F32)
              - jnp.dot(h_im, wci_ref[kc], preferred_element_type=F32))
        ys_hi = ys.astype(BF16)
        ys_lo = (ys - ys_hi.astype(F32)).astype(BF16)
        y = (jnp.dot(to_time, ys_hi, preferred_element_type=F32)
             + jnp.dot(to_time, ys_lo, preferred_element_type=F32))
        cols = slice(kc * kw, (kc + 1) * kw)
        y = y + d[:, cols] * u[:, cols]
        z_scr[:, cols] = 0.5 * y * (1.0 + jnp.tanh(math.sqrt(2.0 / math.pi) * (y + 0.044715 * (y * y * y))))
    z = z_scr[...]
    gate = _sigmoid(jnp.dot(z.astype(BF16), wglu_ref[...], preferred_element_type=F32) + bglu_ref[...])
    o_ref[...] = (z * gate).astype(o_ref.dtype)


def _ssm_call(proj, row0, n_seq, nt, h0_re, h0_im, l, wb_re, wb_im, pw_re, pw_im, wc_re, wc_im, d_skip,
              w_glu, b_glu, *, sps, cast=(), name):
    tt = S5_TILE
    steps = tt // S5_SEG
    power = lambda n: S5_POWERS.index(n)
    pwj = jnp.stack([pw[:, (power(1), power(steps)), :] for pw in (pw_re, pw_im)], axis=1)
    seg_rows = []
    for k in (1, 2, 4):
        keep = (jnp.arange(SUBLANES) % sps >= k)[None, :, None]
        seg_rows += [jnp.where(keep, pw[:, power(k * steps)][:, None, :], 0.0) for pw in (pw_re, pw_im)]
    seg = jnp.stack(seg_rows, axis=1)
    state_rows = h0_re.shape[1]
    assert state_rows == (1 if sps == S5_SEG else S5_SEG) and (sps == S5_SEG or nt == 1)
    t_len = nt * tt
    rb0 = row0 // tt
    n_steps = n_seq * nt
    state_spec = pl.BlockSpec((None, state_rows, STATE_W), lambda b, t: (b, 0, 0))
    cast_in, cast_out, cast_shapes = [], [], []
    for w, wl in cast:
        _, k, n = w.shape
        rows = k // n_steps
        cast_in.append(pl.BlockSpec((None, rows, n), lambda b, t, wl=wl: (wl, b * nt + t, 0)))
        cast_out.append(pl.BlockSpec((None, rows, n), lambda b, t: (0, b * nt + t, 0)))
        cast_shapes.append(jax.ShapeDtypeStruct((1, k, n), BF16))
    resident = lambda shape: pl.BlockSpec((None, *shape), lambda b, t: (l, *(0,) * len(shape)),
                                          pipeline_mode=pl.Buffered(1))
    return pl.pallas_call(
        functools.partial(_ssm_kernel, tt=tt, n_cast=len(cast), sps=sps),
        out_shape=(jax.ShapeDtypeStruct((n_seq * t_len, D_SSM), BF16),
                   jax.ShapeDtypeStruct((n_seq, state_rows, STATE_W), F32),
                   jax.ShapeDtypeStruct((n_seq, state_rows, STATE_W), F32),
                   *cast_shapes),
        grid=(n_seq, nt),
        in_specs=[
            pl.BlockSpec((tt, D_SSM), lambda b, t: (rb0 + b * nt + t, 0)),
            state_spec, state_spec,
            resident(wb_re.shape[1:]), resident(wb_im.shape[1:]),
            _layer_spec(l, pwj.shape[1:]), _layer_spec(l, seg.shape[1:]),
            resident(wc_re.shape[1:]), resident(wc_im.shape[1:]),
            _layer_spec(l, (1, D_SSM)),
            resident((D_SSM, D_SSM)),
            _layer_spec(l, (1, D_SSM)),
            *cast_in,
        ],
        out_specs=(pl.BlockSpec((tt, D_SSM), lambda b, t: (b * nt + t, 0)), state_spec, state_spec, *cast_out),
        scratch_shapes=[pltpu.VMEM((tt, STATE_W), F32), pltpu.VMEM((tt, STATE_W), F32),
                        pltpu.VMEM((SUBLANES, STATE_W), F32), pltpu.VMEM((SUBLANES, STATE_W), F32),
                        pltpu.VMEM((tt, D_SSM), F32)],
        compiler_params=_cparams(("arbitrary", "arbitrary")),
        name=name,
    )(proj, h0_re, h0_im, wb_re, wb_im, pwj, seg, wc_re, wc_im, d_skip.reshape(DEPTH, 1, D_SSM), w_glu,
      b_glu.reshape(DEPTH, 1, D_SSM), *[w for w, _ in cast])


RING = 3
N_KEYS = RING * CHUNK
ATT_CPS = 4


def _stack_blocks(x):
    return jnp.concatenate([x[:, j * LANES:(j + 1) * LANES] for j in range(x.shape[1] // LANES)], axis=0)


def _pair_norm_rope(xs, gain, tab, ones_bd):
    reps = xs.shape[0] // CHUNK
    cos, s_lo, s_hi = (jnp.concatenate([tab[i]] * reps, axis=0) for i in range(3))
    half = ROPE_DIM // 2
    sq = xs * xs
    sq_hi = sq.astype(BF16)
    sq_lo = (sq - sq_hi.astype(F32)).astype(BF16)
    ss = (jnp.dot(sq_hi, ones_bd, preferred_element_type=F32)
          + jnp.dot(sq_lo, ones_bd, preferred_element_type=F32))
    xg = xs * gain
    xr = xg * cos + pltpu.roll(xg, LANES - half, 1) * s_lo + pltpu.roll(xg, half, 1) * s_hi
    return xr * lax.rsqrt(ss * (1.0 / HEAD_DIM) + EPS)


def _attn_kernel(sink_ref, q_ref, k_ref, v_ref, ck_ref, cv_ref, tab_ref, gq_ref, gk_ref, o_ref, kn_ref, kd, vd,
                 *, l, n_prompt_steps, steps_per_seq):
    step = pl.program_id(0)
    is_sample = step >= n_prompt_steps
    c_base = lax.rem(step, steps_per_seq) * ATT_CPS
    first = lax.broadcasted_iota(jnp.int32, (1, LANES), 1) < HEAD_DIM
    ones_bd = jnp.where(lax.broadcasted_iota(jnp.int32, (LANES, LANES), 0) // HEAD_DIM
                        == lax.broadcasted_iota(jnp.int32, (LANES, LANES), 1) // HEAD_DIM, 1.0, 0.0).astype(BF16)
    key_slot = lax.broadcasted_iota(jnp.int32, (1, N_KEYS), 1) // CHUNK
    gq = gq_ref[...]
    gk = gk_ref[...]

    def store_dup(dst, x, rows):
        for pair in range(D_KV // LANES):
            blk = x[:, pair * LANES:(pair + 1) * LANES]
            swapped = pltpu.roll(blk, HEAD_DIM, 1)
            dst[2 * pair, rows, 0:LANES] = jnp.where(first, blk, swapped).astype(BF16)
            dst[2 * pair + 1, rows, 0:LANES] = jnp.where(first, swapped, blk).astype(BF16)

    @pl.when(step == 0)
    def _():
        vd[:, :, LANES:] = jnp.ones((N_KV_HEADS, N_KEYS, LANES), BF16)

    prev = slice(CHUNK, N_KEYS)

    def chunk(ci, from_cache):
        c = 0 if from_cache else c_base + ci
        rows = pl.ds(pl.multiple_of(ci * CHUNK, CHUNK), CHUNK)
        if from_cache:
            cached = pl.ds(pl.multiple_of(ci * WINDOW, WINDOW), WINDOW)
            store_dup(kd, ck_ref[cached, :], prev)
            store_dup(vd, cv_ref[cached, :], prev)

        tab = tab_ref[:, rows, :]
        own = (slice(0, CHUNK) if from_cache else
               pl.ds(pl.multiple_of(lax.rem(c, RING) * CHUNK, CHUNK), CHUNK))
        kn = _pair_norm_rope(_stack_blocks(k_ref[rows, :]), gk, tab, ones_bd)
        kn = jnp.concatenate([kn[0:CHUNK], kn[CHUNK:2 * CHUNK]], axis=1)
        kn_ref[rows, :] = kn
        store_dup(kd, kn, own)
        store_dup(vd, v_ref[rows, :], own)

        qn = _pair_norm_rope(_stack_blocks(q_ref[rows, :]), gq, tab, ones_bd) * (HEAD_DIM ** -0.5)
        q_lo = jnp.where(first, qn, 0.0).astype(BF16)
        q_hi = jnp.where(first, 0.0, qn).astype(BF16)

        valid = None if from_cache else key_slot <= c
        scores = []
        for kh in range(N_KV_HEADS):
            r0 = kh * 2 * CHUNK
            lhs = jnp.concatenate([q_lo[r0:r0 + CHUNK], q_hi[r0:r0 + CHUNK],
                                   q_lo[r0 + CHUNK:r0 + 2 * CHUNK], q_hi[r0 + CHUNK:r0 + 2 * CHUNK]], axis=0)
            scores.append(lax.dot_general(lhs, kd[kh], (((1,), (1,)), ((), ())),
                                          preferred_element_type=F32))
        weighted, sink_terms = [], []
        for kh in range(N_KV_HEADS):
            s = scores[kh] if from_cache else jnp.where(valid, scores[kh], -1e30)
            sink = jnp.concatenate([jnp.full((CHUNK, 1), sink_ref[l, kh * KV_REP + r], F32)
                                    for r in range(KV_REP)], axis=0)
            m = jnp.maximum(jnp.max(s, axis=-1, keepdims=True), sink)
            p = jnp.exp(s - m).astype(BF16)
            sink_terms.append(jnp.exp(sink - m))
            weighted.append(jnp.dot(p, vd[kh], preferred_element_type=F32))
        for kh in range(N_KV_HEADS):
            od = weighted[kh]
            o = od[:, 0:LANES] / (od[:, LANES:] + sink_terms[kh])
            for pair in range(2):
                half = pair * 2 * CHUNK
                blk = jnp.where(first, o[half:half + CHUNK], o[half + CHUNK:half + 2 * CHUNK])
                col = (2 * kh + pair) * LANES
                o_ref[rows, col:col + LANES] = blk.astype(o_ref.dtype)

    @pl.when(jnp.logical_not(is_sample))
    def _():
        @pl.when(c_base == 0)
        def _():
            kd[:, prev, :] = jnp.zeros((N_KV_HEADS, N_KEYS - CHUNK, LANES), BF16)
            vd[:, prev, 0:LANES] = jnp.zeros((N_KV_HEADS, N_KEYS - CHUNK, LANES), BF16)

        lax.fori_loop(0, ATT_CPS, lambda ci, carry: (chunk(ci, False), carry)[1], 0)

    @pl.when(is_sample)
    def _():
        lax.fori_loop(0, ATT_CPS, lambda ci, carry: (chunk(ci, True), carry)[1], 0)


def _attn_call(proj, l, n_prompt, t_prompt, sink, rope_tab, gq, gk, cache_k, cache_v):
    n_tok = proj.shape[0]
    tr = ATT_CPS * CHUNK
    qcol = D_SSM // D_ATTN
    kcol = (D_SSM + D_ATTN) // D_KV
    n_prompt_steps = n_prompt // tr
    cache_spec = pl.BlockSpec((None, ATT_CPS * WINDOW, D_KV),
                              lambda s: (l, jnp.maximum(s - n_prompt_steps, 0), 0))
    return pl.pallas_call(
        functools.partial(_attn_kernel, l=l, n_prompt_steps=n_prompt_steps, steps_per_seq=t_prompt // tr),
        out_shape=(jax.ShapeDtypeStruct((n_tok, D_ATTN), BF16),
                   jax.ShapeDtypeStruct((n_tok, D_KV), F32)),
        grid=(n_tok // tr,),
        in_specs=[
            pl.BlockSpec(memory_space=pltpu.SMEM),
            pl.BlockSpec((tr, D_ATTN), lambda s: (s, qcol)),
            pl.BlockSpec((tr, D_KV), lambda s: (s, kcol)),
            pl.BlockSpec((tr, D_KV), lambda s: (s, kcol + 1)),
            cache_spec, cache_spec,
            pl.BlockSpec((3, tr, LANES), lambda s: (0, s, 0)),
            _layer_spec(l, (1, LANES)), _layer_spec(l, (1, LANES)),
        ],
        out_specs=(pl.BlockSpec((tr, D_ATTN), lambda s: (s, 0)),
                   pl.BlockSpec((tr, D_KV), lambda s: (s, 0))),
        scratch_shapes=[pltpu.VMEM((N_KV_HEADS, N_KEYS, LANES), BF16),
                        pltpu.VMEM((N_KV_HEADS, N_KEYS, 2 * LANES), BF16)],
        compiler_params=_cparams(("arbitrary",)),
        name="banded_attn",
    )(sink, proj, proj, proj, cache_k, cache_v, rope_tab, gq, gk)


def _block_diag_rows(x, n):
    *lead, r, width = x.shape
    keep = jnp.arange(n)[:, None, None] == (jnp.arange(width) // (width // n))[None, None, :]
    return jnp.where(keep, x[..., None, :, :], 0.0).reshape(*lead, n * r, width)


def _rope_table(pos):
    half = ROPE_DIM // 2
    n_pos = pos.shape[0]
    inv_freq = ROPE_THETA ** (-jnp.arange(half, dtype=F32) / half)
    ang = pos.astype(F32)[:, None] * inv_freq[None, :]
    cos, sin = jnp.cos(ang), jnp.sin(ang)
    ones = jnp.ones((n_pos, HEAD_DIM - ROPE_DIM), F32)
    zeros = jnp.zeros((n_pos, HEAD_DIM - half), F32)
    c_tab = jnp.concatenate([cos, cos, ones], axis=1)
    lo_tab = jnp.concatenate([-sin, zeros], axis=1)
    hi_tab = jnp.concatenate([jnp.zeros((n_pos, half), F32), sin, jnp.zeros((n_pos, HEAD_DIM - ROPE_DIM), F32)],
                             axis=1)
    tab = jnp.stack([c_tab, lo_tab, hi_tab])
    return jnp.concatenate([tab, tab], axis=2)


def kernel(x_prompt, x_sample, cache_k, cache_v, state_ssm_re, state_ssm_im, c_prompt, c_sample, w_mod, b_mod, norm1_g, norm2_g, w_in, ssm_a_re, ssm_a_im, ssm_log_dt, ssm_b_re, ssm_b_im, ssm_c_re, ssm_c_im, ssm_d, w_glu, b_glu, q_norm_g, k_norm_g, attn_sink, w_gate, b_gate, w_proj_ssm, w_proj_attn, w_out, w_ffn_gate, w_ffn_up, w_ffn_down):
    bp, tp, _ = x_prompt.shape
    bs, ts, _ = x_sample.shape
    assert ts == CHUNK and tp % (ATT_CPS * CHUNK) == 0 and bs % ATT_CPS == 0
    assert tp % S5_TILE == 0 and S5_TILE % ts == 0 and (bs * ts) % S5_TILE == 0
    n_p, n_s = bp * tp, bs * ts
    tm = 1024
    x = jnp.concatenate([x_prompt.reshape(n_p, D_MODEL), x_sample.reshape(n_s, D_MODEL)], axis=0)

    n_cond = bp + bs
    pad = (-n_cond) % SUBLANES
    c_all = jnp.concatenate([c_prompt, c_sample, jnp.zeros((pad, D_MODEL), F32)], axis=0)
    mod = _mod_call(c_all, w_mod, b_mod)
    n_mod = mod.shape[2]

    def per_block(m, n_seq, t_len):
        reps = t_len // MOD_BLOCK
        return jnp.broadcast_to(m[:, :, None, :], (DEPTH, n_seq, reps, n_mod)).reshape(DEPTH, n_seq * reps, n_mod)

    modx = jnp.concatenate([per_block(mod[:, :bp], bp, tp), per_block(mod[:, bp:n_cond], bs, ts)],
                           axis=1)

    g, p = N_SSM_GROUPS, SSM_STATE
    gpc = g // SSM_KC
    pw_re, pw_im, bb_re, bb_im = _s5_prep_call(ssm_a_re, ssm_a_im, ssm_log_dt,
                                               ssm_b_re.transpose(0, 3, 1, 2), ssm_b_im.transpose(0, 3, 1, 2))
    pw_re = pw_re.reshape(DEPTH, len(S5_POWERS), STATE_W)
    pw_im = pw_im.reshape(DEPTH, len(S5_POWERS), STATE_W)

    def wb_blocks(bb):
        per_chunk = bb.reshape(DEPTH, SSM_GROUP, SSM_KC, gpc * p).transpose(0, 2, 1, 3)
        return _block_diag_rows(per_chunk, gpc).astype(BF16)

    def wc_blocks(cc):
        per_chunk = cc.reshape(DEPTH, SSM_KC, gpc, SSM_GROUP, p).transpose(0, 1, 3, 2, 4)
        transposed = _block_diag_rows(per_chunk.reshape(DEPTH, SSM_KC, SSM_GROUP, gpc * p), gpc)
        return jnp.swapaxes(transposed.astype(BF16), -1, -2)

    ssm_w = (wb_blocks(bb_re), wb_blocks(bb_im), pw_re, pw_im, wc_blocks(ssm_c_re), wc_blocks(ssm_c_im), ssm_d,
             w_glu.astype(BF16), b_glu)
    zeros_state = jnp.zeros((bp, 1, STATE_W), F32)
    sps = ts * S5_SEG // S5_TILE
    n_sample_tiles = n_s // S5_TILE

    def per_segment(state):
        rows = jnp.repeat(state.reshape(DEPTH, bs, STATE_W), sps, axis=1)
        return rows.reshape(DEPTH, n_sample_tiles, S5_SEG, STATE_W)

    def last_segment(state):
        return state.reshape(bs, sps, STATE_W)[:, sps - 1].reshape(bs, g, p)

    h0_re, h0_im = per_segment(state_ssm_re), per_segment(state_ssm_im)

    rope_tab = jnp.concatenate([jnp.tile(_rope_table(jnp.arange(tp)), (1, bp, 1)),
                                jnp.tile(_rope_table(PAST_LEN + jnp.arange(ts)), (1, bs, 1))], axis=1)
    gq = jnp.tile(q_norm_g, (1, LANES // HEAD_DIM)).reshape(DEPTH, 1, LANES)
    gk = jnp.tile(k_norm_g, (1, LANES // HEAD_DIM)).reshape(DEPTH, 1, LANES)
    cache_k2 = cache_k.reshape(DEPTH, bs * WINDOW, D_KV)
    cache_v2 = cache_v.reshape(DEPTH, bs * WINDOW, D_KV)

    w_in_b, w_gate_b = w_in[:1].astype(BF16), w_gate[:1].astype(BF16)
    v0 = D_SSM + D_ATTN + D_KV

    def heads(t):
        return t.reshape(*t.shape[:-1], N_KV_HEADS, HEAD_DIM)

    def last_window(t, col0, col1):
        return jnp.stack([t[(b + 1) * tp - WINDOW:(b + 1) * tp, col0:col1] for b in range(bp)])

    outs = {k: [] for k in ("pk", "pv", "pre", "pim", "sk", "sv", "sre", "sim")}
    for l in range(DEPTH):
        proj, gates = _in_gate_call(x, l, norm1_g, modx, w_in_b, w_gate_b, b_gate, tm=tm)

        nxt = min(l + 1, DEPTH - 1)
        cast = [(w, l) for w in (w_proj_ssm, w_proj_attn, w_out, w_ffn_gate, w_ffn_up, w_ffn_down)]
        cast += [(w_in, nxt), (w_gate, nxt)]
        ssm_p, pre, pim, w_ps_b, w_pa_b, w_out_b, w_fg_b, w_fu_b, w_fd_b, w_in_b, w_gate_b = _ssm_call(
            proj, 0, bp, tp // S5_TILE, zeros_state, zeros_state, l, *ssm_w, sps=S5_SEG, cast=cast, name="s5_prompt")
        ssm_s, sre, sim = _ssm_call(proj, n_p, n_sample_tiles, 1, h0_re[l], h0_im[l], l, *ssm_w, sps=sps,
                                    name="s5_sample")
        attn_out, kn = _attn_call(proj, l, n_p, tp, attn_sink, rope_tab, gq, gk, cache_k2, cache_v2)

        mixed = _mix_call(ssm_p, ssm_s, attn_out, gates, w_ps_b, w_pa_b, tm=tm, tn=1024)
        x = _resid_mm_call(mixed, l, w_out_b, x, modx, 2, tm=tm, tn=1024, name="out_proj")

        act = _ffn_up_call(x, l, norm2_g, modx, w_fg_b, w_fu_b, tm=tm)
        last = l == DEPTH - 1
        x = _resid_mm_call(act, l, w_fd_b, x, modx, 5, n_split=n_p // tm if last else None, tm=tm,
                           name="ffn_down_split" if last else "ffn_down")

        outs["pk"].append(heads(last_window(kn, 0, D_KV)))
        outs["pv"].append(heads(last_window(proj, v0, IN_WIDTH)))
        outs["pre"].append(pre.reshape(bp, g, p))
        outs["pim"].append(pim.reshape(bp, g, p))
        outs["sk"].append(jnp.concatenate([cache_k[l][:, ts:], heads(kn[n_p:].reshape(bs, ts, D_KV))], axis=1))
        outs["sv"].append(jnp.concatenate([cache_v[l][:, ts:], heads(proj[n_p:, v0:].reshape(bs, ts, D_KV))],
                                          axis=1))
        outs["sre"].append(last_segment(sre))
        outs["sim"].append(last_segment(sim))

    y_p, y_s = x
    return (y_p.reshape(bp, tp, D_MODEL), y_s.reshape(bs, ts, D_MODEL),
            jnp.stack(outs["pk"]), jnp.stack(outs["pv"]), jnp.stack(outs["pre"]), jnp.stack(outs["pim"]),
            jnp.stack(outs["sk"]), jnp.stack(outs["sv"]), jnp.stack(outs["sre"]), jnp.stack(outs["sim"]))
```

```python
import functools
import math

import jax
import jax.numpy as jnp
from jax import lax
from jax.experimental import pallas as pl
from jax.experimental.pallas import tpu as pltpu

D_MODEL = 2048
DEPTH = 4
CHUNK = 64
D_SSM = 1024
SSM_GROUP = 16
N_SSM_GROUPS = 64
SSM_STATE = 64
HEAD_DIM = 64
N_HEADS = 16
N_KV_HEADS = 4
KV_REP = N_HEADS // N_KV_HEADS
D_ATTN = N_HEADS * HEAD_DIM
D_KV = N_KV_HEADS * HEAD_DIM
IN_WIDTH = D_SSM + D_ATTN + 2 * D_KV
WINDOW = 128
ROPE_DIM = 16
ROPE_THETA = 500000.0
D_FF = 5632
EPS = 1e-6
PAST_LEN = 2048

LANES = 128
SUBLANES = 8
MOD_BLOCK = CHUNK
STATE_W = N_SSM_GROUPS * SSM_STATE
SSM_KC = 4
VMEM_LIMIT = 56 * 1024 * 1024

F32 = jnp.float32
BF16 = jnp.bfloat16


def _cparams(sem):
    return pltpu.CompilerParams(dimension_semantics=sem, vmem_limit_bytes=VMEM_LIMIT)


def _sigmoid(x):
    return 0.5 + 0.5 * jnp.tanh(0.5 * x)


def _layer_spec(l, shape):
    zeros = (0,) * len(shape)
    return pl.BlockSpec((None, *shape), lambda *_: (l, *zeros))


def _mod_kernel(c_ref, w_ref, b_ref, o_ref):
    c = c_ref[...].astype(BF16)
    w = w_ref[...].astype(BF16)
    o_ref[...] = jnp.dot(c, w, preferred_element_type=F32) + b_ref[...]


def _mod_call(c_all, w_mod, b_mod):
    nb = c_all.shape[0]
    tn = 1024
    n_out = w_mod.shape[2]
    return pl.pallas_call(
        _mod_kernel,
        out_shape=jax.ShapeDtypeStruct((DEPTH, nb, n_out), F32),
        grid=(DEPTH, n_out // tn),
        in_specs=[
            pl.BlockSpec((nb, D_MODEL), lambda l, j: (0, 0)),
            pl.BlockSpec((None, D_MODEL, tn), lambda l, j: (l, 0, j)),
            pl.BlockSpec((None, 1, tn), lambda l, j: (l, 0, j)),
        ],
        out_specs=pl.BlockSpec((None, nb, tn), lambda l, j: (l, 0, j)),
        compiler_params=_cparams(("arbitrary", "arbitrary")),
        name="adaln_mod",
    )(c_all, w_mod, b_mod.reshape(DEPTH, 1, n_out))


S5_TILE = 256
S5_SEG = SUBLANES
S5_POWERS = (1, *(k * S5_TILE // S5_SEG for k in (1, 2, 4)))


def _s5_prep_kernel(are_ref, aim_ref, ldt_ref, bre_ref, bim_ref, pwr_ref, pwi_ref, bbr_ref, bbi_ref):
    a_re = are_ref[...]
    a_im = aim_ref[...]
    dt = jnp.exp(ldt_ref[...])
    z_re = a_re * dt
    z_im = a_im * dt
    for i, n in enumerate(S5_POWERS):
        mag = jnp.exp(z_re * float(n))
        pwr_ref[i] = mag * jnp.cos(z_im * float(n))
        pwi_ref[i] = mag * jnp.sin(z_im * float(n))
    l_re = pwr_ref[S5_POWERS.index(1)]
    l_im = pwi_ref[S5_POWERS.index(1)]
    den = a_re * a_re + a_im * a_im
    n_re = l_re - 1.0
    f_re = (n_re * a_re + l_im * a_im) / den
    f_im = (l_im * a_re - n_re * a_im) / den
    for c in range(SSM_GROUP):
        b_re = bre_ref[c]
        b_im = bim_ref[c]
        bbr_ref[c] = f_re * b_re - f_im * b_im
        bbi_ref[c] = f_re * b_im + f_im * b_re


def _s5_prep_call(a_re, a_im, log_dt, bt_re, bt_im):
    g, p = N_SSM_GROUPS, SSM_STATE
    mat = pl.BlockSpec((None, g, p), lambda l: (l, 0, 0))
    stack_c = pl.BlockSpec((None, SSM_GROUP, g, p), lambda l: (l, 0, 0, 0))
    n_pow = len(S5_POWERS)
    stack_n = pl.BlockSpec((None, n_pow, g, p), lambda l: (l, 0, 0, 0))
    return pl.pallas_call(
        _s5_prep_kernel,
        out_shape=(jax.ShapeDtypeStruct((DEPTH, n_pow, g, p), F32),
                   jax.ShapeDtypeStruct((DEPTH, n_pow, g, p), F32),
                   jax.ShapeDtypeStruct((DEPTH, SSM_GROUP, g, p), F32),
                   jax.ShapeDtypeStruct((DEPTH, SSM_GROUP, g, p), F32)),
        grid=(DEPTH,),
        in_specs=[mat, mat, pl.BlockSpec((None, g, 1), lambda l: (l, 0, 0)), stack_c, stack_c],
        out_specs=(stack_n, stack_n, stack_c, stack_c),
        compiler_params=_cparams(("arbitrary",)),
        name="s5_discretize",
    )(a_re, a_im, log_dt.reshape(DEPTH, g, 1), bt_re, bt_im)


def _norm_mod_chunk(x_ref, g_ref, sc_ref, sh_ref, h_scr, slot, chunk):
    gain = g_ref[...]
    per_chunk = x_ref.shape[0] // MOD_BLOCK
    for s in range(per_chunk):
        x = x_ref[s * MOD_BLOCK:(s + 1) * MOD_BLOCK, :]
        ms = jnp.mean(x * x, axis=-1, keepdims=True)
        y = x * lax.rsqrt(ms + EPS) * gain
        mod_row = pl.ds(chunk * per_chunk + s, 1)
        h = y * (1.0 + sc_ref[mod_row, :]) + sh_ref[mod_row, :]
        rows = pl.ds(pl.multiple_of((chunk * per_chunk + s) * MOD_BLOCK, MOD_BLOCK), MOD_BLOCK)
        h_scr[slot, rows, :] = h.astype(BF16)


def _norm_pipeline(l, n_tiles, tm, sc_idx, sh_idx, chunk_rows):
    n_chunks = tm // chunk_rows
    nsub = tm // MOD_BLOCK
    tile = lambda i: jnp.minimum(i, n_tiles - 1)
    specs = [
        pl.BlockSpec((chunk_rows, D_MODEL), lambda i, j: (tile(i) * n_chunks + jnp.minimum(j, n_chunks - 1), 0)),
        _layer_spec(l, (1, D_MODEL)),
        pl.BlockSpec((None, nsub, D_MODEL), lambda i, j: (l, tile(i), sc_idx)),
        pl.BlockSpec((None, nsub, D_MODEL), lambda i, j: (l, tile(i), sh_idx)),
    ]
    scratch = pltpu.VMEM((2, tm, D_MODEL), BF16)
    out_row = lambda i: jnp.maximum(i - 1, 0)
    out_col = lambda i, col: jnp.where(i == 0, 0, col)
    return specs, scratch, n_chunks, out_row, out_col


def _in_gate_kernel(x_ref, g_ref, sc_ref, sh_ref, wi_ref, wg_ref, b_ref, proj_ref, gate_ref, h_scr, *, n_in,
                    n_chunks):
    i, j = pl.program_id(0), pl.program_id(1)
    ready = lax.rem(i + 1, 2)

    filling = j < n_chunks

    def fill():
        _norm_mod_chunk(x_ref, g_ref, sc_ref, sh_ref, h_scr, lax.rem(i, 2), j)

    def gates():
        acc = jnp.dot(h_scr[ready], wg_ref[...], preferred_element_type=F32)
        gate_ref[...] = _sigmoid(acc + b_ref[...]).astype(gate_ref.dtype)

    pl.when(jnp.logical_and(i == 0, filling))(fill)

    @pl.when(jnp.logical_and(i > 0, j < n_in))
    def _():
        proj_ref[...] = jnp.dot(h_scr[ready], wi_ref[...], preferred_element_type=F32)
        fill()

    @pl.when(jnp.logical_and(i > 0, jnp.logical_and(j >= n_in, filling)))
    def _():
        gates()
        fill()

    pl.when(jnp.logical_and(i > 0, jnp.logical_not(filling)))(gates)


def _in_gate_call(x, l, gain, modx, w_in, w_gate, b_gate, *, tm=1024, tn_in=1280, tn_gate=1024):
    n_tiles = x.shape[0] // tm
    n_in, n_gate = w_in.shape[2] // tn_in, w_gate.shape[2] // tn_gate
    norm_specs, h_scratch, n_chunks, out_row, out_col = _norm_pipeline(l, n_tiles, tm, 1, 0, 4 * MOD_BLOCK)
    assert n_in <= n_chunks <= n_in + n_gate
    in_col = lambda j: jnp.minimum(j, n_in - 1)
    gate_col = lambda j: jnp.maximum(j - n_in, 0)
    return pl.pallas_call(
        functools.partial(_in_gate_kernel, n_in=n_in, n_chunks=n_chunks),
        out_shape=(jax.ShapeDtypeStruct((x.shape[0], w_in.shape[2]), F32),
                   jax.ShapeDtypeStruct((x.shape[0], w_gate.shape[2]), BF16)),
        grid=(n_tiles + 1, n_in + n_gate),
        in_specs=norm_specs + [
            pl.BlockSpec((None, D_MODEL, tn_in), lambda i, j: (0, 0, in_col(j))),
            pl.BlockSpec((None, D_MODEL, tn_gate), lambda i, j: (0, 0, gate_col(j))),
            pl.BlockSpec((None, 1, tn_gate), lambda i, j: (l, 0, gate_col(j))),
        ],
        out_specs=(pl.BlockSpec((tm, tn_in), lambda i, j: (out_row(i), out_col(i, in_col(j)))),
                   pl.BlockSpec((tm, tn_gate), lambda i, j: (out_row(i), out_col(i, gate_col(j))))),
        scratch_shapes=[h_scratch],
        compiler_params=_cparams(("arbitrary", "arbitrary")),
        name="in_proj_gates",
    )(x, gain.reshape(DEPTH, 1, D_MODEL), modx, modx, w_in, w_gate, b_gate.reshape(DEPTH, 1, w_gate.shape[2]))


def _ffn_up_kernel(x_ref, g_ref, sc_ref, sh_ref, wg_ref, wu_ref, o_ref, h_scr, *, n_chunks):
    i, j = pl.program_id(0), pl.program_id(1)

    filling = j < n_chunks

    def fill():
        _norm_mod_chunk(x_ref, g_ref, sc_ref, sh_ref, h_scr, lax.rem(i, 2), j)

    def swiglu():
        h = h_scr[lax.rem(i + 1, 2)]
        gate = jnp.dot(h, wg_ref[...], preferred_element_type=F32)
        up = jnp.dot(h, wu_ref[...], preferred_element_type=F32)
        o_ref[...] = (gate * _sigmoid(gate) * up).astype(o_ref.dtype)

    pl.when(jnp.logical_and(i == 0, filling))(fill)

    @pl.when(jnp.logical_and(i > 0, filling))
    def _():
        swiglu()
        fill()

    pl.when(jnp.logical_and(i > 0, jnp.logical_not(filling)))(swiglu)


def _ffn_up_call(x, l, gain, modx, w_gate, w_up, *, tm=1024, tn=512):
    n_tiles = x.shape[0] // tm
    n_out = w_gate.shape[2]
    norm_specs, h_scratch, n_chunks, out_row, out_col = _norm_pipeline(l, n_tiles, tm, 4, 3, 2 * MOD_BLOCK)
    assert n_out // tn >= n_chunks
    w_spec = pl.BlockSpec((None, D_MODEL, tn), lambda i, j: (0, 0, j))
    return pl.pallas_call(
        functools.partial(_ffn_up_kernel, n_chunks=n_chunks),
        out_shape=jax.ShapeDtypeStruct((x.shape[0], n_out), BF16),
        grid=(n_tiles + 1, n_out // tn),
        in_specs=norm_specs + [w_spec, w_spec],
        out_specs=pl.BlockSpec((tm, tn), lambda i, j: (out_row(i), out_col(i, j))),
        scratch_shapes=[h_scratch],
        compiler_params=_cparams(("arbitrary", "arbitrary")),
        name="ffn_up",
    )(x, gain.reshape(DEPTH, 1, D_MODEL), modx, modx, w_gate, w_up)


def _mix_kernel(sp_ref, ss_ref, a_ref, ga_ref, gb_ref, ws_ref, wa_ref, o_ref, *, n_split):
    def body(s_ref):
        ps = jnp.dot(s_ref[...], ws_ref[...], preferred_element_type=F32)
        pa = jnp.dot(a_ref[...], wa_ref[...], preferred_element_type=F32)
        o_ref[...] = (ga_ref[...] * ps + gb_ref[...] * pa).astype(o_ref.dtype)

    i = pl.program_id(0)
    pl.when(i < n_split)(lambda: body(sp_ref))
    pl.when(i >= n_split)(lambda: body(ss_ref))


def _mix_call(ssm_p, ssm_s, attn_out, gates, w_ps, w_pa, *, tm=1024, tn=512):
    n_tok = attn_out.shape[0]
    nj = D_MODEL // tn
    n_split = ssm_p.shape[0] // tm
    return pl.pallas_call(
        functools.partial(_mix_kernel, n_split=n_split),
        out_shape=jax.ShapeDtypeStruct((n_tok, D_MODEL), BF16),
        grid=(n_tok // tm, nj),
        in_specs=[
            pl.BlockSpec((tm, D_SSM), lambda i, j: (jnp.minimum(i, n_split - 1), 0)),
            pl.BlockSpec((tm, D_SSM), lambda i, j: (jnp.maximum(i - n_split, 0), 0)),
            pl.BlockSpec((tm, D_ATTN), lambda i, j: (i, 0)),
            pl.BlockSpec((tm, tn), lambda i, j: (i, j)),
            pl.BlockSpec((tm, tn), lambda i, j: (i, j + nj)),
            pl.BlockSpec((None, D_SSM, tn), lambda i, j: (0, 0, j)),
            pl.BlockSpec((None, D_ATTN, tn), lambda i, j: (0, 0, j)),
        ],
        out_specs=pl.BlockSpec((tm, tn), lambda i, j: (i, j)),
        compiler_params=_cparams(("arbitrary", "arbitrary")),
        name="branch_merge",
    )(ssm_p, ssm_s, attn_out, gates, gates, w_ps, w_pa)


def _resid_mm_kernel(a_ref, w_ref, x_ref, g_ref, *o_refs, nsub, n_split):
    acc = jnp.dot(a_ref[...], w_ref[...], preferred_element_type=F32)

    def write(o_ref):
        for s in range(nsub):
            rows = slice(s * MOD_BLOCK, (s + 1) * MOD_BLOCK)
            o_ref[rows, :] = x_ref[rows, :] + g_ref[s:s + 1, :] * acc[rows, :]

    if n_split is None:
        write(o_refs[0])
    else:
        i = pl.program_id(0)
        pl.when(i < n_split)(lambda: write(o_refs[0]))
        pl.when(i >= n_split)(lambda: write(o_refs[1]))


def _resid_mm_call(a, l, w, x, modx, g_idx, *, n_split=None, tm=1024, tn=512, name):
    n_tok, k = a.shape
    nsub = tm // MOD_BLOCK
    nj = D_MODEL // tn
    n_tiles = n_tok // tm
    if n_split is None:
        out_shape = jax.ShapeDtypeStruct((n_tok, D_MODEL), F32)
        out_specs = pl.BlockSpec((tm, tn), lambda i, j: (i, j))
    else:
        out_shape = (jax.ShapeDtypeStruct((n_split * tm, D_MODEL), F32),
                     jax.ShapeDtypeStruct(((n_tiles - n_split) * tm, D_MODEL), F32))
        out_specs = (
            pl.BlockSpec((tm, tn), lambda i, j: (jnp.minimum(i, n_split - 1), jnp.where(i < n_split, j, nj - 1))),
            pl.BlockSpec((tm, tn), lambda i, j: (jnp.maximum(i - n_split, 0), jnp.where(i < n_split, 0, j))),
        )
    return pl.pallas_call(
        functools.partial(_resid_mm_kernel, nsub=nsub, n_split=n_split),
        out_shape=out_shape,
        grid=(n_tiles, nj),
        in_specs=[
            pl.BlockSpec((tm, k), lambda i, j: (i, 0)),
            pl.BlockSpec((None, k, tn), lambda i, j: (0, 0, j)),
            pl.BlockSpec((tm, tn), lambda i, j: (i, j)),
            pl.BlockSpec((None, nsub, tn), lambda i, j: (l, i, g_idx * nj + j)),
        ],
        out_specs=out_specs,
        compiler_params=_cparams(("arbitrary", "arbitrary")),
        name=name,
    )(a, w, x, modx)


SCAN_LW = 512


def _ssm_kernel(*refs, tt, n_cast, sps):
    (u_ref, h0r_ref, h0i_ref, wbr_ref, wbi_ref, pwj_ref, seg_ref, wcr_ref, wci_ref, d_ref, wglu_ref,
     bglu_ref) = refs[:12]
    cast_in = refs[12:12 + n_cast]
    o_ref, htr_ref, hti_ref = refs[12 + n_cast:15 + n_cast]
    cast_out = refs[15 + n_cast:15 + 2 * n_cast]
    bur, bui, cre, cim, z_scr = refs[15 + 2 * n_cast:]
    steps = tt // S5_SEG
    for w_ref, wb_ref in zip(cast_in, cast_out):
        wb_ref[...] = w_ref[...].astype(BF16)

    carried = sps == S5_SEG
    if carried:
        @pl.when(pl.program_id(1) == 0)
        def _():
            cre[...] = jnp.broadcast_to(h0r_ref[...], (SUBLANES, STATE_W))
            cim[...] = jnp.broadcast_to(h0i_ref[...], (SUBLANES, STATE_W))

    r_idx = lax.broadcasted_iota(jnp.int32, (tt, tt), 0)
    c_idx = lax.broadcasted_iota(jnp.int32, (tt, tt), 1)
    to_scan = jnp.where(c_idx == (r_idx % S5_SEG) * steps + r_idx // S5_SEG, 1.0, 0.0).astype(BF16)
    to_time = jnp.where(r_idx == (c_idx % S5_SEG) * steps + c_idx // S5_SEG, 1.0, 0.0).astype(BF16)

    u = u_ref[...]
    us = jnp.dot(to_scan, u.astype(BF16), preferred_element_type=F32).astype(BF16)
    kw = D_SSM // SSM_KC
    sw = STATE_W // SSM_KC
    for kc in range(SSM_KC):
        uk = us[:, kc * kw:(kc + 1) * kw]
        bur[:, kc * sw:(kc + 1) * sw] = jnp.dot(uk, wbr_ref[kc], preferred_element_type=F32)
        bui[:, kc * sw:(kc + 1) * sw] = jnp.dot(uk, wbi_ref[kc], preferred_element_type=F32)

    row = lax.broadcasted_iota(jnp.int32, (SUBLANES, SCAN_LW), 0)
    for lc in range(STATE_W // SCAN_LW):
        sl = slice(lc * SCAN_LW, (lc + 1) * SCAN_LW)
        lam_re = jnp.broadcast_to(pwj_ref[0, 0:1, sl], (SUBLANES, SCAN_LW))
        lam_im = jnp.broadcast_to(pwj_ref[1, 0:1, sl], (SUBLANES, SCAN_LW))

        def step(j, carry, sl=sl, lam_re=lam_re, lam_im=lam_im):
            h_re, h_im = carry
            rows = pl.ds(pl.multiple_of(j * SUBLANES, SUBLANES), SUBLANES)
            return (lam_re * h_re - lam_im * h_im + bur[rows, sl],
                    lam_re * h_im + lam_im * h_re + bui[rows, sl])

        zero = jnp.zeros((SUBLANES, SCAN_LW), F32)
        e_re, e_im = lax.fori_loop(0, steps, step, (zero, zero), unroll=4)

        first = row % sps == 0
        in_re, in_im = (cre, cim) if carried else (h0r_ref, h0i_ref)
        x_re = jnp.where(first, in_re[:, sl], pltpu.roll(e_re, 1, 0))
        x_im = jnp.where(first, in_im[:, sl], pltpu.roll(e_im, 1, 0))
        for idx, k in enumerate((1, 2, 4)):
            if k >= sps:
                continue
            m_re = seg_ref[2 * idx, :, sl]
            m_im = seg_ref[2 * idx + 1, :, sl]
            s_re = pltpu.roll(x_re, k, 0)
            s_im = pltpu.roll(x_im, k, 0)
            x_re, x_im = (x_re + m_re * s_re - m_im * s_im,
                          x_im + m_re * s_im + m_im * s_re)
        pj_re = jnp.broadcast_to(pwj_ref[0, 1:2, sl], (SUBLANES, SCAN_LW))
        pj_im = jnp.broadcast_to(pwj_ref[1, 1:2, sl], (SUBLANES, SCAN_LW))
        n_re = pj_re * x_re - pj_im * x_im + e_re
        n_im = pj_re * x_im + pj_im * x_re + e_im
        if carried:
            cre[:, sl] = jnp.broadcast_to(n_re[SUBLANES - 1:SUBLANES, :], (SUBLANES, SCAN_LW))
            cim[:, sl] = jnp.broadcast_to(n_im[SUBLANES - 1:SUBLANES, :], (SUBLANES, SCAN_LW))
        else:
            htr_ref[:, sl] = n_re
            hti_ref[:, sl] = n_im

        def scan(j, carry, sl=sl, step=step):
            h_re, h_im = step(j, carry)
            rows = pl.ds(pl.multiple_of(j * SUBLANES, SUBLANES), SUBLANES)
            bur[rows, sl] = h_re
            bui[rows, sl] = h_im
            return h_re, h_im

        lax.fori_loop(0, steps, scan, (x_re, x_im), unroll=2)

    if carried:
        htr_ref[...] = cre[0:1, :]
        hti_ref[...] = cim[0:1, :]

    d = d_ref[...]
    for kc in range(SSM_KC):
        h_re = bur[:, kc * sw:(kc + 1) * sw].astype(BF16)
        h_im = bui[:, kc * sw:(kc + 1) * sw].astype(BF16)
        ys = (jnp.dot(h_re, wcr_ref[kc], preferred_element_type=F32)
              - jnp.dot(h_im, wci_ref[kc], preferred_element_type=F32))
        ys_hi = ys.astype(BF16)
        ys_lo = (ys - ys_hi.astype(F32)).astype(BF16)
        y = (jnp.dot(to_time, ys_hi, preferred_element_type=F32)
             + jnp.dot(to_time, ys_lo, preferred_element_type=F32))
        cols = slice(kc * kw, (kc + 1) * kw)
        y = y + d[:, cols] * u[:, cols]
        z_scr[:, cols] = 0.5 * y * (1.0 + jnp.tanh(math.sqrt(2.0 / math.pi) * (y + 0.044715 * (y * y * y))))
    z = z_scr[...]
    gate = _sigmoid(jnp.dot(z.astype(BF16), wglu_ref[...], preferred_element_type=F32) + bglu_ref[...])
    o_ref[...] = (z * gate).astype(o_ref.dtype)


def _ssm_call(proj, row0, n_seq, nt, h0_re, h0_im, l, wb_re, wb_im, pw_re, pw_im, wc_re, wc_im, d_skip,
              w_glu, b_glu, *, sps, cast=(), name):
    tt = S5_TILE
    steps = tt // S5_SEG
    power = lambda n: S5_POWERS.index(n)
    pwj = jnp.stack([pw[:, (power(1), power(steps)), :] for pw in (pw_re, pw_im)], axis=1)
    seg_rows = []
    for k in (1, 2, 4):
        keep = (jnp.arange(SUBLANES) % sps >= k)[None, :, None]
        seg_rows += [jnp.where(keep, pw[:, power(k * steps)][:, None, :], 0.0) for pw in (pw_re, pw_im)]
    seg = jnp.stack(seg_rows, axis=1)
    state_rows = h0_re.shape[1]
    assert state_rows == (1 if sps == S5_SEG else S5_SEG) and (sps == S5_SEG or nt == 1)
    t_len = nt * tt
    rb0 = row0 // tt
    n_steps = n_seq * nt
    state_spec = pl.BlockSpec((None, state_rows, STATE_W), lambda b, t: (b, 0, 0))
    cast_in, cast_out, cast_shapes = [], [], []
    for w, wl in cast:
        _, k, n = w.shape
        rows = k // n_steps
        cast_in.append(pl.BlockSpec((None, rows, n), lambda b, t, wl=wl: (wl, b * nt + t, 0)))
        cast_out.append(pl.BlockSpec((None, rows, n), lambda b, t: (0, b * nt + t, 0)))
        cast_shapes.append(jax.ShapeDtypeStruct((1, k, n), BF16))
    resident = lambda shape: pl.BlockSpec((None, *shape), lambda b, t: (l, *(0,) * len(shape)),
                                          pipeline_mode=pl.Buffered(1))
    return pl.pallas_call(
        functools.partial(_ssm_kernel, tt=tt, n_cast=len(cast), sps=sps),
        out_shape=(jax.ShapeDtypeStruct((n_seq * t_len, D_SSM), BF16),
                   jax.ShapeDtypeStruct((n_seq, state_rows, STATE_W), F32),
                   jax.ShapeDtypeStruct((n_seq, state_rows, STATE_W), F32),
                   *cast_shapes),
        grid=(n_seq, nt),
        in_specs=[
            pl.BlockSpec((tt, D_SSM), lambda b, t: (rb0 + b * nt + t, 0)),
            state_spec, state_spec,
            resident(wb_re.shape[1:]), resident(wb_im.shape[1:]),
            _layer_spec(l, pwj.shape[1:]), _layer_spec(l, seg.shape[1:]),
            resident(wc_re.shape[1:]), resident(wc_im.shape[1:]),
            _layer_spec(l, (1, D_SSM)),
            resident((D_SSM, D_SSM)),
            _layer_spec(l, (1, D_SSM)),
            *cast_in,
        ],
        out_specs=(pl.BlockSpec((tt, D_SSM), lambda b, t: (b * nt + t, 0)), state_spec, state_spec, *cast_out),
        scratch_shapes=[pltpu.VMEM((tt, STATE_W), F32), pltpu.VMEM((tt, STATE_W), F32),
                        pltpu.VMEM((SUBLANES, STATE_W), F32), pltpu.VMEM((SUBLANES, STATE_W), F32),
                        pltpu.VMEM((tt, D_SSM), F32)],
        compiler_params=_cparams(("arbitrary", "arbitrary")),
        name=name,
    )(proj, h0_re, h0_im, wb_re, wb_im, pwj, seg, wc_re, wc_im, d_skip.reshape(DEPTH, 1, D_SSM), w_glu,
      b_glu.reshape(DEPTH, 1, D_SSM), *[w for w, _ in cast])


RING = 3
N_KEYS = RING * CHUNK
ATT_CPS = 8


def _stack_blocks(x):
    return jnp.concatenate([x[:, j * LANES:(j + 1) * LANES] for j in range(x.shape[1] // LANES)], axis=0)


def _pair_norm_rope(xs, gain, tab, ones_bd):
    reps = xs.shape[0] // CHUNK
    cos, s_lo, s_hi = (jnp.concatenate([tab[i]] * reps, axis=0) for i in range(3))
    half = ROPE_DIM // 2
    sq = xs * xs
    sq_hi = sq.astype(BF16)
    sq_lo = (sq - sq_hi.astype(F32)).astype(BF16)
    ss = (jnp.dot(sq_hi, ones_bd, preferred_element_type=F32)
          + jnp.dot(sq_lo, ones_bd, preferred_element_type=F32))
    xg = xs * gain
    xr = xg * cos + pltpu.roll(xg, LANES - half, 1) * s_lo + pltpu.roll(xg, half, 1) * s_hi
    return xr * lax.rsqrt(ss * (1.0 / HEAD_DIM) + EPS)


def _attn_kernel(sink_ref, q_ref, k_ref, v_ref, ck_ref, cv_ref, tab_ref, gq_ref, gk_ref, o_ref, kn_ref, kd, vd,
                 *, l, n_prompt_steps, steps_per_seq):
    step = pl.program_id(0)
    is_sample = step >= n_prompt_steps
    c_base = lax.rem(step, steps_per_seq) * ATT_CPS
    first = lax.broadcasted_iota(jnp.int32, (1, LANES), 1) < HEAD_DIM
    ones_bd = jnp.where(lax.broadcasted_iota(jnp.int32, (LANES, LANES), 0) // HEAD_DIM
                        == lax.broadcasted_iota(jnp.int32, (LANES, LANES), 1) // HEAD_DIM, 1.0, 0.0).astype(BF16)
    key_slot = lax.broadcasted_iota(jnp.int32, (1, N_KEYS), 1) // CHUNK
    gq = gq_ref[...]
    gk = gk_ref[...]

    def store_dup(dst, x, rows):
        for pair in range(D_KV // LANES):
            blk = x[:, pair * LANES:(pair + 1) * LANES]
            swapped = pltpu.roll(blk, HEAD_DIM, 1)
            dst[2 * pair, rows, 0:LANES] = jnp.where(first, blk, swapped).astype(BF16)
            dst[2 * pair + 1, rows, 0:LANES] = jnp.where(first, swapped, blk).astype(BF16)

    @pl.when(step == 0)
    def _():
        vd[:, :, LANES:] = jnp.ones((N_KV_HEADS, N_KEYS, LANES), BF16)

    prev = slice(CHUNK, N_KEYS)

    def chunk(ci, from_cache):
        c = 0 if from_cache else c_base + ci
        rows = pl.ds(pl.multiple_of(ci * CHUNK, CHUNK), CHUNK)
        if from_cache:
            cached = pl.ds(pl.multiple_of(ci * WINDOW, WINDOW), WINDOW)
            store_dup(kd, ck_ref[cached, :], prev)
            store_dup(vd, cv_ref[cached, :], prev)

        tab = tab_ref[:, rows, :]
        own = (slice(0, CHUNK) if from_cache else
               pl.ds(pl.multiple_of(lax.rem(c, RING) * CHUNK, CHUNK), CHUNK))
        kn = _pair_norm_rope(_stack_blocks(k_ref[rows, :]), gk, tab, ones_bd)
        kn = jnp.concatenate([kn[0:CHUNK], kn[CHUNK:2 * CHUNK]], axis=1)
        kn_ref[rows, :] = kn
        store_dup(kd, kn, own)
        store_dup(vd, v_ref[rows, :], own)

        qn = _pair_norm_rope(_stack_blocks(q_ref[rows, :]), gq, tab, ones_bd) * (HEAD_DIM ** -0.5)
        q_lo = jnp.where(first, qn, 0.0).astype(BF16)
        q_hi = jnp.where(first, 0.0, qn).astype(BF16)

        valid = None if from_cache else key_slot <= c
        scores = []
        for kh in range(N_KV_HEADS):
            r0 = kh * 2 * CHUNK
            lhs = jnp.concatenate([q_lo[r0:r0 + CHUNK], q_hi[r0:r0 + CHUNK],
                                   q_lo[r0 + CHUNK:r0 + 2 * CHUNK], q_hi[r0 + CHUNK:r0 + 2 * CHUNK]], axis=0)
            scores.append(lax.dot_general(lhs, kd[kh], (((1,), (1,)), ((), ())),
                                          preferred_element_type=F32))
        weighted, sink_terms = [], []
        for kh in range(N_KV_HEADS):
            s = scores[kh] if from_cache else jnp.where(valid, scores[kh], -1e30)
            sink = jnp.concatenate([jnp.full((CHUNK, 1), sink_ref[l, kh * KV_REP + r], F32)
                                    for r in range(KV_REP)], axis=0)
            m = jnp.maximum(jnp.max(s, axis=-1, keepdims=True), sink)
            p = jnp.exp(s - m).astype(BF16)
            sink_terms.append(jnp.exp(sink - m))
            weighted.append(jnp.dot(p, vd[kh], preferred_element_type=F32))
        for kh in range(N_KV_HEADS):
            od = weighted[kh]
            o = od[:, 0:LANES] / (od[:, LANES:] + sink_terms[kh])
            for pair in range(2):
                half = pair * 2 * CHUNK
                blk = jnp.where(first, o[half:half + CHUNK], o[half + CHUNK:half + 2 * CHUNK])
                col = (2 * kh + pair) * LANES
                o_ref[rows, col:col + LANES] = blk.astype(o_ref.dtype)

    @pl.when(jnp.logical_not(is_sample))
    def _():
        @pl.when(c_base == 0)
        def _():
            kd[:, prev, :] = jnp.zeros((N_KV_HEADS, N_KEYS - CHUNK, LANES), BF16)
            vd[:, prev, 0:LANES] = jnp.zeros((N_KV_HEADS, N_KEYS - CHUNK, LANES), BF16)

        lax.fori_loop(0, ATT_CPS, lambda ci, carry: (chunk(ci, False), carry)[1], 0)

    @pl.when(is_sample)
    def _():
        lax.fori_loop(0, ATT_CPS, lambda ci, carry: (chunk(ci, True), carry)[1], 0)


def _attn_call(proj, l, n_prompt, t_prompt, sink, rope_tab, gq, gk, cache_k, cache_v):
    n_tok = proj.shape[0]
    tr = ATT_CPS * CHUNK
    qcol = D_SSM // D_ATTN
    kcol = (D_SSM + D_ATTN) // D_KV
    n_prompt_steps = n_prompt // tr
    cache_spec = pl.BlockSpec((None, ATT_CPS * WINDOW, D_KV),
                              lambda s: (l, jnp.maximum(s - n_prompt_steps, 0), 0))
    return pl.pallas_call(
        functools.partial(_attn_kernel, l=l, n_prompt_steps=n_prompt_steps, steps_per_seq=t_prompt // tr),
        out_shape=(jax.ShapeDtypeStruct((n_tok, D_ATTN), BF16),
                   jax.ShapeDtypeStruct((n_tok, D_KV), F32)),
        grid=(n_tok // tr,),
        in_specs=[
            pl.BlockSpec(memory_space=pltpu.SMEM),
            pl.BlockSpec((tr, D_ATTN), lambda s: (s, qcol)),
            pl.BlockSpec((tr, D_KV), lambda s: (s, kcol)),
            pl.BlockSpec((tr, D_KV), lambda s: (s, kcol + 1)),
            cache_spec, cache_spec,
            pl.BlockSpec((3, tr, LANES), lambda s: (0, s, 0)),
            _layer_spec(l, (1, LANES)), _layer_spec(l, (1, LANES)),
        ],
        out_specs=(pl.BlockSpec((tr, D_ATTN), lambda s: (s, 0)),
                   pl.BlockSpec((tr, D_KV), lambda s: (s, 0))),
        scratch_shapes=[pltpu.VMEM((N_KV_HEADS, N_KEYS, LANES), BF16),
                        pltpu.VMEM((N_KV_HEADS, N_KEYS, 2 * LANES), BF16)],
        compiler_params=_cparams(("arbitrary",)),
        name="banded_attn",
    )(sink, proj, proj, proj, cache_k, cache_v, rope_tab, gq, gk)


def _block_diag_rows(x, n):
    *lead, r, width = x.shape
    keep = jnp.arange(n)[:, None, None] == (jnp.arange(width) // (width // n))[None, None, :]
    return jnp.where(keep, x[..., None, :, :], 0.0).reshape(*lead, n * r, width)


def _rope_table(pos):
    half = ROPE_DIM // 2
    n_pos = pos.shape[0]
    inv_freq = ROPE_THETA ** (-jnp.arange(half, dtype=F32) / half)
    ang = pos.astype(F32)[:, None] * inv_freq[None, :]
    cos, sin = jnp.cos(ang), jnp.sin(ang)
    ones = jnp.ones((n_pos, HEAD_DIM - ROPE_DIM), F32)
    zeros = jnp.zeros((n_pos, HEAD_DIM - half), F32)
    c_tab = jnp.concatenate([cos, cos, ones], axis=1)
    lo_tab = jnp.concatenate([-sin, zeros], axis=1)
    hi_tab = jnp.concatenate([jnp.zeros((n_pos, half), F32), sin, jnp.zeros((n_pos, HEAD_DIM - ROPE_DIM), F32)],
                             axis=1)
    tab = jnp.stack([c_tab, lo_tab, hi_tab])
    return jnp.concatenate([tab, tab], axis=2)


def kernel(x_prompt, x_sample, cache_k, cache_v, state_ssm_re, state_ssm_im, c_prompt, c_sample, w_mod, b_mod, norm1_g, norm2_g, w_in, ssm_a_re, ssm_a_im, ssm_log_dt, ssm_b_re, ssm_b_im, ssm_c_re, ssm_c_im, ssm_d, w_glu, b_glu, q_norm_g, k_norm_g, attn_sink, w_gate, b_gate, w_proj_ssm, w_proj_attn, w_out, w_ffn_gate, w_ffn_up, w_ffn_down):
    bp, tp, _ = x_prompt.shape
    bs, ts, _ = x_sample.shape
    assert ts == CHUNK and tp % (ATT_CPS * CHUNK) == 0 and bs % ATT_CPS == 0
    assert tp % S5_TILE == 0 and S5_TILE % ts == 0 and (bs * ts) % S5_TILE == 0
    n_p, n_s = bp * tp, bs * ts
    tm = 1024
    x = jnp.concatenate([x_prompt.reshape(n_p, D_MODEL), x_sample.reshape(n_s, D_MODEL)], axis=0)

    n_cond = bp + bs
    pad = (-n_cond) % SUBLANES
    c_all = jnp.concatenate([c_prompt, c_sample, jnp.zeros((pad, D_MODEL), F32)], axis=0)
    mod = _mod_call(c_all, w_mod, b_mod)
    n_mod = mod.shape[2]

    def per_block(m, n_seq, t_len):
        reps = t_len // MOD_BLOCK
        return jnp.broadcast_to(m[:, :, None, :], (DEPTH, n_seq, reps, n_mod)).reshape(DEPTH, n_seq * reps, n_mod)

    modx = jnp.concatenate([per_block(mod[:, :bp], bp, tp), per_block(mod[:, bp:n_cond], bs, ts)],
                           axis=1)

    g, p = N_SSM_GROUPS, SSM_STATE
    gpc = g // SSM_KC
    pw_re, pw_im, bb_re, bb_im = _s5_prep_call(ssm_a_re, ssm_a_im, ssm_log_dt,
                                               ssm_b_re.transpose(0, 3, 1, 2), ssm_b_im.transpose(0, 3, 1, 2))
    pw_re = pw_re.reshape(DEPTH, len(S5_POWERS), STATE_W)
    pw_im = pw_im.reshape(DEPTH, len(S5_POWERS), STATE_W)

    def wb_blocks(bb):
        per_chunk = bb.reshape(DEPTH, SSM_GROUP, SSM_KC, gpc * p).transpose(0, 2, 1, 3)
        return _block_diag_rows(per_chunk, gpc).astype(BF16)

    def wc_blocks(cc):
        per_chunk = cc.reshape(DEPTH, SSM_KC, gpc, SSM_GROUP, p).transpose(0, 1, 3, 2, 4)
        transposed = _block_diag_rows(per_chunk.reshape(DEPTH, SSM_KC, SSM_GROUP, gpc * p), gpc)
        return jnp.swapaxes(transposed.astype(BF16), -1, -2)

    ssm_w = (wb_blocks(bb_re), wb_blocks(bb_im), pw_re, pw_im, wc_blocks(ssm_c_re), wc_blocks(ssm_c_im), ssm_d,
             w_glu.astype(BF16), b_glu)
    zeros_state = jnp.zeros((bp, 1, STATE_W), F32)
    sps = ts * S5_SEG // S5_TILE
    n_sample_tiles = n_s // S5_TILE

    def per_segment(state):
        rows = jnp.repeat(state.reshape(DEPTH, bs, STATE_W), sps, axis=1)
        return rows.reshape(DEPTH, n_sample_tiles, S5_SEG, STATE_W)

    def last_segment(state):
        return state.reshape(bs, sps, STATE_W)[:, sps - 1].reshape(bs, g, p)

    h0_re, h0_im = per_segment(state_ssm_re), per_segment(state_ssm_im)

    rope_tab = jnp.concatenate([jnp.tile(_rope_table(jnp.arange(tp)), (1, bp, 1)),
                                jnp.tile(_rope_table(PAST_LEN + jnp.arange(ts)), (1, bs, 1))], axis=1)
    gq = jnp.tile(q_norm_g, (1, LANES // HEAD_DIM)).reshape(DEPTH, 1, LANES)
    gk = jnp.tile(k_norm_g, (1, LANES // HEAD_DIM)).reshape(DEPTH, 1, LANES)
    cache_k2 = cache_k.reshape(DEPTH, bs * WINDOW, D_KV)
    cache_v2 = cache_v.reshape(DEPTH, bs * WINDOW, D_KV)

    w_in_b, w_gate_b = w_in[:1].astype(BF16), w_gate[:1].astype(BF16)
    v0 = D_SSM + D_ATTN + D_KV

    def heads(t):
        return t.reshape(*t.shape[:-1], N_KV_HEADS, HEAD_DIM)

    def last_window(t, col0, col1):
        return jnp.stack([t[(b + 1) * tp - WINDOW:(b + 1) * tp, col0:col1] for b in range(bp)])

    outs = {k: [] for k in ("pk", "pv", "pre", "pim", "sk", "sv", "sre", "sim")}
    for l in range(DEPTH):
        proj, gates = _in_gate_call(x, l, norm1_g, modx, w_in_b, w_gate_b, b_gate, tm=tm)

        nxt = min(l + 1, DEPTH - 1)
        cast = [(w, l) for w in (w_proj_ssm, w_proj_attn, w_out, w_ffn_gate, w_ffn_up, w_ffn_down)]
        cast += [(w_in, nxt), (w_gate, nxt)]
        ssm_p, pre, pim, w_ps_b, w_pa_b, w_out_b, w_fg_b, w_fu_b, w_fd_b, w_in_b, w_gate_b = _ssm_call(
            proj, 0, bp, tp // S5_TILE, zeros_state, zeros_state, l, *ssm_w, sps=S5_SEG, cast=cast, name="s5_prompt")
        ssm_s, sre, sim = _ssm_call(proj, n_p, n_sample_tiles, 1, h0_re[l], h0_im[l], l, *ssm_w, sps=sps,
                                    name="s5_sample")
        attn_out, kn = _attn_call(proj, l, n_p, tp, attn_sink, rope_tab, gq, gk, cache_k2, cache_v2)

        mixed = _mix_call(ssm_p, ssm_s, attn_out, gates, w_ps_b, w_pa_b, tm=tm, tn=1024)
        x = _resid_mm_call(mixed, l, w_out_b, x, modx, 2, tm=tm, tn=1024, name="out_proj")

        act = _ffn_up_call(x, l, norm2_g, modx, w_fg_b, w_fu_b, tm=tm)
        last = l == DEPTH - 1
        x = _resid_mm_call(act, l, w_fd_b, x, modx, 5, n_split=n_p // tm if last else None, tm=tm,
                           name="ffn_down_split" if last else "ffn_down")

        outs["pk"].append(heads(last_window(kn, 0, D_KV)))
        outs["pv"].append(heads(last_window(proj, v0, IN_WIDTH)))
        outs["pre"].append(pre.reshape(bp, g, p))
        outs["pim"].append(pim.reshape(bp, g, p))
        outs["sk"].append(jnp.concatenate([cache_k[l][:, ts:], heads(kn[n_p:].reshape(bs, ts, D_KV))], axis=1))
        outs["sv"].append(jnp.concatenate([cache_v[l][:, ts:], heads(proj[n_p:, v0:].reshape(bs, ts, D_KV))],
                                          axis=1))
        outs["sre"].append(last_segment(sre))
        outs["sim"].append(last_segment(sim))

    y_p, y_s = x
    return (y_p.reshape(bp, tp, D_MODEL), y_s.reshape(bs, ts, D_MODEL),
            jnp.stack(outs["pk"]), jnp.stack(outs["pv"]), jnp.stack(outs["pre"]), jnp.stack(outs["pim"]),
            jnp.stack(outs["sk"]), jnp.stack(outs["sv"]), jnp.stack(outs["sre"]), jnp.stack(outs["sim"]))
```

```python
import functools
import math

import jax
import jax.numpy as jnp
from jax import lax
from jax.experimental import pallas as pl
from jax.experimental.pallas import tpu as pltpu

D_MODEL = 2048
DEPTH = 4
CHUNK = 64
D_SSM = 1024
SSM_GROUP = 16
N_SSM_GROUPS = 64
SSM_STATE = 64
HEAD_DIM = 64
N_HEADS = 16
N_KV_HEADS = 4
KV_REP = N_HEADS // N_KV_HEADS
D_ATTN = N_HEADS * HEAD_DIM
D_KV = N_KV_HEADS * HEAD_DIM
IN_WIDTH = D_SSM + D_ATTN + 2 * D_KV
WINDOW = 128
ROPE_DIM = 16
ROPE_THETA = 500000.0
D_FF = 5632
EPS = 1e-6
PAST_LEN = 2048

LANES = 128
SUBLANES = 8
MOD_BLOCK = CHUNK
STATE_W = N_SSM_GROUPS * SSM_STATE
SSM_KC = 4
VMEM_LIMIT = 56 * 1024 * 1024

F32 = jnp.float32
BF16 = jnp.bfloat16


def _cparams(sem):
    return pltpu.CompilerParams(dimension_semantics=sem, vmem_limit_bytes=VMEM_LIMIT)


def _sigmoid(x):
    return 0.5 + 0.5 * jnp.tanh(0.5 * x)


def _layer_spec(l, shape):
    zeros = (0,) * len(shape)
    return pl.BlockSpec((None, *shape), lambda *_: (l, *zeros))


def _mod_kernel(c_ref, w_ref, b_ref, o_ref):
    c = c_ref[...].astype(BF16)
    w = w_ref[...].astype(BF16)
    o_ref[...] = jnp.dot(c, w, preferred_element_type=F32) + b_ref[...]


def _mod_call(c_all, w_mod, b_mod):
    nb = c_all.shape[0]
    tn = 1024
    n_out = w_mod.shape[2]
    return pl.pallas_call(
        _mod_kernel,
        out_shape=jax.ShapeDtypeStruct((DEPTH, nb, n_out), F32),
        grid=(DEPTH, n_out // tn),
        in_specs=[
            pl.BlockSpec((nb, D_MODEL), lambda l, j: (0, 0)),
            pl.BlockSpec((None, D_MODEL, tn), lambda l, j: (l, 0, j)),
            pl.BlockSpec((None, 1, tn), lambda l, j: (l, 0, j)),
        ],
        out_specs=pl.BlockSpec((None, nb, tn), lambda l, j: (l, 0, j)),
        compiler_params=_cparams(("arbitrary", "arbitrary")),
        name="adaln_mod",
    )(c_all, w_mod, b_mod.reshape(DEPTH, 1, n_out))


S5_TILE = 256
S5_SEG = SUBLANES
S5_POWERS = (1, *(k * S5_TILE // S5_SEG for k in (1, 2, 4)))


def _s5_prep_kernel(are_ref, aim_ref, ldt_ref, bre_ref, bim_ref, pwr_ref, pwi_ref, bbr_ref, bbi_ref):
    a_re = are_ref[...]
    a_im = aim_ref[...]
    dt = jnp.exp(ldt_ref[...])
    z_re = a_re * dt
    z_im = a_im * dt
    for i, n in enumerate(S5_POWERS):
        mag = jnp.exp(z_re * float(n))
        pwr_ref[i] = mag * jnp.cos(z_im * float(n))
        pwi_ref[i] = mag * jnp.sin(z_im * float(n))
    l_re = pwr_ref[S5_POWERS.index(1)]
    l_im = pwi_ref[S5_POWERS.index(1)]
    den = a_re * a_re + a_im * a_im
    n_re = l_re - 1.0
    f_re = (n_re * a_re + l_im * a_im) / den
    f_im = (l_im * a_re - n_re * a_im) / den
    for c in range(SSM_GROUP):
        b_re = bre_ref[c]
        b_im = bim_ref[c]
        bbr_ref[c] = f_re * b_re - f_im * b_im
        bbi_ref[c] = f_re * b_im + f_im * b_re


def _s5_prep_call(a_re, a_im, log_dt, bt_re, bt_im):
    g, p = N_SSM_GROUPS, SSM_STATE
    mat = pl.BlockSpec((None, g, p), lambda l: (l, 0, 0))
    stack_c = pl.BlockSpec((None, SSM_GROUP, g, p), lambda l: (l, 0, 0, 0))
    n_pow = len(S5_POWERS)
    stack_n = pl.BlockSpec((None, n_pow, g, p), lambda l: (l, 0, 0, 0))
    return pl.pallas_call(
        _s5_prep_kernel,
        out_shape=(jax.ShapeDtypeStruct((DEPTH, n_pow, g, p), F32),
                   jax.ShapeDtypeStruct((DEPTH, n_pow, g, p), F32),
                   jax.ShapeDtypeStruct((DEPTH, SSM_GROUP, g, p), F32),
                   jax.ShapeDtypeStruct((DEPTH, SSM_GROUP, g, p), F32)),
        grid=(DEPTH,),
        in_specs=[mat, mat, pl.BlockSpec((None, g, 1), lambda l: (l, 0, 0)), stack_c, stack_c],
        out_specs=(stack_n, stack_n, stack_c, stack_c),
        compiler_params=_cparams(("arbitrary",)),
        name="s5_discretize",
    )(a_re, a_im, log_dt.reshape(DEPTH, g, 1), bt_re, bt_im)


def _norm_mod_chunk(x_ref, g_ref, sc_ref, sh_ref, h_scr, slot, chunk):
    gain = g_ref[...]
    per_chunk = x_ref.shape[0] // MOD_BLOCK
    for s in range(per_chunk):
        x = x_ref[s * MOD_BLOCK:(s + 1) * MOD_BLOCK, :]
        ms = jnp.mean(x * x, axis=-1, keepdims=True)
        y = x * lax.rsqrt(ms + EPS) * gain
        mod_row = pl.ds(chunk * per_chunk + s, 1)
        h = y * (1.0 + sc_ref[mod_row, :]) + sh_ref[mod_row, :]
        rows = pl.ds(pl.multiple_of((chunk * per_chunk + s) * MOD_BLOCK, MOD_BLOCK), MOD_BLOCK)
        h_scr[slot, rows, :] = h.astype(BF16)


def _norm_pipeline(l, n_tiles, tm, sc_idx, sh_idx, chunk_rows):
    n_chunks = tm // chunk_rows
    nsub = tm // MOD_BLOCK
    tile = lambda i: jnp.minimum(i, n_tiles - 1)
    specs = [
        pl.BlockSpec((chunk_rows, D_MODEL), lambda i, j: (tile(i) * n_chunks + jnp.minimum(j, n_chunks - 1), 0)),
        _layer_spec(l, (1, D_MODEL)),
        pl.BlockSpec((None, nsub, D_MODEL), lambda i, j: (l, tile(i), sc_idx)),
        pl.BlockSpec((None, nsub, D_MODEL), lambda i, j: (l, tile(i), sh_idx)),
    ]
    scratch = pltpu.VMEM((2, tm, D_MODEL), BF16)
    out_row = lambda i: jnp.maximum(i - 1, 0)
    out_col = lambda i, col: jnp.where(i == 0, 0, col)
    return specs, scratch, n_chunks, out_row, out_col


def _in_gate_kernel(x_ref, g_ref, sc_ref, sh_ref, wi_ref, wg_ref, b_ref, proj_ref, gate_ref, h_scr, *, n_in,
                    n_chunks):
    i, j = pl.program_id(0), pl.program_id(1)
    ready = lax.rem(i + 1, 2)

    filling = j < n_chunks

    def fill():
        _norm_mod_chunk(x_ref, g_ref, sc_ref, sh_ref, h_scr, lax.rem(i, 2), j)

    def gates():
        acc = jnp.dot(h_scr[ready], wg_ref[...], preferred_element_type=F32)
        gate_ref[...] = _sigmoid(acc + b_ref[...]).astype(gate_ref.dtype)

    pl.when(jnp.logical_and(i == 0, filling))(fill)

    @pl.when(jnp.logical_and(i > 0, j < n_in))
    def _():
        proj_ref[...] = jnp.dot(h_scr[ready], wi_ref[...], preferred_element_type=F32)
        fill()

    @pl.when(jnp.logical_and(i > 0, jnp.logical_and(j >= n_in, filling)))
    def _():
        gates()
        fill()

    pl.when(jnp.logical_and(i > 0, jnp.logical_not(filling)))(gates)


def _in_gate_call(x, l, gain, modx, w_in, w_gate, b_gate, *, tm=1024, tn_in=1280, tn_gate=1024):
    n_tiles = x.shape[0] // tm
    n_in, n_gate = w_in.shape[2] // tn_in, w_gate.shape[2] // tn_gate
    norm_specs, h_scratch, n_chunks, out_row, out_col = _norm_pipeline(l, n_tiles, tm, 1, 0, 4 * MOD_BLOCK)
    assert n_in <= n_chunks <= n_in + n_gate
    in_col = lambda j: jnp.minimum(j, n_in - 1)
    gate_col = lambda j: jnp.maximum(j - n_in, 0)
    return pl.pallas_call(
        functools.partial(_in_gate_kernel, n_in=n_in, n_chunks=n_chunks),
        out_shape=(jax.ShapeDtypeStruct((x.shape[0], w_in.shape[2]), F32),
                   jax.ShapeDtypeStruct((x.shape[0], w_gate.shape[2]), BF16)),
        grid=(n_tiles + 1, n_in + n_gate),
        in_specs=norm_specs + [
            pl.BlockSpec((None, D_MODEL, tn_in), lambda i, j: (0, 0, in_col(j))),
            pl.BlockSpec((None, D_MODEL, tn_gate), lambda i, j: (0, 0, gate_col(j))),
            pl.BlockSpec((None, 1, tn_gate), lambda i, j: (l, 0, gate_col(j))),
        ],
        out_specs=(pl.BlockSpec((tm, tn_in), lambda i, j: (out_row(i), out_col(i, in_col(j)))),
                   pl.BlockSpec((tm, tn_gate), lambda i, j: (out_row(i), out_col(i, gate_col(j))))),
        scratch_shapes=[h_scratch],
        compiler_params=_cparams(("arbitrary", "arbitrary")),
        name="in_proj_gates",
    )(x, gain.reshape(DEPTH, 1, D_MODEL), modx, modx, w_in, w_gate, b_gate.reshape(DEPTH, 1, w_gate.shape[2]))


def _ffn_up_kernel(x_ref, g_ref, sc_ref, sh_ref, wg_ref, wu_ref, o_ref, h_scr, *, n_chunks):
    i, j = pl.program_id(0), pl.program_id(1)

    filling = j < n_chunks

    def fill():
        _norm_mod_chunk(x_ref, g_ref, sc_ref, sh_ref, h_scr, lax.rem(i, 2), j)

    def swiglu():
        h = h_scr[lax.rem(i + 1, 2)]
        gate = jnp.dot(h, wg_ref[...], preferred_element_type=F32)
        up = jnp.dot(h, wu_ref[...], preferred_element_type=F32)
        o_ref[...] = (gate * _sigmoid(gate) * up).astype(o_ref.dtype)

    pl.when(jnp.logical_and(i == 0, filling))(fill)

    @pl.when(jnp.logical_and(i > 0, filling))
    def _():
        swiglu()
        fill()

    pl.when(jnp.logical_and(i > 0, jnp.logical_not(filling)))(swiglu)


def _ffn_up_call(x, l, gain, modx, w_gate, w_up, *, tm=1024, tn=512):
    n_tiles = x.shape[0] // tm
    n_out = w_gate.shape[2]
    norm_specs, h_scratch, n_chunks, out_row, out_col = _norm_pipeline(l, n_tiles, tm, 4, 3, 2 * MOD_BLOCK)
    assert n_out // tn >= n_chunks
    w_spec = pl.BlockSpec((None, D_MODEL, tn), lambda i, j: (0, 0, j))
    return pl.pallas_call(
        functools.partial(_ffn_up_kernel, n_chunks=n_chunks),
        out_shape=jax.ShapeDtypeStruct((x.shape[0], n_out), BF16),
        grid=(n_tiles + 1, n_out // tn),
        in_specs=norm_specs + [w_spec, w_spec],
        out_specs=pl.BlockSpec((tm, tn), lambda i, j: (out_row(i), out_col(i, j))),
        scratch_shapes=[h_scratch],
        compiler_params=_cparams(("arbitrary", "arbitrary")),
        name="ffn_up",
    )(x, gain.reshape(DEPTH, 1, D_MODEL), modx, modx, w_gate, w_up)


def _mix_kernel(sp_ref, ss_ref, a_ref, ga_ref, gb_ref, ws_ref, wa_ref, o_ref, *, n_split):
    def body(s_ref):
        ps = jnp.dot(s_ref[...], ws_ref[...], preferred_element_type=F32)
        pa = jnp.dot(a_ref[...], wa_ref[...], preferred_element_type=F32)
        o_ref[...] = (ga_ref[...] * ps + gb_ref[...] * pa).astype(o_ref.dtype)

    i = pl.program_id(0)
    pl.when(i < n_split)(lambda: body(sp_ref))
    pl.when(i >= n_split)(lambda: body(ss_ref))


def _mix_call(ssm_p, ssm_s, attn_out, gates, w_ps, w_pa, *, tm=1024, tn=512):
    n_tok = attn_out.shape[0]
    nj = D_MODEL // tn
    n_split = ssm_p.shape[0] // tm
    return pl.pallas_call(
        functools.partial(_mix_kernel, n_split=n_split),
        out_shape=jax.ShapeDtypeStruct((n_tok, D_MODEL), BF16),
        grid=(n_tok // tm, nj),
        in_specs=[
            pl.BlockSpec((tm, D_SSM), lambda i, j: (jnp.minimum(i, n_split - 1), 0)),
            pl.BlockSpec((tm, D_SSM), lambda i, j: (jnp.maximum(i - n_split, 0), 0)),
            pl.BlockSpec((tm, D_ATTN), lambda i, j: (i, 0)),
            pl.BlockSpec((tm, tn), lambda i, j: (i, j)),
            pl.BlockSpec((tm, tn), lambda i, j: (i, j + nj)),
            pl.BlockSpec((None, D_SSM, tn), lambda i, j: (0, 0, j)),
            pl.BlockSpec((None, D_ATTN, tn), lambda i, j: (0, 0, j)),
        ],
        out_specs=pl.BlockSpec((tm, tn), lambda i, j: (i, j)),
        compiler_params=_cparams(("arbitrary", "arbitrary")),
        name="branch_merge",
    )(ssm_p, ssm_s, attn_out, gates, gates, w_ps, w_pa)


def _resid_mm_kernel(a_ref, w_ref, x_ref, g_ref, *o_refs, nsub, n_split):
    acc = jnp.dot(a_ref[...], w_ref[...], preferred_element_type=F32)

    def write(o_ref):
        for s in range(nsub):
            rows = slice(s * MOD_BLOCK, (s + 1) * MOD_BLOCK)
            o_ref[rows, :] = x_ref[rows, :] + g_ref[s:s + 1, :] * acc[rows, :]

    if n_split is None:
        write(o_refs[0])
    else:
        i = pl.program_id(0)
        pl.when(i < n_split)(lambda: write(o_refs[0]))
        pl.when(i >= n_split)(lambda: write(o_refs[1]))


def _resid_mm_call(a, l, w, x, modx, g_idx, *, n_split=None, tm=1024, tn=512, name):
    n_tok, k = a.shape
    nsub = tm // MOD_BLOCK
    nj = D_MODEL // tn
    n_tiles = n_tok // tm
    if n_split is None:
        out_shape = jax.ShapeDtypeStruct((n_tok, D_MODEL), F32)
        out_specs = pl.BlockSpec((tm, tn), lambda i, j: (i, j))
    else:
        out_shape = (jax.ShapeDtypeStruct((n_split * tm, D_MODEL), F32),
                     jax.ShapeDtypeStruct(((n_tiles - n_split) * tm, D_MODEL), F32))
        out_specs = (
            pl.BlockSpec((tm, tn), lambda i, j: (jnp.minimum(i, n_split - 1), jnp.where(i < n_split, j, nj - 1))),
            pl.BlockSpec((tm, tn), lambda i, j: (jnp.maximum(i - n_split, 0), jnp.where(i < n_split, 0, j))),
        )
    return pl.pallas_call(
        functools.partial(_resid_mm_kernel, nsub=nsub, n_split=n_split),
        out_shape=out_shape,
        grid=(n_tiles, nj),
        in_specs=[
            pl.BlockSpec((tm, k), lambda i, j: (i, 0)),
            pl.BlockSpec((None, k, tn), lambda i, j: (0, 0, j)),
            pl.BlockSpec((tm, tn), lambda i, j: (i, j)),
            pl.BlockSpec((None, nsub, tn), lambda i, j: (l, i, g_idx * nj + j)),
        ],
        out_specs=out_specs,
        compiler_params=_cparams(("arbitrary", "arbitrary")),
        name=name,
    )(a, w, x, modx)


SCAN_LW = 512


def _ssm_kernel(*refs, tt, n_cast, sps):
    (u_ref, h0r_ref, h0i_ref, wbr_ref, wbi_ref, pwj_ref, seg_ref, wcr_ref, wci_ref, d_ref, wglu_ref,
     bglu_ref) = refs[:12]
    cast_in = refs[12:12 + n_cast]
    o_ref, htr_ref, hti_ref = refs[12 + n_cast:15 + n_cast]
    cast_out = refs[15 + n_cast:15 + 2 * n_cast]
    bur, bui, cre, cim, z_scr = refs[15 + 2 * n_cast:]
    steps = tt // S5_SEG
    for w_ref, wb_ref in zip(cast_in, cast_out):
        wb_ref[...] = w_ref[...].astype(BF16)

    carried = sps == S5_SEG
    if carried:
        @pl.when(pl.program_id(1) == 0)
        def _():
            cre[...] = jnp.broadcast_to(h0r_ref[...], (SUBLANES, STATE_W))
            cim[...] = jnp.broadcast_to(h0i_ref[...], (SUBLANES, STATE_W))

    r_idx = lax.broadcasted_iota(jnp.int32, (tt, tt), 0)
    c_idx = lax.broadcasted_iota(jnp.int32, (tt, tt), 1)
    to_scan = jnp.where(c_idx == (r_idx % S5_SEG) * steps + r_idx // S5_SEG, 1.0, 0.0).astype(BF16)
    to_time = jnp.where(r_idx == (c_idx % S5_SEG) * steps + c_idx // S5_SEG, 1.0, 0.0).astype(BF16)

    u = u_ref[...]
    us = jnp.dot(to_scan, u.astype(BF16), preferred_element_type=F32).astype(BF16)
    kw = D_SSM // SSM_KC
    sw = STATE_W // SSM_KC
    for kc in range(SSM_KC):
        uk = us[:, kc * kw:(kc + 1) * kw]
        bur[:, kc * sw:(kc + 1) * sw] = jnp.dot(uk, wbr_ref[kc], preferred_element_type=F32)
        bui[:, kc * sw:(kc + 1) * sw] = jnp.dot(uk, wbi_ref[kc], preferred_element_type=F32)

    row = lax.broadcasted_iota(jnp.int32, (SUBLANES, SCAN_LW), 0)
    for lc in range(STATE_W // SCAN_LW):
        sl = slice(lc * SCAN_LW, (lc + 1) * SCAN_LW)
        lam_re = jnp.broadcast_to(pwj_ref[0, 0:1, sl], (SUBLANES, SCAN_LW))
        lam_im = jnp.broadcast_to(pwj_ref[1, 0:1, sl], (SUBLANES, SCAN_LW))

        def step(j, carry, sl=sl, lam_re=lam_re, lam_im=lam_im):
            h_re, h_im = carry
            rows = pl.ds(pl.multiple_of(j * SUBLANES, SUBLANES), SUBLANES)
            return (lam_re * h_re - lam_im * h_im + bur[rows, sl],
                    lam_re * h_im + lam_im * h_re + bui[rows, sl])

        zero = jnp.zeros((SUBLANES, SCAN_LW), F32)
        e_re, e_im = lax.fori_loop(0, steps, step, (zero, zero), unroll=4)

        first = row % sps == 0
        in_re, in_im = (cre, cim) if carried else (h0r_ref, h0i_ref)
        x_re = jnp.where(first, in_re[:, sl], pltpu.roll(e_re, 1, 0))
        x_im = jnp.where(first, in_im[:, sl], pltpu.roll(e_im, 1, 0))
        for idx, k in enumerate((1, 2, 4)):
            if k >= sps:
                continue
            m_re = seg_ref[2 * idx, :, sl]
            m_im = seg_ref[2 * idx + 1, :, sl]
            s_re = pltpu.roll(x_re, k, 0)
            s_im = pltpu.roll(x_im, k, 0)
            x_re, x_im = (x_re + m_re * s_re - m_im * s_im,
                          x_im + m_re * s_im + m_im * s_re)
        pj_re = jnp.broadcast_to(pwj_ref[0, 1:2, sl], (SUBLANES, SCAN_LW))
        pj_im = jnp.broadcast_to(pwj_ref[1, 1:2, sl], (SUBLANES, SCAN_LW))
        n_re = pj_re * x_re - pj_im * x_im + e_re
        n_im = pj_re * x_im + pj_im * x_re + e_im
        if carried:
            cre[:, sl] = jnp.broadcast_to(n_re[SUBLANES - 1:SUBLANES, :], (SUBLANES, SCAN_LW))
            cim[:, sl] = jnp.broadcast_to(n_im[SUBLANES - 1:SUBLANES, :], (SUBLANES, SCAN_LW))
        else:
            htr_ref[:, sl] = n_re
            hti_ref[:, sl] = n_im

        def scan(j, carry, sl=sl, step=step):
            h_re, h_im = step(j, carry)
            rows = pl.ds(pl.multiple_of(j * SUBLANES, SUBLANES), SUBLANES)
            bur[rows, sl] = h_re
            bui[rows, sl] = h_im
            return h_re, h_im

        lax.fori_loop(0, steps, scan, (x_re, x_im), unroll=2)

    if carried:
        htr_ref[...] = cre[0:1, :]
        hti_ref[...] = cim[0:1, :]

    d = d_ref[...]
    for kc in range(SSM_KC):
        h_re = bur[:, kc * sw:(kc + 1) * sw].astype(BF16)
        h_im = bui[:, kc * sw:(kc + 1) * sw].astype(BF16)
        ys = (jnp.dot(h_re, wcr_ref[kc], preferred_element_type=F32)
              - jnp.dot(h_im, wci_ref[kc], preferred_element_type=F32))
        ys_hi = ys.astype(BF16)
        ys_lo = (ys - ys_hi.astype(F32)).astype(BF16)
        y = (jnp.dot(to_time, ys_hi, preferred_element_type=F32)
             + jnp.dot(to_time, ys_lo, preferred_element_type=F32))
        cols = slice(kc * kw, (kc + 1) * kw)
        y = y + d[:, cols] * u[:, cols]
        z_scr[:, cols] = 0.5 * y * (1.0 + jnp.tanh(math.sqrt(2.0 / math.pi) * (y + 0.044715 * (y * y * y))))
    z = z_scr[...]
    gate = _sigmoid(jnp.dot(z.astype(BF16), wglu_ref[...], preferred_element_type=F32) + bglu_ref[...])
    o_ref[...] = (z * gate).astype(o_ref.dtype)


def _ssm_call(proj, row0, n_seq, nt, h0_re, h0_im, l, wb_re, wb_im, pw_re, pw_im, wc_re, wc_im, d_skip,
              w_glu, b_glu, *, sps, cast=(), name):
    tt = S5_TILE
    steps = tt // S5_SEG
    power = lambda n: S5_POWERS.index(n)
    pwj = jnp.stack([pw[:, (power(1), power(steps)), :] for pw in (pw_re, pw_im)], axis=1)
    seg_rows = []
    for k in (1, 2, 4):
        keep = (jnp.arange(SUBLANES) % sps >= k)[None, :, None]
        seg_rows += [jnp.where(keep, pw[:, power(k * steps)][:, None, :], 0.0) for pw in (pw_re, pw_im)]
    seg = jnp.stack(seg_rows, axis=1)
    state_rows = h0_re.shape[1]
    assert state_rows == (1 if sps == S5_SEG else S5_SEG) and (sps == S5_SEG or nt == 1)
    t_len = nt * tt
    rb0 = row0 // tt
    n_steps = n_seq * nt
    state_spec = pl.BlockSpec((None, state_rows, STATE_W), lambda b, t: (b, 0, 0))
    cast_in, cast_out, cast_shapes = [], [], []
    for w, wl in cast:
        _, k, n = w.shape
        rows = k // n_steps
        cast_in.append(pl.BlockSpec((None, rows, n), lambda b, t, wl=wl: (wl, b * nt + t, 0)))
        cast_out.append(pl.BlockSpec((None, rows, n), lambda b, t: (0, b * nt + t, 0)))
        cast_shapes.append(jax.ShapeDtypeStruct((1, k, n), BF16))
    resident = lambda shape: pl.BlockSpec((None, *shape), lambda b, t: (l, *(0,) * len(shape)),
                                          pipeline_mode=pl.Buffered(1))
    return pl.pallas_call(
        functools.partial(_ssm_kernel, tt=tt, n_cast=len(cast), sps=sps),
        out_shape=(jax.ShapeDtypeStruct((n_seq * t_len, D_SSM), BF16),
                   jax.ShapeDtypeStruct((n_seq, state_rows, STATE_W), F32),
                   jax.ShapeDtypeStruct((n_seq, state_rows, STATE_W), F32),
                   *cast_shapes),
        grid=(n_seq, nt),
        in_specs=[
            pl.BlockSpec((tt, D_SSM), lambda b, t: (rb0 + b * nt + t, 0)),
            state_spec, state_spec,
            resident(wb_re.shape[1:]), resident(wb_im.shape[1:]),
            _layer_spec(l, pwj.shape[1:]), _layer_spec(l, seg.shape[1:]),
            resident(wc_re.shape[1:]), resident(wc_im.shape[1:]),
            _layer_spec(l, (1, D_SSM)),
            resident((D_SSM, D_SSM)),
            _layer_spec(l, (1, D_SSM)),
            *cast_in,
        ],
        out_specs=(pl.BlockSpec((tt, D_SSM), lambda b, t: (b * nt + t, 0)), state_spec, state_spec, *cast_out),
        scratch_shapes=[pltpu.VMEM((tt, STATE_W), F32), pltpu.VMEM((tt, STATE_W), F32),
                        pltpu.VMEM((SUBLANES, STATE_W), F32), pltpu.VMEM((SUBLANES, STATE_W), F32),
                        pltpu.VMEM((tt, D_SSM), F32)],
        compiler_params=_cparams(("arbitrary", "arbitrary")),
        name=name,
    )(proj, h0_re, h0_im, wb_re, wb_im, pwj, seg, wc_re, wc_im, d_skip.reshape(DEPTH, 1, D_SSM), w_glu,
      b_glu.reshape(DEPTH, 1, D_SSM), *[w for w, _ in cast])


RING = 3
N_KEYS = RING * CHUNK
ATT_CPS = 8


def _stack_blocks(x):
    return jnp.concatenate([x[:, j * LANES:(j + 1) * LANES] for j in range(x.shape[1] // LANES)], axis=0)


def _pair_norm_rope(xs, gain, tab, ones_bd):
    reps = xs.shape[0] // CHUNK
    cos, s_lo, s_hi = (jnp.concatenate([tab[i]] * reps, axis=0) for i in range(3))
    half = ROPE_DIM // 2
    sq = xs * xs
    sq_hi = sq.astype(BF16)
    sq_lo = (sq - sq_hi.astype(F32)).astype(BF16)
    ss = (jnp.dot(sq_hi, ones_bd, preferred_element_type=F32)
          + jnp.dot(sq_lo, ones_bd, preferred_element_type=F32))
    xg = xs * gain
    xr = xg * cos + pltpu.roll(xg, LANES - half, 1) * s_lo + pltpu.roll(xg, half, 1) * s_hi
    return xr * lax.rsqrt(ss * (1.0 / HEAD_DIM) + EPS)


def _attn_kernel(sink_ref, q_ref, k_ref, v_ref, ck_ref, cv_ref, tab_ref, gq_ref, gk_ref, o_ref, kn_ref, kd, vd,
                 *, l, n_prompt_steps, steps_per_seq):
    step = pl.program_id(0)
    is_sample = step >= n_prompt_steps
    c_base = lax.rem(step, steps_per_seq) * ATT_CPS
    first = lax.broadcasted_iota(jnp.int32, (1, LANES), 1) < HEAD_DIM
    ones_bd = jnp.where(lax.broadcasted_iota(jnp.int32, (LANES, LANES), 0) // HEAD_DIM
                        == lax.broadcasted_iota(jnp.int32, (LANES, LANES), 1) // HEAD_DIM, 1.0, 0.0).astype(BF16)
    key_slot = lax.broadcasted_iota(jnp.int32, (1, N_KEYS), 1) // CHUNK
    gq = gq_ref[...]
    gk = gk_ref[...]

    def store_dup(dst, x, rows):
        for pair in range(D_KV // LANES):
            blk = x[:, pair * LANES:(pair + 1) * LANES]
            swapped = pltpu.roll(blk, HEAD_DIM, 1)
            dst[2 * pair, rows, 0:LANES] = jnp.where(first, blk, swapped).astype(BF16)
            dst[2 * pair + 1, rows, 0:LANES] = jnp.where(first, swapped, blk).astype(BF16)

    @pl.when(step == 0)
    def _():
        vd[:, :, LANES:] = jnp.ones((N_KV_HEADS, N_KEYS, LANES), BF16)

    prev = slice(CHUNK, N_KEYS)

    def chunk(ci, from_cache):
        c = 0 if from_cache else c_base + ci
        rows = pl.ds(pl.multiple_of(ci * CHUNK, CHUNK), CHUNK)
        if from_cache:
            cached = pl.ds(pl.multiple_of(ci * WINDOW, WINDOW), WINDOW)
            store_dup(kd, ck_ref[cached, :], prev)
            store_dup(vd, cv_ref[cached, :], prev)

        tab = tab_ref[:, rows, :]
        own = (slice(0, CHUNK) if from_cache else
               pl.ds(pl.multiple_of(lax.rem(c, RING) * CHUNK, CHUNK), CHUNK))
        kn = _pair_norm_rope(_stack_blocks(k_ref[rows, :]), gk, tab, ones_bd)
        kn = jnp.concatenate([kn[0:CHUNK], kn[CHUNK:2 * CHUNK]], axis=1)
        kn_ref[rows, :] = kn
        store_dup(kd, kn, own)
        store_dup(vd, v_ref[rows, :], own)

        qn = _pair_norm_rope(_stack_blocks(q_ref[rows, :]), gq, tab, ones_bd) * (HEAD_DIM ** -0.5)
        q_lo = jnp.where(first, qn, 0.0).astype(BF16)
        q_hi = jnp.where(first, 0.0, qn).astype(BF16)

        valid = None if from_cache else key_slot <= c
        scores = []
        for kh in range(N_KV_HEADS):
            r0 = kh * 2 * CHUNK
            lhs = jnp.concatenate([q_lo[r0:r0 + CHUNK], q_hi[r0:r0 + CHUNK],
                                   q_lo[r0 + CHUNK:r0 + 2 * CHUNK], q_hi[r0 + CHUNK:r0 + 2 * CHUNK]], axis=0)
            scores.append(lax.dot_general(lhs, kd[kh], (((1,), (1,)), ((), ())),
                                          preferred_element_type=F32))
        weighted, sink_terms = [], []
        for kh in range(N_KV_HEADS):
            s = scores[kh] if from_cache else jnp.where(valid, scores[kh], -1e30)
            sink = jnp.concatenate([jnp.full((CHUNK, 1), sink_ref[l, kh * KV_REP + r], F32)
                                    for r in range(KV_REP)], axis=0)
            m = jnp.maximum(jnp.max(s, axis=-1, keepdims=True), sink)
            p = jnp.exp(s - m).astype(BF16)
            sink_terms.append(jnp.exp(sink - m))
            weighted.append(jnp.dot(p, vd[kh], preferred_element_type=F32))
        for kh in range(N_KV_HEADS):
            od = weighted[kh]
            o = od[:, 0:LANES] / (od[:, LANES:] + sink_terms[kh])
            for pair in range(2):
                half = pair * 2 * CHUNK
                blk = jnp.where(first, o[half:half + CHUNK], o[half + CHUNK:half + 2 * CHUNK])
                col = (2 * kh + pair) * LANES
                o_ref[rows, col:col + LANES] = blk.astype(o_ref.dtype)

    @pl.when(jnp.logical_not(is_sample))
    def _():
        @pl.when(c_base == 0)
        def _():
            kd[:, prev, :] = jnp.zeros((N_KV_HEADS, N_KEYS - CHUNK, LANES), BF16)
            vd[:, prev, 0:LANES] = jnp.zeros((N_KV_HEADS, N_KEYS - CHUNK, LANES), BF16)

        lax.fori_loop(0, ATT_CPS, lambda ci, carry: (chunk(ci, False), carry)[1], 0)

    @pl.when(is_sample)
    def _():
        lax.fori_loop(0, ATT_CPS, lambda ci, carry: (chunk(ci, True), carry)[1], 0)


def _attn_call(proj, l, n_prompt, t_prompt, sink, rope_tab, gq, gk, cache_k, cache_v):
    n_tok = proj.shape[0]
    tr = ATT_CPS * CHUNK
    qcol = D_SSM // D_ATTN
    kcol = (D_SSM + D_ATTN) // D_KV
    n_prompt_steps = n_prompt // tr
    cache_spec = pl.BlockSpec((None, ATT_CPS * WINDOW, D_KV),
                              lambda s: (l, jnp.maximum(s - n_prompt_steps, 0), 0))
    return pl.pallas_call(
        functools.partial(_attn_kernel, l=l, n_prompt_steps=n_prompt_steps, steps_per_seq=t_prompt // tr),
        out_shape=(jax.ShapeDtypeStruct((n_tok, D_ATTN), BF16),
                   jax.ShapeDtypeStruct((n_tok, D_KV), F32)),
        grid=(n_tok // tr,),
        in_specs=[
            pl.BlockSpec(memory_space=pltpu.SMEM),
            pl.BlockSpec((tr, D_ATTN), lambda s: (s, qcol)),
            pl.BlockSpec((tr, D_KV), lambda s: (s, kcol)),
            pl.BlockSpec((tr, D_KV), lambda s: (s, kcol + 1)),
            cache_spec, cache_spec,
            pl.BlockSpec((3, tr, LANES), lambda s: (0, s, 0)),
            _layer_spec(l, (1, LANES)), _layer_spec(l, (1, LANES)),
        ],
        out_specs=(pl.BlockSpec((tr, D_ATTN), lambda s: (s, 0)),
                   pl.BlockSpec((tr, D_KV), lambda s: (s, 0))),
        scratch_shapes=[pltpu.VMEM((N_KV_HEADS, N_KEYS, LANES), BF16),
                        pltpu.VMEM((N_KV_HEADS, N_KEYS, 2 * LANES), BF16)],
        compiler_params=_cparams(("arbitrary",)),
        name="banded_attn",
    )(sink, proj, proj, proj, cache_k, cache_v, rope_tab, gq, gk)


def _block_diag_rows(x, n):
    *lead, r, width = x.shape
    keep = jnp.arange(n)[:, None, None] == (jnp.arange(width) // (width // n))[None, None, :]
    return jnp.where(keep, x[..., None, :, :], 0.0).reshape(*lead, n * r, width)


def _rope_table(pos):
    half = ROPE_DIM // 2
    n_pos = pos.shape[0]
    inv_freq = ROPE_THETA ** (-jnp.arange(half, dtype=F32) / half)
    ang = pos.astype(F32)[:, None] * inv_freq[None, :]
    cos, sin = jnp.cos(ang), jnp.sin(ang)
    ones = jnp.ones((n_pos, HEAD_DIM - ROPE_DIM), F32)
    zeros = jnp.zeros((n_pos, HEAD_DIM - half), F32)
    c_tab = jnp.concatenate([cos, cos, ones], axis=1)
    lo_tab = jnp.concatenate([-sin, zeros], axis=1)
    hi_tab = jnp.concatenate([jnp.zeros((n_pos, half), F32), sin, jnp.zeros((n_pos, HEAD_DIM - ROPE_DIM), F32)],
                             axis=1)
    tab = jnp.stack([c_tab, lo_tab, hi_tab])
    return jnp.concatenate([tab, tab], axis=2)


def kernel(x_prompt, x_sample, cache_k, cache_v, state_ssm_re, state_ssm_im, c_prompt, c_sample, w_mod, b_mod, norm1_g, norm2_g, w_in, ssm_a_re, ssm_a_im, ssm_log_dt, ssm_b_re, ssm_b_im, ssm_c_re, ssm_c_im, ssm_d, w_glu, b_glu, q_norm_g, k_norm_g, attn_sink, w_gate, b_gate, w_proj_ssm, w_proj_attn, w_out, w_ffn_gate, w_ffn_up, w_ffn_down):
    bp, tp, _ = x_prompt.shape
    bs, ts, _ = x_sample.shape
    assert ts == CHUNK and tp % (ATT_CPS * CHUNK) == 0 and bs % ATT_CPS == 0
    assert tp % S5_TILE == 0 and S5_TILE % ts == 0 and (bs * ts) % S5_TILE == 0
    n_p, n_s = bp * tp, bs * ts
    tm = 1024
    assert n_p % tm == 0 and n_s % tm == 0
    x = jnp.concatenate([x_prompt.reshape(n_p, D_MODEL), x_sample.reshape(n_s, D_MODEL)], axis=0)

    n_cond = bp + bs
    pad = (-n_cond) % SUBLANES
    c_all = jnp.concatenate([c_prompt, c_sample, jnp.zeros((pad, D_MODEL), F32)], axis=0)
    mod = _mod_call(c_all, w_mod, b_mod)
    n_mod = mod.shape[2]

    def per_block(m, n_seq, t_len):
        reps = t_len // MOD_BLOCK
        return jnp.broadcast_to(m[:, :, None, :], (DEPTH, n_seq, reps, n_mod)).reshape(DEPTH, n_seq * reps, n_mod)

    modx = jnp.concatenate([per_block(mod[:, :bp], bp, tp), per_block(mod[:, bp:n_cond], bs, ts)],
                           axis=1)

    g, p = N_SSM_GROUPS, SSM_STATE
    gpc = g // SSM_KC
    pw_re, pw_im, bb_re, bb_im = _s5_prep_call(ssm_a_re, ssm_a_im, ssm_log_dt,
                                               ssm_b_re.transpose(0, 3, 1, 2), ssm_b_im.transpose(0, 3, 1, 2))
    pw_re = pw_re.reshape(DEPTH, len(S5_POWERS), STATE_W)
    pw_im = pw_im.reshape(DEPTH, len(S5_POWERS), STATE_W)

    def wb_blocks(bb):
        per_chunk = bb.reshape(DEPTH, SSM_GROUP, SSM_KC, gpc * p).transpose(0, 2, 1, 3)
        return _block_diag_rows(per_chunk, gpc).astype(BF16)

    def wc_blocks(cc):
        per_chunk = cc.reshape(DEPTH, SSM_KC, gpc, SSM_GROUP, p).transpose(0, 1, 3, 2, 4)
        transposed = _block_diag_rows(per_chunk.reshape(DEPTH, SSM_KC, SSM_GROUP, gpc * p), gpc)
        return jnp.swapaxes(transposed.astype(BF16), -1, -2)

    ssm_w = (wb_blocks(bb_re), wb_blocks(bb_im), pw_re, pw_im, wc_blocks(ssm_c_re), wc_blocks(ssm_c_im), ssm_d,
             w_glu.astype(BF16), b_glu)
    zeros_state = jnp.zeros((bp, 1, STATE_W), F32)
    sps = ts * S5_SEG // S5_TILE
    n_sample_tiles = n_s // S5_TILE

    def per_segment(state):
        rows = jnp.repeat(state.reshape(DEPTH, bs, STATE_W), sps, axis=1)
        return rows.reshape(DEPTH, n_sample_tiles, S5_SEG, STATE_W)

    def last_segment(state):
        return state.reshape(bs, sps, STATE_W)[:, sps - 1].reshape(bs, g, p)

    h0_re, h0_im = per_segment(state_ssm_re), per_segment(state_ssm_im)

    rope_tab = jnp.concatenate([jnp.tile(_rope_table(jnp.arange(tp)), (1, bp, 1)),
                                jnp.tile(_rope_table(PAST_LEN + jnp.arange(ts)), (1, bs, 1))], axis=1)
    gq = jnp.tile(q_norm_g, (1, LANES // HEAD_DIM)).reshape(DEPTH, 1, LANES)
    gk = jnp.tile(k_norm_g, (1, LANES // HEAD_DIM)).reshape(DEPTH, 1, LANES)
    cache_k2 = cache_k.reshape(DEPTH, bs * WINDOW, D_KV)
    cache_v2 = cache_v.reshape(DEPTH, bs * WINDOW, D_KV)

    w_in_b, w_gate_b = w_in[:1].astype(BF16), w_gate[:1].astype(BF16)
    v0 = D_SSM + D_ATTN + D_KV

    def heads(t):
        return t.reshape(*t.shape[:-1], N_KV_HEADS, HEAD_DIM)

    def last_window(t, col0, col1):
        return jnp.stack([t[(b + 1) * tp - WINDOW:(b + 1) * tp, col0:col1] for b in range(bp)])

    outs = {k: [] for k in ("pk", "pv", "pre", "pim", "sk", "sv", "sre", "sim")}
    for l in range(DEPTH):
        proj, gates = _in_gate_call(x, l, norm1_g, modx, w_in_b, w_gate_b, b_gate, tm=tm)

        nxt = min(l + 1, DEPTH - 1)
        cast = [(w, l) for w in (w_proj_ssm, w_proj_attn, w_out, w_ffn_gate, w_ffn_up, w_ffn_down)]
        cast += [(w_in, nxt), (w_gate, nxt)]
        ssm_p, pre, pim, w_ps_b, w_pa_b, w_out_b, w_fg_b, w_fu_b, w_fd_b, w_in_b, w_gate_b = _ssm_call(
            proj, 0, bp, tp // S5_TILE, zeros_state, zeros_state, l, *ssm_w, sps=S5_SEG, cast=cast, name="s5_prompt")
        ssm_s, sre, sim = _ssm_call(proj, n_p, n_sample_tiles, 1, h0_re[l], h0_im[l], l, *ssm_w, sps=sps,
                                    name="s5_sample")
        attn_out, kn = _attn_call(proj, l, n_p, tp, attn_sink, rope_tab, gq, gk, cache_k2, cache_v2)

        mixed = _mix_call(ssm_p, ssm_s, attn_out, gates, w_ps_b, w_pa_b, tm=tm, tn=1024)
        x = _resid_mm_call(mixed, l, w_out_b, x, modx, 2, tm=tm, tn=1024, name="out_proj")

        act = _ffn_up_call(x, l, norm2_g, modx, w_fg_b, w_fu_b, tm=tm)
        last = l == DEPTH - 1
        x = _resid_mm_call(act, l, w_fd_b, x, modx, 5, n_split=n_p // tm if last else None, tm=tm,
                           name="ffn_down_split" if last else "ffn_down")

        outs["pk"].append(heads(last_window(kn, 0, D_KV)))
        outs["pv"].append(heads(last_window(proj, v0, IN_WIDTH)))
        outs["pre"].append(pre.reshape(bp, g, p))
        outs["pim"].append(pim.reshape(bp, g, p))
        outs["sk"].append(jnp.concatenate([cache_k[l][:, ts:], heads(kn[n_p:].reshape(bs, ts, D_KV))], axis=1))
        outs["sv"].append(jnp.concatenate([cache_v[l][:, ts:], heads(proj[n_p:, v0:].reshape(bs, ts, D_KV))],
                                          axis=1))
        outs["sre"].append(last_segment(sre))
        outs["sim"].append(last_segment(sim))

    y_p, y_s = x
    return (y_p.reshape(bp, tp, D_MODEL), y_s.reshape(bs, ts, D_MODEL),
            jnp.stack(outs["pk"]), jnp.stack(outs["pv"]), jnp.stack(outs["pre"]), jnp.stack(outs["pim"]),
            jnp.stack(outs["sk"]), jnp.stack(outs["sv"]), jnp.stack(outs["sre"]), jnp.stack(outs["sim"]))
```

```python
import functools
import math

import jax
import jax.numpy as jnp
from jax import lax
from jax.experimental import pallas as pl
from jax.experimental.pallas import tpu as pltpu

D_MODEL = 2048
DEPTH = 4
CHUNK = 64
D_SSM = 1024
SSM_GROUP = 16
N_SSM_GROUPS = 64
SSM_STATE = 64
HEAD_DIM = 64
N_HEADS = 16
N_KV_HEADS = 4
KV_REP = N_HEADS // N_KV_HEADS
D_ATTN = N_HEADS * HEAD_DIM
D_KV = N_KV_HEADS * HEAD_DIM
IN_WIDTH = D_SSM + D_ATTN + 2 * D_KV
WINDOW = 128
ROPE_DIM = 16
ROPE_THETA = 500000.0
D_FF = 5632
EPS = 1e-6
PAST_LEN = 2048

LANES = 128
SUBLANES = 8
MOD_BLOCK = CHUNK
STATE_W = N_SSM_GROUPS * SSM_STATE
SSM_KC = 4
VMEM_LIMIT = 56 * 1024 * 1024

F32 = jnp.float32
BF16 = jnp.bfloat16


def _cparams(sem):
    return pltpu.CompilerParams(dimension_semantics=sem, vmem_limit_bytes=VMEM_LIMIT)


def _sigmoid(x):
    return 0.5 + 0.5 * jnp.tanh(0.5 * x)


def _layer_spec(l, shape):
    zeros = (0,) * len(shape)
    return pl.BlockSpec((None, *shape), lambda *_: (l, *zeros))


def _mod_kernel(c_ref, w_ref, b_ref, o_ref):
    c = c_ref[...].astype(BF16)
    w = w_ref[...].astype(BF16)
    o_ref[...] = jnp.dot(c, w, preferred_element_type=F32) + b_ref[...]


def _mod_call(c_all, w_mod, b_mod):
    nb = c_all.shape[0]
    tn = 1024
    n_out = w_mod.shape[2]
    return pl.pallas_call(
        _mod_kernel,
        out_shape=jax.ShapeDtypeStruct((DEPTH, nb, n_out), F32),
        grid=(DEPTH, n_out // tn),
        in_specs=[
            pl.BlockSpec((nb, D_MODEL), lambda l, j: (0, 0)),
            pl.BlockSpec((None, D_MODEL, tn), lambda l, j: (l, 0, j)),
            pl.BlockSpec((None, 1, tn), lambda l, j: (l, 0, j)),
        ],
        out_specs=pl.BlockSpec((None, nb, tn), lambda l, j: (l, 0, j)),
        compiler_params=_cparams(("arbitrary", "arbitrary")),
        name="adaln_mod",
    )(c_all, w_mod, b_mod.reshape(DEPTH, 1, n_out))


S5_TILE = 256
S5_SEG = SUBLANES
S5_POWERS = (1, *(k * S5_TILE // S5_SEG for k in (1, 2, 4)))


def _s5_prep_kernel(are_ref, aim_ref, ldt_ref, bre_ref, bim_ref, pwr_ref, pwi_ref, bbr_ref, bbi_ref):
    a_re = are_ref[...]
    a_im = aim_ref[...]
    dt = jnp.exp(ldt_ref[...])
    z_re = a_re * dt
    z_im = a_im * dt
    for i, n in enumerate(S5_POWERS):
        mag = jnp.exp(z_re * float(n))
        pwr_ref[i] = mag * jnp.cos(z_im * float(n))
        pwi_ref[i] = mag * jnp.sin(z_im * float(n))
    l_re = pwr_ref[S5_POWERS.index(1)]
    l_im = pwi_ref[S5_POWERS.index(1)]
    den = a_re * a_re + a_im * a_im
    n_re = l_re - 1.0
    f_re = (n_re * a_re + l_im * a_im) / den
    f_im = (l_im * a_re - n_re * a_im) / den
    for c in range(SSM_GROUP):
        b_re = bre_ref[c]
        b_im = bim_ref[c]
        bbr_ref[c] = f_re * b_re - f_im * b_im
        bbi_ref[c] = f_re * b_im + f_im * b_re


def _s5_prep_call(a_re, a_im, log_dt, bt_re, bt_im):
    g, p = N_SSM_GROUPS, SSM_STATE
    mat = pl.BlockSpec((None, g, p), lambda l: (l, 0, 0))
    stack_c = pl.BlockSpec((None, SSM_GROUP, g, p), lambda l: (l, 0, 0, 0))
    n_pow = len(S5_POWERS)
    stack_n = pl.BlockSpec((None, n_pow, g, p), lambda l: (l, 0, 0, 0))
    return pl.pallas_call(
        _s5_prep_kernel,
        out_shape=(jax.ShapeDtypeStruct((DEPTH, n_pow, g, p), F32),
                   jax.ShapeDtypeStruct((DEPTH, n_pow, g, p), F32),
                   jax.ShapeDtypeStruct((DEPTH, SSM_GROUP, g, p), F32),
                   jax.ShapeDtypeStruct((DEPTH, SSM_GROUP, g, p), F32)),
        grid=(DEPTH,),
        in_specs=[mat, mat, pl.BlockSpec((None, g, 1), lambda l: (l, 0, 0)), stack_c, stack_c],
        out_specs=(stack_n, stack_n, stack_c, stack_c),
        compiler_params=_cparams(("arbitrary",)),
        name="s5_discretize",
    )(a_re, a_im, log_dt.reshape(DEPTH, g, 1), bt_re, bt_im)


def _norm_mod_chunk(x_ref, g_ref, sc_ref, sh_ref, h_scr, slot, chunk):
    gain = g_ref[...]
    per_chunk = x_ref.shape[0] // MOD_BLOCK
    for s in range(per_chunk):
        x = x_ref[s * MOD_BLOCK:(s + 1) * MOD_BLOCK, :]
        ms = jnp.mean(x * x, axis=-1, keepdims=True)
        y = x * lax.rsqrt(ms + EPS) * gain
        mod_row = pl.ds(chunk * per_chunk + s, 1)
        h = y * (1.0 + sc_ref[mod_row, :]) + sh_ref[mod_row, :]
        rows = pl.ds(pl.multiple_of((chunk * per_chunk + s) * MOD_BLOCK, MOD_BLOCK), MOD_BLOCK)
        h_scr[slot, rows, :] = h.astype(BF16)


def _norm_pipeline(l, n_tiles, tm, sc_idx, sh_idx, chunk_rows):
    n_chunks = tm // chunk_rows
    nsub = tm // MOD_BLOCK
    tile = lambda i: jnp.minimum(i, n_tiles - 1)
    specs = [
        pl.BlockSpec((chunk_rows, D_MODEL), lambda i, j: (tile(i) * n_chunks + jnp.minimum(j, n_chunks - 1), 0)),
        _layer_spec(l, (1, D_MODEL)),
        pl.BlockSpec((None, nsub, D_MODEL), lambda i, j: (l, tile(i), sc_idx)),
        pl.BlockSpec((None, nsub, D_MODEL), lambda i, j: (l, tile(i), sh_idx)),
    ]
    scratch = pltpu.VMEM((2, tm, D_MODEL), BF16)
    out_row = lambda i: jnp.maximum(i - 1, 0)
    out_col = lambda i, col: jnp.where(i == 0, 0, col)
    return specs, scratch, n_chunks, out_row, out_col


def _in_gate_kernel(x_ref, g_ref, sc_ref, sh_ref, wi_ref, wg_ref, b_ref, proj_ref, gate_ref, h_scr, *, n_in,
                    n_chunks):
    i, j = pl.program_id(0), pl.program_id(1)
    ready = lax.rem(i + 1, 2)

    filling = j < n_chunks

    def fill():
        _norm_mod_chunk(x_ref, g_ref, sc_ref, sh_ref, h_scr, lax.rem(i, 2), j)

    def gates():
        acc = jnp.dot(h_scr[ready], wg_ref[...], preferred_element_type=F32)
        gate_ref[...] = _sigmoid(acc + b_ref[...]).astype(gate_ref.dtype)

    pl.when(jnp.logical_and(i == 0, filling))(fill)

    @pl.when(jnp.logical_and(i > 0, j < n_in))
    def _():
        proj_ref[...] = jnp.dot(h_scr[ready], wi_ref[...], preferred_element_type=F32)
        fill()

    @pl.when(jnp.logical_and(i > 0, jnp.logical_and(j >= n_in, filling)))
    def _():
        gates()
        fill()

    pl.when(jnp.logical_and(i > 0, jnp.logical_not(filling)))(gates)


def _in_gate_call(x, l, gain, modx, w_in, w_gate, b_gate, *, tm=1024, tn_in=1280, tn_gate=1024):
    n_tiles = x.shape[0] // tm
    n_in, n_gate = w_in.shape[2] // tn_in, w_gate.shape[2] // tn_gate
    norm_specs, h_scratch, n_chunks, out_row, out_col = _norm_pipeline(l, n_tiles, tm, 1, 0, 4 * MOD_BLOCK)
    assert n_in <= n_chunks <= n_in + n_gate
    in_col = lambda j: jnp.minimum(j, n_in - 1)
    gate_col = lambda j: jnp.maximum(j - n_in, 0)
    return pl.pallas_call(
        functools.partial(_in_gate_kernel, n_in=n_in, n_chunks=n_chunks),
        out_shape=(jax.ShapeDtypeStruct((x.shape[0], w_in.shape[2]), F32),
                   jax.ShapeDtypeStruct((x.shape[0], w_gate.shape[2]), BF16)),
        grid=(n_tiles + 1, n_in + n_gate),
        in_specs=norm_specs + [
            pl.BlockSpec((None, D_MODEL, tn_in), lambda i, j: (0, 0, out_col(i, in_col(j)))),
            pl.BlockSpec((None, D_MODEL, tn_gate), lambda i, j: (0, 0, out_col(i, gate_col(j)))),
            pl.BlockSpec((None, 1, tn_gate), lambda i, j: (l, 0, out_col(i, gate_col(j)))),
        ],
        out_specs=(pl.BlockSpec((tm, tn_in), lambda i, j: (out_row(i), out_col(i, in_col(j)))),
                   pl.BlockSpec((tm, tn_gate), lambda i, j: (out_row(i), out_col(i, gate_col(j))))),
        scratch_shapes=[h_scratch],
        compiler_params=_cparams(("arbitrary", "arbitrary")),
        name="in_proj_gates",
    )(x, gain.reshape(DEPTH, 1, D_MODEL), modx, modx, w_in, w_gate, b_gate.reshape(DEPTH, 1, w_gate.shape[2]))


def _ffn_up_kernel(x_ref, g_ref, sc_ref, sh_ref, wg_ref, wu_ref, o_ref, h_scr, *, n_chunks):
    i, j = pl.program_id(0), pl.program_id(1)

    filling = j < n_chunks

    def fill():
        _norm_mod_chunk(x_ref, g_ref, sc_ref, sh_ref, h_scr, lax.rem(i, 2), j)

    def swiglu():
        h = h_scr[lax.rem(i + 1, 2)]
        gate = jnp.dot(h, wg_ref[...], preferred_element_type=F32)
        up = jnp.dot(h, wu_ref[...], preferred_element_type=F32)
        o_ref[...] = (gate * _sigmoid(gate) * up).astype(o_ref.dtype)

    pl.when(jnp.logical_and(i == 0, filling))(fill)

    @pl.when(jnp.logical_and(i > 0, filling))
    def _():
        swiglu()
        fill()

    pl.when(jnp.logical_and(i > 0, jnp.logical_not(filling)))(swiglu)


def _ffn_up_call(x, l, gain, modx, w_gate, w_up, *, tm=1024, tn=512):
    n_tiles = x.shape[0] // tm
    n_out = w_gate.shape[2]
    norm_specs, h_scratch, n_chunks, out_row, out_col = _norm_pipeline(l, n_tiles, tm, 4, 3, 2 * MOD_BLOCK)
    assert n_out // tn >= n_chunks
    w_spec = pl.BlockSpec((None, D_MODEL, tn), lambda i, j: (0, 0, out_col(i, j)))
    return pl.pallas_call(
        functools.partial(_ffn_up_kernel, n_chunks=n_chunks),
        out_shape=jax.ShapeDtypeStruct((x.shape[0], n_out), BF16),
        grid=(n_tiles + 1, n_out // tn),
        in_specs=norm_specs + [w_spec, w_spec],
        out_specs=pl.BlockSpec((tm, tn), lambda i, j: (out_row(i), out_col(i, j))),
        scratch_shapes=[h_scratch],
        compiler_params=_cparams(("arbitrary", "arbitrary")),
        name="ffn_up",
    )(x, gain.reshape(DEPTH, 1, D_MODEL), modx, modx, w_gate, w_up)


def _mix_kernel(sp_ref, ss_ref, a_ref, ga_ref, gb_ref, ws_ref, wa_ref, o_ref, *, n_split):
    def body(s_ref):
        ps = jnp.dot(s_ref[...], ws_ref[...], preferred_element_type=F32)
        pa = jnp.dot(a_ref[...], wa_ref[...], preferred_element_type=F32)
        o_ref[...] = (ga_ref[...] * ps + gb_ref[...] * pa).astype(o_ref.dtype)

    i = pl.program_id(0)
    pl.when(i < n_split)(lambda: body(sp_ref))
    pl.when(i >= n_split)(lambda: body(ss_ref))


def _mix_call(ssm_p, ssm_s, attn_out, gates, w_ps, w_pa, *, tm=1024, tn=512):
    n_tok = attn_out.shape[0]
    nj = D_MODEL // tn
    n_split = ssm_p.shape[0] // tm
    return pl.pallas_call(
        functools.partial(_mix_kernel, n_split=n_split),
        out_shape=jax.ShapeDtypeStruct((n_tok, D_MODEL), BF16),
        grid=(n_tok // tm, nj),
        in_specs=[
            pl.BlockSpec((tm, D_SSM), lambda i, j: (jnp.minimum(i, n_split - 1), 0)),
            pl.BlockSpec((tm, D_SSM), lambda i, j: (jnp.maximum(i - n_split, 0), 0)),
            pl.BlockSpec((tm, D_ATTN), lambda i, j: (i, 0)),
            pl.BlockSpec((tm, tn), lambda i, j: (i, j)),
            pl.BlockSpec((tm, tn), lambda i, j: (i, j + nj)),
            pl.BlockSpec((None, D_SSM, tn), lambda i, j: (0, 0, j)),
            pl.BlockSpec((None, D_ATTN, tn), lambda i, j: (0, 0, j)),
        ],
        out_specs=pl.BlockSpec((tm, tn), lambda i, j: (i, j)),
        compiler_params=_cparams(("arbitrary", "arbitrary")),
        name="branch_merge",
    )(ssm_p, ssm_s, attn_out, gates, gates, w_ps, w_pa)


def _resid_mm_kernel(a_ref, w_ref, x_ref, g_ref, *o_refs, nsub, n_split):
    acc = jnp.dot(a_ref[...], w_ref[...], preferred_element_type=F32)

    def write(o_ref):
        for s in range(nsub):
            rows = slice(s * MOD_BLOCK, (s + 1) * MOD_BLOCK)
            o_ref[rows, :] = x_ref[rows, :] + g_ref[s:s + 1, :] * acc[rows, :]

    if n_split is None:
        write(o_refs[0])
    else:
        i = pl.program_id(0)
        pl.when(i < n_split)(lambda: write(o_refs[0]))
        pl.when(i >= n_split)(lambda: write(o_refs[1]))


def _resid_mm_call(a, l, w, x, modx, g_idx, *, n_split=None, tm=1024, tn=512, name):
    n_tok, k = a.shape
    nsub = tm // MOD_BLOCK
    nj = D_MODEL // tn
    n_tiles = n_tok // tm
    if n_split is None:
        out_shape = jax.ShapeDtypeStruct((n_tok, D_MODEL), F32)
        out_specs = pl.BlockSpec((tm, tn), lambda i, j: (i, j))
    else:
        out_shape = (jax.ShapeDtypeStruct((n_split * tm, D_MODEL), F32),
                     jax.ShapeDtypeStruct(((n_tiles - n_split) * tm, D_MODEL), F32))
        out_specs = (
            pl.BlockSpec((tm, tn), lambda i, j: (jnp.minimum(i, n_split - 1), jnp.where(i < n_split, j, nj - 1))),
            pl.BlockSpec((tm, tn), lambda i, j: (jnp.maximum(i - n_split, 0), jnp.where(i < n_split, 0, j))),
        )
    return pl.pallas_call(
        functools.partial(_resid_mm_kernel, nsub=nsub, n_split=n_split),
        out_shape=out_shape,
        grid=(n_tiles, nj),
        in_specs=[
            pl.BlockSpec((tm, k), lambda i, j: (i, 0)),
            pl.BlockSpec((None, k, tn), lambda i, j: (0, 0, j)),
            pl.BlockSpec((tm, tn), lambda i, j: (i, j)),
            pl.BlockSpec((None, nsub, tn), lambda i, j: (l, i, g_idx * nj + j)),
        ],
        out_specs=out_specs,
        compiler_params=_cparams(("arbitrary", "arbitrary")),
        name=name,
    )(a, w, x, modx)


SCAN_LW = 512


def _ssm_kernel(*refs, tt, n_cast, sps):
    (u_ref, h0r_ref, h0i_ref, wbr_ref, wbi_ref, pwj_ref, seg_ref, wcr_ref, wci_ref, d_ref, wglu_ref,
     bglu_ref) = refs[:12]
    cast_in = refs[12:12 + n_cast]
    o_ref, htr_ref, hti_ref = refs[12 + n_cast:15 + n_cast]
    cast_out = refs[15 + n_cast:15 + 2 * n_cast]
    bur, bui, cre, cim, z_scr = refs[15 + 2 * n_cast:]
    steps = tt // S5_SEG
    for w_ref, wb_ref in zip(cast_in, cast_out):
        wb_ref[...] = w_ref[...].astype(BF16)

    carried = sps == S5_SEG
    if carried:
        @pl.when(pl.program_id(1) == 0)
        def _():
            cre[...] = jnp.broadcast_to(h0r_ref[...], (SUBLANES, STATE_W))
            cim[...] = jnp.broadcast_to(h0i_ref[...], (SUBLANES, STATE_W))

    r_idx = lax.broadcasted_iota(jnp.int32, (tt, tt), 0)
    c_idx = lax.broadcasted_iota(jnp.int32, (tt, tt), 1)
    to_scan = jnp.where(c_idx == (r_idx % S5_SEG) * steps + r_idx // S5_SEG, 1.0, 0.0).astype(BF16)
    to_time = jnp.where(r_idx == (c_idx % S5_SEG) * steps + c_idx // S5_SEG, 1.0, 0.0).astype(BF16)

    u = u_ref[...]
    us = jnp.dot(to_scan, u.astype(BF16), preferred_element_type=F32).astype(BF16)
    kw = D_SSM // SSM_KC
    sw = STATE_W // SSM_KC
    for kc in range(SSM_KC):
        uk = us[:, kc * kw:(kc + 1) * kw]
        bur[:, kc * sw:(kc + 1) * sw] = jnp.dot(uk, wbr_ref[kc], preferred_element_type=F32)
        bui[:, kc * sw:(kc + 1) * sw] = jnp.dot(uk, wbi_ref[kc], preferred_element_type=F32)

    row = lax.broadcasted_iota(jnp.int32, (SUBLANES, SCAN_LW), 0)
    for lc in range(STATE_W // SCAN_LW):
        sl = slice(lc * SCAN_LW, (lc + 1) * SCAN_LW)
        lam_re = jnp.broadcast_to(pwj_ref[0, 0:1, sl], (SUBLANES, SCAN_LW))
        lam_im = jnp.broadcast_to(pwj_ref[1, 0:1, sl], (SUBLANES, SCAN_LW))

        def step(j, carry, sl=sl, lam_re=lam_re, lam_im=lam_im):
            h_re, h_im = carry
            rows = pl.ds(pl.multiple_of(j * SUBLANES, SUBLANES), SUBLANES)
            return (lam_re * h_re - lam_im * h_im + bur[rows, sl],
                    lam_re * h_im + lam_im * h_re + bui[rows, sl])

        zero = jnp.zeros((SUBLANES, SCAN_LW), F32)
        e_re, e_im = lax.fori_loop(0, steps, step, (zero, zero), unroll=4)

        first = row % sps == 0
        in_re, in_im = (cre, cim) if carried else (h0r_ref, h0i_ref)
        x_re = jnp.where(first, in_re[:, sl], pltpu.roll(e_re, 1, 0))
        x_im = jnp.where(first, in_im[:, sl], pltpu.roll(e_im, 1, 0))
        for idx, k in enumerate((1, 2, 4)):
            if k >= sps:
                continue
            m_re = seg_ref[2 * idx, :, sl]
            m_im = seg_ref[2 * idx + 1, :, sl]
            s_re = pltpu.roll(x_re, k, 0)
            s_im = pltpu.roll(x_im, k, 0)
            x_re, x_im = (x_re + m_re * s_re - m_im * s_im,
                          x_im + m_re * s_im + m_im * s_re)
        pj_re = jnp.broadcast_to(pwj_ref[0, 1:2, sl], (SUBLANES, SCAN_LW))
        pj_im = jnp.broadcast_to(pwj_ref[1, 1:2, sl], (SUBLANES, SCAN_LW))
        n_re = pj_re * x_re - pj_im * x_im + e_re
        n_im = pj_re * x_im + pj_im * x_re + e_im
        if carried:
            cre[:, sl] = jnp.broadcast_to(n_re[SUBLANES - 1:SUBLANES, :], (SUBLANES, SCAN_LW))
            cim[:, sl] = jnp.broadcast_to(n_im[SUBLANES - 1:SUBLANES, :], (SUBLANES, SCAN_LW))
        else:
            htr_ref[:, sl] = n_re
            hti_ref[:, sl] = n_im

        def scan(j, carry, sl=sl, step=step):
            h_re, h_im = step(j, carry)
            rows = pl.ds(pl.multiple_of(j * SUBLANES, SUBLANES), SUBLANES)
            bur[rows, sl] = h_re
            bui[rows, sl] = h_im
            return h_re, h_im

        lax.fori_loop(0, steps, scan, (x_re, x_im), unroll=2)

    if carried:
        htr_ref[...] = cre[0:1, :]
        hti_ref[...] = cim[0:1, :]

    d = d_ref[...]
    for kc in range(SSM_KC):
        h_re = bur[:, kc * sw:(kc + 1) * sw].astype(BF16)
        h_im = bui[:, kc * sw:(kc + 1) * sw].astype(BF16)
        ys = (jnp.dot(h_re, wcr_ref[kc], preferred_element_type=F32)
              - jnp.dot(h_im, wci_ref[kc], preferred_element_type=F32))
        ys_hi = ys.astype(BF16)
        ys_lo = (ys - ys_hi.astype(F32)).astype(BF16)
        y = (jnp.dot(to_time, ys_hi, preferred_element_type=F32)
             + jnp.dot(to_time, ys_lo, preferred_element_type=F32))
        cols = slice(kc * kw, (kc + 1) * kw)
        y = y + d[:, cols] * u[:, cols]
        z_scr[:, cols] = 0.5 * y * (1.0 + jnp.tanh(math.sqrt(2.0 / math.pi) * (y + 0.044715 * (y * y * y))))
    z = z_scr[...]
    gate = _sigmoid(jnp.dot(z.astype(BF16), wglu_ref[...], preferred_element_type=F32) + bglu_ref[...])
    o_ref[...] = (z * gate).astype(o_ref.dtype)


def _ssm_call(proj, row0, n_seq, nt, h0_re, h0_im, l, wb_re, wb_im, pw_re, pw_im, wc_re, wc_im, d_skip,
              w_glu, b_glu, *, sps, cast=(), name):
    tt = S5_TILE
    steps = tt // S5_SEG
    power = lambda n: S5_POWERS.index(n)
    pwj = jnp.stack([pw[:, (power(1), power(steps)), :] for pw in (pw_re, pw_im)], axis=1)
    seg_rows = []
    for k in (1, 2, 4):
        keep = (jnp.arange(SUBLANES) % sps >= k)[None, :, None]
        seg_rows += [jnp.where(keep, pw[:, power(k * steps)][:, None, :], 0.0) for pw in (pw_re, pw_im)]
    seg = jnp.stack(seg_rows, axis=1)
    state_rows = h0_re.shape[1]
    assert state_rows == (1 if sps == S5_SEG else S5_SEG) and (sps == S5_SEG or nt == 1)
    t_len = nt * tt
    rb0 = row0 // tt
    n_steps = n_seq * nt
    state_spec = pl.BlockSpec((None, state_rows, STATE_W), lambda b, t: (b, 0, 0))
    cast_in, cast_out, cast_shapes = [], [], []
    for w, wl in cast:
        _, k, n = w.shape
        rows = k // n_steps
        cast_in.append(pl.BlockSpec((None, rows, n), lambda b, t, wl=wl: (wl, b * nt + t, 0)))
        cast_out.append(pl.BlockSpec((None, rows, n), lambda b, t: (0, b * nt + t, 0)))
        cast_shapes.append(jax.ShapeDtypeStruct((1, k, n), BF16))
    resident = lambda shape: pl.BlockSpec((None, *shape), lambda b, t: (l, *(0,) * len(shape)),
                                          pipeline_mode=pl.Buffered(1))
    return pl.pallas_call(
        functools.partial(_ssm_kernel, tt=tt, n_cast=len(cast), sps=sps),
        out_shape=(jax.ShapeDtypeStruct((n_seq * t_len, D_SSM), BF16),
                   jax.ShapeDtypeStruct((n_seq, state_rows, STATE_W), F32),
                   jax.ShapeDtypeStruct((n_seq, state_rows, STATE_W), F32),
                   *cast_shapes),
        grid=(n_seq, nt),
        in_specs=[
            pl.BlockSpec((tt, D_SSM), lambda b, t: (rb0 + b * nt + t, 0)),
            state_spec, state_spec,
            resident(wb_re.shape[1:]), resident(wb_im.shape[1:]),
            _layer_spec(l, pwj.shape[1:]), _layer_spec(l, seg.shape[1:]),
            resident(wc_re.shape[1:]), resident(wc_im.shape[1:]),
            _layer_spec(l, (1, D_SSM)),
            resident((D_SSM, D_SSM)),
            _layer_spec(l, (1, D_SSM)),
            *cast_in,
        ],
        out_specs=(pl.BlockSpec((tt, D_SSM), lambda b, t: (b * nt + t, 0)), state_spec, state_spec, *cast_out),
        scratch_shapes=[pltpu.VMEM((tt, STATE_W), F32), pltpu.VMEM((tt, STATE_W), F32),
                        pltpu.VMEM((SUBLANES, STATE_W), F32), pltpu.VMEM((SUBLANES, STATE_W), F32),
                        pltpu.VMEM((tt, D_SSM), F32)],
        compiler_params=_cparams(("arbitrary", "arbitrary")),
        name=name,
    )(proj, h0_re, h0_im, wb_re, wb_im, pwj, seg, wc_re, wc_im, d_skip.reshape(DEPTH, 1, D_SSM), w_glu,
      b_glu.reshape(DEPTH, 1, D_SSM), *[w for w, _ in cast])


RING = 3
N_KEYS = RING * CHUNK
ATT_CPS = 8


def _stack_blocks(x):
    return jnp.concatenate([x[:, j * LANES:(j + 1) * LANES] for j in range(x.shape[1] // LANES)], axis=0)


def _pair_norm_rope(xs, gain, tab, ones_bd):
    reps = xs.shape[0] // CHUNK
    cos, s_lo, s_hi = (jnp.concatenate([tab[i]] * reps, axis=0) for i in range(3))
    half = ROPE_DIM // 2
    sq = xs * xs
    sq_hi = sq.astype(BF16)
    sq_lo = (sq - sq_hi.astype(F32)).astype(BF16)
    ss = (jnp.dot(sq_hi, ones_bd, preferred_element_type=F32)
          + jnp.dot(sq_lo, ones_bd, preferred_element_type=F32))
    xg = xs * gain
    xr = xg * cos + pltpu.roll(xg, LANES - half, 1) * s_lo + pltpu.roll(xg, half, 1) * s_hi
    return xr * lax.rsqrt(ss * (1.0 / HEAD_DIM) + EPS)


def _attn_kernel(sink_ref, q_ref, k_ref, v_ref, ck_ref, cv_ref, tab_ref, gq_ref, gk_ref, o_ref, kn_ref, kd, vd,
                 *, l, n_prompt_steps, steps_per_seq):
    step = pl.program_id(0)
    is_sample = step >= n_prompt_steps
    c_base = lax.rem(step, steps_per_seq) * ATT_CPS
    first = lax.broadcasted_iota(jnp.int32, (1, LANES), 1) < HEAD_DIM
    ones_bd = jnp.where(lax.broadcasted_iota(jnp.int32, (LANES, LANES), 0) // HEAD_DIM
                        == lax.broadcasted_iota(jnp.int32, (LANES, LANES), 1) // HEAD_DIM, 1.0, 0.0).astype(BF16)
    key_slot = lax.broadcasted_iota(jnp.int32, (1, N_KEYS), 1) // CHUNK
    gq = gq_ref[...]
    gk = gk_ref[...]

    def store_dup(dst, x, rows):
        for pair in range(D_KV // LANES):
            blk = x[:, pair * LANES:(pair + 1) * LANES]
            swapped = pltpu.roll(blk, HEAD_DIM, 1)
            dst[2 * pair, rows, 0:LANES] = jnp.where(first, blk, swapped).astype(BF16)
            dst[2 * pair + 1, rows, 0:LANES] = jnp.where(first, swapped, blk).astype(BF16)

    @pl.when(step == 0)
    def _():
        vd[:, :, LANES:] = jnp.ones((N_KV_HEADS, N_KEYS, LANES), BF16)

    prev = slice(CHUNK, N_KEYS)

    def chunk(ci, from_cache):
        c = 0 if from_cache else c_base + ci
        rows = pl.ds(pl.multiple_of(ci * CHUNK, CHUNK), CHUNK)
        if from_cache:
            cached = pl.ds(pl.multiple_of(ci * WINDOW, WINDOW), WINDOW)
            store_dup(kd, ck_ref[cached, :], prev)
            store_dup(vd, cv_ref[cached, :], prev)

        tab = tab_ref[:, rows, :]
        own = (slice(0, CHUNK) if from_cache else
               pl.ds(pl.multiple_of(lax.rem(c, RING) * CHUNK, CHUNK), CHUNK))
        kn = _pair_norm_rope(_stack_blocks(k_ref[rows, :]), gk, tab, ones_bd)
        kn = jnp.concatenate([kn[0:CHUNK], kn[CHUNK:2 * CHUNK]], axis=1)
        kn_ref[rows, :] = kn
        store_dup(kd, kn, own)
        store_dup(vd, v_ref[rows, :], own)

        qn = _pair_norm_rope(_stack_blocks(q_ref[rows, :]), gq, tab, ones_bd) * (HEAD_DIM ** -0.5)
        q_lo = jnp.where(first, qn, 0.0).astype(BF16)
        q_hi = jnp.where(first, 0.0, qn).astype(BF16)

        valid = None if from_cache else key_slot <= c
        scores = []
        for kh in range(N_KV_HEADS):
            r0 = kh * 2 * CHUNK
            lhs = jnp.concatenate([q_lo[r0:r0 + CHUNK], q_hi[r0:r0 + CHUNK],
                                   q_lo[r0 + CHUNK:r0 + 2 * CHUNK], q_hi[r0 + CHUNK:r0 + 2 * CHUNK]], axis=0)
            scores.append(lax.dot_general(lhs, kd[kh], (((1,), (1,)), ((), ())),
                                          preferred_element_type=F32))
        weighted, sink_terms = [], []
        for kh in range(N_KV_HEADS):
            s = scores[kh] if from_cache else jnp.where(valid, scores[kh], -1e30)
            sink = jnp.concatenate([jnp.full((CHUNK, 1), sink_ref[l, kh * KV_REP + r], F32)
                                    for r in range(KV_REP)], axis=0)
            m = jnp.maximum(jnp.max(s, axis=-1, keepdims=True), sink)
            p = jnp.exp(s - m).astype(BF16)
            sink_terms.append(jnp.exp(sink - m))
            weighted.append(jnp.dot(p, vd[kh], preferred_element_type=F32))
        for kh in range(N_KV_HEADS):
            od = weighted[kh]
            o = od[:, 0:LANES] / (od[:, LANES:] + sink_terms[kh])
            for pair in range(2):
                half = pair * 2 * CHUNK
                blk = jnp.where(first, o[half:half + CHUNK], o[half + CHUNK:half + 2 * CHUNK])
                col = (2 * kh + pair) * LANES
                o_ref[rows, col:col + LANES] = blk.astype(o_ref.dtype)

    @pl.when(jnp.logical_not(is_sample))
    def _():
        @pl.when(c_base == 0)
        def _():
            kd[:, prev, :] = jnp.zeros((N_KV_HEADS, N_KEYS - CHUNK, LANES), BF16)
            vd[:, prev, 0:LANES] = jnp.zeros((N_KV_HEADS, N_KEYS - CHUNK, LANES), BF16)

        lax.fori_loop(0, ATT_CPS, lambda ci, carry: (chunk(ci, False), carry)[1], 0)

    @pl.when(is_sample)
    def _():
        lax.fori_loop(0, ATT_CPS, lambda ci, carry: (chunk(ci, True), carry)[1], 0)


def _attn_call(proj, l, n_prompt, t_prompt, sink, rope_tab, gq, gk, cache_k, cache_v):
    n_tok = proj.shape[0]
    tr = ATT_CPS * CHUNK
    qcol = D_SSM // D_ATTN
    kcol = (D_SSM + D_ATTN) // D_KV
    n_prompt_steps = n_prompt // tr
    cache_spec = pl.BlockSpec((None, ATT_CPS * WINDOW, D_KV),
                              lambda s: (l, jnp.maximum(s - n_prompt_steps, 0), 0))
    return pl.pallas_call(
        functools.partial(_attn_kernel, l=l, n_prompt_steps=n_prompt_steps, steps_per_seq=t_prompt // tr),
        out_shape=(jax.ShapeDtypeStruct((n_tok, D_ATTN), BF16),
                   jax.ShapeDtypeStruct((n_tok, D_KV), F32)),
        grid=(n_tok // tr,),
        in_specs=[
            pl.BlockSpec(memory_space=pltpu.SMEM),
            pl.BlockSpec((tr, D_ATTN), lambda s: (s, qcol)),
            pl.BlockSpec((tr, D_KV), lambda s: (s, kcol)),
            pl.BlockSpec((tr, D_KV), lambda s: (s, kcol + 1)),
            cache_spec, cache_spec,
            pl.BlockSpec((3, tr, LANES), lambda s: (0, s, 0)),
            _layer_spec(l, (1, LANES)), _layer_spec(l, (1, LANES)),
        ],
        out_specs=(pl.BlockSpec((tr, D_ATTN), lambda s: (s, 0)),
                   pl.BlockSpec((tr, D_KV), lambda s: (s, 0))),
        scratch_shapes=[pltpu.VMEM((N_KV_HEADS, N_KEYS, LANES), BF16),
                        pltpu.VMEM((N_KV_HEADS, N_KEYS, 2 * LANES), BF16)],
        compiler_params=_cparams(("arbitrary",)),
        name="banded_attn",
    )(sink, proj, proj, proj, cache_k, cache_v, rope_tab, gq, gk)


def _block_diag_rows(x, n):
    *lead, r, width = x.shape
    keep = jnp.arange(n)[:, None, None] == (jnp.arange(width) // (width // n))[None, None, :]
    return jnp.where(keep, x[..., None, :, :], 0.0).reshape(*lead, n * r, width)


def _rope_table(pos):
    half = ROPE_DIM // 2
    n_pos = pos.shape[0]
    inv_freq = ROPE_THETA ** (-jnp.arange(half, dtype=F32) / half)
    ang = pos.astype(F32)[:, None] * inv_freq[None, :]
    cos, sin = jnp.cos(ang), jnp.sin(ang)
    ones = jnp.ones((n_pos, HEAD_DIM - ROPE_DIM), F32)
    zeros = jnp.zeros((n_pos, HEAD_DIM - half), F32)
    c_tab = jnp.concatenate([cos, cos, ones], axis=1)
    lo_tab = jnp.concatenate([-sin, zeros], axis=1)
    hi_tab = jnp.concatenate([jnp.zeros((n_pos, half), F32), sin, jnp.zeros((n_pos, HEAD_DIM - ROPE_DIM), F32)],
                             axis=1)
    tab = jnp.stack([c_tab, lo_tab, hi_tab])
    return jnp.concatenate([tab, tab], axis=2)


def kernel(x_prompt, x_sample, cache_k, cache_v, state_ssm_re, state_ssm_im, c_prompt, c_sample, w_mod, b_mod, norm1_g, norm2_g, w_in, ssm_a_re, ssm_a_im, ssm_log_dt, ssm_b_re, ssm_b_im, ssm_c_re, ssm_c_im, ssm_d, w_glu, b_glu, q_norm_g, k_norm_g, attn_sink, w_gate, b_gate, w_proj_ssm, w_proj_attn, w_out, w_ffn_gate, w_ffn_up, w_ffn_down):
    bp, tp, _ = x_prompt.shape
    bs, ts, _ = x_sample.shape
    assert ts == CHUNK and tp % (ATT_CPS * CHUNK) == 0 and bs % ATT_CPS == 0
    assert tp % S5_TILE == 0 and S5_TILE % ts == 0 and (bs * ts) % S5_TILE == 0
    n_p, n_s = bp * tp, bs * ts
    tm = 1024
    assert n_p % tm == 0 and n_s % tm == 0
    x = jnp.concatenate([x_prompt.reshape(n_p, D_MODEL), x_sample.reshape(n_s, D_MODEL)], axis=0)

    n_cond = bp + bs
    pad = (-n_cond) % SUBLANES
    c_all = jnp.concatenate([c_prompt, c_sample, jnp.zeros((pad, D_MODEL), F32)], axis=0)
    mod = _mod_call(c_all, w_mod, b_mod)
    n_mod = mod.shape[2]

    def per_block(m, n_seq, t_len):
        reps = t_len // MOD_BLOCK
        return jnp.broadcast_to(m[:, :, None, :], (DEPTH, n_seq, reps, n_mod)).reshape(DEPTH, n_seq * reps, n_mod)

    modx = jnp.concatenate([per_block(mod[:, :bp], bp, tp), per_block(mod[:, bp:n_cond], bs, ts)],
                           axis=1)

    g, p = N_SSM_GROUPS, SSM_STATE
    gpc = g // SSM_KC
    pw_re, pw_im, bb_re, bb_im = _s5_prep_call(ssm_a_re, ssm_a_im, ssm_log_dt,
                                               ssm_b_re.transpose(0, 3, 1, 2), ssm_b_im.transpose(0, 3, 1, 2))
    pw_re = pw_re.reshape(DEPTH, len(S5_POWERS), STATE_W)
    pw_im = pw_im.reshape(DEPTH, len(S5_POWERS), STATE_W)

    def wb_blocks(bb):
        per_chunk = bb.reshape(DEPTH, SSM_GROUP, SSM_KC, gpc * p).transpose(0, 2, 1, 3)
        return _block_diag_rows(per_chunk, gpc).astype(BF16)

    def wc_blocks(cc):
        per_chunk = cc.reshape(DEPTH, SSM_KC, gpc, SSM_GROUP, p).transpose(0, 1, 3, 2, 4)
        transposed = _block_diag_rows(per_chunk.reshape(DEPTH, SSM_KC, SSM_GROUP, gpc * p), gpc)
        return jnp.swapaxes(transposed.astype(BF16), -1, -2)

    ssm_w = (wb_blocks(bb_re), wb_blocks(bb_im), pw_re, pw_im, wc_blocks(ssm_c_re), wc_blocks(ssm_c_im), ssm_d,
             w_glu.astype(BF16), b_glu)
    zeros_state = jnp.zeros((bp, 1, STATE_W), F32)
    sps = ts * S5_SEG // S5_TILE
    n_sample_tiles = n_s // S5_TILE

    def per_segment(state):
        rows = jnp.repeat(state.reshape(DEPTH, bs, STATE_W), sps, axis=1)
        return rows.reshape(DEPTH, n_sample_tiles, S5_SEG, STATE_W)

    def last_segment(state):
        return state.reshape(bs, sps, STATE_W)[:, sps - 1].reshape(bs, g, p)

    h0_re, h0_im = per_segment(state_ssm_re), per_segment(state_ssm_im)

    rope_tab = jnp.concatenate([jnp.tile(_rope_table(jnp.arange(tp)), (1, bp, 1)),
                                jnp.tile(_rope_table(PAST_LEN + jnp.arange(ts)), (1, bs, 1))], axis=1)
    gq = jnp.tile(q_norm_g, (1, LANES // HEAD_DIM)).reshape(DEPTH, 1, LANES)
    gk = jnp.tile(k_norm_g, (1, LANES // HEAD_DIM)).reshape(DEPTH, 1, LANES)
    cache_k2 = cache_k.reshape(DEPTH, bs * WINDOW, D_KV)
    cache_v2 = cache_v.reshape(DEPTH, bs * WINDOW, D_KV)

    w_in_b, w_gate_b = w_in[:1].astype(BF16), w_gate[:1].astype(BF16)
    v0 = D_SSM + D_ATTN + D_KV

    def heads(t):
        return t.reshape(*t.shape[:-1], N_KV_HEADS, HEAD_DIM)

    def last_window(t, col0, col1):
        return jnp.stack([t[(b + 1) * tp - WINDOW:(b + 1) * tp, col0:col1] for b in range(bp)])

    outs = {k: [] for k in ("pk", "pv", "pre", "pim", "sk", "sv", "sre", "sim")}
    for l in range(DEPTH):
        proj, gates = _in_gate_call(x, l, norm1_g, modx, w_in_b, w_gate_b, b_gate, tm=tm)

        nxt = min(l + 1, DEPTH - 1)
        cast = [(w, l) for w in (w_proj_ssm, w_proj_attn, w_out, w_ffn_gate, w_ffn_up, w_ffn_down)]
        cast += [(w_in, nxt), (w_gate, nxt)]
        ssm_p, pre, pim, w_ps_b, w_pa_b, w_out_b, w_fg_b, w_fu_b, w_fd_b, w_in_b, w_gate_b = _ssm_call(
            proj, 0, bp, tp // S5_TILE, zeros_state, zeros_state, l, *ssm_w, sps=S5_SEG, cast=cast, name="s5_prompt")
        ssm_s, sre, sim = _ssm_call(proj, n_p, n_sample_tiles, 1, h0_re[l], h0_im[l], l, *ssm_w, sps=sps,
                                    name="s5_sample")
        attn_out, kn = _attn_call(proj, l, n_p, tp, attn_sink, rope_tab, gq, gk, cache_k2, cache_v2)

        mixed = _mix_call(ssm_p, ssm_s, attn_out, gates, w_ps_b, w_pa_b, tm=tm, tn=1024)
        x = _resid_mm_call(mixed, l, w_out_b, x, modx, 2, tm=tm, tn=1024, name="out_proj")

        act = _ffn_up_call(x, l, norm2_g, modx, w_fg_b, w_fu_b, tm=tm)
        last = l == DEPTH - 1
        x = _resid_mm_call(act, l, w_fd_b, x, modx, 5, n_split=n_p // tm if last else None, tm=tm,
                           name="ffn_down_split" if last else "ffn_down")

        outs["pk"].append(heads(last_window(kn, 0, D_KV)))
        outs["pv"].append(heads(last_window(proj, v0, IN_WIDTH)))
        outs["pre"].append(pre.reshape(bp, g, p))
        outs["pim"].append(pim.reshape(bp, g, p))
        outs["sk"].append(jnp.concatenate([cache_k[l][:, ts:], heads(kn[n_p:].reshape(bs, ts, D_KV))], axis=1))
        outs["sv"].append(jnp.concatenate([cache_v[l][:, ts:], heads(proj[n_p:, v0:].reshape(bs, ts, D_KV))],
                                          axis=1))
        outs["sre"].append(last_segment(sre))
        outs["sim"].append(last_segment(sim))

    y_p, y_s = x
    return (y_p.reshape(bp, tp, D_MODEL), y_s.reshape(bs, ts, D_MODEL),
            jnp.stack(outs["pk"]), jnp.stack(outs["pv"]), jnp.stack(outs["pre"]), jnp.stack(outs["pim"]),
            jnp.stack(outs["sk"]), jnp.stack(outs["sv"]), jnp.stack(outs["sre"]), jnp.stack(outs["sim"]))
```

```python
import functools
import math

import jax
import jax.numpy as jnp
from jax import lax
from jax.experimental import pallas as pl
from jax.experimental.pallas import tpu as pltpu

D_MODEL = 2048
DEPTH = 4
CHUNK = 64
D_SSM = 1024
SSM_GROUP = 16
N_SSM_GROUPS = 64
SSM_STATE = 64
HEAD_DIM = 64
N_HEADS = 16
N_KV_HEADS = 4
KV_REP = N_HEADS // N_KV_HEADS
D_ATTN = N_HEADS * HEAD_DIM
D_KV = N_KV_HEADS * HEAD_DIM
IN_WIDTH = D_SSM + D_ATTN + 2 * D_KV
WINDOW = 128
ROPE_DIM = 16
ROPE_THETA = 500000.0
D_FF = 5632
EPS = 1e-6
PAST_LEN = 2048

LANES = 128
SUBLANES = 8
MOD_BLOCK = CHUNK
STATE_W = N_SSM_GROUPS * SSM_STATE
SSM_KC = 4
VMEM_LIMIT = 56 * 1024 * 1024

F32 = jnp.float32
BF16 = jnp.bfloat16


def _cparams(sem):
    return pltpu.CompilerParams(dimension_semantics=sem, vmem_limit_bytes=VMEM_LIMIT)


def _sigmoid(x):
    return 0.5 + 0.5 * jnp.tanh(0.5 * x)


def _layer_spec(l, shape):
    zeros = (0,) * len(shape)
    return pl.BlockSpec((None, *shape), lambda *_: (l, *zeros))


def _mod_kernel(c_ref, w_ref, b_ref, o_ref):
    c = c_ref[...].astype(BF16)
    w = w_ref[...].astype(BF16)
    o_ref[...] = jnp.dot(c, w, preferred_element_type=F32) + b_ref[...]


def _mod_call(c_all, w_mod, b_mod, n_layers):
    nb = c_all.shape[0]
    tn = 1024
    n_out = w_mod.shape[2]
    return pl.pallas_call(
        _mod_kernel,
        out_shape=jax.ShapeDtypeStruct((n_layers, nb, n_out), F32),
        grid=(n_layers, n_out // tn),
        in_specs=[
            pl.BlockSpec((nb, D_MODEL), lambda l, j: (0, 0)),
            pl.BlockSpec((None, D_MODEL, tn), lambda l, j: (l, 0, j)),
            pl.BlockSpec((None, 1, tn), lambda l, j: (l, 0, j)),
        ],
        out_specs=pl.BlockSpec((None, nb, tn), lambda l, j: (l, 0, j)),
        compiler_params=_cparams(("arbitrary", "arbitrary")),
        name="adaln_mod",
    )(c_all, w_mod, b_mod.reshape(DEPTH, 1, n_out))


S5_TILE = 256
S5_SEG = SUBLANES
S5_POWERS = (1, *(k * S5_TILE // S5_SEG for k in (1, 2, 4)))


def _s5_prep_kernel(are_ref, aim_ref, ldt_ref, bre_ref, bim_ref, pwr_ref, pwi_ref, bbr_ref, bbi_ref):
    a_re = are_ref[...]
    a_im = aim_ref[...]
    dt = jnp.exp(ldt_ref[...])
    z_re = a_re * dt
    z_im = a_im * dt
    for i, n in enumerate(S5_POWERS):
        mag = jnp.exp(z_re * float(n))
        pwr_ref[i] = mag * jnp.cos(z_im * float(n))
        pwi_ref[i] = mag * jnp.sin(z_im * float(n))
    l_re = pwr_ref[S5_POWERS.index(1)]
    l_im = pwi_ref[S5_POWERS.index(1)]
    den = a_re * a_re + a_im * a_im
    n_re = l_re - 1.0
    f_re = (n_re * a_re + l_im * a_im) / den
    f_im = (l_im * a_re - n_re * a_im) / den
    for c in range(SSM_GROUP):
        b_re = bre_ref[c]
        b_im = bim_ref[c]
        bbr_ref[c] = f_re * b_re - f_im * b_im
        bbi_ref[c] = f_re * b_im + f_im * b_re


def _s5_prep_call(a_re, a_im, log_dt, bt_re, bt_im):
    g, p = N_SSM_GROUPS, SSM_STATE
    mat = pl.BlockSpec((None, g, p), lambda l: (l, 0, 0))
    stack_c = pl.BlockSpec((None, SSM_GROUP, g, p), lambda l: (l, 0, 0, 0))
    n_pow = len(S5_POWERS)
    stack_n = pl.BlockSpec((None, n_pow, g, p), lambda l: (l, 0, 0, 0))
    return pl.pallas_call(
        _s5_prep_kernel,
        out_shape=(jax.ShapeDtypeStruct((DEPTH, n_pow, g, p), F32),
                   jax.ShapeDtypeStruct((DEPTH, n_pow, g, p), F32),
                   jax.ShapeDtypeStruct((DEPTH, SSM_GROUP, g, p), F32),
                   jax.ShapeDtypeStruct((DEPTH, SSM_GROUP, g, p), F32)),
        grid=(DEPTH,),
        in_specs=[mat, mat, pl.BlockSpec((None, g, 1), lambda l: (l, 0, 0)), stack_c, stack_c],
        out_specs=(stack_n, stack_n, stack_c, stack_c),
        compiler_params=_cparams(("arbitrary",)),
        name="s5_discretize",
    )(a_re, a_im, log_dt.reshape(DEPTH, g, 1), bt_re, bt_im)


def _norm_mod_chunk(x_ref, g_ref, sc_ref, sh_ref, h_scr, slot, chunk):
    gain = g_ref[...]
    per_chunk = x_ref.shape[0] // MOD_BLOCK
    for s in range(per_chunk):
        x = x_ref[s * MOD_BLOCK:(s + 1) * MOD_BLOCK, :]
        ms = jnp.mean(x * x, axis=-1, keepdims=True)
        y = x * lax.rsqrt(ms + EPS) * gain
        mod_row = pl.ds(chunk * per_chunk + s, 1)
        h = y * (1.0 + sc_ref[mod_row, :]) + sh_ref[mod_row, :]
        rows = pl.ds(pl.multiple_of((chunk * per_chunk + s) * MOD_BLOCK, MOD_BLOCK), MOD_BLOCK)
        h_scr[slot, rows, :] = h.astype(BF16)


def _norm_pipeline(l, n_tiles, tm, sc_idx, sh_idx, chunk_rows):
    n_chunks = tm // chunk_rows
    nsub = tm // MOD_BLOCK
    tile = lambda i: jnp.minimum(i, n_tiles - 1)
    specs = [
        pl.BlockSpec((chunk_rows, D_MODEL), lambda i, j: (tile(i) * n_chunks + jnp.minimum(j, n_chunks - 1), 0)),
        _layer_spec(l, (1, D_MODEL)),
        pl.BlockSpec((None, nsub, D_MODEL), lambda i, j: (0, tile(i), sc_idx)),
        pl.BlockSpec((None, nsub, D_MODEL), lambda i, j: (0, tile(i), sh_idx)),
    ]
    scratch = pltpu.VMEM((2, tm, D_MODEL), BF16)
    out_row = lambda i: jnp.maximum(i - 1, 0)
    out_col = lambda i, col: jnp.where(i == 0, 0, col)
    return specs, scratch, n_chunks, out_row, out_col


def _in_gate_kernel(x_ref, g_ref, sc_ref, sh_ref, wi_ref, wg_ref, b_ref, proj_ref, gate_ref, h_scr, *, n_in,
                    n_chunks):
    i, j = pl.program_id(0), pl.program_id(1)
    ready = lax.rem(i + 1, 2)

    filling = j < n_chunks

    def fill():
        _norm_mod_chunk(x_ref, g_ref, sc_ref, sh_ref, h_scr, lax.rem(i, 2), j)

    def gates():
        acc = jnp.dot(h_scr[ready], wg_ref[...], preferred_element_type=F32)
        gate_ref[...] = _sigmoid(acc + b_ref[...]).astype(gate_ref.dtype)

    pl.when(jnp.logical_and(i == 0, filling))(fill)

    @pl.when(jnp.logical_and(i > 0, j < n_in))
    def _():
        proj_ref[...] = jnp.dot(h_scr[ready], wi_ref[...], preferred_element_type=F32)
        fill()

    @pl.when(jnp.logical_and(i > 0, jnp.logical_and(j >= n_in, filling)))
    def _():
        gates()
        fill()

    pl.when(jnp.logical_and(i > 0, jnp.logical_not(filling)))(gates)


def _in_gate_call(x, l, gain, modx, w_in, w_gate, b_gate, *, tm=1024, tn_in=1280, tn_gate=1024):
    n_tiles = x.shape[0] // tm
    n_in, n_gate = w_in.shape[2] // tn_in, w_gate.shape[2] // tn_gate
    norm_specs, h_scratch, n_chunks, out_row, out_col = _norm_pipeline(l, n_tiles, tm, 1, 0, 4 * MOD_BLOCK)
    assert n_in <= n_chunks <= n_in + n_gate
    in_col = lambda j: jnp.minimum(j, n_in - 1)
    gate_col = lambda j: jnp.maximum(j - n_in, 0)
    return pl.pallas_call(
        functools.partial(_in_gate_kernel, n_in=n_in, n_chunks=n_chunks),
        out_shape=(jax.ShapeDtypeStruct((x.shape[0], w_in.shape[2]), F32),
                   jax.ShapeDtypeStruct((x.shape[0], w_gate.shape[2]), BF16)),
        grid=(n_tiles + 1, n_in + n_gate),
        in_specs=norm_specs + [
            pl.BlockSpec((None, D_MODEL, tn_in), lambda i, j: (0, 0, out_col(i, in_col(j)))),
            pl.BlockSpec((None, D_MODEL, tn_gate), lambda i, j: (0, 0, out_col(i, gate_col(j)))),
            pl.BlockSpec((None, 1, tn_gate), lambda i, j: (l, 0, out_col(i, gate_col(j)))),
        ],
        out_specs=(pl.BlockSpec((tm, tn_in), lambda i, j: (out_row(i), out_col(i, in_col(j)))),
                   pl.BlockSpec((tm, tn_gate), lambda i, j: (out_row(i), out_col(i, gate_col(j))))),
        scratch_shapes=[h_scratch],
        compiler_params=_cparams(("arbitrary", "arbitrary")),
        name="in_proj_gates",
    )(x, gain.reshape(DEPTH, 1, D_MODEL), modx, modx, w_in, w_gate, b_gate.reshape(DEPTH, 1, w_gate.shape[2]))


def _ffn_up_kernel(x_ref, g_ref, sc_ref, sh_ref, wg_ref, wu_ref, o_ref, h_scr, *, n_chunks):
    i, j = pl.program_id(0), pl.program_id(1)

    filling = j < n_chunks

    def fill():
        _norm_mod_chunk(x_ref, g_ref, sc_ref, sh_ref, h_scr, lax.rem(i, 2), j)

    def swiglu():
        h = h_scr[lax.rem(i + 1, 2)]
        gate = jnp.dot(h, wg_ref[...], preferred_element_type=F32)
        up = jnp.dot(h, wu_ref[...], preferred_element_type=F32)
        o_ref[...] = (gate * _sigmoid(gate) * up).astype(o_ref.dtype)

    pl.when(jnp.logical_and(i == 0, filling))(fill)

    @pl.when(jnp.logical_and(i > 0, filling))
    def _():
        swiglu()
        fill()

    pl.when(jnp.logical_and(i > 0, jnp.logical_not(filling)))(swiglu)


def _ffn_up_call(x, l, gain, modx, w_gate, w_up, *, tm=1024, tn=512):
    n_tiles = x.shape[0] // tm
    n_out = w_gate.shape[2]
    norm_specs, h_scratch, n_chunks, out_row, out_col = _norm_pipeline(l, n_tiles, tm, 4, 3, 2 * MOD_BLOCK)
    assert n_out // tn >= n_chunks
    w_spec = pl.BlockSpec((None, D_MODEL, tn), lambda i, j: (0, 0, out_col(i, j)))
    return pl.pallas_call(
        functools.partial(_ffn_up_kernel, n_chunks=n_chunks),
        out_shape=jax.ShapeDtypeStruct((x.shape[0], n_out), BF16),
        grid=(n_tiles + 1, n_out // tn),
        in_specs=norm_specs + [w_spec, w_spec],
        out_specs=pl.BlockSpec((tm, tn), lambda i, j: (out_row(i), out_col(i, j))),
        scratch_shapes=[h_scratch],
        compiler_params=_cparams(("arbitrary", "arbitrary")),
        name="ffn_up",
    )(x, gain.reshape(DEPTH, 1, D_MODEL), modx, modx, w_gate, w_up)


def _mix_kernel(sp_ref, ss_ref, a_ref, ga_ref, gb_ref, ws_ref, wa_ref, o_ref, *, n_split):
    def body(s_ref):
        ps = jnp.dot(s_ref[...], ws_ref[...], preferred_element_type=F32)
        pa = jnp.dot(a_ref[...], wa_ref[...], preferred_element_type=F32)
        o_ref[...] = (ga_ref[...] * ps + gb_ref[...] * pa).astype(o_ref.dtype)

    i = pl.program_id(0)
    pl.when(i < n_split)(lambda: body(sp_ref))
    pl.when(i >= n_split)(lambda: body(ss_ref))


def _mix_call(ssm_p, ssm_s, attn_out, gates, w_ps, w_pa, *, tm=1024, tn=512):
    n_tok = attn_out.shape[0]
    nj = D_MODEL // tn
    n_split = ssm_p.shape[0] // tm
    return pl.pallas_call(
        functools.partial(_mix_kernel, n_split=n_split),
        out_shape=jax.ShapeDtypeStruct((n_tok, D_MODEL), BF16),
        grid=(n_tok // tm, nj),
        in_specs=[
            pl.BlockSpec((tm, D_SSM), lambda i, j: (jnp.minimum(i, n_split - 1), 0)),
            pl.BlockSpec((tm, D_SSM), lambda i, j: (jnp.maximum(i - n_split, 0), 0)),
            pl.BlockSpec((tm, D_ATTN), lambda i, j: (i, 0)),
            pl.BlockSpec((tm, tn), lambda i, j: (i, j)),
            pl.BlockSpec((tm, tn), lambda i, j: (i, j + nj)),
            pl.BlockSpec((None, D_SSM, tn), lambda i, j: (0, 0, j)),
            pl.BlockSpec((None, D_ATTN, tn), lambda i, j: (0, 0, j)),
        ],
        out_specs=pl.BlockSpec((tm, tn), lambda i, j: (i, j)),
        compiler_params=_cparams(("arbitrary", "arbitrary")),
        name="branch_merge",
    )(ssm_p, ssm_s, attn_out, gates, gates, w_ps, w_pa)


def _resid_mm_kernel(a_ref, w_ref, x_ref, g_ref, *o_refs, nsub, n_split):
    acc = jnp.dot(a_ref[...], w_ref[...], preferred_element_type=F32)

    def write(o_ref):
        for s in range(nsub):
            rows = slice(s * MOD_BLOCK, (s + 1) * MOD_BLOCK)
            o_ref[rows, :] = x_ref[rows, :] + g_ref[s:s + 1, :] * acc[rows, :]

    if n_split is None:
        write(o_refs[0])
    else:
        i = pl.program_id(0)
        pl.when(i < n_split)(lambda: write(o_refs[0]))
        pl.when(i >= n_split)(lambda: write(o_refs[1]))


def _resid_mm_call(a, l, w, x, modx, g_idx, *, n_split=None, tm=1024, tn=512, name):
    n_tok, k = a.shape
    nsub = tm // MOD_BLOCK
    nj = D_MODEL // tn
    n_tiles = n_tok // tm
    if n_split is None:
        out_shape = jax.ShapeDtypeStruct((n_tok, D_MODEL), F32)
        out_specs = pl.BlockSpec((tm, tn), lambda i, j: (i, j))
    else:
        out_shape = (jax.ShapeDtypeStruct((n_split * tm, D_MODEL), F32),
                     jax.ShapeDtypeStruct(((n_tiles - n_split) * tm, D_MODEL), F32))
        out_specs = (
            pl.BlockSpec((tm, tn), lambda i, j: (jnp.minimum(i, n_split - 1), jnp.where(i < n_split, j, nj - 1))),
            pl.BlockSpec((tm, tn), lambda i, j: (jnp.maximum(i - n_split, 0), jnp.where(i < n_split, 0, j))),
        )
    return pl.pallas_call(
        functools.partial(_resid_mm_kernel, nsub=nsub, n_split=n_split),
        out_shape=out_shape,
        grid=(n_tiles, nj),
        in_specs=[
            pl.BlockSpec((tm, k), lambda i, j: (i, 0)),
            pl.BlockSpec((None, k, tn), lambda i, j: (0, 0, j)),
            pl.BlockSpec((tm, tn), lambda i, j: (i, j)),
            pl.BlockSpec((None, nsub, tn), lambda i, j: (0, i, g_idx * nj + j)),
        ],
        out_specs=out_specs,
        compiler_params=_cparams(("arbitrary", "arbitrary")),
        name=name,
    )(a, w, x, modx)


SCAN_LW = 512


def _ssm_kernel(*refs, tt, n_cast, sps, with_mod):
    (u_ref, h0r_ref, h0i_ref, wbr_ref, wbi_ref, pwj_ref, seg_ref, wcr_ref, wci_ref, d_ref, wglu_ref,
     bglu_ref) = refs[:12]
    n_extra = n_cast + (3 if with_mod else 0)
    cast_in = refs[12:12 + n_cast]
    o_ref, htr_ref, hti_ref = refs[12 + n_extra:15 + n_extra]
    cast_out = refs[15 + n_extra:15 + n_extra + n_cast]
    bur, bui, cre, cim, z_scr = refs[-5:]
    steps = tt // S5_SEG
    for w_ref, wb_ref in zip(cast_in, cast_out):
        wb_ref[...] = w_ref[...].astype(BF16)
    if with_mod:
        c_ref, wm_ref, bm_ref = refs[12 + n_cast:12 + n_extra]
        mod_ref = refs[15 + n_extra + n_cast]
        mod_ref[...] = jnp.dot(c_ref[...].astype(BF16), wm_ref[...].astype(BF16),
                               preferred_element_type=F32) + bm_ref[...]

    carried = sps == S5_SEG
    if carried:
        @pl.when(pl.program_id(1) == 0)
        def _():
            cre[...] = jnp.broadcast_to(h0r_ref[...], (SUBLANES, STATE_W))
            cim[...] = jnp.broadcast_to(h0i_ref[...], (SUBLANES, STATE_W))

    r_idx = lax.broadcasted_iota(jnp.int32, (tt, tt), 0)
    c_idx = lax.broadcasted_iota(jnp.int32, (tt, tt), 1)
    to_scan = jnp.where(c_idx == (r_idx % S5_SEG) * steps + r_idx // S5_SEG, 1.0, 0.0).astype(BF16)
    to_time = jnp.where(r_idx == (c_idx % S5_SEG) * steps + c_idx // S5_SEG, 1.0, 0.0).astype(BF16)

    u = u_ref[...]
    us = jnp.dot(to_scan, u.astype(BF16), preferred_element_type=F32).astype(BF16)
    kw = D_SSM // SSM_KC
    sw = STATE_W // SSM_KC
    for kc in range(SSM_KC):
        uk = us[:, kc * kw:(kc + 1) * kw]
        bur[:, kc * sw:(kc + 1) * sw] = jnp.dot(uk, wbr_ref[kc], preferred_element_type=F32)
        bui[:, kc * sw:(kc + 1) * sw] = jnp.dot(uk, wbi_ref[kc], preferred_element_type=F32)

    row = lax.broadcasted_iota(jnp.int32, (SUBLANES, SCAN_LW), 0)
    for lc in range(STATE_W // SCAN_LW):
        sl = slice(lc * SCAN_LW, (lc + 1) * SCAN_LW)
        lam_re = jnp.broadcast_to(pwj_ref[0, 0:1, sl], (SUBLANES, SCAN_LW))
        lam_im = jnp.broadcast_to(pwj_ref[1, 0:1, sl], (SUBLANES, SCAN_LW))

        def step(j, carry, sl=sl, lam_re=lam_re, lam_im=lam_im):
            h_re, h_im = carry
            rows = pl.ds(pl.multiple_of(j * SUBLANES, SUBLANES), SUBLANES)
            return (lam_re * h_re - lam_im * h_im + bur[rows, sl],
                    lam_re * h_im + lam_im * h_re + bui[rows, sl])

        zero = jnp.zeros((SUBLANES, SCAN_LW), F32)
        e_re, e_im = lax.fori_loop(0, steps, step, (zero, zero), unroll=4)

        first = row % sps == 0
        in_re, in_im = (cre, cim) if carried else (h0r_ref, h0i_ref)
        x_re = jnp.where(first, in_re[:, sl], pltpu.roll(e_re, 1, 0))
        x_im = jnp.where(first, in_im[:, sl], pltpu.roll(e_im, 1, 0))
        for idx, k in enumerate((1, 2, 4)):
            if k >= sps:
                continue
            m_re = seg_ref[2 * idx, :, sl]
            m_im = seg_ref[2 * idx + 1, :, sl]
            s_re = pltpu.roll(x_re, k, 0)
            s_im = pltpu.roll(x_im, k, 0)
            x_re, x_im = (x_re + m_re * s_re - m_im * s_im,
                          x_im + m_re * s_im + m_im * s_re)
        pj_re = jnp.broadcast_to(pwj_ref[0, 1:2, sl], (SUBLANES, SCAN_LW))
        pj_im = jnp.broadcast_to(pwj_ref[1, 1:2, sl], (SUBLANES, SCAN_LW))
        n_re = pj_re * x_re - pj_im * x_im + e_re
        n_im = pj_re * x_im + pj_im * x_re + e_im
        if carried:
            cre[:, sl] = jnp.broadcast_to(n_re[SUBLANES - 1:SUBLANES, :], (SUBLANES, SCAN_LW))
            cim[:, sl] = jnp.broadcast_to(n_im[SUBLANES - 1:SUBLANES, :], (SUBLANES, SCAN_LW))
        else:
            htr_ref[:, sl] = n_re
            hti_ref[:, sl] = n_im

        def scan(j, carry, sl=sl, step=step):
            h_re, h_im = step(j, carry)
            rows = pl.ds(pl.multiple_of(j * SUBLANES, SUBLANES), SUBLANES)
            bur[rows, sl] = h_re
            bui[rows, sl] = h_im
            return h_re, h_im

        lax.fori_loop(0, steps, scan, (x_re, x_im), unroll=2)

    if carried:
        htr_ref[...] = cre[0:1, :]
        hti_ref[...] = cim[0:1, :]

    d = d_ref[...]
    for kc in range(SSM_KC):
        h_re = bur[:, kc * sw:(kc + 1) * sw].astype(BF16)
        h_im = bui[:, kc * sw:(kc + 1) * sw].astype(BF16)
        ys = (jnp.dot(h_re, wcr_ref[kc], preferred_element_type=F32)
              - jnp.dot(h_im, wci_ref[kc], preferred_element_type=F32))
        ys_hi = ys.astype(BF16)
        ys_lo = (ys - ys_hi.astype(F32)).astype(BF16)
        y = (jnp.dot(to_time, ys_hi, preferred_element_type=F32)
             + jnp.dot(to_time, ys_lo, preferred_element_type=F32))
        cols = slice(kc * kw, (kc + 1) * kw)
        y = y + d[:, cols] * u[:, cols]
        z_scr[:, cols] = 0.5 * y * (1.0 + jnp.tanh(math.sqrt(2.0 / math.pi) * (y + 0.044715 * (y * y * y))))
    z = z_scr[...]
    gate = _sigmoid(jnp.dot(z.astype(BF16), wglu_ref[...], preferred_element_type=F32) + bglu_ref[...])
    o_ref[...] = (z * gate).astype(o_ref.dtype)


def _ssm_call(proj, row0, n_seq, nt, h0_re, h0_im, l, wb_re, wb_im, pw_re, pw_im, wc_re, wc_im, d_skip,
              w_glu, b_glu, *, sps, cast=(), mod_next=None, name):
    tt = S5_TILE
    steps = tt // S5_SEG
    power = lambda n: S5_POWERS.index(n)
    pwj = jnp.stack([pw[:, (power(1), power(steps)), :] for pw in (pw_re, pw_im)], axis=1)
    seg_rows = []
    for k in (1, 2, 4):
        keep = (jnp.arange(SUBLANES) % sps >= k)[None, :, None]
        seg_rows += [jnp.where(keep, pw[:, power(k * steps)][:, None, :], 0.0) for pw in (pw_re, pw_im)]
    seg = jnp.stack(seg_rows, axis=1)
    state_rows = h0_re.shape[1]
    assert state_rows == (1 if sps == S5_SEG else S5_SEG) and (sps == S5_SEG or nt == 1)
    t_len = nt * tt
    rb0 = row0 // tt
    n_steps = n_seq * nt
    state_spec = pl.BlockSpec((None, state_rows, STATE_W), lambda b, t: (b, 0, 0))
    cast_in, cast_out, cast_shapes = [], [], []
    for w, wl in cast:
        _, k, n = w.shape
        rows = k // n_steps
        cast_in.append(pl.BlockSpec((None, rows, n), lambda b, t, wl=wl: (wl, b * nt + t, 0)))
        cast_out.append(pl.BlockSpec((None, rows, n), lambda b, t: (0, b * nt + t, 0)))
        cast_shapes.append(jax.ShapeDtypeStruct((1, k, n), BF16))
    mod_in, mod_out, mod_shape, mod_args = [], [], [], []
    if mod_next is not None:
        c_all, w_mod, b_mod, ml = mod_next
        nb, n_mod = c_all.shape[0], w_mod.shape[2]
        cols = n_mod // n_steps
        mod_in = [pl.BlockSpec((nb, D_MODEL), lambda b, t: (0, 0)),
                  pl.BlockSpec((None, D_MODEL, cols), lambda b, t: (ml, 0, b * nt + t)),
                  pl.BlockSpec((None, 1, cols), lambda b, t: (ml, 0, b * nt + t))]
        mod_out = [pl.BlockSpec((None, nb, cols), lambda b, t: (0, 0, b * nt + t))]
        mod_shape = [jax.ShapeDtypeStruct((1, nb, n_mod), F32)]
        mod_args = [c_all, w_mod, b_mod.reshape(DEPTH, 1, n_mod)]
    resident = lambda shape: pl.BlockSpec((None, *shape), lambda b, t: (l, *(0,) * len(shape)),
                                          pipeline_mode=pl.Buffered(1))
    return pl.pallas_call(
        functools.partial(_ssm_kernel, tt=tt, n_cast=len(cast), sps=sps, with_mod=mod_next is not None),
        out_shape=(jax.ShapeDtypeStruct((n_seq * t_len, D_SSM), BF16),
                   jax.ShapeDtypeStruct((n_seq, state_rows, STATE_W), F32),
                   jax.ShapeDtypeStruct((n_seq, state_rows, STATE_W), F32),
                   *cast_shapes, *mod_shape),
        grid=(n_seq, nt),
        in_specs=[
            pl.BlockSpec((tt, D_SSM), lambda b, t: (rb0 + b * nt + t, 0)),
            state_spec, state_spec,
            resident(wb_re.shape[1:]), resident(wb_im.shape[1:]),
            _layer_spec(l, pwj.shape[1:]), _layer_spec(l, seg.shape[1:]),
            resident(wc_re.shape[1:]), resident(wc_im.shape[1:]),
            _layer_spec(l, (1, D_SSM)),
            resident((D_SSM, D_SSM)),
            _layer_spec(l, (1, D_SSM)),
            *cast_in, *mod_in,
        ],
        out_specs=(pl.BlockSpec((tt, D_SSM), lambda b, t: (b * nt + t, 0)), state_spec, state_spec, *cast_out,
                   *mod_out),
        scratch_shapes=[pltpu.VMEM((tt, STATE_W), F32), pltpu.VMEM((tt, STATE_W), F32),
                        pltpu.VMEM((SUBLANES, STATE_W), F32), pltpu.VMEM((SUBLANES, STATE_W), F32),
                        pltpu.VMEM((tt, D_SSM), F32)],
        compiler_params=_cparams(("arbitrary", "arbitrary")),
        name=name,
    )(proj, h0_re, h0_im, wb_re, wb_im, pwj, seg, wc_re, wc_im, d_skip.reshape(DEPTH, 1, D_SSM), w_glu,
      b_glu.reshape(DEPTH, 1, D_SSM), *[w for w, _ in cast], *mod_args)


RING = 3
N_KEYS = RING * CHUNK
ATT_CPS = 8


def _stack_blocks(x):
    return jnp.concatenate([x[:, j * LANES:(j + 1) * LANES] for j in range(x.shape[1] // LANES)], axis=0)


def _pair_norm_rope(xs, gain, tab, ones_bd):
    reps = xs.shape[0] // CHUNK
    cos, s_lo, s_hi = (jnp.concatenate([tab[i]] * reps, axis=0) for i in range(3))
    half = ROPE_DIM // 2
    sq = xs * xs
    sq_hi = sq.astype(BF16)
    sq_lo = (sq - sq_hi.astype(F32)).astype(BF16)
    ss = (jnp.dot(sq_hi, ones_bd, preferred_element_type=F32)
          + jnp.dot(sq_lo, ones_bd, preferred_element_type=F32))
    xg = xs * gain
    xr = xg * cos + pltpu.roll(xg, LANES - half, 1) * s_lo + pltpu.roll(xg, half, 1) * s_hi
    return xr * lax.rsqrt(ss * (1.0 / HEAD_DIM) + EPS)


def _attn_kernel(sink_ref, q_ref, k_ref, v_ref, ck_ref, cv_ref, tab_ref, gq_ref, gk_ref, o_ref, kn_ref, kd, vd,
                 *, l, n_prompt_steps, steps_per_seq):
    step = pl.program_id(0)
    is_sample = step >= n_prompt_steps
    c_base = lax.rem(step, steps_per_seq) * ATT_CPS
    first = lax.broadcasted_iota(jnp.int32, (1, LANES), 1) < HEAD_DIM
    ones_bd = jnp.where(lax.broadcasted_iota(jnp.int32, (LANES, LANES), 0) // HEAD_DIM
                        == lax.broadcasted_iota(jnp.int32, (LANES, LANES), 1) // HEAD_DIM, 1.0, 0.0).astype(BF16)
    key_slot = lax.broadcasted_iota(jnp.int32, (1, N_KEYS), 1) // CHUNK
    gq = gq_ref[...]
    gk = gk_ref[...]

    def store_dup(dst, x, rows):
        for pair in range(D_KV // LANES):
            blk = x[:, pair * LANES:(pair + 1) * LANES]
            swapped = pltpu.roll(blk, HEAD_DIM, 1)
            dst[2 * pair, rows, 0:LANES] = jnp.where(first, blk, swapped).astype(BF16)
            dst[2 * pair + 1, rows, 0:LANES] = jnp.where(first, swapped, blk).astype(BF16)

    @pl.when(step == 0)
    def _():
        vd[:, :, LANES:] = jnp.ones((N_KV_HEADS, N_KEYS, LANES), BF16)

    prev = slice(CHUNK, N_KEYS)

    def chunk(ci, from_cache):
        c = 0 if from_cache else c_base + ci
        rows = pl.ds(pl.multiple_of(ci * CHUNK, CHUNK), CHUNK)
        if from_cache:
            cached = pl.ds(pl.multiple_of(ci * WINDOW, WINDOW), WINDOW)
            store_dup(kd, ck_ref[cached, :], prev)
            store_dup(vd, cv_ref[cached, :], prev)

        tab = tab_ref[:, rows, :]
        own = (slice(0, CHUNK) if from_cache else
               pl.ds(pl.multiple_of(lax.rem(c, RING) * CHUNK, CHUNK), CHUNK))
        kn = _pair_norm_rope(_stack_blocks(k_ref[rows, :]), gk, tab, ones_bd)
        kn = jnp.concatenate([kn[0:CHUNK], kn[CHUNK:2 * CHUNK]], axis=1)
        kn_ref[rows, :] = kn
        store_dup(kd, kn, own)
        store_dup(vd, v_ref[rows, :], own)

        qn = _pair_norm_rope(_stack_blocks(q_ref[rows, :]), gq, tab, ones_bd) * (HEAD_DIM ** -0.5)
        q_lo = jnp.where(first, qn, 0.0).astype(BF16)
        q_hi = jnp.where(first, 0.0, qn).astype(BF16)

        valid = None if from_cache else key_slot <= c
        scores = []
        for kh in range(N_KV_HEADS):
            r0 = kh * 2 * CHUNK
            lhs = jnp.concatenate([q_lo[r0:r0 + CHUNK], q_hi[r0:r0 + CHUNK],
                                   q_lo[r0 + CHUNK:r0 + 2 * CHUNK], q_hi[r0 + CHUNK:r0 + 2 * CHUNK]], axis=0)
            scores.append(lax.dot_general(lhs, kd[kh], (((1,), (1,)), ((), ())),
                                          preferred_element_type=F32))
        weighted, sink_terms = [], []
        for kh in range(N_KV_HEADS):
            s = scores[kh] if from_cache else jnp.where(valid, scores[kh], -1e30)
            sink = jnp.concatenate([jnp.full((CHUNK, 1), sink_ref[l, kh * KV_REP + r], F32)
                                    for r in range(KV_REP)], axis=0)
            m = jnp.maximum(jnp.max(s, axis=-1, keepdims=True), sink)
            p = jnp.exp(s - m).astype(BF16)
            sink_terms.append(jnp.exp(sink - m))
            weighted.append(jnp.dot(p, vd[kh], preferred_element_type=F32))
        for kh in range(N_KV_HEADS):
            od = weighted[kh]
            o = od[:, 0:LANES] / (od[:, LANES:] + sink_terms[kh])
            for pair in range(2):
                half = pair * 2 * CHUNK
                blk = jnp.where(first, o[half:half + CHUNK], o[half + CHUNK:half + 2 * CHUNK])
                col = (2 * kh + pair) * LANES
                o_ref[rows, col:col + LANES] = blk.astype(o_ref.dtype)

    @pl.when(jnp.logical_not(is_sample))
    def _():
        @pl.when(c_base == 0)
        def _():
            kd[:, prev, :] = jnp.zeros((N_KV_HEADS, N_KEYS - CHUNK, LANES), BF16)
            vd[:, prev, 0:LANES] = jnp.zeros((N_KV_HEADS, N_KEYS - CHUNK, LANES), BF16)

        lax.fori_loop(0, ATT_CPS, lambda ci, carry: (chunk(ci, False), carry)[1], 0)

    @pl.when(is_sample)
    def _():
        lax.fori_loop(0, ATT_CPS, lambda ci, carry: (chunk(ci, True), carry)[1], 0)


def _attn_call(proj, l, n_prompt, t_prompt, sink, rope_tab, gq, gk, cache_k, cache_v):
    n_tok = proj.shape[0]
    tr = ATT_CPS * CHUNK
    qcol = D_SSM // D_ATTN
    kcol = (D_SSM + D_ATTN) // D_KV
    n_prompt_steps = n_prompt // tr
    cache_spec = pl.BlockSpec((None, ATT_CPS * WINDOW, D_KV),
                              lambda s: (l, jnp.maximum(s - n_prompt_steps, 0), 0))
    return pl.pallas_call(
        functools.partial(_attn_kernel, l=l, n_prompt_steps=n_prompt_steps, steps_per_seq=t_prompt // tr),
        out_shape=(jax.ShapeDtypeStruct((n_tok, D_ATTN), BF16),
                   jax.ShapeDtypeStruct((n_tok, D_KV), F32)),
        grid=(n_tok // tr,),
        in_specs=[
            pl.BlockSpec(memory_space=pltpu.SMEM),
            pl.BlockSpec((tr, D_ATTN), lambda s: (s, qcol)),
            pl.BlockSpec((tr, D_KV), lambda s: (s, kcol)),
            pl.BlockSpec((tr, D_KV), lambda s: (s, kcol + 1)),
            cache_spec, cache_spec,
            pl.BlockSpec((3, tr, LANES), lambda s: (0, s, 0)),
            _layer_spec(l, (1, LANES)), _layer_spec(l, (1, LANES)),
        ],
        out_specs=(pl.BlockSpec((tr, D_ATTN), lambda s: (s, 0)),
                   pl.BlockSpec((tr, D_KV), lambda s: (s, 0))),
        scratch_shapes=[pltpu.VMEM((N_KV_HEADS, N_KEYS, LANES), BF16),
                        pltpu.VMEM((N_KV_HEADS, N_KEYS, 2 * LANES), BF16)],
        compiler_params=_cparams(("arbitrary",)),
        name="banded_attn",
    )(sink, proj, proj, proj, cache_k, cache_v, rope_tab, gq, gk)


def _block_diag_rows(x, n):
    *lead, r, width = x.shape
    keep = jnp.arange(n)[:, None, None] == (jnp.arange(width) // (width // n))[None, None, :]
    return jnp.where(keep, x[..., None, :, :], 0.0).reshape(*lead, n * r, width)


def _rope_table(pos):
    half = ROPE_DIM // 2
    n_pos = pos.shape[0]
    inv_freq = ROPE_THETA ** (-jnp.arange(half, dtype=F32) / half)
    ang = pos.astype(F32)[:, None] * inv_freq[None, :]
    cos, sin = jnp.cos(ang), jnp.sin(ang)
    ones = jnp.ones((n_pos, HEAD_DIM - ROPE_DIM), F32)
    zeros = jnp.zeros((n_pos, HEAD_DIM - half), F32)
    c_tab = jnp.concatenate([cos, cos, ones], axis=1)
    lo_tab = jnp.concatenate([-sin, zeros], axis=1)
    hi_tab = jnp.concatenate([jnp.zeros((n_pos, half), F32), sin, jnp.zeros((n_pos, HEAD_DIM - ROPE_DIM), F32)],
                             axis=1)
    tab = jnp.stack([c_tab, lo_tab, hi_tab])
    return jnp.concatenate([tab, tab], axis=2)


def kernel(x_prompt, x_sample, cache_k, cache_v, state_ssm_re, state_ssm_im, c_prompt, c_sample, w_mod, b_mod, norm1_g, norm2_g, w_in, ssm_a_re, ssm_a_im, ssm_log_dt, ssm_b_re, ssm_b_im, ssm_c_re, ssm_c_im, ssm_d, w_glu, b_glu, q_norm_g, k_norm_g, attn_sink, w_gate, b_gate, w_proj_ssm, w_proj_attn, w_out, w_ffn_gate, w_ffn_up, w_ffn_down):
    bp, tp, _ = x_prompt.shape
    bs, ts, _ = x_sample.shape
    assert ts == CHUNK and tp % (ATT_CPS * CHUNK) == 0 and bs % ATT_CPS == 0
    assert tp % S5_TILE == 0 and S5_TILE % ts == 0 and (bs * ts) % S5_TILE == 0
    n_p, n_s = bp * tp, bs * ts
    tm = 1024
    assert n_p % tm == 0 and n_s % tm == 0
    x = jnp.concatenate([x_prompt.reshape(n_p, D_MODEL), x_sample.reshape(n_s, D_MODEL)], axis=0)

    n_cond = bp + bs
    pad = (-n_cond) % SUBLANES
    c_all = jnp.concatenate([c_prompt, c_sample, jnp.zeros((pad, D_MODEL), F32)], axis=0)
    n_mod = w_mod.shape[2]

    def per_block(m, n_seq, t_len):
        reps = t_len // MOD_BLOCK
        return jnp.broadcast_to(m[:, :, None, :], (1, n_seq, reps, n_mod)).reshape(1, n_seq * reps, n_mod)

    def mod_rows(mod):
        return jnp.concatenate([per_block(mod[:, :bp], bp, tp), per_block(mod[:, bp:n_cond], bs, ts)], axis=1)

    modx = mod_rows(_mod_call(c_all, w_mod, b_mod, 1))

    g, p = N_SSM_GROUPS, SSM_STATE
    gpc = g // SSM_KC
    pw_re, pw_im, bb_re, bb_im = _s5_prep_call(ssm_a_re, ssm_a_im, ssm_log_dt,
                                               ssm_b_re.transpose(0, 3, 1, 2), ssm_b_im.transpose(0, 3, 1, 2))
    pw_re = pw_re.reshape(DEPTH, len(S5_POWERS), STATE_W)
    pw_im = pw_im.reshape(DEPTH, len(S5_POWERS), STATE_W)

    def wb_blocks(bb):
        per_chunk = bb.reshape(DEPTH, SSM_GROUP, SSM_KC, gpc * p).transpose(0, 2, 1, 3)
        return _block_diag_rows(per_chunk, gpc).astype(BF16)

    def wc_blocks(cc):
        per_chunk = cc.reshape(DEPTH, SSM_KC, gpc, SSM_GROUP, p).transpose(0, 1, 3, 2, 4)
        transposed = _block_diag_rows(per_chunk.reshape(DEPTH, SSM_KC, SSM_GROUP, gpc * p), gpc)
        return jnp.swapaxes(transposed.astype(BF16), -1, -2)

    ssm_w = (wb_blocks(bb_re), wb_blocks(bb_im), pw_re, pw_im, wc_blocks(ssm_c_re), wc_blocks(ssm_c_im), ssm_d,
             w_glu.astype(BF16), b_glu)
    zeros_state = jnp.zeros((bp, 1, STATE_W), F32)
    sps = ts * S5_SEG // S5_TILE
    n_sample_tiles = n_s // S5_TILE

    def per_segment(state):
        rows = jnp.repeat(state.reshape(DEPTH, bs, STATE_W), sps, axis=1)
        return rows.reshape(DEPTH, n_sample_tiles, S5_SEG, STATE_W)

    def last_segment(state):
        return state.reshape(bs, sps, STATE_W)[:, sps - 1].reshape(bs, g, p)

    h0_re, h0_im = per_segment(state_ssm_re), per_segment(state_ssm_im)

    rope_tab = jnp.concatenate([jnp.tile(_rope_table(jnp.arange(tp)), (1, bp, 1)),
                                jnp.tile(_rope_table(PAST_LEN + jnp.arange(ts)), (1, bs, 1))], axis=1)
    gq = jnp.tile(q_norm_g, (1, LANES // HEAD_DIM)).reshape(DEPTH, 1, LANES)
    gk = jnp.tile(k_norm_g, (1, LANES // HEAD_DIM)).reshape(DEPTH, 1, LANES)
    cache_k2 = cache_k.reshape(DEPTH, bs * WINDOW, D_KV)
    cache_v2 = cache_v.reshape(DEPTH, bs * WINDOW, D_KV)

    w_in_b, w_gate_b = w_in[:1].astype(BF16), w_gate[:1].astype(BF16)
    v0 = D_SSM + D_ATTN + D_KV

    def heads(t):
        return t.reshape(*t.shape[:-1], N_KV_HEADS, HEAD_DIM)

    def last_window(t, col0, col1):
        return jnp.stack([t[(b + 1) * tp - WINDOW:(b + 1) * tp, col0:col1] for b in range(bp)])

    outs = {k: [] for k in ("pk", "pv", "pre", "pim", "sk", "sv", "sre", "sim")}
    for l in range(DEPTH):
        proj, gates = _in_gate_call(x, l, norm1_g, modx, w_in_b, w_gate_b, b_gate, tm=tm)

        nxt = min(l + 1, DEPTH - 1)
        cast = [(w, l) for w in (w_proj_ssm, w_proj_attn, w_out, w_ffn_gate, w_ffn_up, w_ffn_down)]
        cast += [(w_in, nxt), (w_gate, nxt)]
        ssm_p, pre, pim, w_ps_b, w_pa_b, w_out_b, w_fg_b, w_fu_b, w_fd_b, w_in_b, w_gate_b, mod_nxt = _ssm_call(
            proj, 0, bp, tp // S5_TILE, zeros_state, zeros_state, l, *ssm_w, sps=S5_SEG, cast=cast,
            mod_next=(c_all, w_mod, b_mod, nxt), name="s5_prompt")
        ssm_s, sre, sim = _ssm_call(proj, n_p, n_sample_tiles, 1, h0_re[l], h0_im[l], l, *ssm_w, sps=sps,
                                    name="s5_sample")
        attn_out, kn = _attn_call(proj, l, n_p, tp, attn_sink, rope_tab, gq, gk, cache_k2, cache_v2)

        mixed = _mix_call(ssm_p, ssm_s, attn_out, gates, w_ps_b, w_pa_b, tm=tm, tn=1024)
        x = _resid_mm_call(mixed, l, w_out_b, x, modx, 2, tm=tm, tn=1024, name="out_proj")

        act = _ffn_up_call(x, l, norm2_g, modx, w_fg_b, w_fu_b, tm=tm)
        last = l == DEPTH - 1
        x = _resid_mm_call(act, l, w_fd_b, x, modx, 5, n_split=n_p // tm if last else None, tm=tm,
                           name="ffn_down_split" if last else "ffn_down")

        outs["pk"].append(heads(last_window(kn, 0, D_KV)))
        outs["pv"].append(heads(last_window(proj, v0, IN_WIDTH)))
        outs["pre"].append(pre.reshape(bp, g, p))
        outs["pim"].append(pim.reshape(bp, g, p))
        outs["sk"].append(jnp.concatenate([cache_k[l][:, ts:], heads(kn[n_p:].reshape(bs, ts, D_KV))], axis=1))
        outs["sv"].append(jnp.concatenate([cache_v[l][:, ts:], heads(proj[n_p:, v0:].reshape(bs, ts, D_KV))],
                                          axis=1))
        outs["sre"].append(last_segment(sre))
        outs["sim"].append(last_segment(sim))
        modx = mod_rows(mod_nxt)

    y_p, y_s = x
    return (y_p.reshape(bp, tp, D_MODEL), y_s.reshape(bs, ts, D_MODEL),
            jnp.stack(outs["pk"]), jnp.stack(outs["pv"]), jnp.stack(outs["pre"]), jnp.stack(outs["pim"]),
            jnp.stack(outs["sk"]), jnp.stack(outs["sv"]), jnp.stack(outs["sre"]), jnp.stack(outs["sim"]))
```

```python
import functools
import math

import jax
import jax.numpy as jnp
from jax import lax
from jax.experimental import pallas as pl
from jax.experimental.pallas import tpu as pltpu

D_MODEL = 2048
DEPTH = 4
CHUNK = 64
D_SSM = 1024
SSM_GROUP = 16
N_SSM_GROUPS = 64
SSM_STATE = 64
HEAD_DIM = 64
N_HEADS = 16
N_KV_HEADS = 4
KV_REP = N_HEADS // N_KV_HEADS
D_ATTN = N_HEADS * HEAD_DIM
D_KV = N_KV_HEADS * HEAD_DIM
IN_WIDTH = D_SSM + D_ATTN + 2 * D_KV
WINDOW = 128
ROPE_DIM = 16
ROPE_THETA = 500000.0
D_FF = 5632
EPS = 1e-6
PAST_LEN = 2048

LANES = 128
SUBLANES = 8
MOD_BLOCK = CHUNK
STATE_W = N_SSM_GROUPS * SSM_STATE
SSM_KC = 4
VMEM_LIMIT = 56 * 1024 * 1024

F32 = jnp.float32
BF16 = jnp.bfloat16


def _cparams(sem):
    return pltpu.CompilerParams(dimension_semantics=sem, vmem_limit_bytes=VMEM_LIMIT)


def _sigmoid(x):
    return 0.5 + 0.5 * jnp.tanh(0.5 * x)


def _layer_spec(l, shape):
    zeros = (0,) * len(shape)
    return pl.BlockSpec((None, *shape), lambda *_: (l, *zeros))


def _mod_kernel(c_ref, w_ref, b_ref, o_ref):
    c = c_ref[...].astype(BF16)
    w = w_ref[...].astype(BF16)
    o_ref[...] = jnp.dot(c, w, preferred_element_type=F32) + b_ref[...]


def _mod_call(c_all, w_mod, b_mod, n_layers):
    nb = c_all.shape[0]
    tn = 1024
    n_out = w_mod.shape[2]
    return pl.pallas_call(
        _mod_kernel,
        out_shape=jax.ShapeDtypeStruct((n_layers, nb, n_out), F32),
        grid=(n_layers, n_out // tn),
        in_specs=[
            pl.BlockSpec((nb, D_MODEL), lambda l, j: (0, 0)),
            pl.BlockSpec((None, D_MODEL, tn), lambda l, j: (l, 0, j)),
            pl.BlockSpec((None, 1, tn), lambda l, j: (l, 0, j)),
        ],
        out_specs=pl.BlockSpec((None, nb, tn), lambda l, j: (l, 0, j)),
        compiler_params=_cparams(("arbitrary", "arbitrary")),
        name="adaln_mod",
    )(c_all, w_mod, b_mod.reshape(DEPTH, 1, n_out))


S5_TILE = 256
S5_SEG = SUBLANES
S5_POWERS = (1, *(k * S5_TILE // S5_SEG for k in (1, 2, 4)))


def _s5_prep_kernel(are_ref, aim_ref, ldt_ref, bre_ref, bim_ref, pwr_ref, pwi_ref, bbr_ref, bbi_ref):
    a_re = are_ref[...]
    a_im = aim_ref[...]
    dt = jnp.exp(ldt_ref[...])
    z_re = a_re * dt
    z_im = a_im * dt
    for i, n in enumerate(S5_POWERS):
        mag = jnp.exp(z_re * float(n))
        pwr_ref[i] = mag * jnp.cos(z_im * float(n))
        pwi_ref[i] = mag * jnp.sin(z_im * float(n))
    l_re = pwr_ref[S5_POWERS.index(1)]
    l_im = pwi_ref[S5_POWERS.index(1)]
    den = a_re * a_re + a_im * a_im
    n_re = l_re - 1.0
    f_re = (n_re * a_re + l_im * a_im) / den
    f_im = (l_im * a_re - n_re * a_im) / den
    for c in range(SSM_GROUP):
        b_re = bre_ref[c]
        b_im = bim_ref[c]
        bbr_ref[c] = f_re * b_re - f_im * b_im
        bbi_ref[c] = f_re * b_im + f_im * b_re


def _s5_prep_call(a_re, a_im, log_dt, bt_re, bt_im):
    g, p = N_SSM_GROUPS, SSM_STATE
    mat = pl.BlockSpec((None, g, p), lambda l: (l, 0, 0))
    stack_c = pl.BlockSpec((None, SSM_GROUP, g, p), lambda l: (l, 0, 0, 0))
    n_pow = len(S5_POWERS)
    stack_n = pl.BlockSpec((None, n_pow, g, p), lambda l: (l, 0, 0, 0))
    return pl.pallas_call(
        _s5_prep_kernel,
        out_shape=(jax.ShapeDtypeStruct((DEPTH, n_pow, g, p), F32),
                   jax.ShapeDtypeStruct((DEPTH, n_pow, g, p), F32),
                   jax.ShapeDtypeStruct((DEPTH, SSM_GROUP, g, p), F32),
                   jax.ShapeDtypeStruct((DEPTH, SSM_GROUP, g, p), F32)),
        grid=(DEPTH,),
        in_specs=[mat, mat, pl.BlockSpec((None, g, 1), lambda l: (l, 0, 0)), stack_c, stack_c],
        out_specs=(stack_n, stack_n, stack_c, stack_c),
        compiler_params=_cparams(("arbitrary",)),
        name="s5_discretize",
    )(a_re, a_im, log_dt.reshape(DEPTH, g, 1), bt_re, bt_im)


def _norm_mod_chunk(x_ref, g_ref, sc_ref, sh_ref, h_scr, slot, chunk):
    gain = g_ref[...]
    per_chunk = x_ref.shape[0] // MOD_BLOCK
    for s in range(per_chunk):
        x = x_ref[s * MOD_BLOCK:(s + 1) * MOD_BLOCK, :]
        ms = jnp.mean(x * x, axis=-1, keepdims=True)
        y = x * lax.rsqrt(ms + EPS) * gain
        mod_row = pl.ds(chunk * per_chunk + s, 1)
        h = y * (1.0 + sc_ref[mod_row, :]) + sh_ref[mod_row, :]
        rows = pl.ds(pl.multiple_of((chunk * per_chunk + s) * MOD_BLOCK, MOD_BLOCK), MOD_BLOCK)
        h_scr[slot, rows, :] = h.astype(BF16)


def _norm_pipeline(l, n_tiles, tm, sc_idx, sh_idx, chunk_rows):
    n_chunks = tm // chunk_rows
    nsub = tm // MOD_BLOCK
    tile = lambda i: jnp.minimum(i, n_tiles - 1)
    specs = [
        pl.BlockSpec((chunk_rows, D_MODEL), lambda i, j: (tile(i) * n_chunks + jnp.minimum(j, n_chunks - 1), 0)),
        _layer_spec(l, (1, D_MODEL)),
        pl.BlockSpec((None, nsub, D_MODEL), lambda i, j: (0, tile(i), sc_idx)),
        pl.BlockSpec((None, nsub, D_MODEL), lambda i, j: (0, tile(i), sh_idx)),
    ]
    scratch = pltpu.VMEM((2, tm, D_MODEL), BF16)
    out_row = lambda i: jnp.maximum(i - 1, 0)
    out_col = lambda i, col: jnp.where(i == 0, 0, col)
    return specs, scratch, n_chunks, out_row, out_col


def _in_gate_kernel(x_ref, g_ref, sc_ref, sh_ref, wi_ref, wg_ref, b_ref, proj_ref, gate_ref, h_scr, *, n_in,
                    n_chunks):
    i, j = pl.program_id(0), pl.program_id(1)
    ready = lax.rem(i + 1, 2)

    filling = j < n_chunks

    def fill():
        _norm_mod_chunk(x_ref, g_ref, sc_ref, sh_ref, h_scr, lax.rem(i, 2), j)

    def gates():
        acc = jnp.dot(h_scr[ready], wg_ref[...], preferred_element_type=F32)
        gate_ref[...] = _sigmoid(acc + b_ref[...]).astype(gate_ref.dtype)

    pl.when(jnp.logical_and(i == 0, filling))(fill)

    @pl.when(jnp.logical_and(i > 0, j < n_in))
    def _():
        proj_ref[...] = jnp.dot(h_scr[ready], wi_ref[...], preferred_element_type=F32)
        fill()

    @pl.when(jnp.logical_and(i > 0, jnp.logical_and(j >= n_in, filling)))
    def _():
        gates()
        fill()

    pl.when(jnp.logical_and(i > 0, jnp.logical_not(filling)))(gates)


def _in_gate_call(x, l, gain, modx, w_in, w_gate, b_gate, *, tm=1024, tn_in=1280, tn_gate=1024):
    n_tiles = x.shape[0] // tm
    n_in, n_gate = w_in.shape[2] // tn_in, w_gate.shape[2] // tn_gate
    norm_specs, h_scratch, n_chunks, out_row, out_col = _norm_pipeline(l, n_tiles, tm, 1, 0, 4 * MOD_BLOCK)
    assert n_in <= n_chunks <= n_in + n_gate
    in_col = lambda j: jnp.minimum(j, n_in - 1)
    gate_col = lambda j: jnp.maximum(j - n_in, 0)
    return pl.pallas_call(
        functools.partial(_in_gate_kernel, n_in=n_in, n_chunks=n_chunks),
        out_shape=(jax.ShapeDtypeStruct((x.shape[0], w_in.shape[2]), F32),
                   jax.ShapeDtypeStruct((x.shape[0], w_gate.shape[2]), BF16)),
        grid=(n_tiles + 1, n_in + n_gate),
        in_specs=norm_specs + [
            pl.BlockSpec((None, D_MODEL, tn_in), lambda i, j: (0, 0, out_col(i, in_col(j)))),
            pl.BlockSpec((None, D_MODEL, tn_gate), lambda i, j: (0, 0, out_col(i, gate_col(j)))),
            pl.BlockSpec((None, 1, tn_gate), lambda i, j: (l, 0, out_col(i, gate_col(j)))),
        ],
        out_specs=(pl.BlockSpec((tm, tn_in), lambda i, j: (out_row(i), out_col(i, in_col(j)))),
                   pl.BlockSpec((tm, tn_gate), lambda i, j: (out_row(i), out_col(i, gate_col(j))))),
        scratch_shapes=[h_scratch],
        compiler_params=_cparams(("arbitrary", "arbitrary")),
        name="in_proj_gates",
    )(x, gain.reshape(DEPTH, 1, D_MODEL), modx, modx, w_in, w_gate, b_gate.reshape(DEPTH, 1, w_gate.shape[2]))


def _ffn_up_kernel(x_ref, g_ref, sc_ref, sh_ref, wg_ref, wu_ref, o_ref, h_scr, *, n_chunks):
    i, j = pl.program_id(0), pl.program_id(1)

    filling = j < n_chunks

    def fill():
        _norm_mod_chunk(x_ref, g_ref, sc_ref, sh_ref, h_scr, lax.rem(i, 2), j)

    def swiglu():
        h = h_scr[lax.rem(i + 1, 2)]
        gate = jnp.dot(h, wg_ref[...], preferred_element_type=F32)
        up = jnp.dot(h, wu_ref[...], preferred_element_type=F32)
        o_ref[...] = (gate * _sigmoid(gate) * up).astype(o_ref.dtype)

    pl.when(jnp.logical_and(i == 0, filling))(fill)

    @pl.when(jnp.logical_and(i > 0, filling))
    def _():
        swiglu()
        fill()

    pl.when(jnp.logical_and(i > 0, jnp.logical_not(filling)))(swiglu)


def _ffn_up_call(x, l, gain, modx, w_gate, w_up, *, tm=1024, tn=512):
    n_tiles = x.shape[0] // tm
    n_out = w_gate.shape[2]
    norm_specs, h_scratch, n_chunks, out_row, out_col = _norm_pipeline(l, n_tiles, tm, 4, 3, 2 * MOD_BLOCK)
    assert n_out // tn >= n_chunks
    w_spec = pl.BlockSpec((None, D_MODEL, tn), lambda i, j: (0, 0, out_col(i, j)))
    return pl.pallas_call(
        functools.partial(_ffn_up_kernel, n_chunks=n_chunks),
        out_shape=jax.ShapeDtypeStruct((x.shape[0], n_out), BF16),
        grid=(n_tiles + 1, n_out // tn),
        in_specs=norm_specs + [w_spec, w_spec],
        out_specs=pl.BlockSpec((tm, tn), lambda i, j: (out_row(i), out_col(i, j))),
        scratch_shapes=[h_scratch],
        compiler_params=_cparams(("arbitrary", "arbitrary")),
        name="ffn_up",
    )(x, gain.reshape(DEPTH, 1, D_MODEL), modx, modx, w_gate, w_up)


def _mix_kernel(sp_ref, ss_ref, a_ref, ga_ref, gb_ref, ws_ref, wa_ref, o_ref, *, n_split):
    def body(s_ref):
        ps = jnp.dot(s_ref[...], ws_ref[...], preferred_element_type=F32)
        pa = jnp.dot(a_ref[...], wa_ref[...], preferred_element_type=F32)
        o_ref[...] = (ga_ref[...] * ps + gb_ref[...] * pa).astype(o_ref.dtype)

    i = pl.program_id(0)
    pl.when(i < n_split)(lambda: body(sp_ref))
    pl.when(i >= n_split)(lambda: body(ss_ref))


def _mix_call(ssm_p, ssm_s, attn_out, gates, w_ps, w_pa, *, tm=1024, tn=512):
    n_tok = attn_out.shape[0]
    nj = D_MODEL // tn
    n_split = ssm_p.shape[0] // tm
    return pl.pallas_call(
        functools.partial(_mix_kernel, n_split=n_split),
        out_shape=jax.ShapeDtypeStruct((n_tok, D_MODEL), BF16),
        grid=(n_tok // tm, nj),
        in_specs=[
            pl.BlockSpec((tm, D_SSM), lambda i, j: (jnp.minimum(i, n_split - 1), 0)),
            pl.BlockSpec((tm, D_SSM), lambda i, j: (jnp.maximum(i - n_split, 0), 0)),
            pl.BlockSpec((tm, D_ATTN), lambda i, j: (i, 0)),
            pl.BlockSpec((tm, tn), lambda i, j: (i, j)),
            pl.BlockSpec((tm, tn), lambda i, j: (i, j + nj)),
            pl.BlockSpec((None, D_SSM, tn), lambda i, j: (0, 0, j)),
            pl.BlockSpec((None, D_ATTN, tn), lambda i, j: (0, 0, j)),
        ],
        out_specs=pl.BlockSpec((tm, tn), lambda i, j: (i, j)),
        compiler_params=_cparams(("arbitrary", "arbitrary")),
        name="branch_merge",
    )(ssm_p, ssm_s, attn_out, gates, gates, w_ps, w_pa)


def _resid_mm_kernel(a_ref, w_ref, x_ref, g_ref, *o_refs, nsub, n_split):
    acc = jnp.dot(a_ref[...], w_ref[...], preferred_element_type=F32)

    def write(o_ref):
        for s in range(nsub):
            rows = slice(s * MOD_BLOCK, (s + 1) * MOD_BLOCK)
            o_ref[rows, :] = x_ref[rows, :] + g_ref[s:s + 1, :] * acc[rows, :]

    if n_split is None:
        write(o_refs[0])
    else:
        i = pl.program_id(0)
        pl.when(i < n_split)(lambda: write(o_refs[0]))
        pl.when(i >= n_split)(lambda: write(o_refs[1]))


def _resid_mm_call(a, l, w, x, modx, g_idx, *, n_split=None, tm=1024, tn=512, name):
    n_tok, k = a.shape
    nsub = tm // MOD_BLOCK
    nj = D_MODEL // tn
    n_tiles = n_tok // tm
    if n_split is None:
        out_shape = jax.ShapeDtypeStruct((n_tok, D_MODEL), F32)
        out_specs = pl.BlockSpec((tm, tn), lambda i, j: (i, j))
    else:
        out_shape = (jax.ShapeDtypeStruct((n_split * tm, D_MODEL), F32),
                     jax.ShapeDtypeStruct(((n_tiles - n_split) * tm, D_MODEL), F32))
        out_specs = (
            pl.BlockSpec((tm, tn), lambda i, j: (jnp.minimum(i, n_split - 1), jnp.where(i < n_split, j, nj - 1))),
            pl.BlockSpec((tm, tn), lambda i, j: (jnp.maximum(i - n_split, 0), jnp.where(i < n_split, 0, j))),
        )
    return pl.pallas_call(
        functools.partial(_resid_mm_kernel, nsub=nsub, n_split=n_split),
        out_shape=out_shape,
        grid=(n_tiles, nj),
        in_specs=[
            pl.BlockSpec((tm, k), lambda i, j: (i, 0)),
            pl.BlockSpec((None, k, tn), lambda i, j: (0, 0, j)),
            pl.BlockSpec((tm, tn), lambda i, j: (i, j)),
            pl.BlockSpec((None, nsub, tn), lambda i, j: (0, i, g_idx * nj + j)),
        ],
        out_specs=out_specs,
        compiler_params=_cparams(("arbitrary", "arbitrary")),
        name=name,
    )(a, w, x, modx)


SCAN_LW = 512


def _ssm_kernel(*refs, tt, n_cast, sps, with_mod):
    (u_ref, h0r_ref, h0i_ref, wbr_ref, wbi_ref, pwj_ref, seg_ref, wcr_ref, wci_ref, d_ref, wglu_ref,
     bglu_ref) = refs[:12]
    n_extra = n_cast + (3 if with_mod else 0)
    cast_in = refs[12:12 + n_cast]
    o_ref, htr_ref, hti_ref = refs[12 + n_extra:15 + n_extra]
    cast_out = refs[15 + n_extra:15 + n_extra + n_cast]
    bur, bui, cre, cim, z_scr = refs[-5:]
    steps = tt // S5_SEG
    for w_ref, wb_ref in zip(cast_in, cast_out):
        wb_ref[...] = w_ref[...].astype(BF16)
    if with_mod:
        c_ref, wm_ref, bm_ref = refs[12 + n_cast:12 + n_extra]
        mod_ref = refs[15 + n_extra + n_cast]
        mod_ref[...] = jnp.dot(c_ref[...].astype(BF16), wm_ref[...].astype(BF16),
                               preferred_element_type=F32) + bm_ref[...]

    carried = sps == S5_SEG
    if carried:
        @pl.when(pl.program_id(1) == 0)
        def _():
            cre[...] = jnp.broadcast_to(h0r_ref[...], (SUBLANES, STATE_W))
            cim[...] = jnp.broadcast_to(h0i_ref[...], (SUBLANES, STATE_W))

    r_idx = lax.broadcasted_iota(jnp.int32, (tt, tt), 0)
    c_idx = lax.broadcasted_iota(jnp.int32, (tt, tt), 1)
    to_scan = jnp.where(c_idx == (r_idx % S5_SEG) * steps + r_idx // S5_SEG, 1.0, 0.0).astype(BF16)
    to_time = jnp.where(r_idx == (c_idx % S5_SEG) * steps + c_idx // S5_SEG, 1.0, 0.0).astype(BF16)

    u = u_ref[...]
    us = jnp.dot(to_scan, u.astype(BF16), preferred_element_type=F32).astype(BF16)
    kw = D_SSM // SSM_KC
    sw = STATE_W // SSM_KC
    for kc in range(SSM_KC):
        uk = us[:, kc * kw:(kc + 1) * kw]
        bur[:, kc * sw:(kc + 1) * sw] = jnp.dot(uk, wbr_ref[kc], preferred_element_type=F32)
        bui[:, kc * sw:(kc + 1) * sw] = jnp.dot(uk, wbi_ref[kc], preferred_element_type=F32)

    row = lax.broadcasted_iota(jnp.int32, (SUBLANES, SCAN_LW), 0)
    for lc in range(STATE_W // SCAN_LW):
        sl = slice(lc * SCAN_LW, (lc + 1) * SCAN_LW)
        lam_re = jnp.broadcast_to(pwj_ref[0, 0:1, sl], (SUBLANES, SCAN_LW))
        lam_im = jnp.broadcast_to(pwj_ref[1, 0:1, sl], (SUBLANES, SCAN_LW))

        def step(j, carry, sl=sl, lam_re=lam_re, lam_im=lam_im):
            h_re, h_im = carry
            rows = pl.ds(pl.multiple_of(j * SUBLANES, SUBLANES), SUBLANES)
            return (lam_re * h_re - lam_im * h_im + bur[rows, sl],
                    lam_re * h_im + lam_im * h_re + bui[rows, sl])

        zero = jnp.zeros((SUBLANES, SCAN_LW), F32)
        e_re, e_im = lax.fori_loop(0, steps, step, (zero, zero), unroll=4)

        first = row % sps == 0
        in_re, in_im = (cre, cim) if carried else (h0r_ref, h0i_ref)
        x_re = jnp.where(first, in_re[:, sl], pltpu.roll(e_re, 1, 0))
        x_im = jnp.where(first, in_im[:, sl], pltpu.roll(e_im, 1, 0))
        for idx, k in enumerate((1, 2, 4)):
            if k >= sps:
                continue
            m_re = seg_ref[2 * idx, :, sl]
            m_im = seg_ref[2 * idx + 1, :, sl]
            s_re = pltpu.roll(x_re, k, 0)
            s_im = pltpu.roll(x_im, k, 0)
            x_re, x_im = (x_re + m_re * s_re - m_im * s_im,
                          x_im + m_re * s_im + m_im * s_re)
        pj_re = jnp.broadcast_to(pwj_ref[0, 1:2, sl], (SUBLANES, SCAN_LW))
        pj_im = jnp.broadcast_to(pwj_ref[1, 1:2, sl], (SUBLANES, SCAN_LW))
        n_re = pj_re * x_re - pj_im * x_im + e_re
        n_im = pj_re * x_im + pj_im * x_re + e_im
        if carried:
            cre[:, sl] = jnp.broadcast_to(n_re[SUBLANES - 1:SUBLANES, :], (SUBLANES, SCAN_LW))
            cim[:, sl] = jnp.broadcast_to(n_im[SUBLANES - 1:SUBLANES, :], (SUBLANES, SCAN_LW))
        else:
            htr_ref[:, sl] = n_re
            hti_ref[:, sl] = n_im

        def scan(j, carry, sl=sl, step=step):
            h_re, h_im = step(j, carry)
            rows = pl.ds(pl.multiple_of(j * SUBLANES, SUBLANES), SUBLANES)
            bur[rows, sl] = h_re
            bui[rows, sl] = h_im
            return h_re, h_im

        lax.fori_loop(0, steps, scan, (x_re, x_im), unroll=2)

    if carried:
        htr_ref[...] = cre[0:1, :]
        hti_ref[...] = cim[0:1, :]

    d = d_ref[...]
    for kc in range(SSM_KC):
        h_re = bur[:, kc * sw:(kc + 1) * sw].astype(BF16)
        h_im = bui[:, kc * sw:(kc + 1) * sw].astype(BF16)
        ys = (jnp.dot(h_re, wcr_ref[kc], preferred_element_type=F32)
              - jnp.dot(h_im, wci_ref[kc], preferred_element_type=F32))
        ys_hi = ys.astype(BF16)
        ys_lo = (ys - ys_hi.astype(F32)).astype(BF16)
        y = (jnp.dot(to_time, ys_hi, preferred_element_type=F32)
             + jnp.dot(to_time, ys_lo, preferred_element_type=F32))
        cols = slice(kc * kw, (kc + 1) * kw)
        y = y + d[:, cols] * u[:, cols]
        z_scr[:, cols] = 0.5 * y * (1.0 + jnp.tanh(math.sqrt(2.0 / math.pi) * (y + 0.044715 * (y * y * y))))
    z = z_scr[...]
    gate = _sigmoid(jnp.dot(z.astype(BF16), wglu_ref[...], preferred_element_type=F32) + bglu_ref[...])
    o_ref[...] = (z * gate).astype(o_ref.dtype)


def _ssm_call(proj, row0, n_seq, nt, h0_re, h0_im, l, wb_re, wb_im, pw_re, pw_im, wc_re, wc_im, d_skip,
              w_glu, b_glu, *, sps, cast=(), mod_next=None, name):
    tt = S5_TILE
    steps = tt // S5_SEG
    power = lambda n: S5_POWERS.index(n)
    pwj = jnp.stack([pw[:, (power(1), power(steps)), :] for pw in (pw_re, pw_im)], axis=1)
    seg_rows = []
    for k in (1, 2, 4):
        keep = (jnp.arange(SUBLANES) % sps >= k)[None, :, None]
        seg_rows += [jnp.where(keep, pw[:, power(k * steps)][:, None, :], 0.0) for pw in (pw_re, pw_im)]
    seg = jnp.stack(seg_rows, axis=1)
    state_rows = h0_re.shape[1]
    assert state_rows == (1 if sps == S5_SEG else S5_SEG) and (sps == S5_SEG or nt == 1)
    t_len = nt * tt
    rb0 = row0 // tt
    n_steps = n_seq * nt
    state_spec = pl.BlockSpec((None, state_rows, STATE_W), lambda b, t: (b, 0, 0))
    cast_in, cast_out, cast_shapes = [], [], []
    for w, wl in cast:
        _, k, n = w.shape
        rows = k // n_steps
        cast_in.append(pl.BlockSpec((None, rows, n), lambda b, t, wl=wl: (wl, b * nt + t, 0)))
        cast_out.append(pl.BlockSpec((None, rows, n), lambda b, t: (0, b * nt + t, 0)))
        cast_shapes.append(jax.ShapeDtypeStruct((1, k, n), BF16))
    mod_in, mod_out, mod_shape, mod_args = [], [], [], []
    if mod_next is not None:
        c_all, w_mod, b_mod, ml = mod_next
        nb, n_mod = c_all.shape[0], w_mod.shape[2]
        cols = n_mod // n_steps
        mod_in = [pl.BlockSpec((nb, D_MODEL), lambda b, t: (0, 0)),
                  pl.BlockSpec((None, D_MODEL, cols), lambda b, t: (ml, 0, b * nt + t)),
                  pl.BlockSpec((None, 1, cols), lambda b, t: (ml, 0, b * nt + t))]
        mod_out = [pl.BlockSpec((None, nb, cols), lambda b, t: (0, 0, b * nt + t))]
        mod_shape = [jax.ShapeDtypeStruct((1, nb, n_mod), F32)]
        mod_args = [c_all, w_mod, b_mod.reshape(DEPTH, 1, n_mod)]
    resident = lambda shape: pl.BlockSpec((None, *shape), lambda b, t: (l, *(0,) * len(shape)),
                                          pipeline_mode=pl.Buffered(1))
    return pl.pallas_call(
        functools.partial(_ssm_kernel, tt=tt, n_cast=len(cast), sps=sps, with_mod=mod_next is not None),
        out_shape=(jax.ShapeDtypeStruct((n_seq * t_len, D_SSM), BF16),
                   jax.ShapeDtypeStruct((n_seq, state_rows, STATE_W), F32),
                   jax.ShapeDtypeStruct((n_seq, state_rows, STATE_W), F32),
                   *cast_shapes, *mod_shape),
        grid=(n_seq, nt),
        in_specs=[
            pl.BlockSpec((tt, D_SSM), lambda b, t: (rb0 + b * nt + t, 0)),
            state_spec, state_spec,
            resident(wb_re.shape[1:]), resident(wb_im.shape[1:]),
            _layer_spec(l, pwj.shape[1:]), _layer_spec(l, seg.shape[1:]),
            resident(wc_re.shape[1:]), resident(wc_im.shape[1:]),
            _layer_spec(l, (1, D_SSM)),
            resident((D_SSM, D_SSM)),
            _layer_spec(l, (1, D_SSM)),
            *cast_in, *mod_in,
        ],
        out_specs=(pl.BlockSpec((tt, D_SSM), lambda b, t: (b * nt + t, 0)), state_spec, state_spec, *cast_out,
                   *mod_out),
        scratch_shapes=[pltpu.VMEM((tt, STATE_W), F32), pltpu.VMEM((tt, STATE_W), F32),
                        pltpu.VMEM((SUBLANES, STATE_W), F32), pltpu.VMEM((SUBLANES, STATE_W), F32),
                        pltpu.VMEM((tt, D_SSM), F32)],
        compiler_params=_cparams(("arbitrary", "arbitrary")),
        name=name,
    )(proj, h0_re, h0_im, wb_re, wb_im, pwj, seg, wc_re, wc_im, d_skip.reshape(DEPTH, 1, D_SSM), w_glu,
      b_glu.reshape(DEPTH, 1, D_SSM), *[w for w, _ in cast], *mod_args)


RING = 3
N_KEYS = RING * CHUNK
ATT_CPS = 8


def _stack_blocks(x):
    return jnp.concatenate([x[:, j * LANES:(j + 1) * LANES] for j in range(x.shape[1] // LANES)], axis=0)


def _pair_norm_rope(xs, gain, tab, ones_bd):
    reps = xs.shape[0] // CHUNK
    cos, s_lo, s_hi = (jnp.concatenate([tab[i]] * reps, axis=0) for i in range(3))
    half = ROPE_DIM // 2
    sq = xs * xs
    sq_hi = sq.astype(BF16)
    sq_lo = (sq - sq_hi.astype(F32)).astype(BF16)
    ss = (jnp.dot(sq_hi, ones_bd, preferred_element_type=F32)
          + jnp.dot(sq_lo, ones_bd, preferred_element_type=F32))
    xg = xs * gain
    xr = xg * cos + pltpu.roll(xg, LANES - half, 1) * s_lo + pltpu.roll(xg, half, 1) * s_hi
    return xr * lax.rsqrt(ss * (1.0 / HEAD_DIM) + EPS)


def _attn_kernel(sink_ref, q_ref, k_ref, v_ref, ck_ref, cv_ref, tab_ref, gq_ref, gk_ref, o_ref, kn_ref, kd, vd,
                 *, l, n_prompt_steps, steps_per_seq):
    step = pl.program_id(0)
    is_sample = step >= n_prompt_steps
    c_base = lax.rem(step, steps_per_seq) * ATT_CPS
    first = lax.broadcasted_iota(jnp.int32, (1, LANES), 1) < HEAD_DIM
    ones_bd = jnp.where(lax.broadcasted_iota(jnp.int32, (LANES, LANES), 0) // HEAD_DIM
                        == lax.broadcasted_iota(jnp.int32, (LANES, LANES), 1) // HEAD_DIM, 1.0, 0.0).astype(BF16)
    key_slot = lax.broadcasted_iota(jnp.int32, (1, N_KEYS), 1) // CHUNK
    gq = gq_ref[...]
    gk = gk_ref[...]

    def store_dup(dst, x, rows):
        for pair in range(D_KV // LANES):
            blk = x[:, pair * LANES:(pair + 1) * LANES]
            swapped = pltpu.roll(blk, HEAD_DIM, 1)
            dst[2 * pair, rows, 0:LANES] = jnp.where(first, blk, swapped).astype(BF16)
            dst[2 * pair + 1, rows, 0:LANES] = jnp.where(first, swapped, blk).astype(BF16)

    @pl.when(step == 0)
    def _():
        vd[:, :, LANES:] = jnp.ones((N_KV_HEADS, N_KEYS, LANES), BF16)

    prev = slice(CHUNK, N_KEYS)

    def chunk(ci, from_cache):
        c = 0 if from_cache else c_base + ci
        rows = pl.ds(pl.multiple_of(ci * CHUNK, CHUNK), CHUNK)
        if from_cache:
            cached = pl.ds(pl.multiple_of(ci * WINDOW, WINDOW), WINDOW)
            store_dup(kd, ck_ref[cached, :], prev)
            store_dup(vd, cv_ref[cached, :], prev)

        tab = tab_ref[:, rows, :]
        own = (slice(0, CHUNK) if from_cache else
               pl.ds(pl.multiple_of(lax.rem(c, RING) * CHUNK, CHUNK), CHUNK))
        kn = _pair_norm_rope(_stack_blocks(k_ref[rows, :]), gk, tab, ones_bd)
        kn = jnp.concatenate([kn[0:CHUNK], kn[CHUNK:2 * CHUNK]], axis=1)
        kn_ref[rows, :] = kn
        store_dup(kd, kn, own)
        store_dup(vd, v_ref[rows, :], own)

        qn = _pair_norm_rope(_stack_blocks(q_ref[rows, :]), gq, tab, ones_bd) * (HEAD_DIM ** -0.5)
        q_lo = jnp.where(first, qn, 0.0).astype(BF16)
        q_hi = jnp.where(first, 0.0, qn).astype(BF16)

        valid = None if from_cache else key_slot <= c
        scores = []
        for kh in range(N_KV_HEADS):
            r0 = kh * 2 * CHUNK
            lhs = jnp.concatenate([q_lo[r0:r0 + CHUNK], q_hi[r0:r0 + CHUNK],
                                   q_lo[r0 + CHUNK:r0 + 2 * CHUNK], q_hi[r0 + CHUNK:r0 + 2 * CHUNK]], axis=0)
            scores.append(lax.dot_general(lhs, kd[kh], (((1,), (1,)), ((), ())),
                                          preferred_element_type=F32))
        weighted, sink_terms = [], []
        for kh in range(N_KV_HEADS):
            s = scores[kh] if from_cache else jnp.where(valid, scores[kh], -1e30)
            sink = jnp.concatenate([jnp.full((CHUNK, 1), sink_ref[l, kh * KV_REP + r], F32)
                                    for r in range(KV_REP)], axis=0)
            m = jnp.maximum(jnp.max(s, axis=-1, keepdims=True), sink)
            p = jnp.exp(s - m).astype(BF16)
            sink_terms.append(jnp.exp(sink - m))
            weighted.append(jnp.dot(p, vd[kh], preferred_element_type=F32))
        for kh in range(N_KV_HEADS):
            od = weighted[kh]
            o = od[:, 0:LANES] / (od[:, LANES:] + sink_terms[kh])
            for pair in range(2):
                half = pair * 2 * CHUNK
                blk = jnp.where(first, o[half:half + CHUNK], o[half + CHUNK:half + 2 * CHUNK])
                col = (2 * kh + pair) * LANES
                o_ref[rows, col:col + LANES] = blk.astype(o_ref.dtype)

    @pl.when(jnp.logical_not(is_sample))
    def _():
        @pl.when(c_base == 0)
        def _():
            kd[:, prev, :] = jnp.zeros((N_KV_HEADS, N_KEYS - CHUNK, LANES), BF16)
            vd[:, prev, 0:LANES] = jnp.zeros((N_KV_HEADS, N_KEYS - CHUNK, LANES), BF16)

        lax.fori_loop(0, ATT_CPS, lambda ci, carry: (chunk(ci, False), carry)[1], 0)

    @pl.when(is_sample)
    def _():
        lax.fori_loop(0, ATT_CPS, lambda ci, carry: (chunk(ci, True), carry)[1], 0)


def _attn_call(proj, l, n_prompt, t_prompt, sink, rope_tab, gq, gk, cache_k, cache_v):
    n_tok = proj.shape[0]
    tr = ATT_CPS * CHUNK
    qcol = D_SSM // D_ATTN
    kcol = (D_SSM + D_ATTN) // D_KV
    n_prompt_steps = n_prompt // tr
    cache_spec = pl.BlockSpec((None, ATT_CPS * WINDOW, D_KV),
                              lambda s: (l, jnp.maximum(s - n_prompt_steps, 0), 0))
    return pl.pallas_call(
        functools.partial(_attn_kernel, l=l, n_prompt_steps=n_prompt_steps, steps_per_seq=t_prompt // tr),
        out_shape=(jax.ShapeDtypeStruct((n_tok, D_ATTN), BF16),
                   jax.ShapeDtypeStruct((n_tok, D_KV), F32)),
        grid=(n_tok // tr,),
        in_specs=[
            pl.BlockSpec(memory_space=pltpu.SMEM),
            pl.BlockSpec((tr, D_ATTN), lambda s: (s, qcol)),
            pl.BlockSpec((tr, D_KV), lambda s: (s, kcol)),
            pl.BlockSpec((tr, D_KV), lambda s: (s, kcol + 1)),
            cache_spec, cache_spec,
            pl.BlockSpec((3, tr, LANES), lambda s: (0, s, 0)),
            _layer_spec(l, (1, LANES)), _layer_spec(l, (1, LANES)),
        ],
        out_specs=(pl.BlockSpec((tr, D_ATTN), lambda s: (s, 0)),
                   pl.BlockSpec((tr, D_KV), lambda s: (s, 0))),
        scratch_shapes=[pltpu.VMEM((N_KV_HEADS, N_KEYS, LANES), BF16),
                        pltpu.VMEM((N_KV_HEADS, N_KEYS, 2 * LANES), BF16)],
        compiler_params=_cparams(("arbitrary",)),
        name="banded_attn",
    )(sink, proj, proj, proj, cache_k, cache_v, rope_tab, gq, gk)


def _block_diag_rows(x, n):
    *lead, r, width = x.shape
    keep = jnp.arange(n)[:, None, None] == (jnp.arange(width) // (width // n))[None, None, :]
    return jnp.where(keep, x[..., None, :, :], 0.0).reshape(*lead, n * r, width)


def _rope_table(pos):
    half = ROPE_DIM // 2
    n_pos = pos.shape[0]
    inv_freq = ROPE_THETA ** (-jnp.arange(half, dtype=F32) / half)
    ang = pos.astype(F32)[:, None] * inv_freq[None, :]
    cos, sin = jnp.cos(ang), jnp.sin(ang)
    ones = jnp.ones((n_pos, HEAD_DIM - ROPE_DIM), F32)
    zeros = jnp.zeros((n_pos, HEAD_DIM - half), F32)
    c_tab = jnp.concatenate([cos, cos, ones], axis=1)
    lo_tab = jnp.concatenate([-sin, zeros], axis=1)
    hi_tab = jnp.concatenate([jnp.zeros((n_pos, half), F32), sin, jnp.zeros((n_pos, HEAD_DIM - ROPE_DIM), F32)],
                             axis=1)
    tab = jnp.stack([c_tab, lo_tab, hi_tab])
    return jnp.concatenate([tab, tab], axis=2)


def kernel(x_prompt, x_sample, cache_k, cache_v, state_ssm_re, state_ssm_im, c_prompt, c_sample, w_mod, b_mod, norm1_g, norm2_g, w_in, ssm_a_re, ssm_a_im, ssm_log_dt, ssm_b_re, ssm_b_im, ssm_c_re, ssm_c_im, ssm_d, w_glu, b_glu, q_norm_g, k_norm_g, attn_sink, w_gate, b_gate, w_proj_ssm, w_proj_attn, w_out, w_ffn_gate, w_ffn_up, w_ffn_down):
    bp, tp, _ = x_prompt.shape
    bs, ts, _ = x_sample.shape
    assert ts == CHUNK and tp % (ATT_CPS * CHUNK) == 0 and bs % ATT_CPS == 0
    assert tp % S5_TILE == 0 and S5_TILE % ts == 0 and (bs * ts) % S5_TILE == 0
    n_p, n_s = bp * tp, bs * ts
    tm = 1024
    assert n_p % tm == 0 and n_s % tm == 0
    x = jnp.concatenate([x_prompt.reshape(n_p, D_MODEL), x_sample.reshape(n_s, D_MODEL)], axis=0)

    n_cond = bp + bs
    pad = (-n_cond) % SUBLANES
    c_all = jnp.concatenate([c_prompt, c_sample, jnp.zeros((pad, D_MODEL), F32)], axis=0)
    n_mod = w_mod.shape[2]

    def per_block(m, n_seq, t_len):
        reps = t_len // MOD_BLOCK
        return jnp.broadcast_to(m[:, :, None, :], (1, n_seq, reps, n_mod)).reshape(1, n_seq * reps, n_mod)

    def mod_rows(mod):
        return jnp.concatenate([per_block(mod[:, :bp], bp, tp), per_block(mod[:, bp:n_cond], bs, ts)], axis=1)

    modx = mod_rows(_mod_call(c_all, w_mod, b_mod, 1))

    g, p = N_SSM_GROUPS, SSM_STATE
    gpc = g // SSM_KC
    pw_re, pw_im, bb_re, bb_im = _s5_prep_call(ssm_a_re, ssm_a_im, ssm_log_dt,
                                               ssm_b_re.transpose(0, 3, 1, 2), ssm_b_im.transpose(0, 3, 1, 2))
    pw_re = pw_re.reshape(DEPTH, len(S5_POWERS), STATE_W)
    pw_im = pw_im.reshape(DEPTH, len(S5_POWERS), STATE_W)

    def wb_blocks(bb):
        per_chunk = bb.reshape(DEPTH, SSM_GROUP, SSM_KC, gpc * p).transpose(0, 2, 1, 3)
        return _block_diag_rows(per_chunk, gpc).astype(BF16)

    def wc_blocks(cc):
        per_chunk = cc.reshape(DEPTH, SSM_KC, gpc, SSM_GROUP, p).transpose(0, 1, 3, 2, 4)
        transposed = _block_diag_rows(per_chunk.reshape(DEPTH, SSM_KC, SSM_GROUP, gpc * p), gpc)
        return jnp.swapaxes(transposed.astype(BF16), -1, -2)

    ssm_w = (wb_blocks(bb_re), wb_blocks(bb_im), pw_re, pw_im, wc_blocks(ssm_c_re), wc_blocks(ssm_c_im), ssm_d,
             w_glu.astype(BF16), b_glu)
    zeros_state = jnp.zeros((bp, 1, STATE_W), F32)
    sps = ts * S5_SEG // S5_TILE
    n_sample_tiles = n_s // S5_TILE

    def per_segment(state):
        rows = jnp.repeat(state.reshape(DEPTH, bs, STATE_W), sps, axis=1)
        return rows.reshape(DEPTH, n_sample_tiles, S5_SEG, STATE_W)

    def last_segment(state):
        return state.reshape(bs, sps, STATE_W)[:, sps - 1].reshape(bs, g, p)

    h0_re, h0_im = per_segment(state_ssm_re), per_segment(state_ssm_im)

    rope_tab = jnp.concatenate([jnp.tile(_rope_table(jnp.arange(tp)), (1, bp, 1)),
                                jnp.tile(_rope_table(PAST_LEN + jnp.arange(ts)), (1, bs, 1))], axis=1)
    gq = jnp.tile(q_norm_g, (1, LANES // HEAD_DIM)).reshape(DEPTH, 1, LANES)
    gk = jnp.tile(k_norm_g, (1, LANES // HEAD_DIM)).reshape(DEPTH, 1, LANES)
    cache_k2 = cache_k.reshape(DEPTH, bs * WINDOW, D_KV)
    cache_v2 = cache_v.reshape(DEPTH, bs * WINDOW, D_KV)

    w_in_b, w_gate_b = w_in[:1].astype(BF16), w_gate[:1].astype(BF16)
    v0 = D_SSM + D_ATTN + D_KV

    def heads(t):
        return t.reshape(*t.shape[:-1], N_KV_HEADS, HEAD_DIM)

    def last_window(t, col0, col1):
        return jnp.stack([t[(b + 1) * tp - WINDOW:(b + 1) * tp, col0:col1] for b in range(bp)])

    outs = {k: [] for k in ("pk", "pv", "pre", "pim", "sk", "sv", "sre", "sim")}
    for l in range(DEPTH):
        proj, gates = _in_gate_call(x, l, norm1_g, modx, w_in_b, w_gate_b, b_gate, tm=tm)

        last = l == DEPTH - 1
        cast = [(w, l) for w in (w_proj_ssm, w_proj_attn, w_out, w_ffn_gate, w_ffn_up, w_ffn_down)]
        if not last:
            cast += [(w_in, l + 1), (w_gate, l + 1)]
        ssm_p, pre, pim, w_ps_b, w_pa_b, w_out_b, w_fg_b, w_fu_b, w_fd_b, *for_next = _ssm_call(
            proj, 0, bp, tp // S5_TILE, zeros_state, zeros_state, l, *ssm_w, sps=S5_SEG, cast=cast,
            mod_next=None if last else (c_all, w_mod, b_mod, l + 1), name="s5_prompt")
        ssm_s, sre, sim = _ssm_call(proj, n_p, n_sample_tiles, 1, h0_re[l], h0_im[l], l, *ssm_w, sps=sps,
                                    name="s5_sample")
        attn_out, kn = _attn_call(proj, l, n_p, tp, attn_sink, rope_tab, gq, gk, cache_k2, cache_v2)

        mixed = _mix_call(ssm_p, ssm_s, attn_out, gates, w_ps_b, w_pa_b, tm=tm, tn=1024)
        x = _resid_mm_call(mixed, l, w_out_b, x, modx, 2, tm=tm, tn=1024, name="out_proj")

        act = _ffn_up_call(x, l, norm2_g, modx, w_fg_b, w_fu_b, tm=tm)
        last = l == DEPTH - 1
        x = _resid_mm_call(act, l, w_fd_b, x, modx, 5, n_split=n_p // tm if last else None, tm=tm,
                           name="ffn_down_split" if last else "ffn_down")

        outs["pk"].append(heads(last_window(kn, 0, D_KV)))
        outs["pv"].append(heads(last_window(proj, v0, IN_WIDTH)))
        outs["pre"].append(pre.reshape(bp, g, p))
        outs["pim"].append(pim.reshape(bp, g, p))
        outs["sk"].append(jnp.concatenate([cache_k[l][:, ts:], heads(kn[n_p:].reshape(bs, ts, D_KV))], axis=1))
        outs["sv"].append(jnp.concatenate([cache_v[l][:, ts:], heads(proj[n_p:, v0:].reshape(bs, ts, D_KV))],
                                          axis=1))
        outs["sre"].append(last_segment(sre))
        outs["sim"].append(last_segment(sim))
        if not last:
            w_in_b, w_gate_b, mod_nxt = for_next
            modx = mod_rows(mod_nxt)

    y_p, y_s = x
    return (y_p.reshape(bp, tp, D_MODEL), y_s.reshape(bs, ts, D_MODEL),
            jnp.stack(outs["pk"]), jnp.stack(outs["pv"]), jnp.stack(outs["pre"]), jnp.stack(outs["pim"]),
            jnp.stack(outs["sk"]), jnp.stack(outs["sv"]), jnp.stack(outs["sre"]), jnp.stack(outs["sim"]))
```
